```python
import math
import jax
import jax.numpy as jnp
from jax import lax
import numpy as np

D_MODEL = 1024
BATCH = 32
SEQ = 2048
DEPTH = 4

HYENA_CH = D_MODEL // 2
CONF_CH = D_MODEL // 2
HYENA_ORDER = 2
HYENA_SHORT_W = 3
HYENA_EMB_DIM = 33
HYENA_FILTER_DIM = 64
HYENA_SHORT_DECAY_PCT = 0.3
HYENA_LONG_DECAY_PCT = 1.5
HYENA_DECAY_TARGET = 1e-2
HYENA_FILTER_OUT_STD = 0.02
CONF_WIDTH = 31
EVEN_IN = 3 * HYENA_CH + 2 * CONF_CH
EVEN_MIX = HYENA_CH + CONF_CH
N_HEADS = 8
HEAD_DIM = 64
ATTN_W = N_HEADS * 2 * HEAD_DIM
Q_BLOCK = 128
N_EXPERTS = 32
TOP_K = 4
D_FF = D_MODEL
SWIGLU_LIMIT = 7.0
SWIGLU_ALPHA = 1.702
EXPERT_BLOCK = 512
DEEPNORM_ALPHA = (2 * DEPTH) ** 0.25
DEEPNORM_BETA = (8 * DEPTH) ** -0.25
N_EVEN = (DEPTH + 1) // 2
N_ODD = DEPTH // 2
LN_EPS = 1e-5

kernel_name = "hyena_conformer_diffattn_moe_deepnorm"


def layer_norm(x, g, b):
    xf = x.astype(jnp.float32)
    mu = jnp.mean(xf, axis=-1, keepdims=True)
    var = jnp.mean(jnp.square(xf - mu), axis=-1, keepdims=True)
    y = (xf - mu) * lax.rsqrt(var + LN_EPS) * g.astype(jnp.float32) + b.astype(jnp.float32)
    return y.astype(x.dtype)


def rms_norm(x, g):
    xf = x.astype(jnp.float32)
    y = xf * lax.rsqrt(jnp.mean(jnp.square(xf), axis=-1, keepdims=True) + LN_EPS)
    return (y * g.astype(jnp.float32)).astype(x.dtype)


def depthwise_conv(x, w, b):
    pad = w.shape[0] // 2
    y = lax.conv_general_dilated(x, w[:, None, :], window_strides=(1,),
                                 padding=[(pad, pad)],
                                 dimension_numbers=("NWC", "WIO", "NWC"),
                                 feature_group_count=x.shape[-1])
    return y + b


def alibi_slopes(n):
    return np.array([2.0 ** (-8.0 * (i + 1) / n) for i in range(n)], dtype=np.float32)


def hyena_filter_spectrum(L, f1_w, f1_b, f1_freq, f2_w, f2_b, f2_freq, f3_w):
    f32 = jnp.float32
    pos = jnp.arange(L, dtype=f32)
    t = jnp.linspace(0.0, 1.0, L, dtype=f32)[:, None]
    bands = (HYENA_EMB_DIM - 1) // 2
    f = jnp.linspace(1e-4, bands - 1, bands, dtype=f32)
    ang = (2.0 * math.pi / L) * pos[:, None] * f[None, :]
    feats = jnp.concatenate([t, jnp.cos(ang), -jnp.sin(ang)], axis=-1)
    h = jnp.sin(f1_freq.astype(f32) * (feats @ f1_w.astype(f32) + f1_b.astype(f32)))
    h = jnp.sin(f2_freq.astype(f32) * (h @ f2_w.astype(f32) + f2_b.astype(f32)))
    h = (h @ f3_w.astype(f32)).reshape(L, 2, HYENA_ORDER, HYENA_CH)
    max_decay = math.log(HYENA_DECAY_TARGET) / HYENA_SHORT_DECAY_PCT
    min_decay = math.log(HYENA_DECAY_TARGET) / HYENA_LONG_DECAY_PCT
    deltas = jnp.abs(jnp.linspace(min_decay, max_decay, HYENA_CH, dtype=f32))
    h = h * jnp.exp(-t * deltas[None, :])[:, None, None, :]
    fwd, bwd = h[:, 0], h[:, 1]
    two_sided = jnp.concatenate([fwd, jnp.zeros_like(fwd[:1]), bwd[:0:-1]], axis=0)
    return jnp.fft.rfft(two_sided, axis=0)


def long_conv(z, kf, skip):
    L = z.shape[1]
    zf32 = z.astype(jnp.float32)
    zf = jnp.fft.rfft(zf32, n=2 * L, axis=1)
    y = jnp.fft.irfft(zf * kf[None], n=2 * L, axis=1)[:, :L]
    return (y + zf32 * skip.astype(jnp.float32)).astype(z.dtype)


def hyena_conformer_mixer(x, w_in, b_in, short_w, short_b, f1_w, f1_b, f1_freq, f2_w, f2_b,
                          f2_freq, f3_w, skip, dw_w, dw_b, cln_g, cln_b, w_out, b_out):
    S = x.shape[1]
    proj = x @ w_in + b_in
    hy, cf = proj[..., :3 * HYENA_CH], proj[..., 3 * HYENA_CH:]
    hy = depthwise_conv(hy, short_w, short_b)
    x1, x2, v = jnp.split(hy, 3, axis=-1)
    kf = hyena_filter_spectrum(S, f1_w, f1_b, f1_freq, f2_w, f2_b, f2_freq, f3_w)
    z = x1 * long_conv(v, kf[:, 0], skip[0])
    z = x2 * long_conv(z, kf[:, 1], skip[1])
    a, g = jnp.split(cf, 2, axis=-1)
    u = a * jax.nn.sigmoid(g)
    u = depthwise_conv(u, dw_w, dw_b)
    u = jax.nn.silu(layer_norm(u, cln_g, cln_b))
    return jnp.concatenate([z, u], axis=-1) @ w_out + b_out


def diff_attention(x, w_qkv, lq1, lk1, lq2, lk2, subln_g, w_out, layer_idx):
    Bb, S, _ = x.shape
    f32 = jnp.float32
    q, k, v = jnp.split(x @ w_qkv, 3, axis=-1)
    q = q.reshape(Bb, S, N_HEADS, 2, HEAD_DIM) * (HEAD_DIM ** -0.5)
    k = k.reshape(Bb, S, N_HEADS, 2, HEAD_DIM)
    v = v.reshape(Bb, S, N_HEADS, 2 * HEAD_DIM)
    lam_init = 0.8 - 0.6 * math.exp(-0.3 * layer_idx)
    lam = (jnp.exp(jnp.sum(lq1.astype(f32) * lk1.astype(f32)))
           - jnp.exp(jnp.sum(lq2.astype(f32) * lk2.astype(f32))) + lam_init)
    slopes = jnp.asarray(alibi_slopes(N_HEADS))
    nq = S // Q_BLOCK
    qb = q.reshape(Bb, nq, Q_BLOCK, N_HEADS, 2, HEAD_DIM).transpose(1, 0, 2, 3, 4, 5)
    kpos = jnp.arange(S, dtype=jnp.int32)

    def block(args):
        qblk, i = args
        qpos = i * Q_BLOCK + jnp.arange(Q_BLOCK, dtype=jnp.int32)
        dist = jnp.abs(qpos[:, None] - kpos[None, :]).astype(f32)
        bias = -slopes[:, None, None] * dist[None]
        s = jnp.einsum("bqhcd,bkhcd->bhcqk", qblk, k).astype(f32) + bias[None, :, None]
        p = jax.nn.softmax(s, axis=-1)
        a = (p[:, :, 0] - lam * p[:, :, 1]).astype(v.dtype)
        return jnp.einsum("bhqk,bkhe->bqhe", a, v)

    o = lax.map(block, (qb, jnp.arange(nq, dtype=jnp.int32)))
    o = o.transpose(1, 0, 2, 3, 4).reshape(Bb, S, N_HEADS, 2 * HEAD_DIM)
    o = rms_norm(o, subln_g) * (1.0 - lam_init)
    return o.reshape(Bb, S, ATTN_W) @ w_out


def moe(x, w_r, b_r, w1, b1, w2, b2):
    Bb, S, D = x.shape
    T = Bb * S
    A = T * TOP_K
    xt = x.reshape(T, D)
    logits = (xt @ w_r).astype(jnp.float32) + b_r.astype(jnp.float32)
    top_val, top_idx = lax.top_k(logits, TOP_K)
    gate = jax.nn.softmax(top_val, axis=-1)
    e_flat = top_idx.reshape(A).astype(jnp.int32)
    tok_flat = jnp.arange(A, dtype=jnp.int32) // TOP_K
    order = jnp.argsort(e_flat)
    e_sorted = e_flat[order]
    counts = jnp.bincount(e_flat, length=N_EXPERTS).astype(jnp.int32)
    padded = (counts + EXPERT_BLOCK - 1) // EXPERT_BLOCK * EXPERT_BLOCK
    pad_end = jnp.cumsum(padded)
    pad_start = pad_end - padded
    grp_start = jnp.cumsum(counts) - counts
    dest = pad_start[e_sorted] + jnp.arange(A, dtype=jnp.int32) - grp_start[e_sorted]
    P = A + N_EXPERTS * EXPERT_BLOCK
    nb = P // EXPERT_BLOCK
    row_tok = jnp.zeros((P,), jnp.int32).at[dest].set(tok_flat[order])
    row_gate = jnp.zeros((P,), jnp.float32).at[dest].set(gate.reshape(A)[order])
    starts = jnp.arange(nb, dtype=jnp.int32) * EXPERT_BLOCK
    blk_expert = jnp.minimum(jnp.searchsorted(pad_end, starts, side="right"),
                             N_EXPERTS - 1).astype(jnp.int32)

    def expert_block(args):
        tok, g, e = args
        h = xt[tok] @ w1[e] + b1[e]
        hg, hu = h[:, :D_FF], h[:, D_FF:]
        hg = jnp.minimum(hg, SWIGLU_LIMIT)
        hu = jnp.clip(hu, -SWIGLU_LIMIT, SWIGLU_LIMIT)
        act = (hu + 1.0) * (hg * jax.nn.sigmoid(hg * SWIGLU_ALPHA))
        return (act @ w2[e] + b2[e]) * g[:, None]

    ys = lax.map(expert_block, (row_tok.reshape(nb, EXPERT_BLOCK),
                                row_gate.reshape(nb, EXPERT_BLOCK).astype(x.dtype),
                                blk_expert))
    out = jax.ops.segment_sum(ys.reshape(P, D), row_tok, num_segments=T)
    return out.reshape(Bb, S, D)


def setup_inputs(seed: int = 0) -> dict:
    key = jax.random.key(seed)
    keys = jax.random.split(key, 40)
    cnt = [0]

    def nrm(shape, scale=1.0):
        k = keys[cnt[0]]
        cnt[0] += 1
        return jax.random.normal(k, shape, jnp.float32) * scale

    beta = DEEPNORM_BETA
    inp = {}
    inp["x"] = nrm((BATCH, SEQ, D_MODEL))
    inp["hy_cf_w_in"] = nrm((N_EVEN, D_MODEL, EVEN_IN), D_MODEL ** -0.5)
    inp["hy_cf_b_in"] = nrm((N_EVEN, EVEN_IN), 0.02)
    inp["hy_short_w"] = nrm((N_EVEN, HYENA_SHORT_W, 3 * HYENA_CH), HYENA_SHORT_W ** -0.5)
    inp["hy_short_b"] = nrm((N_EVEN, 3 * HYENA_CH), 0.02)
    inp["hy_f1_w"] = nrm((N_EVEN, HYENA_EMB_DIM, HYENA_FILTER_DIM), HYENA_EMB_DIM ** -0.5)
    inp["hy_f1_b"] = nrm((N_EVEN, HYENA_FILTER_DIM), 0.02)
    inp["hy_f1_freq"] = 1.0 + nrm((N_EVEN, HYENA_FILTER_DIM), 0.1)
    inp["hy_f2_w"] = nrm((N_EVEN, HYENA_FILTER_DIM, HYENA_FILTER_DIM), HYENA_FILTER_DIM ** -0.5)
    inp["hy_f2_b"] = nrm((N_EVEN, HYENA_FILTER_DIM), 0.02)
    inp["hy_f2_freq"] = 1.0 + nrm((N_EVEN, HYENA_FILTER_DIM), 0.1)
    inp["hy_f3_w"] = nrm((N_EVEN, HYENA_FILTER_DIM, 2 * HYENA_ORDER * HYENA_CH), HYENA_FILTER_OUT_STD)
    inp["hy_skip"] = nrm((N_EVEN, HYENA_ORDER, HYENA_CH))
    inp["cf_dw_w"] = nrm((N_EVEN, CONF_WIDTH, CONF_CH), CONF_WIDTH ** -0.5)
    inp["cf_dw_b"] = nrm((N_EVEN, CONF_CH), 0.02)
    inp["cf_ln_g"] = 1.0 + nrm((N_EVEN, CONF_CH), 0.02)
    inp["cf_ln_b"] = nrm((N_EVEN, CONF_CH), 0.02)
    inp["even_w_out"] = nrm((N_EVEN, EVEN_MIX, D_MODEL), EVEN_MIX ** -0.5 * beta)
    inp["even_b_out"] = nrm((N_EVEN, D_MODEL), 0.02)
    v_scale = jnp.concatenate([jnp.ones((2 * ATTN_W,), jnp.float32),
                               jnp.full((ATTN_W,), beta, jnp.float32)])
    inp["attn_w_qkv"] = nrm((N_ODD, D_MODEL, 3 * ATTN_W), D_MODEL ** -0.5) * v_scale
    inp["attn_lq1"] = nrm((N_ODD, HEAD_DIM), 0.1)
    inp["attn_lk1"] = nrm((N_ODD, HEAD_DIM), 0.1)
    inp["attn_lq2"] = nrm((N_ODD, HEAD_DIM), 0.1)
    inp["attn_lk2"] = nrm((N_ODD, HEAD_DIM), 0.1)
    inp["attn_subln_g"] = 1.0 + nrm((N_ODD, 2 * HEAD_DIM), 0.02)
    inp["attn_w_out"] = nrm((N_ODD, ATTN_W, D_MODEL), ATTN_W ** -0.5 * beta)
    inp["ln1_g"] = 1.0 + nrm((DEPTH, D_MODEL), 0.02)
    inp["ln1_b"] = nrm((DEPTH, D_MODEL), 0.02)
    inp["ln2_g"] = 1.0 + nrm((DEPTH, D_MODEL), 0.02)
    inp["ln2_b"] = nrm((DEPTH, D_MODEL), 0.02)
    inp["moe_w_r"] = nrm((DEPTH, D_MODEL, N_EXPERTS), D_MODEL ** -0.5)
    inp["moe_b_r"] = nrm((DEPTH, N_EXPERTS), 0.01)
    inp["moe_w1"] = nrm((DEPTH, N_EXPERTS, D_MODEL, 2 * D_FF), D_MODEL ** -0.5 * beta)
    inp["moe_b1"] = nrm((DEPTH, N_EXPERTS, 2 * D_FF), 0.02)
    inp["moe_w2"] = nrm((DEPTH, N_EXPERTS, D_FF, D_MODEL), D_FF ** -0.5 * beta)
    inp["moe_b2"] = nrm((DEPTH, N_EXPERTS, D_MODEL), 0.02)
    return inp


def reference(x, hy_cf_w_in, hy_cf_b_in, hy_short_w, hy_short_b, hy_f1_w, hy_f1_b, hy_f1_freq,
              hy_f2_w, hy_f2_b, hy_f2_freq, hy_f3_w, hy_skip, cf_dw_w, cf_dw_b, cf_ln_g, cf_ln_b,
              even_w_out, even_b_out, attn_w_qkv, attn_lq1, attn_lk1, attn_lq2, attn_lk2,
              attn_subln_g, attn_w_out, ln1_g, ln1_b, ln2_g, ln2_b, moe_w_r, moe_b_r,
              moe_w1, moe_b1, moe_w2, moe_b2):
    for i in range(DEPTH):
        j = i // 2
        if i % 2 == 0:
            m = hyena_conformer_mixer(x, hy_cf_w_in[j], hy_cf_b_in[j], hy_short_w[j], hy_short_b[j],
                                      hy_f1_w[j], hy_f1_b[j], hy_f1_freq[j], hy_f2_w[j], hy_f2_b[j],
                                      hy_f2_freq[j], hy_f3_w[j], hy_skip[j], cf_dw_w[j], cf_dw_b[j],
                                      cf_ln_g[j], cf_ln_b[j], even_w_out[j], even_b_out[j])
        else:
            m = diff_attention(x, attn_w_qkv[j], attn_lq1[j], attn_lk1[j], attn_lq2[j], attn_lk2[j],
                               attn_subln_g[j], attn_w_out[j], i)
        x = layer_norm(DEEPNORM_ALPHA * x + m, ln1_g[i], ln1_b[i])
        f = moe(x, moe_w_r[i], moe_b_r[i], moe_w1[i], moe_b1[i], moe_w2[i], moe_b2[i])
        x = layer_norm(DEEPNORM_ALPHA * x + f, ln2_g[i], ln2_b[i])
    return x
```

```python
import functools
import math

import jax
import jax.numpy as jnp
import numpy as np
from jax import lax
from jax.experimental import pallas as pl
from jax.experimental.pallas import tpu as pltpu

F32 = jnp.float32
BF16 = jnp.bfloat16
U32 = jnp.uint32
I32 = jnp.int32

D_MODEL = 1024
HALF = D_MODEL // 2
DEPTH = 4
HYENA_CH = D_MODEL // 2
CONF_CH = D_MODEL // 2
HYENA_ORDER = 2
HYENA_EMB_DIM = 33
HYENA_FILTER_DIM = 64
HYENA_SHORT_DECAY_PCT = 0.3
HYENA_LONG_DECAY_PCT = 1.5
HYENA_DECAY_TARGET = 1e-2
CONF_WIDTH = 31
EVEN_IN = 3 * HYENA_CH + 2 * CONF_CH
N_HEADS = 8
HEAD_DIM = 64
ATTN_W = N_HEADS * 2 * HEAD_DIM
N_EXPERTS = 32
TOP_K = 4
D_FF = D_MODEL
SWIGLU_LIMIT = 7.0
SWIGLU_ALPHA = 1.702
DEEPNORM_ALPHA = (2 * DEPTH) ** 0.25
LN_EPS = 1e-5

LANES = 128
VMEM_LIMIT_BYTES = 56 * 1024 * 1024
ROW_TILE = 512
EXPERT_ROWS = 512
FREQ_TILE = 512
ATTN_Q_TILE = 512
CONV_PAD = 16

_NT = (((1,), (1,)), ((), ()))


def _params(*sem):
    return pltpu.CompilerParams(dimension_semantics=sem, vmem_limit_bytes=VMEM_LIMIT_BYTES)


def _split_bf16(a):
    hi = a.astype(BF16)
    lo = (a - hi.astype(F32)).astype(BF16)
    return hi, lo


def _dot3(a, b):
    a_hi, a_lo = _split_bf16(a)
    b_hi, b_lo = _split_bf16(b)
    d = functools.partial(jnp.dot, preferred_element_type=F32)
    return d(a_hi, b_hi) + d(a_hi, b_lo) + d(a_lo, b_hi)


def _layer_norm_rows(y, g, b):
    mu = jnp.mean(y, axis=-1, keepdims=True)
    yc = y - mu
    var = jnp.mean(yc * yc, axis=-1, keepdims=True)
    return yc * lax.rsqrt(var + LN_EPS) * g + b


def _pack_halves(y):
    lo = lax.bitcast_convert_type(y[:, :HALF].astype(BF16).astype(F32), U32)
    hi = lax.bitcast_convert_type(y[:, HALF:].astype(BF16).astype(F32), U32)
    return hi | (lo >> 16)


def _unpack_halves(p):
    lo = lax.bitcast_convert_type(p << 16, F32)
    hi = lax.bitcast_convert_type(p & jnp.uint32(0xFFFF0000), F32)
    return lo, hi


def _proj_body(x_ref, w_ref, b_ref, o_ref, *, col_chunk):
    x = x_ref[...].astype(BF16)
    for j in range(0, o_ref.shape[1], col_chunk):
        acc = jnp.dot(x, w_ref[:, j:j + col_chunk], preferred_element_type=F32)
        o_ref[:, j:j + col_chunk] = (acc + b_ref[:, j:j + col_chunk]).astype(o_ref.dtype)


def _project(x, w, b):
    T, K = x.shape
    N = w.shape[1]
    return pl.pallas_call(
        functools.partial(_proj_body, col_chunk=512),
        grid=(T // ROW_TILE,),
        in_specs=[pl.BlockSpec((ROW_TILE, K), lambda i: (i, 0)),
                  pl.BlockSpec((K, N), lambda i: (0, 0)),
                  pl.BlockSpec((1, N), lambda i: (0, 0))],
        out_specs=pl.BlockSpec((ROW_TILE, N), lambda i: (i, 0)),
        out_shape=jax.ShapeDtypeStruct((T, N), BF16),
        compiler_params=_params("parallel"),
        name="project",
    )(x, w, b)


def _outproj_ln_body(a1_ref, a2_ref, w1_ref, w2_ref, b_ref, x_ref, g_ref, beta_ref, xo_ref, xp_ref):
    m = (jnp.dot(a1_ref[...], w1_ref[...], preferred_element_type=F32)
         + jnp.dot(a2_ref[...], w2_ref[...], preferred_element_type=F32) + b_ref[...])
    y = _layer_norm_rows(DEEPNORM_ALPHA * x_ref[...] + m, g_ref[...], beta_ref[...])
    xo_ref[...] = y
    xp_ref[...] = _pack_halves(y)


def _outproj_ln(a1, a1_col, a2, a2_col, w, b, x, g, beta):
    T = x.shape[0]
    return pl.pallas_call(
        _outproj_ln_body,
        grid=(T // ROW_TILE,),
        in_specs=[pl.BlockSpec((ROW_TILE, HALF), lambda i: (i, a1_col)),
                  pl.BlockSpec((ROW_TILE, HALF), lambda i: (i, a2_col)),
                  pl.BlockSpec((HALF, D_MODEL), lambda i: (0, 0)),
                  pl.BlockSpec((HALF, D_MODEL), lambda i: (1, 0)),
                  pl.BlockSpec((1, D_MODEL), lambda i: (0, 0)),
                  pl.BlockSpec((ROW_TILE, D_MODEL), lambda i: (i, 0)),
                  pl.BlockSpec((1, D_MODEL), lambda i: (0, 0)),
                  pl.BlockSpec((1, D_MODEL), lambda i: (0, 0))],
        out_specs=[pl.BlockSpec((ROW_TILE, D_MODEL), lambda i: (i, 0)),
                   pl.BlockSpec((ROW_TILE, HALF), lambda i: (i, 0))],
        out_shape=[jax.ShapeDtypeStruct((T, D_MODEL), F32), jax.ShapeDtypeStruct((T, HALF), U32)],
        compiler_params=_params("parallel"),
        name="outproj_ln",
    )(a1, a2, w, w, b, x, g, beta)


def _short_conv_body(x_ref, w_ref, b_ref, o_ref):
    x = x_ref[...].astype(F32)
    S = x.shape[0]
    row = lax.broadcasted_iota(I32, x.shape, 0)
    prev = jnp.where(row == 0, 0.0, pltpu.roll(x, 1, 0))
    nxt = jnp.where(row == S - 1, 0.0, pltpu.roll(x, S - 1, 0))
    y = w_ref[0:1, :] * prev + w_ref[1:2, :] * x + w_ref[2:3, :] * nxt + b_ref[...]
    o_ref[...] = y.astype(o_ref.dtype)


def _short_conv(proj, w, b, B, S):
    T = B * S
    C = HYENA_CH
    return pl.pallas_call(
        _short_conv_body,
        grid=(B, 3),
        in_specs=[pl.BlockSpec((S, C), lambda bi, j: (bi, j)),
                  pl.BlockSpec((3, C), lambda bi, j: (0, j)),
                  pl.BlockSpec((1, C), lambda bi, j: (0, j))],
        out_specs=pl.BlockSpec((S, C), lambda bi, j: (bi, j)),
        out_shape=jax.ShapeDtypeStruct((T, 3 * C), BF16),
        compiler_params=_params("parallel", "parallel"),
        name="hyena_short_conv",
    )(proj, w, b)


def _conformer_body(a_ref, g_ref, w_ref, b_ref, lg_ref, lb_ref, o_ref, pad_ref, *, row_chunk):
    S, C = a_ref.shape
    zeros = jnp.zeros((CONV_PAD, C), F32)
    pad_ref[0:CONV_PAD, :] = zeros
    pad_ref[CONV_PAD + S:CONV_PAD + S + CONV_PAD, :] = zeros
    a = a_ref[...].astype(F32)
    g = g_ref[...].astype(F32)
    pad_ref[CONV_PAD:CONV_PAD + S, :] = a * jax.nn.sigmoid(g)
    half = CONF_WIDTH // 2
    for r0 in range(0, S, row_chunk):
        acc = jnp.zeros((row_chunk, C), F32) + b_ref[...]
        for j in range(CONF_WIDTH):
            start = CONV_PAD + r0 + j - half
            acc = acc + w_ref[j:j + 1, :] * pad_ref[start:start + row_chunk, :]
        y = _layer_norm_rows(acc, lg_ref[...], lb_ref[...])
        o_ref[r0:r0 + row_chunk, :] = (y * jax.nn.sigmoid(y)).astype(o_ref.dtype)


def _conformer(proj, w, b, lg, lb, B, S):
    T = B * S
    C = CONF_CH
    return pl.pallas_call(
        functools.partial(_conformer_body, row_chunk=min(S, 256)),
        grid=(B,),
        in_specs=[pl.BlockSpec((S, C), lambda bi: (bi, 3)),
                  pl.BlockSpec((S, C), lambda bi: (bi, 4)),
                  pl.BlockSpec((CONF_WIDTH, C), lambda bi: (0, 0)),
                  pl.BlockSpec((1, C), lambda bi: (0, 0)),
                  pl.BlockSpec((1, C), lambda bi: (0, 0)),
                  pl.BlockSpec((1, C), lambda bi: (0, 0))],
        out_specs=pl.BlockSpec((S, C), lambda bi: (bi, 0)),
        out_shape=jax.ShapeDtypeStruct((T, C), BF16),
        scratch_shapes=[pltpu.VMEM((S + 2 * CONV_PAD, C), F32)],
        compiler_params=_params("parallel"),
        name="conformer_conv",
    )(proj, proj, w, b, lg, lb)


def _filter_body(feat_ref, w1_ref, b1_ref, q1_ref, w2_ref, b2_ref, q2_ref, w3_ref, t_ref, delta_ref, o_ref):
    h = jnp.sin(q1_ref[...] * (_dot3(feat_ref[...], w1_ref[...]) + b1_ref[...]))
    h = jnp.sin(q2_ref[...] * (_dot3(h, w2_ref[...]) + b2_ref[...]))
    h = _dot3(h, w3_ref[...])
    o_ref[...] = h * jnp.exp(-t_ref[...] * delta_ref[...])


def _hyena_filters(feats, w1, b1, q1, w2, b2, q2, w3, tcol, deltas):
    S = feats.shape[0]
    C = HYENA_CH
    n = 2 * HYENA_ORDER
    fd = HYENA_FILTER_DIM
    return pl.pallas_call(
        _filter_body,
        grid=(n,),
        in_specs=[pl.BlockSpec((S, LANES), lambda j: (0, 0)),
                  pl.BlockSpec((LANES, fd), lambda j: (0, 0)),
                  pl.BlockSpec((1, fd), lambda j: (0, 0)),
                  pl.BlockSpec((1, fd), lambda j: (0, 0)),
                  pl.BlockSpec((fd, fd), lambda j: (0, 0)),
                  pl.BlockSpec((1, fd), lambda j: (0, 0)),
                  pl.BlockSpec((1, fd), lambda j: (0, 0)),
                  pl.BlockSpec((fd, C), lambda j: (0, j)),
                  pl.BlockSpec((S, 1), lambda j: (0, 0)),
                  pl.BlockSpec((1, C), lambda j: (0, 0))],
        out_specs=pl.BlockSpec((S, C), lambda j: (0, j)),
        out_shape=jax.ShapeDtypeStruct((S, n * C), F32),
        compiler_params=_params("parallel"),
        name="hyena_filter_mlp",
    )(feats, w1, b1, q1, w2, b2, q2, w3, tcol, deltas)


def _spectrum_body(fc_ref, fs_ref, fwd_ref, bwd_ref, hre_ref, him_ref):
    fwd = fwd_ref[...]
    row = lax.broadcasted_iota(I32, fwd.shape, 0)
    bwd = jnp.where(row == 0, 0.0, bwd_ref[...])
    hre_ref[...] = _dot3(fc_ref[...], fwd + bwd)
    him_ref[...] = _dot3(fs_ref[...], fwd - bwd)


def _filter_spectrum(fc32, fs32, taps):
    S = fc32.shape[0]
    C = HYENA_CH
    ft = min(S, 256)
    spec = pl.BlockSpec((ft, C), lambda kb, o: (kb, o))
    return pl.pallas_call(
        _spectrum_body,
        grid=(S // ft, HYENA_ORDER),
        in_specs=[pl.BlockSpec((ft, S), lambda kb, o: (kb, 0)),
                  pl.BlockSpec((ft, S), lambda kb, o: (kb, 0)),
                  pl.BlockSpec((S, C), lambda kb, o: (0, o)),
                  pl.BlockSpec((S, C), lambda kb, o: (0, HYENA_ORDER + o))],
        out_specs=[spec, spec],
        out_shape=[jax.ShapeDtypeStruct((S, HYENA_ORDER * C), F32)] * 2,
        compiler_params=_params("parallel", "parallel"),
        name="hyena_filter_spectrum",
    )(fc32, fs32, taps, taps)


def _long_conv_body(v_ref, gate_ref, fc_ref, fs_ref, hre_ref, him_ref, gc_ref, gs_ref, skip_ref, o_ref, acc_ref):
    kb = pl.program_id(1)
    v = v_ref[...]
    zre = jnp.dot(fc_ref[...], v, preferred_element_type=F32)
    zim = jnp.dot(fs_ref[...], v, preferred_element_type=F32)
    hre = hre_ref[...]
    him = him_ref[...]
    yre = (zre * hre - zim * him).astype(BF16)
    yim = (zre * him + zim * hre).astype(BF16)
    part = (jnp.dot(gc_ref[...], yre, preferred_element_type=F32)
            + jnp.dot(gs_ref[...], yim, preferred_element_type=F32))

    @pl.when(kb == 0)
    def _():
        acc_ref[...] = part

    @pl.when(kb > 0)
    def _():
        acc_ref[...] += part

    @pl.when(kb == pl.num_programs(1) - 1)
    def _():
        y = acc_ref[...] + v.astype(F32) * skip_ref[...]
        o_ref[...] = (gate_ref[...].astype(F32) * y).astype(o_ref.dtype)


def _long_conv(v_arr, v_col, gate_arr, gate_col, tabs, hre, him, order, skip, B, S):
    fc, fs, gc, gs = tabs
    T = B * S
    C = HYENA_CH
    ft = min(S, FREQ_TILE)
    return pl.pallas_call(
        _long_conv_body,
        grid=(B, S // ft),
        in_specs=[pl.BlockSpec((S, C), lambda bi, kb: (bi, v_col)),
                  pl.BlockSpec((S, C), lambda bi, kb: (bi, gate_col)),
                  pl.BlockSpec((ft, S), lambda bi, kb: (kb, 0)),
                  pl.BlockSpec((ft, S), lambda bi, kb: (kb, 0)),
                  pl.BlockSpec((ft, C), lambda bi, kb: (kb, order)),
                  pl.BlockSpec((ft, C), lambda bi, kb: (kb, order)),
                  pl.BlockSpec((S, ft), lambda bi, kb: (0, kb)),
                  pl.BlockSpec((S, ft), lambda bi, kb: (0, kb)),
                  pl.BlockSpec((1, C), lambda bi, kb: (0, 0))],
        out_specs=pl.BlockSpec((S, C), lambda bi, kb: (bi, 0)),
        out_shape=jax.ShapeDtypeStruct((T, C), BF16),
        scratch_shapes=[pltpu.VMEM((S, C), F32)],
        compiler_params=_params("parallel", "arbitrary"),
        name="hyena_long_conv",
    )(v_arr, gate_arr, fc, fs, hre, him, gc, gs, skip[order][None, :])


def _attn_body(slope_ref, lam_ref, q_ref, k_ref, v_ref, g_ref, o_ref, *, lam_init):
    h = pl.program_id(1)
    qi = pl.program_id(2)
    tq = q_ref.shape[0]
    S = k_ref.shape[0]
    q = q_ref[...]
    k = k_ref[...]
    lane = lax.broadcasted_iota(I32, q.shape, 1)
    zero = jnp.zeros_like(q)
    q1 = jnp.where(lane < HEAD_DIM, q, zero)
    q2 = jnp.where(lane >= HEAD_DIM, q, zero)
    qpos = qi * tq + lax.broadcasted_iota(I32, (tq, S), 0)
    kpos = lax.broadcasted_iota(I32, (tq, S), 1)
    bias = jnp.abs(qpos - kpos).astype(F32) * (-slope_ref[h])

    def softmax_unnorm(qh):
        s = lax.dot_general(qh, k, _NT, preferred_element_type=F32) + bias
        e = jnp.exp(s - jnp.max(s, axis=-1, keepdims=True))
        return e, jnp.sum(e, axis=-1, keepdims=True)

    e1, l1 = softmax_unnorm(q1)
    e2, l2 = softmax_unnorm(q2)
    a = (e1 * (1.0 / l1) - e2 * (lam_ref[0] / l2)).astype(BF16)
    o = jnp.dot(a, v_ref[...], preferred_element_type=F32)
    o = o * lax.rsqrt(jnp.mean(o * o, axis=-1, keepdims=True) + LN_EPS) * g_ref[...]
    o_ref[...] = (o * (1.0 - lam_init)).astype(o_ref.dtype)


def _diff_attention(qkv, slopes, lam, subln_g, lam_init, B, S):
    T = B * S
    hw = 2 * HEAD_DIM
    tq = min(S, ATTN_Q_TILE)
    nq = S // tq
    smem = pl.BlockSpec(memory_space=pltpu.SMEM)
    return pl.pallas_call(
        functools.partial(_attn_body, lam_init=lam_init),
        grid=(B, N_HEADS, nq),
        in_specs=[smem, smem,
                  pl.BlockSpec((tq, hw), lambda bi, h, qi: (bi * nq + qi, h)),
                  pl.BlockSpec((S, hw), lambda bi, h, qi: (bi, N_HEADS + h)),
                  pl.BlockSpec((S, hw), lambda bi, h, qi: (bi, 2 * N_HEADS + h)),
                  pl.BlockSpec((1, hw), lambda bi, h, qi: (0, 0))],
        out_specs=pl.BlockSpec((tq, hw), lambda bi, h, qi: (bi * nq + qi, h)),
        out_shape=jax.ShapeDtypeStruct((T, ATTN_W), BF16),
        compiler_params=_params("parallel", "parallel", "parallel"),
        name="diff_attention",
    )(slopes, lam, qkv, qkv, qkv, subln_g)


def _router_body(x_ref, wt_ref, b_ref, idx_ref, gate_ref, rank_ref, cnt_ref, run_ref):
    i = pl.program_id(0)
    E = N_EXPERTS
    tm = x_ref.shape[0]

    @pl.when(i == 0)
    def _():
        run_ref[...] = jnp.zeros_like(run_ref)

    x_hi, x_lo = _split_bf16(x_ref[...])
    w_hi, w_lo = _split_bf16(wt_ref[...])
    nt = functools.partial(lax.dot_general, dimension_numbers=_NT, preferred_element_type=F32)
    logits = nt(w_hi, x_hi) + nt(w_lo, x_hi) + nt(w_hi, x_lo) + b_ref[...]

    eid = lax.broadcasted_iota(I32, (E, tm), 0).astype(F32)
    work = logits
    vals, idxs = [], []
    for _ in range(TOP_K):
        m = jnp.max(work, axis=0, keepdims=True)
        sel = jnp.min(jnp.where(work == m, eid, float(E)), axis=0, keepdims=True)
        vals.append(m)
        idxs.append(sel)
        work = jnp.where(eid == sel, -jnp.inf, work)
    exps = [jnp.exp(v - vals[0]) for v in vals]
    denom = exps[0] + exps[1] + exps[2] + exps[3]

    chosen = jnp.zeros((E, tm), F32)
    for sel in idxs:
        chosen = chosen + jnp.where(eid == sel, 1.0, 0.0)
    earlier = jnp.where(lax.broadcasted_iota(I32, (tm, tm), 0) < lax.broadcasted_iota(I32, (tm, tm), 1), 1.0, 0.0)
    before = jnp.dot(chosen.astype(BF16), earlier.astype(BF16), preferred_element_type=F32) + run_ref[:, 0:1]
    for k in range(TOP_K):
        gate_ref[k:k + 1, :] = exps[k] / denom
        idx_ref[k:k + 1, :] = idxs[k].astype(I32)
        rank_ref[k:k + 1, :] = jnp.sum(jnp.where(eid == idxs[k], before, 0.0), axis=0, keepdims=True).astype(I32)
    run_ref[...] = run_ref[...] + jnp.sum(chosen, axis=1, keepdims=True)
    cnt_ref[...] = run_ref[...]


def _router(x, w_rt, b_r):
    T = x.shape[0]
    E = N_EXPERTS
    tm = min(T, ROW_TILE)
    tok = pl.BlockSpec((TOP_K, tm), lambda i: (0, i))
    return pl.pallas_call(
        _router_body,
        grid=(T // tm,),
        in_specs=[pl.BlockSpec((tm, D_MODEL), lambda i: (i, 0)),
                  pl.BlockSpec((E, D_MODEL), lambda i: (0, 0)),
                  pl.BlockSpec((E, 1), lambda i: (0, 0))],
        out_specs=[tok, tok, tok, pl.BlockSpec((E, LANES), lambda i: (0, 0))],
        out_shape=[jax.ShapeDtypeStruct((TOP_K, T), I32), jax.ShapeDtypeStruct((TOP_K, T), F32),
                   jax.ShapeDtypeStruct((TOP_K, T), I32), jax.ShapeDtypeStruct((E, LANES), F32)],
        scratch_shapes=[pltpu.VMEM((E, LANES), F32)],
        compiler_params=_params("arbitrary"),
        name="moe_router",
    )(x, w_rt, b_r)


def _row_copy(src, src_row, dst, dst_row, sem):
    return pltpu.make_async_copy(src.at[pl.ds(src_row, 1)], dst.at[pl.ds(dst_row, 1)], sem)


def _dispatch_body(dest_ref, tail_ref, xp_hbm, xs_hbm, zero_ref, sem):
    i = pl.program_id(0)
    tm = dest_ref.shape[1]

    @pl.when(i == 0)
    def _():
        zero_ref[...] = jnp.zeros_like(zero_ref)
        for e in range(N_EXPERTS):
            pltpu.make_async_copy(zero_ref, xs_hbm.at[pl.ds(pl.multiple_of(tail_ref[e], 8), EXPERT_ROWS)], sem).start()
        for e in range(N_EXPERTS):
            pltpu.make_async_copy(zero_ref, xs_hbm.at[pl.ds(pl.multiple_of(tail_ref[e], 8), EXPERT_ROWS)], sem).wait()

    def issue(t, carry):
        for k in range(TOP_K):
            _row_copy(xp_hbm, i * tm + t, xs_hbm, dest_ref[k, t], sem).start()
        return carry

    def drain(t, carry):
        for k in range(TOP_K):
            _row_copy(xp_hbm, i * tm + t, xs_hbm, dest_ref[k, t], sem).wait()
        return carry

    lax.fori_loop(0, tm, issue, 0)
    lax.fori_loop(0, tm, drain, 0)


def _dispatch(xp, dest, tail_start, n_rows):
    T = xp.shape[0]
    tm = min(T, 2048)
    return pl.pallas_call(
        _dispatch_body,
        grid=(T // tm,),
        in_specs=[pl.BlockSpec((TOP_K, tm), lambda i: (0, i), memory_space=pltpu.SMEM),
                  pl.BlockSpec(memory_space=pltpu.SMEM),
                  pl.BlockSpec(memory_space=pl.ANY)],
        out_specs=pl.BlockSpec(memory_space=pl.ANY),
        out_shape=jax.ShapeDtypeStruct((n_rows, HALF), U32),
        scratch_shapes=[pltpu.VMEM((EXPERT_ROWS, HALF), U32), pltpu.SemaphoreType.DMA(())],
        compiler_params=_params("arbitrary"),
        name="moe_dispatch",
    )(dest, tail_start, xp)


def _ffn_body(be_ref, nu_ref, xs_ref, w1_ref, b1_ref, w2_ref, b2_ref, ys_ref, w1b_ref, w2b_ref):
    i = pl.program_id(0)
    used = i < nu_ref[0]
    fresh = jnp.logical_or(i == 0, be_ref[i] != be_ref[jnp.maximum(i - 1, 0)])

    @pl.when(jnp.logical_and(used, fresh))
    def _():
        w1b_ref[...] = w1_ref[...].astype(BF16)
        w2b_ref[...] = w2_ref[...].astype(BF16)

    @pl.when(used)
    def _():
        lo, hi = _unpack_halves(xs_ref[...])
        h = (jnp.dot(lo.astype(BF16), w1b_ref[0:HALF, :], preferred_element_type=F32)
             + jnp.dot(hi.astype(BF16), w1b_ref[HALF:D_MODEL, :], preferred_element_type=F32) + b1_ref[...])
        hg = jnp.minimum(h[:, :D_FF], SWIGLU_LIMIT)
        hu = jnp.clip(h[:, D_FF:], -SWIGLU_LIMIT, SWIGLU_LIMIT)
        act = (hu + 1.0) * (hg * jax.nn.sigmoid(hg * SWIGLU_ALPHA))
        y = jnp.dot(act.astype(BF16), w2b_ref[...], preferred_element_type=F32) + b2_ref[...]
        ys_ref[...] = _pack_halves(y)


def _expert_ffn(xs, blk_expert, n_used, w1, b1, w2, b2, layer, n_blocks):
    rows = pl.BlockSpec((EXPERT_ROWS, HALF), lambda i, be, nu: (jnp.minimum(i, nu[0] - 1), 0))
    return pl.pallas_call(
        _ffn_body,
        grid_spec=pltpu.PrefetchScalarGridSpec(
            num_scalar_prefetch=2,
            grid=(n_blocks,),
            in_specs=[rows,
                      pl.BlockSpec((None, None, D_MODEL, 2 * D_FF), lambda i, be, nu: (layer, be[i], 0, 0)),
                      pl.BlockSpec((None, None, 1, 2 * D_FF), lambda i, be, nu: (layer, be[i], 0, 0)),
                      pl.BlockSpec((None, None, D_FF, D_MODEL), lambda i, be, nu: (layer, be[i], 0, 0)),
                      pl.BlockSpec((None, None, 1, D_MODEL), lambda i, be, nu: (layer, be[i], 0, 0))],
            out_specs=rows,
            scratch_shapes=[pltpu.VMEM((D_MODEL, 2 * D_FF), BF16), pltpu.VMEM((D_FF, D_MODEL), BF16)]),
        out_shape=jax.ShapeDtypeStruct((xs.shape[0], HALF), U32),
        compiler_params=_params("arbitrary"),
        name="moe_expert_ffn",
    )(blk_expert, n_used, xs, w1, b1, w2, b2)


def _combine_body(dest_ref, gate_ref, x_ref, g_ref, beta_ref, ys_hbm, xo_ref, xp_ref, buf_ref, sem):
    tm = x_ref.shape[0]

    def issue(t, carry):
        for k in range(TOP_K):
            pltpu.make_async_copy(ys_hbm.at[pl.ds(dest_ref[k, t], 1)], buf_ref.at[k, pl.ds(t, 1)], sem).start()
        return carry

    def drain(t, carry):
        for k in range(TOP_K):
            pltpu.make_async_copy(ys_hbm.at[pl.ds(dest_ref[k, t], 1)], buf_ref.at[k, pl.ds(t, 1)], sem).wait()
        return carry

    lax.fori_loop(0, tm, issue, 0)
    lax.fori_loop(0, tm, drain, 0)

    f_lo = jnp.zeros((tm, HALF), F32)
    f_hi = jnp.zeros((tm, HALF), F32)
    gates = gate_ref[...]
    for k in range(TOP_K):
        lo, hi = _unpack_halves(buf_ref[k])
        gk = gates[:, k:k + 1]
        f_lo = f_lo + gk * lo
        f_hi = f_hi + gk * hi
    f = jnp.concatenate([f_lo, f_hi], axis=1)
    y = _layer_norm_rows(DEEPNORM_ALPHA * x_ref[...] + f, g_ref[...], beta_ref[...])
    xo_ref[...] = y
    xp_ref[...] = _pack_halves(y)


def _combine_ln(dest, gate_t, x, g, beta, ys):
    T = x.shape[0]
    tm = min(T, ROW_TILE)
    return pl.pallas_call(
        _combine_body,
        grid=(T // tm,),
        in_specs=[pl.BlockSpec((TOP_K, tm), lambda i: (0, i), memory_space=pltpu.SMEM),
                  pl.BlockSpec((tm, TOP_K), lambda i: (i, 0)),
                  pl.BlockSpec((tm, D_MODEL), lambda i: (i, 0)),
                  pl.BlockSpec((1, D_MODEL), lambda i: (0, 0)),
                  pl.BlockSpec((1, D_MODEL), lambda i: (0, 0)),
                  pl.BlockSpec(memory_space=pl.ANY)],
        out_specs=[pl.BlockSpec((tm, D_MODEL), lambda i: (i, 0)),
                   pl.BlockSpec((tm, HALF), lambda i: (i, 0))],
        out_shape=[jax.ShapeDtypeStruct((T, D_MODEL), F32), jax.ShapeDtypeStruct((T, HALF), U32)],
        scratch_shapes=[pltpu.VMEM((TOP_K, tm, HALF), U32), pltpu.SemaphoreType.DMA(())],
        compiler_params=_params("arbitrary"),
        name="moe_combine_ln",
    )(dest, gate_t, x, g, beta, ys)


def _dft_tables(S):
    n2 = 4 * S
    k = jnp.arange(S, dtype=I32)
    m = ((2 * k[:, None] + 1) * k[None, :]) % n2
    ang = m.astype(F32) * F32(2.0 * math.pi / n2)
    fc32, fs32 = jnp.cos(ang), -jnp.sin(ang)
    scale = F32(1.0 / S)
    gc, gs = (fc32.T * scale).astype(BF16), (fs32.T * scale).astype(BF16)
    return fc32, fs32, (fc32.astype(BF16), fs32.astype(BF16), gc, gs)


def _hyena_positional(S):
    pos = jnp.arange(S, dtype=F32)
    t = jnp.linspace(0.0, 1.0, S, dtype=F32)[:, None]
    bands = (HYENA_EMB_DIM - 1) // 2
    f = jnp.linspace(1e-4, bands - 1, bands, dtype=F32)
    ang = (2.0 * math.pi / S) * pos[:, None] * f[None, :]
    feats = jnp.concatenate([t, jnp.cos(ang), -jnp.sin(ang)], axis=-1)
    feats = jnp.pad(feats, ((0, 0), (0, LANES - HYENA_EMB_DIM)))
    max_decay = math.log(HYENA_DECAY_TARGET) / HYENA_SHORT_DECAY_PCT
    min_decay = math.log(HYENA_DECAY_TARGET) / HYENA_LONG_DECAY_PCT
    deltas = jnp.abs(jnp.linspace(min_decay, max_decay, HYENA_CH, dtype=F32))[None, :]
    return feats, t, deltas


def _alibi_slopes():
    return jnp.asarray(np.array([2.0 ** (-8.0 * (i + 1) / N_HEADS) for i in range(N_HEADS)], dtype=np.float32))


def _even_mixer(x, xshape, tabs, w_in, b_in, short_w, short_b, f1_w, f1_b, f1_freq, f2_w, f2_b, f2_freq, f3_w,
                skip, dw_w, dw_b, cln_g, cln_b, w_out, b_out, ln_g, ln_b):
    B, S = xshape
    fc32, fs32, tabs16 = tabs
    proj = _project(x, w_in.astype(BF16), b_in[None, :])
    hy = _short_conv(proj, short_w, short_b[None, :], B, S)
    u = _conformer(proj, dw_w, dw_b[None, :], cln_g[None, :], cln_b[None, :], B, S)
    feats, tcol, deltas = _hyena_positional(S)
    f1_wp = jnp.pad(f1_w, ((0, LANES - HYENA_EMB_DIM), (0, 0)))
    taps = _hyena_filters(feats, f1_wp, f1_b[None, :], f1_freq[None, :], f2_w, f2_b[None, :], f2_freq[None, :],
                          f3_w, tcol, deltas)
    hre, him = _filter_spectrum(fc32, fs32, taps)
    z = _long_conv(hy, 2, hy, 0, tabs16, hre, him, 0, skip, B, S)
    z = _long_conv(z, 0, hy, 1, tabs16, hre, him, 1, skip, B, S)
    return _outproj_ln(z, 0, u, 0, w_out.astype(BF16), b_out[None, :], x, ln_g[None, :], ln_b[None, :])


def _odd_mixer(x, xshape, layer_idx, w_qkv, lq1, lk1, lq2, lk2, subln_g, w_out, ln_g, ln_b):
    B, S = xshape
    lam_init = 0.8 - 0.6 * math.exp(-0.3 * layer_idx)
    lam = (jnp.exp(jnp.sum(lq1 * lk1)) - jnp.exp(jnp.sum(lq2 * lk2)) + lam_init).reshape(1)
    q_scale = jnp.concatenate([jnp.full((ATTN_W,), HEAD_DIM ** -0.5, F32), jnp.ones((2 * ATTN_W,), F32)])
    w = (w_qkv * q_scale).astype(BF16)
    qkv = _project(x, w, jnp.zeros((1, 3 * ATTN_W), F32))
    o = _diff_attention(qkv, _alibi_slopes(), lam, subln_g[None, :], lam_init, B, S)
    return _outproj_ln(o, 0, o, 1, w_out.astype(BF16), jnp.zeros((1, D_MODEL), F32), x, ln_g[None, :], ln_b[None, :])


def _moe_layer(x, xp, layer, w_r, b_r, w1, b1, w2, b2, ln_g, ln_b):
    T = x.shape[0]
    E = N_EXPERTS
    n_rows = T * TOP_K + E * EXPERT_ROWS
    n_blocks = n_rows // EXPERT_ROWS
    idx, gate, rank, cnt = _router(x, w_r.T, b_r[:, None])
    counts = cnt[:, 0].astype(I32)
    padded = (counts + EXPERT_ROWS - 1) // EXPERT_ROWS * EXPERT_ROWS
    pad_end = jnp.cumsum(padded)
    pad_start = pad_end - padded
    dest = pad_start[idx] + rank
    starts = jnp.arange(n_blocks, dtype=I32) * EXPERT_ROWS
    blk_expert = jnp.minimum(jnp.searchsorted(pad_end, starts, side="right"), E - 1).astype(I32)
    n_used = (pad_end[-1:] // EXPERT_ROWS).astype(I32)
    tail_start = (pad_start + counts) // 8 * 8
    xs = _dispatch(xp, dest, tail_start, n_rows + EXPERT_ROWS)
    ys = _expert_ffn(xs, blk_expert, n_used, w1, b1[:, :, None, :], w2, b2[:, :, None, :], layer, n_blocks)
    return _combine_ln(dest, gate.T, x, ln_g[None, :], ln_b[None, :], ys)


def kernel(x, hy_cf_w_in, hy_cf_b_in, hy_short_w, hy_short_b, hy_f1_w, hy_f1_b, hy_f1_freq, hy_f2_w, hy_f2_b, hy_f2_freq, hy_f3_w, hy_skip, cf_dw_w, cf_dw_b, cf_ln_g, cf_ln_b, even_w_out, even_b_out, attn_w_qkv, attn_lq1, attn_lk1, attn_lq2, attn_lk2, attn_subln_g, attn_w_out, ln1_g, ln1_b, ln2_g, ln2_b, moe_w_r, moe_b_r, moe_w1, moe_b1, moe_w2, moe_b2):
    B, S, D = x.shape
    assert D == D_MODEL and (B * S) % ROW_TILE == 0 and S % LANES == 0
    depth = ln1_g.shape[0]
    xf = x.reshape(B * S, D)
    tabs = _dft_tables(S)
    for i in range(depth):
        j = i // 2
        if i % 2 == 0:
            xf, xp = _even_mixer(xf, (B, S), tabs, hy_cf_w_in[j], hy_cf_b_in[j], hy_short_w[j], hy_short_b[j],
                                 hy_f1_w[j], hy_f1_b[j], hy_f1_freq[j], hy_f2_w[j], hy_f2_b[j], hy_f2_freq[j],
                                 hy_f3_w[j], hy_skip[j], cf_dw_w[j], cf_dw_b[j], cf_ln_g[j], cf_ln_b[j],
                                 even_w_out[j], even_b_out[j], ln1_g[i], ln1_b[i])
        else:
            xf, xp = _odd_mixer(xf, (B, S), i, attn_w_qkv[j], attn_lq1[j], attn_lk1[j], attn_lq2[j], attn_lk2[j],
                                attn_subln_g[j], attn_w_out[j], ln1_g[i], ln1_b[i])
        xf, xp = _moe_layer(xf, xp, i, moe_w_r[i], moe_b_r[i], moe_w1, moe_b1, moe_w2, moe_b2, ln2_g[i], ln2_b[i])
    return xf.reshape(B, S, D)
```

```python
import functools
import math

import jax
import jax.numpy as jnp
import numpy as np
from jax import lax
from jax.experimental import pallas as pl
from jax.experimental.pallas import tpu as pltpu

F32 = jnp.float32
BF16 = jnp.bfloat16
U32 = jnp.uint32
I32 = jnp.int32

D_MODEL = 1024
HALF = D_MODEL // 2
DEPTH = 4
HYENA_CH = D_MODEL // 2
CONF_CH = D_MODEL // 2
HYENA_ORDER = 2
HYENA_EMB_DIM = 33
HYENA_FILTER_DIM = 64
HYENA_SHORT_DECAY_PCT = 0.3
HYENA_LONG_DECAY_PCT = 1.5
HYENA_DECAY_TARGET = 1e-2
CONF_WIDTH = 31
EVEN_IN = 3 * HYENA_CH + 2 * CONF_CH
N_HEADS = 8
HEAD_DIM = 64
ATTN_W = N_HEADS * 2 * HEAD_DIM
N_EXPERTS = 32
TOP_K = 4
D_FF = D_MODEL
SWIGLU_LIMIT = 7.0
SWIGLU_ALPHA = 1.702
DEEPNORM_ALPHA = (2 * DEPTH) ** 0.25
LN_EPS = 1e-5

LANES = 128
VMEM_LIMIT_BYTES = 56 * 1024 * 1024
ROW_TILE = 512
EXPERT_ROWS = 512
SEG_ALIGN = 8
FREQ_TILE = 512
ATTN_Q_TILE = 512
ATTN_ROW_CHUNK = 128
CONV_PAD = 16

_NT = (((1,), (1,)), ((), ()))


def _params(*sem):
    return pltpu.CompilerParams(dimension_semantics=sem, vmem_limit_bytes=VMEM_LIMIT_BYTES)


def _split_bf16(a):
    hi = a.astype(BF16)
    lo = (a - hi.astype(F32)).astype(BF16)
    return hi, lo


def _dot3(a, b):
    a_hi, a_lo = _split_bf16(a)
    b_hi, b_lo = _split_bf16(b)
    d = functools.partial(jnp.dot, preferred_element_type=F32)
    return d(a_hi, b_hi) + d(a_hi, b_lo) + d(a_lo, b_hi)


def _layer_norm_rows(y, g, b):
    mu = jnp.mean(y, axis=-1, keepdims=True)
    yc = y - mu
    var = jnp.mean(yc * yc, axis=-1, keepdims=True)
    return yc * lax.rsqrt(var + LN_EPS) * g + b


def _pack_halves(y):
    lo = lax.bitcast_convert_type(y[:, :HALF].astype(BF16).astype(F32), U32)
    hi = lax.bitcast_convert_type(y[:, HALF:].astype(BF16).astype(F32), U32)
    return hi | (lo >> 16)


def _unpack_halves(p):
    lo = lax.bitcast_convert_type(p << 16, F32)
    hi = lax.bitcast_convert_type(p & jnp.uint32(0xFFFF0000), F32)
    return lo, hi


def _proj_body(x_ref, w_ref, b_ref, o_ref, *, col_chunk):
    x = x_ref[...].astype(BF16)
    for j in range(0, o_ref.shape[1], col_chunk):
        acc = jnp.dot(x, w_ref[:, j:j + col_chunk], preferred_element_type=F32)
        o_ref[:, j:j + col_chunk] = (acc + b_ref[:, j:j + col_chunk]).astype(o_ref.dtype)


def _project(x, w, b):
    T, K = x.shape
    N = w.shape[1]
    return pl.pallas_call(
        functools.partial(_proj_body, col_chunk=512),
        grid=(T // ROW_TILE,),
        in_specs=[pl.BlockSpec((ROW_TILE, K), lambda i: (i, 0)),
                  pl.BlockSpec((K, N), lambda i: (0, 0)),
                  pl.BlockSpec((1, N), lambda i: (0, 0))],
        out_specs=pl.BlockSpec((ROW_TILE, N), lambda i: (i, 0)),
        out_shape=jax.ShapeDtypeStruct((T, N), BF16),
        compiler_params=_params("parallel"),
        name="project",
    )(x, w, b)


def _outproj_ln_body(a1_ref, a2_ref, w1_ref, w2_ref, b_ref, x_ref, g_ref, beta_ref, xo_ref):
    m = (jnp.dot(a1_ref[...], w1_ref[...], preferred_element_type=F32)
         + jnp.dot(a2_ref[...], w2_ref[...], preferred_element_type=F32) + b_ref[...])
    xo_ref[...] = _layer_norm_rows(DEEPNORM_ALPHA * x_ref[...] + m, g_ref[...], beta_ref[...])


def _outproj_ln(a1, a1_col, a2, a2_col, w, b, x, g, beta):
    T = x.shape[0]
    return pl.pallas_call(
        _outproj_ln_body,
        grid=(T // ROW_TILE,),
        in_specs=[pl.BlockSpec((ROW_TILE, HALF), lambda i: (i, a1_col)),
                  pl.BlockSpec((ROW_TILE, HALF), lambda i: (i, a2_col)),
                  pl.BlockSpec((HALF, D_MODEL), lambda i: (0, 0)),
                  pl.BlockSpec((HALF, D_MODEL), lambda i: (1, 0)),
                  pl.BlockSpec((1, D_MODEL), lambda i: (0, 0)),
                  pl.BlockSpec((ROW_TILE, D_MODEL), lambda i: (i, 0)),
                  pl.BlockSpec((1, D_MODEL), lambda i: (0, 0)),
                  pl.BlockSpec((1, D_MODEL), lambda i: (0, 0))],
        out_specs=pl.BlockSpec((ROW_TILE, D_MODEL), lambda i: (i, 0)),
        out_shape=jax.ShapeDtypeStruct((T, D_MODEL), F32),
        compiler_params=_params("parallel"),
        name="outproj_ln",
    )(a1, a2, w, w, b, x, g, beta)


def _short_conv_body(x_ref, w_ref, b_ref, o_ref):
    x = x_ref[...].astype(F32)
    S = x.shape[0]
    row = lax.broadcasted_iota(I32, x.shape, 0)
    prev = jnp.where(row == 0, 0.0, pltpu.roll(x, 1, 0))
    nxt = jnp.where(row == S - 1, 0.0, pltpu.roll(x, S - 1, 0))
    y = w_ref[0:1, :] * prev + w_ref[1:2, :] * x + w_ref[2:3, :] * nxt + b_ref[...]
    o_ref[...] = y.astype(o_ref.dtype)


def _short_conv(proj, w, b, B, S):
    T = B * S
    C = HYENA_CH
    return pl.pallas_call(
        _short_conv_body,
        grid=(B, 3),
        in_specs=[pl.BlockSpec((S, C), lambda bi, j: (bi, j)),
                  pl.BlockSpec((3, C), lambda bi, j: (0, j)),
                  pl.BlockSpec((1, C), lambda bi, j: (0, j))],
        out_specs=pl.BlockSpec((S, C), lambda bi, j: (bi, j)),
        out_shape=jax.ShapeDtypeStruct((T, 3 * C), BF16),
        compiler_params=_params("parallel", "parallel"),
        name="hyena_short_conv",
    )(proj, w, b)


def _conformer_body(a_ref, g_ref, w_ref, b_ref, lg_ref, lb_ref, o_ref, pad_ref, *, row_chunk):
    S, C = a_ref.shape
    zeros = jnp.zeros((CONV_PAD, C), F32)
    pad_ref[0:CONV_PAD, :] = zeros
    pad_ref[CONV_PAD + S:CONV_PAD + S + CONV_PAD, :] = zeros
    a = a_ref[...].astype(F32)
    g = g_ref[...].astype(F32)
    pad_ref[CONV_PAD:CONV_PAD + S, :] = a * jax.nn.sigmoid(g)
    half = CONF_WIDTH // 2
    for r0 in range(0, S, row_chunk):
        acc = jnp.zeros((row_chunk, C), F32) + b_ref[...]
        for j in range(CONF_WIDTH):
            start = CONV_PAD + r0 + j - half
            acc = acc + w_ref[j:j + 1, :] * pad_ref[start:start + row_chunk, :]
        y = _layer_norm_rows(acc, lg_ref[...], lb_ref[...])
        o_ref[r0:r0 + row_chunk, :] = (y * jax.nn.sigmoid(y)).astype(o_ref.dtype)


def _conformer(proj, w, b, lg, lb, B, S):
    T = B * S
    C = CONF_CH
    return pl.pallas_call(
        functools.partial(_conformer_body, row_chunk=min(S, 256)),
        grid=(B,),
        in_specs=[pl.BlockSpec((S, C), lambda bi: (bi, 3)),
                  pl.BlockSpec((S, C), lambda bi: (bi, 4)),
                  pl.BlockSpec((CONF_WIDTH, C), lambda bi: (0, 0)),
                  pl.BlockSpec((1, C), lambda bi: (0, 0)),
                  pl.BlockSpec((1, C), lambda bi: (0, 0)),
                  pl.BlockSpec((1, C), lambda bi: (0, 0))],
        out_specs=pl.BlockSpec((S, C), lambda bi: (bi, 0)),
        out_shape=jax.ShapeDtypeStruct((T, C), BF16),
        scratch_shapes=[pltpu.VMEM((S + 2 * CONV_PAD, C), F32)],
        compiler_params=_params("parallel"),
        name="conformer_conv",
    )(proj, proj, w, b, lg, lb)


def _filter_body(feat_ref, w1_ref, b1_ref, q1_ref, w2_ref, b2_ref, q2_ref, w3_ref, t_ref, delta_ref, o_ref):
    h = jnp.sin(q1_ref[...] * (_dot3(feat_ref[...], w1_ref[...]) + b1_ref[...]))
    h = jnp.sin(q2_ref[...] * (_dot3(h, w2_ref[...]) + b2_ref[...]))
    h = _dot3(h, w3_ref[...])
    o_ref[...] = h * jnp.exp(-t_ref[...] * delta_ref[...])


def _hyena_filters(feats, w1, b1, q1, w2, b2, q2, w3, tcol, deltas):
    S = feats.shape[0]
    C = HYENA_CH
    n = 2 * HYENA_ORDER
    fd = HYENA_FILTER_DIM
    return pl.pallas_call(
        _filter_body,
        grid=(n,),
        in_specs=[pl.BlockSpec((S, LANES), lambda j: (0, 0)),
                  pl.BlockSpec((LANES, fd), lambda j: (0, 0)),
                  pl.BlockSpec((1, fd), lambda j: (0, 0)),
                  pl.BlockSpec((1, fd), lambda j: (0, 0)),
                  pl.BlockSpec((fd, fd), lambda j: (0, 0)),
                  pl.BlockSpec((1, fd), lambda j: (0, 0)),
                  pl.BlockSpec((1, fd), lambda j: (0, 0)),
                  pl.BlockSpec((fd, C), lambda j: (0, j)),
                  pl.BlockSpec((S, 1), lambda j: (0, 0)),
                  pl.BlockSpec((1, C), lambda j: (0, 0))],
        out_specs=pl.BlockSpec((S, C), lambda j: (0, j)),
        out_shape=jax.ShapeDtypeStruct((S, n * C), F32),
        compiler_params=_params("parallel"),
        name="hyena_filter_mlp",
    )(feats, w1, b1, q1, w2, b2, q2, w3, tcol, deltas)


def _spectrum_body(fc_ref, fs_ref, fwd_ref, bwd_ref, hre_ref, him_ref):
    fwd = fwd_ref[...]
    row = lax.broadcasted_iota(I32, fwd.shape, 0)
    bwd = jnp.where(row == 0, 0.0, bwd_ref[...])
    hre_ref[...] = _dot3(fc_ref[...], fwd + bwd)
    him_ref[...] = _dot3(fs_ref[...], fwd - bwd)


def _filter_spectrum(fc32, fs32, taps):
    S = fc32.shape[0]
    C = HYENA_CH
    ft = min(S, 256)
    spec = pl.BlockSpec((ft, C), lambda kb, o: (kb, o))
    return pl.pallas_call(
        _spectrum_body,
        grid=(S // ft, HYENA_ORDER),
        in_specs=[pl.BlockSpec((ft, S), lambda kb, o: (kb, 0)),
                  pl.BlockSpec((ft, S), lambda kb, o: (kb, 0)),
                  pl.BlockSpec((S, C), lambda kb, o: (0, o)),
                  pl.BlockSpec((S, C), lambda kb, o: (0, HYENA_ORDER + o))],
        out_specs=[spec, spec],
        out_shape=[jax.ShapeDtypeStruct((S, HYENA_ORDER * C), F32)] * 2,
        compiler_params=_params("parallel", "parallel"),
        name="hyena_filter_spectrum",
    )(fc32, fs32, taps, taps)


def _long_conv_body(v_ref, gate_ref, fc_ref, fs_ref, hre_ref, him_ref, gc_ref, gs_ref, skip_ref, o_ref, acc_ref):
    kb = pl.program_id(1)
    v = v_ref[...]
    zre = jnp.dot(fc_ref[...], v, preferred_element_type=F32)
    zim = jnp.dot(fs_ref[...], v, preferred_element_type=F32)
    hre = hre_ref[...]
    him = him_ref[...]
    yre = (zre * hre - zim * him).astype(BF16)
    yim = (zre * him + zim * hre).astype(BF16)
    part = (jnp.dot(gc_ref[...], yre, preferred_element_type=F32)
            + jnp.dot(gs_ref[...], yim, preferred_element_type=F32))

    @pl.when(kb == 0)
    def _():
        acc_ref[...] = part

    @pl.when(kb > 0)
    def _():
        acc_ref[...] += part

    @pl.when(kb == pl.num_programs(1) - 1)
    def _():
        y = acc_ref[...] + v.astype(F32) * skip_ref[...]
        o_ref[...] = (gate_ref[...].astype(F32) * y).astype(o_ref.dtype)


def _long_conv(v_arr, v_col, gate_arr, gate_col, tabs, hre, him, order, skip, B, S):
    fc, fs, gc, gs = tabs
    T = B * S
    C = HYENA_CH
    ft = min(S, FREQ_TILE)
    return pl.pallas_call(
        _long_conv_body,
        grid=(B, S // ft),
        in_specs=[pl.BlockSpec((S, C), lambda bi, kb: (bi, v_col)),
                  pl.BlockSpec((S, C), lambda bi, kb: (bi, gate_col)),
                  pl.BlockSpec((ft, S), lambda bi, kb: (kb, 0)),
                  pl.BlockSpec((ft, S), lambda bi, kb: (kb, 0)),
                  pl.BlockSpec((ft, C), lambda bi, kb: (kb, order)),
                  pl.BlockSpec((ft, C), lambda bi, kb: (kb, order)),
                  pl.BlockSpec((S, ft), lambda bi, kb: (0, kb)),
                  pl.BlockSpec((S, ft), lambda bi, kb: (0, kb)),
                  pl.BlockSpec((1, C), lambda bi, kb: (0, 0))],
        out_specs=pl.BlockSpec((S, C), lambda bi, kb: (bi, 0)),
        out_shape=jax.ShapeDtypeStruct((T, C), BF16),
        scratch_shapes=[pltpu.VMEM((S, C), F32)],
        compiler_params=_params("parallel", "arbitrary"),
        name="hyena_long_conv",
    )(v_arr, gate_arr, fc, fs, hre, him, gc, gs, skip[order][None, :])


def _attn_body(slope_ref, lam_ref, q_ref, k_ref, v_ref, g_ref, o_ref, *, lam_init, row_chunk):
    h = pl.program_id(1)
    qi = pl.program_id(2)
    tq = q_ref.shape[0]
    S = k_ref.shape[0]
    hw = 2 * HEAD_DIM
    k = k_ref[...]
    slope = slope_ref[h]
    kpos = lax.broadcasted_iota(I32, (1, S), 1).astype(F32) * slope
    ones_col = jnp.where(lax.broadcasted_iota(I32, (S, hw), 1) == 0, 1.0, 0.0).astype(BF16)
    v_aug = jnp.concatenate([v_ref[...], ones_col], axis=1)

    for r0 in range(0, tq, row_chunk):
        q = q_ref[r0:r0 + row_chunk, :]
        lane = lax.broadcasted_iota(I32, q.shape, 1)
        zero = jnp.zeros_like(q)
        qpos = (qi * tq + r0 + lax.broadcasted_iota(I32, (row_chunk, 1), 0)).astype(F32) * slope
        bias = lax.bitcast_convert_type(lax.bitcast_convert_type(qpos - kpos, U32) | jnp.uint32(0x80000000), F32)

        def weighted_values(qh):
            s = lax.dot_general(qh, k, _NT, preferred_element_type=F32) + bias
            e = jnp.exp((s - jnp.max(s, axis=-1, keepdims=True)).astype(BF16))
            return jnp.dot(e, v_aug, preferred_element_type=F32)

        o1 = weighted_values(jnp.where(lane < HEAD_DIM, q, zero))
        o2 = weighted_values(jnp.where(lane >= HEAD_DIM, q, zero))
        o = o1[:, :hw] * (1.0 / o1[:, hw:hw + 1]) - o2[:, :hw] * (lam_ref[0] / o2[:, hw:hw + 1])
        o = o * lax.rsqrt(jnp.mean(o * o, axis=-1, keepdims=True) + LN_EPS) * g_ref[...]
        o_ref[r0:r0 + row_chunk, :] = (o * (1.0 - lam_init)).astype(o_ref.dtype)


def _diff_attention(qkv, slopes, lam, subln_g, lam_init, B, S):
    T = B * S
    hw = 2 * HEAD_DIM
    tq = min(S, ATTN_Q_TILE)
    nq = S // tq
    smem = pl.BlockSpec(memory_space=pltpu.SMEM)
    return pl.pallas_call(
        functools.partial(_attn_body, lam_init=lam_init, row_chunk=min(tq, ATTN_ROW_CHUNK)),
        grid=(B, N_HEADS, nq),
        in_specs=[smem, smem,
                  pl.BlockSpec((tq, hw), lambda bi, h, qi: (bi * nq + qi, h)),
                  pl.BlockSpec((S, hw), lambda bi, h, qi: (bi, N_HEADS + h)),
                  pl.BlockSpec((S, hw), lambda bi, h, qi: (bi, 2 * N_HEADS + h)),
                  pl.BlockSpec((1, hw), lambda bi, h, qi: (0, 0))],
        out_specs=pl.BlockSpec((tq, hw), lambda bi, h, qi: (bi * nq + qi, h)),
        out_shape=jax.ShapeDtypeStruct((T, ATTN_W), BF16),
        compiler_params=_params("parallel", "parallel", "parallel"),
        name="diff_attention",
    )(slopes, lam, qkv, qkv, qkv, subln_g)


def _router_body(x_ref, wt_ref, b_ref, idx_ref, gate_ref, rank_ref, cnt_ref):
    E = N_EXPERTS
    tm = x_ref.shape[0]
    x_hi, x_lo = _split_bf16(x_ref[...])
    w_hi, w_lo = _split_bf16(wt_ref[...])
    nt = functools.partial(lax.dot_general, dimension_numbers=_NT, preferred_element_type=F32)
    logits = nt(w_hi, x_hi) + nt(w_lo, x_hi) + nt(w_hi, x_lo) + b_ref[...]

    eid = lax.broadcasted_iota(I32, (E, tm), 0).astype(F32)
    work = logits
    vals, idxs = [], []
    for _ in range(TOP_K):
        m = jnp.max(work, axis=0, keepdims=True)
        sel = jnp.min(jnp.where(work == m, eid, float(E)), axis=0, keepdims=True)
        vals.append(m)
        idxs.append(sel)
        work = jnp.where(eid == sel, -jnp.inf, work)
    exps = [jnp.exp(v - vals[0]) for v in vals]
    denom = exps[0] + exps[1] + exps[2] + exps[3]

    chosen = jnp.zeros((E, tm), F32)
    for sel in idxs:
        chosen = chosen + jnp.where(eid == sel, 1.0, 0.0)
    earlier = jnp.where(lax.broadcasted_iota(I32, (tm, tm), 0) < lax.broadcasted_iota(I32, (tm, tm), 1), 1.0, 0.0)
    before = jnp.dot(chosen.astype(BF16), earlier.astype(BF16), preferred_element_type=F32)
    for k in range(TOP_K):
        gate_ref[k:k + 1, :] = exps[k] / denom
        idx_ref[k:k + 1, :] = idxs[k].astype(I32)
        rank_ref[k:k + 1, :] = jnp.sum(jnp.where(eid == idxs[k], before, 0.0), axis=0, keepdims=True).astype(I32)
    cnt_ref[...] = jnp.broadcast_to(jnp.sum(chosen, axis=1, keepdims=True), cnt_ref.shape)


def _router(x, w_rt, b_r):
    T = x.shape[0]
    E = N_EXPERTS
    tm = ROW_TILE
    tok = pl.BlockSpec((TOP_K, tm), lambda i: (0, i))
    return pl.pallas_call(
        _router_body,
        grid=(T // tm,),
        in_specs=[pl.BlockSpec((tm, D_MODEL), lambda i: (i, 0)),
                  pl.BlockSpec((E, D_MODEL), lambda i: (0, 0)),
                  pl.BlockSpec((E, 1), lambda i: (0, 0))],
        out_specs=[tok, tok, tok, pl.BlockSpec((None, E, LANES), lambda i: (i, 0, 0))],
        out_shape=[jax.ShapeDtypeStruct((TOP_K, T), I32), jax.ShapeDtypeStruct((TOP_K, T), F32),
                   jax.ShapeDtypeStruct((TOP_K, T), I32), jax.ShapeDtypeStruct((T // tm, E, LANES), F32)],
        compiler_params=_params("parallel"),
        name="moe_router",
    )(x, w_rt, b_r)


def _wait_units(make_copy, slot, n):
    def body(u, carry):
        make_copy(slot, 0, 0).wait()
        return carry
    lax.fori_loop(0, n, body, 0)


def _dispatch_body(nu_ref, tail_ref, udst_ref, x_ref, idx_ref, rank_ref, loff_ref, xs_hbm, lrow_ref,
                   buf_ref, zero_ref, sem, zsem, *, n_tok_blocks):
    b = pl.program_id(0)
    slot = b % 2
    tm = x_ref.shape[0]
    R = buf_ref.shape[1]

    def unit_copy(s, u, dst_row):
        return pltpu.make_async_copy(buf_ref.at[s, pl.ds(pl.multiple_of(u * SEG_ALIGN, SEG_ALIGN), SEG_ALIGN)],
                                     xs_hbm.at[pl.ds(pl.multiple_of(dst_row, SEG_ALIGN), SEG_ALIGN)], sem.at[s])

    @pl.when(b == 0)
    def _():
        zero_ref[...] = jnp.zeros_like(zero_ref)
        for e in range(N_EXPERTS):
            fill = pltpu.make_async_copy(
                zero_ref, xs_hbm.at[pl.ds(pl.multiple_of(tail_ref[e], SEG_ALIGN), EXPERT_ROWS)], zsem)
            fill.start()
            fill.wait()

    @pl.when(b >= 2)
    def _():
        _wait_units(unit_copy, slot, nu_ref[jnp.maximum(b - 2, 0)])

    eid = lax.broadcasted_iota(I32, (N_EXPERTS, tm), 0)
    loff = loff_ref[...].astype(F32)
    rid = lax.broadcasted_iota(I32, (R, tm), 0)
    sel_t = jnp.zeros((R, tm), F32)
    for k in range(TOP_K):
        base = jnp.sum(jnp.where(eid == idx_ref[k:k + 1, :], loff, 0.0), axis=0, keepdims=True).astype(I32)
        row = base + rank_ref[k:k + 1, :]
        lrow_ref[k:k + 1, :] = row
        sel_t = jnp.where(rid == row, 1.0, sel_t)
    sel_t = sel_t.astype(BF16)
    xb = x_ref[...].astype(BF16)
    lo = jnp.dot(sel_t, xb[:, :HALF], preferred_element_type=F32)
    hi = jnp.dot(sel_t, xb[:, HALF:], preferred_element_type=F32)
    packed = (lax.bitcast_convert_type(hi, U32) & jnp.uint32(0xFFFF0000)) | (lax.bitcast_convert_type(lo, U32) >> 16)
    buf_ref[slot] = packed

    def issue(u, carry):
        unit_copy(slot, u, udst_ref[0, u]).start()
        return carry
    lax.fori_loop(0, nu_ref[b], issue, 0)

    @pl.when(b == n_tok_blocks - 1)
    def _():
        if n_tok_blocks >= 2:
            _wait_units(unit_copy, 1 - slot, nu_ref[jnp.maximum(b - 1, 0)])
        _wait_units(unit_copy, slot, nu_ref[b])


def _dispatch(x, idx, rank, loff, unit_dst, n_units, tail_start, n_rows):
    T = x.shape[0]
    tm = ROW_TILE
    nb = T // tm
    R = TOP_K * tm + N_EXPERTS * SEG_ALIGN
    nu_max = R // SEG_ALIGN
    tok = pl.BlockSpec((TOP_K, tm), lambda b, nu, tl: (0, b))
    return pl.pallas_call(
        functools.partial(_dispatch_body, n_tok_blocks=nb),
        grid_spec=pltpu.PrefetchScalarGridSpec(
            num_scalar_prefetch=2,
            grid=(nb,),
            in_specs=[pl.BlockSpec((None, 1, nu_max), lambda b, nu, tl: (b, 0, 0), memory_space=pltpu.SMEM),
                      pl.BlockSpec((tm, D_MODEL), lambda b, nu, tl: (b, 0)),
                      tok, tok,
                      pl.BlockSpec((None, N_EXPERTS, 1), lambda b, nu, tl: (b, 0, 0))],
            out_specs=[pl.BlockSpec(memory_space=pl.ANY), tok],
            scratch_shapes=[pltpu.VMEM((2, R, HALF), U32), pltpu.VMEM((EXPERT_ROWS, HALF), U32),
                            pltpu.SemaphoreType.DMA((2,)), pltpu.SemaphoreType.DMA(())]),
        out_shape=[jax.ShapeDtypeStruct((n_rows, HALF), U32), jax.ShapeDtypeStruct((TOP_K, T), I32)],
        compiler_params=_params("arbitrary"),
        name="moe_dispatch",
    )(n_units, tail_start, unit_dst, x, idx, rank, loff)


def _ffn_body(be_ref, nu_ref, xs_ref, w1_ref, b1_ref, w2_ref, b2_ref, ys_ref, w1b_ref, w2b_ref):
    i = pl.program_id(0)
    used = i < nu_ref[0]
    fresh = jnp.logical_or(i == 0, be_ref[i] != be_ref[jnp.maximum(i - 1, 0)])

    @pl.when(jnp.logical_and(used, fresh))
    def _():
        w1b_ref[...] = w1_ref[...].astype(BF16)
        w2b_ref[...] = w2_ref[...].astype(BF16)

    @pl.when(used)
    def _():
        lo, hi = _unpack_halves(xs_ref[...])
        h = (jnp.dot(lo.astype(BF16), w1b_ref[0:HALF, :], preferred_element_type=F32)
             + jnp.dot(hi.astype(BF16), w1b_ref[HALF:D_MODEL, :], preferred_element_type=F32) + b1_ref[...])
        hg = jnp.minimum(h[:, :D_FF], SWIGLU_LIMIT)
        hu = jnp.clip(h[:, D_FF:], -SWIGLU_LIMIT, SWIGLU_LIMIT)
        act = (hu + 1.0) * (hg * jax.nn.sigmoid(hg * SWIGLU_ALPHA))
        y = jnp.dot(act.astype(BF16), w2b_ref[...], preferred_element_type=F32) + b2_ref[...]
        ys_ref[...] = _pack_halves(y)


def _expert_ffn(xs, blk_expert, n_used, w1, b1, w2, b2, layer, n_blocks):
    rows = pl.BlockSpec((EXPERT_ROWS, HALF), lambda i, be, nu: (jnp.minimum(i, nu[0] - 1), 0))
    return pl.pallas_call(
        _ffn_body,
        grid_spec=pltpu.PrefetchScalarGridSpec(
            num_scalar_prefetch=2,
            grid=(n_blocks,),
            in_specs=[rows,
                      pl.BlockSpec((None, None, D_MODEL, 2 * D_FF), lambda i, be, nu: (layer, be[i], 0, 0)),
                      pl.BlockSpec((None, None, 1, 2 * D_FF), lambda i, be, nu: (layer, be[i], 0, 0)),
                      pl.BlockSpec((None, None, D_FF, D_MODEL), lambda i, be, nu: (layer, be[i], 0, 0)),
                      pl.BlockSpec((None, None, 1, D_MODEL), lambda i, be, nu: (layer, be[i], 0, 0))],
            out_specs=rows,
            scratch_shapes=[pltpu.VMEM((D_MODEL, 2 * D_FF), BF16), pltpu.VMEM((D_FF, D_MODEL), BF16)]),
        out_shape=jax.ShapeDtypeStruct((xs.shape[0], HALF), U32),
        compiler_params=_params("arbitrary"),
        name="moe_expert_ffn",
    )(blk_expert, n_used, xs, w1, b1, w2, b2)


def _combine_body(nu_ref, usrc_ref, usrc_next_ref, lrow_ref, gate_ref, x_ref, g_ref, beta_ref, ys_hbm, xo_ref,
                  buf_ref, sem, *, n_tok_blocks):
    b = pl.program_id(0)
    slot = b % 2
    tm = x_ref.shape[0]
    R = buf_ref.shape[1]

    def unit_copy(s, u, src_row):
        return pltpu.make_async_copy(ys_hbm.at[pl.ds(pl.multiple_of(src_row, SEG_ALIGN), SEG_ALIGN)],
                                     buf_ref.at[s, pl.ds(pl.multiple_of(u * SEG_ALIGN, SEG_ALIGN), SEG_ALIGN)], sem.at[s])

    def fetch(s, table_ref, n):
        def body(u, carry):
            unit_copy(s, u, table_ref[0, u]).start()
            return carry
        lax.fori_loop(0, n, body, 0)

    @pl.when(b == 0)
    def _():
        buf_ref[...] = jnp.zeros_like(buf_ref)
        fetch(0, usrc_ref, nu_ref[0])

    @pl.when(b + 1 < n_tok_blocks)
    def _():
        fetch(1 - slot, usrc_next_ref, nu_ref[jnp.minimum(b + 1, n_tok_blocks - 1)])

    _wait_units(unit_copy, slot, nu_ref[b])

    lo, hi = _unpack_halves(buf_ref[slot])
    cid = lax.broadcasted_iota(I32, (tm, R), 1)
    lrow = lrow_ref[...]
    gates = gate_ref[...]
    sel = jnp.zeros((tm, R), F32)
    for k in range(TOP_K):
        sel = jnp.where(cid == lrow[:, k:k + 1], gates[:, k:k + 1], sel)
    sel = sel.astype(BF16)
    f = jnp.concatenate([jnp.dot(sel, lo.astype(BF16), preferred_element_type=F32),
                         jnp.dot(sel, hi.astype(BF16), preferred_element_type=F32)], axis=1)
    xo_ref[...] = _layer_norm_rows(DEEPNORM_ALPHA * x_ref[...] + f, g_ref[...], beta_ref[...])


def _combine_ln(lrow_t, gate_t, x, g, beta, ys, unit_src, n_units):
    T = x.shape[0]
    tm = ROW_TILE
    nb = T // tm
    R = TOP_K * tm + N_EXPERTS * SEG_ALIGN
    nu_max = R // SEG_ALIGN
    return pl.pallas_call(
        functools.partial(_combine_body, n_tok_blocks=nb),
        grid_spec=pltpu.PrefetchScalarGridSpec(
            num_scalar_prefetch=1,
            grid=(nb,),
            in_specs=[pl.BlockSpec((None, 1, nu_max), lambda b, nu: (b, 0, 0), memory_space=pltpu.SMEM),
                      pl.BlockSpec((None, 1, nu_max), lambda b, nu: (jnp.minimum(b + 1, nb - 1), 0, 0),
                                   memory_space=pltpu.SMEM),
                      pl.BlockSpec((tm, TOP_K), lambda b, nu: (b, 0)),
                      pl.BlockSpec((tm, TOP_K), lambda b, nu: (b, 0)),
                      pl.BlockSpec((tm, D_MODEL), lambda b, nu: (b, 0)),
                      pl.BlockSpec((1, D_MODEL), lambda b, nu: (0, 0)),
                      pl.BlockSpec((1, D_MODEL), lambda b, nu: (0, 0)),
                      pl.BlockSpec(memory_space=pl.ANY)],
            out_specs=pl.BlockSpec((tm, D_MODEL), lambda b, nu: (b, 0)),
            scratch_shapes=[pltpu.VMEM((2, R, HALF), U32), pltpu.SemaphoreType.DMA((2,))]),
        out_shape=jax.ShapeDtypeStruct((T, D_MODEL), F32),
        compiler_params=_params("arbitrary"),
        name="moe_combine_ln",
    )(n_units, unit_src, unit_src, lrow_t, gate_t, x, g, beta, ys)


def _dft_tables(S):
    n2 = 4 * S
    k = jnp.arange(S, dtype=I32)
    m = ((2 * k[:, None] + 1) * k[None, :]) % n2
    ang = m.astype(F32) * F32(2.0 * math.pi / n2)
    fc32, fs32 = jnp.cos(ang), -jnp.sin(ang)
    scale = F32(1.0 / S)
    gc, gs = (fc32.T * scale).astype(BF16), (fs32.T * scale).astype(BF16)
    return fc32, fs32, (fc32.astype(BF16), fs32.astype(BF16), gc, gs)


def _hyena_positional(S):
    pos = jnp.arange(S, dtype=F32)
    t = jnp.linspace(0.0, 1.0, S, dtype=F32)[:, None]
    bands = (HYENA_EMB_DIM - 1) // 2
    f = jnp.linspace(1e-4, bands - 1, bands, dtype=F32)
    ang = (2.0 * math.pi / S) * pos[:, None] * f[None, :]
    feats = jnp.concatenate([t, jnp.cos(ang), -jnp.sin(ang)], axis=-1)
    feats = jnp.pad(feats, ((0, 0), (0, LANES - HYENA_EMB_DIM)))
    max_decay = math.log(HYENA_DECAY_TARGET) / HYENA_SHORT_DECAY_PCT
    min_decay = math.log(HYENA_DECAY_TARGET) / HYENA_LONG_DECAY_PCT
    deltas = jnp.abs(jnp.linspace(min_decay, max_decay, HYENA_CH, dtype=F32))[None, :]
    return feats, t, deltas


def _alibi_slopes():
    return jnp.asarray(np.array([2.0 ** (-8.0 * (i + 1) / N_HEADS) for i in range(N_HEADS)], dtype=np.float32))


def _even_mixer(x, xshape, tabs, w_in, b_in, short_w, short_b, f1_w, f1_b, f1_freq, f2_w, f2_b, f2_freq, f3_w,
                skip, dw_w, dw_b, cln_g, cln_b, w_out, b_out, ln_g, ln_b):
    B, S = xshape
    fc32, fs32, tabs16 = tabs
    proj = _project(x, w_in.astype(BF16), b_in[None, :])
    hy = _short_conv(proj, short_w, short_b[None, :], B, S)
    u = _conformer(proj, dw_w, dw_b[None, :], cln_g[None, :], cln_b[None, :], B, S)
    feats, tcol, deltas = _hyena_positional(S)
    f1_wp = jnp.pad(f1_w, ((0, LANES - HYENA_EMB_DIM), (0, 0)))
    taps = _hyena_filters(feats, f1_wp, f1_b[None, :], f1_freq[None, :], f2_w, f2_b[None, :], f2_freq[None, :],
                          f3_w, tcol, deltas)
    hre, him = _filter_spectrum(fc32, fs32, taps)
    z = _long_conv(hy, 2, hy, 0, tabs16, hre, him, 0, skip, B, S)
    z = _long_conv(z, 0, hy, 1, tabs16, hre, him, 1, skip, B, S)
    return _outproj_ln(z, 0, u, 0, w_out.astype(BF16), b_out[None, :], x, ln_g[None, :], ln_b[None, :])


def _odd_mixer(x, xshape, layer_idx, w_qkv, lq1, lk1, lq2, lk2, subln_g, w_out, ln_g, ln_b):
    B, S = xshape
    lam_init = 0.8 - 0.6 * math.exp(-0.3 * layer_idx)
    lam = (jnp.exp(jnp.sum(lq1 * lk1)) - jnp.exp(jnp.sum(lq2 * lk2)) + lam_init).reshape(1)
    q_scale = jnp.concatenate([jnp.full((ATTN_W,), HEAD_DIM ** -0.5, F32), jnp.ones((2 * ATTN_W,), F32)])
    w = (w_qkv * q_scale).astype(BF16)
    qkv = _project(x, w, jnp.zeros((1, 3 * ATTN_W), F32))
    o = _diff_attention(qkv, _alibi_slopes(), lam, subln_g[None, :], lam_init, B, S)
    return _outproj_ln(o, 0, o, 1, w_out.astype(BF16), jnp.zeros((1, D_MODEL), F32), x, ln_g[None, :], ln_b[None, :])


def _round_up(a, m):
    return (a + m - 1) // m * m


def _routing_tables(cnt_blocks, n_ffn_blocks):
    E = N_EXPERTS
    cnt8 = _round_up(cnt_blocks[:, :, 0].astype(I32), SEG_ALIGN)
    seg_end = jnp.cumsum(cnt8, axis=1)
    loff = seg_end - cnt8
    n_units = seg_end[:, -1] // SEG_ALIGN
    tot8 = jnp.sum(cnt8, axis=0)
    group = _round_up(tot8, EXPERT_ROWS)
    group_end = jnp.cumsum(group)
    group_start = group_end - group
    goff = group_start[None, :] + jnp.cumsum(cnt8, axis=0) - cnt8
    nu_max = (TOP_K * ROW_TILE + E * SEG_ALIGN) // SEG_ALIGN
    u_row = jnp.arange(nu_max, dtype=I32) * SEG_ALIGN
    seg_of_unit = jnp.sum((seg_end[:, None, :] <= u_row[None, :, None]).astype(I32), axis=2)
    onehot = seg_of_unit[:, :, None] == jnp.arange(E, dtype=I32)[None, None, :]
    unit_row = jnp.sum(jnp.where(onehot, (goff - loff)[:, None, :], 0), axis=2) + u_row[None, :]
    starts = jnp.arange(n_ffn_blocks, dtype=I32) * EXPERT_ROWS
    blk_expert = jnp.minimum(jnp.sum((group_end[None, :] <= starts[:, None]).astype(I32), axis=1), E - 1)
    n_used = group_end[-1:] // EXPERT_ROWS
    tail_start = group_start + tot8
    return loff[:, :, None], unit_row[:, None, :], n_units, blk_expert, n_used, tail_start


def _moe_layer(x, layer, w_r, b_r, w1, b1, w2, b2, ln_g, ln_b):
    T = x.shape[0]
    nb = T // ROW_TILE
    n_rows = _round_up(T * TOP_K + nb * N_EXPERTS * (SEG_ALIGN - 1), EXPERT_ROWS) + N_EXPERTS * EXPERT_ROWS
    n_ffn_blocks = n_rows // EXPERT_ROWS
    idx, gate, rank, cnt = _router(x, w_r.T, b_r[:, None])
    loff, unit_row, n_units, blk_expert, n_used, tail_start = _routing_tables(cnt, n_ffn_blocks)
    xs, lrow = _dispatch(x, idx, rank, loff, unit_row, n_units, tail_start, n_rows + EXPERT_ROWS)
    ys = _expert_ffn(xs, blk_expert, n_used, w1, b1[:, :, None, :], w2, b2[:, :, None, :], layer, n_ffn_blocks)
    return _combine_ln(lrow.T, gate.T, x, ln_g[None, :], ln_b[None, :], ys, unit_row, n_units)


def kernel(x, hy_cf_w_in, hy_cf_b_in, hy_short_w, hy_short_b, hy_f1_w, hy_f1_b, hy_f1_freq, hy_f2_w, hy_f2_b, hy_f2_freq, hy_f3_w, hy_skip, cf_dw_w, cf_dw_b, cf_ln_g, cf_ln_b, even_w_out, even_b_out, attn_w_qkv, attn_lq1, attn_lk1, attn_lq2, attn_lk2, attn_subln_g, attn_w_out, ln1_g, ln1_b, ln2_g, ln2_b, moe_w_r, moe_b_r, moe_w1, moe_b1, moe_w2, moe_b2):
    B, S, D = x.shape
    assert D == D_MODEL and (B * S) % ROW_TILE == 0 and S % LANES == 0
    depth = ln1_g.shape[0]
    xf = x.reshape(B * S, D)
    tabs = _dft_tables(S)
    for i in range(depth):
        j = i // 2
        if i % 2 == 0:
            xf = _even_mixer(xf, (B, S), tabs, hy_cf_w_in[j], hy_cf_b_in[j], hy_short_w[j], hy_short_b[j],
                                 hy_f1_w[j], hy_f1_b[j], hy_f1_freq[j], hy_f2_w[j], hy_f2_b[j], hy_f2_freq[j],
                                 hy_f3_w[j], hy_skip[j], cf_dw_w[j], cf_dw_b[j], cf_ln_g[j], cf_ln_b[j],
                                 even_w_out[j], even_b_out[j], ln1_g[i], ln1_b[i])
        else:
            xf = _odd_mixer(xf, (B, S), i, attn_w_qkv[j], attn_lq1[j], attn_lk1[j], attn_lq2[j], attn_lk2[j],
                            attn_subln_g[j], attn_w_out[j], ln1_g[i], ln1_b[i])
        xf = _moe_layer(xf, i, moe_w_r[i], moe_b_r[i], moe_w1, moe_b1, moe_w2, moe_b2, ln2_g[i], ln2_b[i])
    return xf.reshape(B, S, D)
```

```python
import functools
import math

import jax
import jax.numpy as jnp
import numpy as np
from jax import lax
from jax.experimental import pallas as pl
from jax.experimental.pallas import tpu as pltpu

F32 = jnp.float32
BF16 = jnp.bfloat16
U32 = jnp.uint32
I32 = jnp.int32

D_MODEL = 1024
HALF = D_MODEL // 2
DEPTH = 4
HYENA_CH = D_MODEL // 2
CONF_CH = D_MODEL // 2
HYENA_ORDER = 2
HYENA_EMB_DIM = 33
HYENA_FILTER_DIM = 64
HYENA_SHORT_DECAY_PCT = 0.3
HYENA_LONG_DECAY_PCT = 1.5
HYENA_DECAY_TARGET = 1e-2
CONF_WIDTH = 31
EVEN_IN = 3 * HYENA_CH + 2 * CONF_CH
N_HEADS = 8
HEAD_DIM = 64
ATTN_W = N_HEADS * 2 * HEAD_DIM
N_EXPERTS = 32
TOP_K = 4
D_FF = D_MODEL
SWIGLU_LIMIT = 7.0
SWIGLU_ALPHA = 1.702
DEEPNORM_ALPHA = (2 * DEPTH) ** 0.25
LN_EPS = 1e-5

LANES = 128
VMEM_LIMIT_BYTES = 56 * 1024 * 1024
ROW_TILE = 512
EXPERT_ROWS = 512
SEG_ALIGN = 8
COPY_ROWS = (64, 32, 16, 8)
FREQ_TILE = 512
ATTN_Q_TILE = 1024
ATTN_ROW_CHUNK = 128
CONV_PAD = 16

_NT = (((1,), (1,)), ((), ()))


def _params(*sem):
    return pltpu.CompilerParams(dimension_semantics=sem, vmem_limit_bytes=VMEM_LIMIT_BYTES)


def _split_bf16(a):
    hi = a.astype(BF16)
    lo = (a - hi.astype(F32)).astype(BF16)
    return hi, lo


def _dot3(a, b):
    a_hi, a_lo = _split_bf16(a)
    b_hi, b_lo = _split_bf16(b)
    d = functools.partial(jnp.dot, preferred_element_type=F32)
    return d(a_hi, b_hi) + d(a_hi, b_lo) + d(a_lo, b_hi)


def _layer_norm_rows(y, g, b):
    mu = jnp.mean(y, axis=-1, keepdims=True)
    yc = y - mu
    var = jnp.mean(yc * yc, axis=-1, keepdims=True)
    return yc * lax.rsqrt(var + LN_EPS) * g + b


def _pack_halves(y):
    lo = lax.bitcast_convert_type(y[:, :HALF].astype(BF16).astype(F32), U32)
    hi = lax.bitcast_convert_type(y[:, HALF:].astype(BF16).astype(F32), U32)
    return hi | (lo >> 16)


def _unpack_halves(p):
    lo = lax.bitcast_convert_type(p << 16, F32)
    hi = lax.bitcast_convert_type(p & jnp.uint32(0xFFFF0000), F32)
    return lo, hi


def _proj_body(x_ref, w_ref, b_ref, o_ref, *, col_chunk):
    x = x_ref[...].astype(BF16)
    for j in range(0, o_ref.shape[1], col_chunk):
        acc = jnp.dot(x, w_ref[:, j:j + col_chunk], preferred_element_type=F32)
        o_ref[:, j:j + col_chunk] = (acc + b_ref[:, j:j + col_chunk]).astype(o_ref.dtype)


def _project(x, w, b):
    T, K = x.shape
    N = w.shape[1]
    return pl.pallas_call(
        functools.partial(_proj_body, col_chunk=512),
        grid=(T // ROW_TILE,),
        in_specs=[pl.BlockSpec((ROW_TILE, K), lambda i: (i, 0)),
                  pl.BlockSpec((K, N), lambda i: (0, 0)),
                  pl.BlockSpec((1, N), lambda i: (0, 0))],
        out_specs=pl.BlockSpec((ROW_TILE, N), lambda i: (i, 0)),
        out_shape=jax.ShapeDtypeStruct((T, N), BF16),
        compiler_params=_params("parallel"),
        name="project",
    )(x, w, b)


def _outproj_ln_body(a1_ref, a2_ref, w1_ref, w2_ref, b_ref, x_ref, g_ref, beta_ref, xo_ref):
    m = (jnp.dot(a1_ref[...], w1_ref[...], preferred_element_type=F32)
         + jnp.dot(a2_ref[...], w2_ref[...], preferred_element_type=F32) + b_ref[...])
    xo_ref[...] = _layer_norm_rows(DEEPNORM_ALPHA * x_ref[...] + m, g_ref[...], beta_ref[...])


def _outproj_ln(a1, a1_col, a2, a2_col, w, b, x, g, beta):
    T = x.shape[0]
    return pl.pallas_call(
        _outproj_ln_body,
        grid=(T // ROW_TILE,),
        in_specs=[pl.BlockSpec((ROW_TILE, HALF), lambda i: (i, a1_col)),
                  pl.BlockSpec((ROW_TILE, HALF), lambda i: (i, a2_col)),
                  pl.BlockSpec((HALF, D_MODEL), lambda i: (0, 0)),
                  pl.BlockSpec((HALF, D_MODEL), lambda i: (1, 0)),
                  pl.BlockSpec((1, D_MODEL), lambda i: (0, 0)),
                  pl.BlockSpec((ROW_TILE, D_MODEL), lambda i: (i, 0)),
                  pl.BlockSpec((1, D_MODEL), lambda i: (0, 0)),
                  pl.BlockSpec((1, D_MODEL), lambda i: (0, 0))],
        out_specs=pl.BlockSpec((ROW_TILE, D_MODEL), lambda i: (i, 0)),
        out_shape=jax.ShapeDtypeStruct((T, D_MODEL), F32),
        compiler_params=_params("parallel"),
        name="outproj_ln",
    )(a1, a2, w, w, b, x, g, beta)


def _short_conv_body(x_ref, w_ref, b_ref, o_ref):
    x = x_ref[...].astype(F32)
    S = x.shape[0]
    row = lax.broadcasted_iota(I32, x.shape, 0)
    prev = jnp.where(row == 0, 0.0, pltpu.roll(x, 1, 0))
    nxt = jnp.where(row == S - 1, 0.0, pltpu.roll(x, S - 1, 0))
    y = w_ref[0:1, :] * prev + w_ref[1:2, :] * x + w_ref[2:3, :] * nxt + b_ref[...]
    o_ref[...] = y.astype(o_ref.dtype)


def _short_conv(proj, w, b, B, S):
    T = B * S
    C = HYENA_CH
    return pl.pallas_call(
        _short_conv_body,
        grid=(B, 3),
        in_specs=[pl.BlockSpec((S, C), lambda bi, j: (bi, j)),
                  pl.BlockSpec((3, C), lambda bi, j: (0, j)),
                  pl.BlockSpec((1, C), lambda bi, j: (0, j))],
        out_specs=pl.BlockSpec((S, C), lambda bi, j: (bi, j)),
        out_shape=jax.ShapeDtypeStruct((T, 3 * C), BF16),
        compiler_params=_params("parallel", "parallel"),
        name="hyena_short_conv",
    )(proj, w, b)


def _conformer_body(a_ref, g_ref, w_ref, b_ref, lg_ref, lb_ref, o_ref, pad_ref, *, row_chunk):
    S, C = a_ref.shape
    zeros = jnp.zeros((CONV_PAD, C), F32)
    pad_ref[0:CONV_PAD, :] = zeros
    pad_ref[CONV_PAD + S:CONV_PAD + S + CONV_PAD, :] = zeros
    a = a_ref[...].astype(F32)
    g = g_ref[...].astype(F32)
    pad_ref[CONV_PAD:CONV_PAD + S, :] = a * jax.nn.sigmoid(g)
    half = CONF_WIDTH // 2
    for r0 in range(0, S, row_chunk):
        acc = jnp.zeros((row_chunk, C), F32) + b_ref[...]
        for j in range(CONF_WIDTH):
            start = CONV_PAD + r0 + j - half
            acc = acc + w_ref[j:j + 1, :] * pad_ref[start:start + row_chunk, :]
        y = _layer_norm_rows(acc, lg_ref[...], lb_ref[...])
        o_ref[r0:r0 + row_chunk, :] = (y * jax.nn.sigmoid(y)).astype(o_ref.dtype)


def _conformer(proj, w, b, lg, lb, B, S):
    T = B * S
    C = CONF_CH
    return pl.pallas_call(
        functools.partial(_conformer_body, row_chunk=min(S, 256)),
        grid=(B,),
        in_specs=[pl.BlockSpec((S, C), lambda bi: (bi, 3)),
                  pl.BlockSpec((S, C), lambda bi: (bi, 4)),
                  pl.BlockSpec((CONF_WIDTH, C), lambda bi: (0, 0)),
                  pl.BlockSpec((1, C), lambda bi: (0, 0)),
                  pl.BlockSpec((1, C), lambda bi: (0, 0)),
                  pl.BlockSpec((1, C), lambda bi: (0, 0))],
        out_specs=pl.BlockSpec((S, C), lambda bi: (bi, 0)),
        out_shape=jax.ShapeDtypeStruct((T, C), BF16),
        scratch_shapes=[pltpu.VMEM((S + 2 * CONV_PAD, C), F32)],
        compiler_params=_params("parallel"),
        name="conformer_conv",
    )(proj, proj, w, b, lg, lb)


def _filter_body(feat_ref, w1_ref, b1_ref, q1_ref, w2_ref, b2_ref, q2_ref, w3_ref, t_ref, delta_ref, o_ref):
    h = jnp.sin(q1_ref[...] * (_dot3(feat_ref[...], w1_ref[...]) + b1_ref[...]))
    h = jnp.sin(q2_ref[...] * (_dot3(h, w2_ref[...]) + b2_ref[...]))
    h = _dot3(h, w3_ref[...])
    o_ref[...] = h * jnp.exp(-t_ref[...] * delta_ref[...])


def _hyena_filters(feats, w1, b1, q1, w2, b2, q2, w3, tcol, deltas):
    S = feats.shape[0]
    C = HYENA_CH
    n = 2 * HYENA_ORDER
    fd = HYENA_FILTER_DIM
    return pl.pallas_call(
        _filter_body,
        grid=(n,),
        in_specs=[pl.BlockSpec((S, LANES), lambda j: (0, 0)),
                  pl.BlockSpec((LANES, fd), lambda j: (0, 0)),
                  pl.BlockSpec((1, fd), lambda j: (0, 0)),
                  pl.BlockSpec((1, fd), lambda j: (0, 0)),
                  pl.BlockSpec((fd, fd), lambda j: (0, 0)),
                  pl.BlockSpec((1, fd), lambda j: (0, 0)),
                  pl.BlockSpec((1, fd), lambda j: (0, 0)),
                  pl.BlockSpec((fd, C), lambda j: (0, j)),
                  pl.BlockSpec((S, 1), lambda j: (0, 0)),
                  pl.BlockSpec((1, C), lambda j: (0, 0))],
        out_specs=pl.BlockSpec((S, C), lambda j: (0, j)),
        out_shape=jax.ShapeDtypeStruct((S, n * C), F32),
        compiler_params=_params("parallel"),
        name="hyena_filter_mlp",
    )(feats, w1, b1, q1, w2, b2, q2, w3, tcol, deltas)


def _spectrum_body(fc_ref, fs_ref, fwd_ref, bwd_ref, hre_ref, him_ref):
    fwd = fwd_ref[...]
    row = lax.broadcasted_iota(I32, fwd.shape, 0)
    bwd = jnp.where(row == 0, 0.0, bwd_ref[...])
    hre_ref[...] = _dot3(fc_ref[...], fwd + bwd)
    him_ref[...] = _dot3(fs_ref[...], fwd - bwd)


def _filter_spectrum(fc32, fs32, taps):
    S = fc32.shape[0]
    C = HYENA_CH
    ft = min(S, 256)
    spec = pl.BlockSpec((ft, C), lambda kb, o: (kb, o))
    return pl.pallas_call(
        _spectrum_body,
        grid=(S // ft, HYENA_ORDER),
        in_specs=[pl.BlockSpec((ft, S), lambda kb, o: (kb, 0)),
                  pl.BlockSpec((ft, S), lambda kb, o: (kb, 0)),
                  pl.BlockSpec((S, C), lambda kb, o: (0, o)),
                  pl.BlockSpec((S, C), lambda kb, o: (0, HYENA_ORDER + o))],
        out_specs=[spec, spec],
        out_shape=[jax.ShapeDtypeStruct((S, HYENA_ORDER * C), F32)] * 2,
        compiler_params=_params("parallel", "parallel"),
        name="hyena_filter_spectrum",
    )(fc32, fs32, taps, taps)


def _long_conv_body(v_ref, gate_ref, fc_ref, fs_ref, hre_ref, him_ref, gc_ref, gs_ref, skip_ref, o_ref, acc_ref):
    kb = pl.program_id(1)
    v = v_ref[...]
    zre = jnp.dot(fc_ref[...], v, preferred_element_type=F32)
    zim = jnp.dot(fs_ref[...], v, preferred_element_type=F32)
    hre = hre_ref[...]
    him = him_ref[...]
    yre = (zre * hre - zim * him).astype(BF16)
    yim = (zre * him + zim * hre).astype(BF16)
    part = (jnp.dot(gc_ref[...], yre, preferred_element_type=F32)
            + jnp.dot(gs_ref[...], yim, preferred_element_type=F32))

    @pl.when(kb == 0)
    def _():
        acc_ref[...] = part

    @pl.when(kb > 0)
    def _():
        acc_ref[...] += part

    @pl.when(kb == pl.num_programs(1) - 1)
    def _():
        y = acc_ref[...] + v.astype(F32) * skip_ref[...]
        o_ref[...] = (gate_ref[...].astype(F32) * y).astype(o_ref.dtype)


def _long_conv(v_arr, v_col, gate_arr, gate_col, tabs, hre, him, order, skip, B, S):
    fc, fs, gc, gs = tabs
    T = B * S
    C = HYENA_CH
    ft = min(S, FREQ_TILE)
    return pl.pallas_call(
        _long_conv_body,
        grid=(B, S // ft),
        in_specs=[pl.BlockSpec((S, C), lambda bi, kb: (bi, v_col)),
                  pl.BlockSpec((S, C), lambda bi, kb: (bi, gate_col)),
                  pl.BlockSpec((ft, S), lambda bi, kb: (kb, 0)),
                  pl.BlockSpec((ft, S), lambda bi, kb: (kb, 0)),
                  pl.BlockSpec((ft, C), lambda bi, kb: (kb, order)),
                  pl.BlockSpec((ft, C), lambda bi, kb: (kb, order)),
                  pl.BlockSpec((S, ft), lambda bi, kb: (0, kb)),
                  pl.BlockSpec((S, ft), lambda bi, kb: (0, kb)),
                  pl.BlockSpec((1, C), lambda bi, kb: (0, 0))],
        out_specs=pl.BlockSpec((S, C), lambda bi, kb: (bi, 0)),
        out_shape=jax.ShapeDtypeStruct((T, C), BF16),
        scratch_shapes=[pltpu.VMEM((S, C), F32)],
        compiler_params=_params("parallel", "arbitrary"),
        name="hyena_long_conv",
    )(v_arr, gate_arr, fc, fs, hre, him, gc, gs, skip[order][None, :])


def _attn_body(slope_ref, lam_ref, q_ref, k_ref, v_ref, g_ref, o_ref, vaug_ref, *, lam_init, row_chunk):
    h = pl.program_id(1)
    qi = pl.program_id(2)
    tq = q_ref.shape[0]
    S = k_ref.shape[0]
    hw = 2 * HEAD_DIM
    k = k_ref[...]
    slope = slope_ref[h]
    kpos = lax.broadcasted_iota(I32, (1, S), 1).astype(F32) * slope

    @pl.when(qi == 0)
    def _():
        vaug_ref[:, :hw] = v_ref[...]
        vaug_ref[:, hw:] = jnp.where(lax.broadcasted_iota(I32, (S, hw), 1) == 0, 1.0, 0.0).astype(BF16)

    v_aug = vaug_ref[...]

    for r0 in range(0, tq, row_chunk):
        q = q_ref[r0:r0 + row_chunk, :]
        lane = lax.broadcasted_iota(I32, q.shape, 1)
        zero = jnp.zeros_like(q)
        qpos = (qi * tq + r0 + lax.broadcasted_iota(I32, (row_chunk, 1), 0)).astype(F32) * slope
        bias = lax.bitcast_convert_type(lax.bitcast_convert_type(qpos - kpos, U32) | jnp.uint32(0x80000000), F32)

        def weighted_values(qh):
            s = lax.dot_general(qh, k, _NT, preferred_element_type=F32) + bias
            e = jnp.exp((s - jnp.max(s, axis=-1, keepdims=True)).astype(BF16))
            return jnp.dot(e, v_aug, preferred_element_type=F32)

        o1 = weighted_values(jnp.where(lane < HEAD_DIM, q, zero))
        o2 = weighted_values(jnp.where(lane >= HEAD_DIM, q, zero))
        o = o1[:, :hw] * (1.0 / o1[:, hw:hw + 1]) - o2[:, :hw] * (lam_ref[0] / o2[:, hw:hw + 1])
        o = o * lax.rsqrt(jnp.mean(o * o, axis=-1, keepdims=True) + LN_EPS) * g_ref[...]
        o_ref[r0:r0 + row_chunk, :] = (o * (1.0 - lam_init)).astype(o_ref.dtype)


def _diff_attention(qkv, slopes, lam, subln_g, lam_init, B, S):
    T = B * S
    hw = 2 * HEAD_DIM
    tq = min(S, ATTN_Q_TILE)
    nq = S // tq
    smem = pl.BlockSpec(memory_space=pltpu.SMEM)
    return pl.pallas_call(
        functools.partial(_attn_body, lam_init=lam_init, row_chunk=min(tq, ATTN_ROW_CHUNK)),
        grid=(B, N_HEADS, nq),
        in_specs=[smem, smem,
                  pl.BlockSpec((tq, hw), lambda bi, h, qi: (bi * nq + qi, h)),
                  pl.BlockSpec((S, hw), lambda bi, h, qi: (bi, N_HEADS + h)),
                  pl.BlockSpec((S, hw), lambda bi, h, qi: (bi, 2 * N_HEADS + h)),
                  pl.BlockSpec((1, hw), lambda bi, h, qi: (0, 0))],
        out_specs=pl.BlockSpec((tq, hw), lambda bi, h, qi: (bi * nq + qi, h)),
        out_shape=jax.ShapeDtypeStruct((T, ATTN_W), BF16),
        scratch_shapes=[pltpu.VMEM((S, 2 * hw), BF16)],
        compiler_params=_params("parallel", "parallel", "arbitrary"),
        name="diff_attention",
    )(slopes, lam, qkv, qkv, qkv, subln_g)


def _router_body(x_ref, wt_ref, b_ref, idx_ref, gate_ref, rank_ref, cnt_ref):
    E = N_EXPERTS
    tm = x_ref.shape[0]
    x_hi, x_lo = _split_bf16(x_ref[...])
    w_hi, w_lo = _split_bf16(wt_ref[...])
    nt = functools.partial(lax.dot_general, dimension_numbers=_NT, preferred_element_type=F32)
    logits = nt(w_hi, x_hi) + nt(w_lo, x_hi) + nt(w_hi, x_lo) + b_ref[...]

    eid = lax.broadcasted_iota(I32, (E, tm), 0).astype(F32)
    work = logits
    vals, idxs = [], []
    for _ in range(TOP_K):
        m = jnp.max(work, axis=0, keepdims=True)
        sel = jnp.min(jnp.where(work == m, eid, float(E)), axis=0, keepdims=True)
        vals.append(m)
        idxs.append(sel)
        work = jnp.where(eid == sel, -jnp.inf, work)
    exps = [jnp.exp(v - vals[0]) for v in vals]
    denom = exps[0] + exps[1] + exps[2] + exps[3]

    chosen = jnp.zeros((E, tm), F32)
    for sel in idxs:
        chosen = chosen + jnp.where(eid == sel, 1.0, 0.0)
    earlier = jnp.where(lax.broadcasted_iota(I32, (tm, tm), 0) < lax.broadcasted_iota(I32, (tm, tm), 1), 1.0, 0.0)
    before = jnp.dot(chosen.astype(BF16), earlier.astype(BF16), preferred_element_type=F32)
    for k in range(TOP_K):
        gate_ref[k:k + 1, :] = exps[k] / denom
        idx_ref[k:k + 1, :] = idxs[k].astype(I32)
        rank_ref[k:k + 1, :] = jnp.sum(jnp.where(eid == idxs[k], before, 0.0), axis=0, keepdims=True).astype(I32)
    cnt_ref[...] = jnp.broadcast_to(jnp.sum(chosen, axis=1, keepdims=True), cnt_ref.shape)


def _router(x, w_rt, b_r):
    T = x.shape[0]
    E = N_EXPERTS
    tm = ROW_TILE
    tok = pl.BlockSpec((TOP_K, tm), lambda i: (0, i))
    return pl.pallas_call(
        _router_body,
        grid=(T // tm,),
        in_specs=[pl.BlockSpec((tm, D_MODEL), lambda i: (i, 0)),
                  pl.BlockSpec((E, D_MODEL), lambda i: (0, 0)),
                  pl.BlockSpec((E, 1), lambda i: (0, 0))],
        out_specs=[tok, tok, tok, pl.BlockSpec((None, E, LANES), lambda i: (i, 0, 0))],
        out_shape=[jax.ShapeDtypeStruct((TOP_K, T), I32), jax.ShapeDtypeStruct((TOP_K, T), F32),
                   jax.ShapeDtypeStruct((TOP_K, T), I32), jax.ShapeDtypeStruct((T // tm, E, LANES), F32)],
        compiler_params=_params("parallel"),
        name="moe_router",
    )(x, w_rt, b_r)


def _segment_copies(seg_ref, make_copy, slot, wait):
    big = COPY_ROWS[0]
    big_shift = (big // SEG_ALIGN).bit_length() - 1

    def go(copy):
        if wait:
            copy.wait()
        else:
            copy.start()

    def per_expert(e, carry):
        lrow = seg_ref[0, e]
        grow = seg_ref[0, N_EXPERTS + e]
        n = seg_ref[0, 2 * N_EXPERTS + e]
        n_big = lax.shift_right_logical(n, big_shift)

        def big_body(j, c):
            go(make_copy(slot, lrow + j * big, grow + j * big, big))
            return c
        lax.fori_loop(0, n_big, big_body, 0)
        off = n_big * big
        for rows in COPY_ROWS[1:]:
            has = (n & (rows // SEG_ALIGN)) != 0

            @pl.when(has)
            def _():
                go(make_copy(slot, lrow + off, grow + off, rows))
            off = off + jnp.where(has, rows, 0)
        return carry
    lax.fori_loop(0, N_EXPERTS, per_expert, 0)


def _dispatch_body(tail_ref, seg_ref, seg_prev_ref, x_ref, idx_ref, rank_ref, loff_ref, xs_hbm, lrow_ref,
                   buf_ref, zero_ref, sem, zsem, *, n_tok_blocks):
    b = pl.program_id(0)
    slot = b % 2
    tm = x_ref.shape[0]
    R = buf_ref.shape[1]

    def seg_copy(s, local_row, dst_row, rows):
        return pltpu.make_async_copy(buf_ref.at[s, pl.ds(pl.multiple_of(local_row, SEG_ALIGN), rows)],
                                     xs_hbm.at[pl.ds(pl.multiple_of(dst_row, SEG_ALIGN), rows)], sem.at[s])

    @pl.when(b == 0)
    def _():
        zero_ref[...] = jnp.zeros_like(zero_ref)
        for e in range(N_EXPERTS):
            fill = pltpu.make_async_copy(
                zero_ref, xs_hbm.at[pl.ds(pl.multiple_of(tail_ref[e], SEG_ALIGN), EXPERT_ROWS)], zsem)
            fill.start()
            fill.wait()

    eid = lax.broadcasted_iota(I32, (N_EXPERTS, tm), 0)
    loff = loff_ref[...].astype(F32)
    rid = lax.broadcasted_iota(I32, (R, tm), 0)
    sel_t = jnp.zeros((R, tm), F32)
    for k in range(TOP_K):
        base = jnp.sum(jnp.where(eid == idx_ref[k:k + 1, :], loff, 0.0), axis=0, keepdims=True).astype(I32)
        row = base + rank_ref[k:k + 1, :]
        lrow_ref[k:k + 1, :] = row
        sel_t = jnp.where(rid == row, 1.0, sel_t)
    sel_t = sel_t.astype(BF16)
    xb = x_ref[...].astype(BF16)
    lo = jnp.dot(sel_t, xb[:, :HALF], preferred_element_type=F32)
    hi = jnp.dot(sel_t, xb[:, HALF:], preferred_element_type=F32)
    packed = (lax.bitcast_convert_type(hi, U32) & jnp.uint32(0xFFFF0000)) | (lax.bitcast_convert_type(lo, U32) >> 16)
    buf_ref[slot] = packed

    @pl.when(b >= 1)
    def _():
        _segment_copies(seg_prev_ref, seg_copy, 1 - slot, wait=True)
    _segment_copies(seg_ref, seg_copy, slot, wait=False)

    @pl.when(b == n_tok_blocks - 1)
    def _():
        _segment_copies(seg_ref, seg_copy, slot, wait=True)


def _dispatch(x, idx, rank, loff, seg_table, tail_start, n_rows):
    T = x.shape[0]
    tm = ROW_TILE
    nb = T // tm
    R = TOP_K * tm + N_EXPERTS * SEG_ALIGN
    tok = pl.BlockSpec((TOP_K, tm), lambda b, tl: (0, b))
    seg_w = seg_table.shape[-1]
    return pl.pallas_call(
        functools.partial(_dispatch_body, n_tok_blocks=nb),
        grid_spec=pltpu.PrefetchScalarGridSpec(
            num_scalar_prefetch=1,
            grid=(nb,),
            in_specs=[pl.BlockSpec((None, 1, seg_w), lambda b, tl: (b, 0, 0), memory_space=pltpu.SMEM),
                      pl.BlockSpec((None, 1, seg_w), lambda b, tl: (jnp.maximum(b - 1, 0), 0, 0),
                                   memory_space=pltpu.SMEM),
                      pl.BlockSpec((tm, D_MODEL), lambda b, tl: (b, 0)),
                      tok, tok,
                      pl.BlockSpec((None, N_EXPERTS, 1), lambda b, tl: (b, 0, 0))],
            out_specs=[pl.BlockSpec(memory_space=pl.ANY), tok],
            scratch_shapes=[pltpu.VMEM((2, R, HALF), U32), pltpu.VMEM((EXPERT_ROWS, HALF), U32),
                            pltpu.SemaphoreType.DMA((2,)), pltpu.SemaphoreType.DMA(())]),
        out_shape=[jax.ShapeDtypeStruct((n_rows, HALF), U32), jax.ShapeDtypeStruct((TOP_K, T), I32)],
        compiler_params=_params("arbitrary"),
        name="moe_dispatch",
    )(tail_start, seg_table, seg_table, x, idx, rank, loff)


def _ffn_body(be_ref, nu_ref, xs_ref, w1_ref, b1_ref, w2_ref, b2_ref, ys_ref, w1b_ref, w2b_ref):
    i = pl.program_id(0)
    used = i < nu_ref[0]
    fresh = jnp.logical_or(i == 0, be_ref[i] != be_ref[jnp.maximum(i - 1, 0)])

    @pl.when(jnp.logical_and(used, fresh))
    def _():
        w1b_ref[...] = w1_ref[...].astype(BF16)
        w2b_ref[...] = w2_ref[...].astype(BF16)

    @pl.when(used)
    def _():
        lo, hi = _unpack_halves(xs_ref[...])
        h = (jnp.dot(lo.astype(BF16), w1b_ref[0:HALF, :], preferred_element_type=F32)
             + jnp.dot(hi.astype(BF16), w1b_ref[HALF:D_MODEL, :], preferred_element_type=F32) + b1_ref[...])
        hg = jnp.minimum(h[:, :D_FF], SWIGLU_LIMIT)
        hu = jnp.clip(h[:, D_FF:], -SWIGLU_LIMIT, SWIGLU_LIMIT)
        act = (hu + 1.0) * (hg * jax.nn.sigmoid(hg * SWIGLU_ALPHA))
        y = jnp.dot(act.astype(BF16), w2b_ref[...], preferred_element_type=F32) + b2_ref[...]
        ys_ref[...] = _pack_halves(y)


def _expert_ffn(xs, blk_expert, n_used, w1, b1, w2, b2, layer, n_blocks):
    rows = pl.BlockSpec((EXPERT_ROWS, HALF), lambda i, be, nu: (jnp.minimum(i, nu[0] - 1), 0))
    return pl.pallas_call(
        _ffn_body,
        grid_spec=pltpu.PrefetchScalarGridSpec(
            num_scalar_prefetch=2,
            grid=(n_blocks,),
            in_specs=[rows,
                      pl.BlockSpec((None, None, D_MODEL, 2 * D_FF), lambda i, be, nu: (layer, be[i], 0, 0)),
                      pl.BlockSpec((None, None, 1, 2 * D_FF), lambda i, be, nu: (layer, be[i], 0, 0)),
                      pl.BlockSpec((None, None, D_FF, D_MODEL), lambda i, be, nu: (layer, be[i], 0, 0)),
                      pl.BlockSpec((None, None, 1, D_MODEL), lambda i, be, nu: (layer, be[i], 0, 0))],
            out_specs=rows,
            scratch_shapes=[pltpu.VMEM((D_MODEL, 2 * D_FF), BF16), pltpu.VMEM((D_FF, D_MODEL), BF16)]),
        out_shape=jax.ShapeDtypeStruct((xs.shape[0], HALF), U32),
        compiler_params=_params("arbitrary"),
        name="moe_expert_ffn",
    )(blk_expert, n_used, xs, w1, b1, w2, b2)


def _combine_body(seg_ref, seg_next_ref, lrow_ref, gate_ref, x_ref, g_ref, beta_ref, ys_hbm, xo_ref,
                  buf_ref, sem, *, n_tok_blocks):
    b = pl.program_id(0)
    slot = b % 2
    tm = x_ref.shape[0]
    R = buf_ref.shape[1]

    def seg_copy(s, local_row, src_row, rows):
        return pltpu.make_async_copy(ys_hbm.at[pl.ds(pl.multiple_of(src_row, SEG_ALIGN), rows)],
                                     buf_ref.at[s, pl.ds(pl.multiple_of(local_row, SEG_ALIGN), rows)], sem.at[s])

    @pl.when(b == 0)
    def _():
        buf_ref[...] = jnp.zeros_like(buf_ref)
        _segment_copies(seg_ref, seg_copy, 0, wait=False)

    @pl.when(b + 1 < n_tok_blocks)
    def _():
        _segment_copies(seg_next_ref, seg_copy, 1 - slot, wait=False)

    _segment_copies(seg_ref, seg_copy, slot, wait=True)

    lo, hi = _unpack_halves(buf_ref[slot])
    cid = lax.broadcasted_iota(I32, (tm, R), 1)
    lrow = lrow_ref[...]
    gates = gate_ref[...]
    sel = jnp.zeros((tm, R), F32)
    for k in range(TOP_K):
        sel = jnp.where(cid == lrow[:, k:k + 1], gates[:, k:k + 1], sel)
    sel = sel.astype(BF16)
    f = jnp.concatenate([jnp.dot(sel, lo.astype(BF16), preferred_element_type=F32),
                         jnp.dot(sel, hi.astype(BF16), preferred_element_type=F32)], axis=1)
    xo_ref[...] = _layer_norm_rows(DEEPNORM_ALPHA * x_ref[...] + f, g_ref[...], beta_ref[...])


def _combine_ln(lrow_t, gate_t, x, g, beta, ys, seg_table):
    T = x.shape[0]
    tm = ROW_TILE
    nb = T // tm
    R = TOP_K * tm + N_EXPERTS * SEG_ALIGN
    seg_w = seg_table.shape[-1]
    return pl.pallas_call(
        functools.partial(_combine_body, n_tok_blocks=nb),
        grid=(nb,),
        in_specs=[pl.BlockSpec((None, 1, seg_w), lambda b: (b, 0, 0), memory_space=pltpu.SMEM),
                  pl.BlockSpec((None, 1, seg_w), lambda b: (jnp.minimum(b + 1, nb - 1), 0, 0),
                               memory_space=pltpu.SMEM),
                  pl.BlockSpec((tm, TOP_K), lambda b: (b, 0)),
                  pl.BlockSpec((tm, TOP_K), lambda b: (b, 0)),
                  pl.BlockSpec((tm, D_MODEL), lambda b: (b, 0)),
                  pl.BlockSpec((1, D_MODEL), lambda b: (0, 0)),
                  pl.BlockSpec((1, D_MODEL), lambda b: (0, 0)),
                  pl.BlockSpec(memory_space=pl.ANY)],
        out_specs=pl.BlockSpec((tm, D_MODEL), lambda b: (b, 0)),
        out_shape=jax.ShapeDtypeStruct((T, D_MODEL), F32),
        scratch_shapes=[pltpu.VMEM((2, R, HALF), U32), pltpu.SemaphoreType.DMA((2,))],
        compiler_params=_params("arbitrary"),
        name="moe_combine_ln",
    )(seg_table, seg_table, lrow_t, gate_t, x, g, beta, ys)


def _dft_tables(S):
    n2 = 4 * S
    k = jnp.arange(S, dtype=I32)
    m = ((2 * k[:, None] + 1) * k[None, :]) % n2
    ang = m.astype(F32) * F32(2.0 * math.pi / n2)
    fc32, fs32 = jnp.cos(ang), -jnp.sin(ang)
    scale = F32(1.0 / S)
    gc, gs = (fc32.T * scale).astype(BF16), (fs32.T * scale).astype(BF16)
    return fc32, fs32, (fc32.astype(BF16), fs32.astype(BF16), gc, gs)


def _hyena_positional(S):
    pos = jnp.arange(S, dtype=F32)
    t = jnp.linspace(0.0, 1.0, S, dtype=F32)[:, None]
    bands = (HYENA_EMB_DIM - 1) // 2
    f = jnp.linspace(1e-4, bands - 1, bands, dtype=F32)
    ang = (2.0 * math.pi / S) * pos[:, None] * f[None, :]
    feats = jnp.concatenate([t, jnp.cos(ang), -jnp.sin(ang)], axis=-1)
    feats = jnp.pad(feats, ((0, 0), (0, LANES - HYENA_EMB_DIM)))
    max_decay = math.log(HYENA_DECAY_TARGET) / HYENA_SHORT_DECAY_PCT
    min_decay = math.log(HYENA_DECAY_TARGET) / HYENA_LONG_DECAY_PCT
    deltas = jnp.abs(jnp.linspace(min_decay, max_decay, HYENA_CH, dtype=F32))[None, :]
    return feats, t, deltas


def _alibi_slopes():
    return jnp.asarray(np.array([2.0 ** (-8.0 * (i + 1) / N_HEADS) for i in range(N_HEADS)], dtype=np.float32))


def _even_mixer(x, xshape, tabs, w_in, b_in, short_w, short_b, f1_w, f1_b, f1_freq, f2_w, f2_b, f2_freq, f3_w,
                skip, dw_w, dw_b, cln_g, cln_b, w_out, b_out, ln_g, ln_b):
    B, S = xshape
    fc32, fs32, tabs16 = tabs
    proj = _project(x, w_in.astype(BF16), b_in[None, :])
    hy = _short_conv(proj, short_w, short_b[None, :], B, S)
    u = _conformer(proj, dw_w, dw_b[None, :], cln_g[None, :], cln_b[None, :], B, S)
    feats, tcol, deltas = _hyena_positional(S)
    f1_wp = jnp.pad(f1_w, ((0, LANES - HYENA_EMB_DIM), (0, 0)))
    taps = _hyena_filters(feats, f1_wp, f1_b[None, :], f1_freq[None, :], f2_w, f2_b[None, :], f2_freq[None, :],
                          f3_w, tcol, deltas)
    hre, him = _filter_spectrum(fc32, fs32, taps)
    z = _long_conv(hy, 2, hy, 0, tabs16, hre, him, 0, skip, B, S)
    z = _long_conv(z, 0, hy, 1, tabs16, hre, him, 1, skip, B, S)
    return _outproj_ln(z, 0, u, 0, w_out.astype(BF16), b_out[None, :], x, ln_g[None, :], ln_b[None, :])


def _odd_mixer(x, xshape, layer_idx, w_qkv, lq1, lk1, lq2, lk2, subln_g, w_out, ln_g, ln_b):
    B, S = xshape
    lam_init = 0.8 - 0.6 * math.exp(-0.3 * layer_idx)
    lam = (jnp.exp(jnp.sum(lq1 * lk1)) - jnp.exp(jnp.sum(lq2 * lk2)) + lam_init).reshape(1)
    q_scale = jnp.concatenate([jnp.full((ATTN_W,), HEAD_DIM ** -0.5, F32), jnp.ones((2 * ATTN_W,), F32)])
    w = (w_qkv * q_scale).astype(BF16)
    qkv = _project(x, w, jnp.zeros((1, 3 * ATTN_W), F32))
    o = _diff_attention(qkv, _alibi_slopes(), lam, subln_g[None, :], lam_init, B, S)
    return _outproj_ln(o, 0, o, 1, w_out.astype(BF16), jnp.zeros((1, D_MODEL), F32), x, ln_g[None, :], ln_b[None, :])


def _round_up(a, m):
    return (a + m - 1) // m * m


def _routing_tables(cnt_blocks, n_ffn_blocks):
    E = N_EXPERTS
    cnt8 = _round_up(cnt_blocks[:, :, 0].astype(I32), SEG_ALIGN)
    seg_end = jnp.cumsum(cnt8, axis=1)
    loff = seg_end - cnt8
    tot8 = jnp.sum(cnt8, axis=0)
    group = _round_up(tot8, EXPERT_ROWS)
    group_end = jnp.cumsum(group)
    group_start = group_end - group
    goff = group_start[None, :] + jnp.cumsum(cnt8, axis=0) - cnt8
    seg_table = jnp.concatenate([loff, goff, cnt8 // SEG_ALIGN], axis=1)
    starts = jnp.arange(n_ffn_blocks, dtype=I32) * EXPERT_ROWS
    blk_expert = jnp.minimum(jnp.sum((group_end[None, :] <= starts[:, None]).astype(I32), axis=1), E - 1)
    n_used = group_end[-1:] // EXPERT_ROWS
    tail_start = group_start + tot8
    return loff[:, :, None], seg_table[:, None, :], blk_expert, n_used, tail_start


def _moe_layer(x, layer, w_r, b_r, w1, b1, w2, b2, ln_g, ln_b):
    T = x.shape[0]
    nb = T // ROW_TILE
    n_rows = _round_up(T * TOP_K + nb * N_EXPERTS * (SEG_ALIGN - 1), EXPERT_ROWS) + N_EXPERTS * EXPERT_ROWS
    n_ffn_blocks = n_rows // EXPERT_ROWS
    idx, gate, rank, cnt = _router(x, w_r.T, b_r[:, None])
    loff, seg_table, blk_expert, n_used, tail_start = _routing_tables(cnt, n_ffn_blocks)
    xs, lrow = _dispatch(x, idx, rank, loff, seg_table, tail_start, n_rows + EXPERT_ROWS)
    ys = _expert_ffn(xs, blk_expert, n_used, w1, b1[:, :, None, :], w2, b2[:, :, None, :], layer, n_ffn_blocks)
    return _combine_ln(lrow.T, gate.T, x, ln_g[None, :], ln_b[None, :], ys, seg_table)


def kernel(x, hy_cf_w_in, hy_cf_b_in, hy_short_w, hy_short_b, hy_f1_w, hy_f1_b, hy_f1_freq, hy_f2_w, hy_f2_b, hy_f2_freq, hy_f3_w, hy_skip, cf_dw_w, cf_dw_b, cf_ln_g, cf_ln_b, even_w_out, even_b_out, attn_w_qkv, attn_lq1, attn_lk1, attn_lq2, attn_lk2, attn_subln_g, attn_w_out, ln1_g, ln1_b, ln2_g, ln2_b, moe_w_r, moe_b_r, moe_w1, moe_b1, moe_w2, moe_b2):
    B, S, D = x.shape
    assert D == D_MODEL and (B * S) % ROW_TILE == 0 and S % LANES == 0
    depth = ln1_g.shape[0]
    xf = x.reshape(B * S, D)
    tabs = _dft_tables(S)
    for i in range(depth):
        j = i // 2
        if i % 2 == 0:
            xf = _even_mixer(xf, (B, S), tabs, hy_cf_w_in[j], hy_cf_b_in[j], hy_short_w[j], hy_short_b[j],
                                 hy_f1_w[j], hy_f1_b[j], hy_f1_freq[j], hy_f2_w[j], hy_f2_b[j], hy_f2_freq[j],
                                 hy_f3_w[j], hy_skip[j], cf_dw_w[j], cf_dw_b[j], cf_ln_g[j], cf_ln_b[j],
                                 even_w_out[j], even_b_out[j], ln1_g[i], ln1_b[i])
        else:
            xf = _odd_mixer(xf, (B, S), i, attn_w_qkv[j], attn_lq1[j], attn_lk1[j], attn_lq2[j], attn_lk2[j],
                            attn_subln_g[j], attn_w_out[j], ln1_g[i], ln1_b[i])
        xf = _moe_layer(xf, i, moe_w_r[i], moe_b_r[i], moe_w1, moe_b1, moe_w2, moe_b2, ln2_g[i], ln2_b[i])
    return xf.reshape(B, S, D)
```

```python
import functools
import math

import jax
import jax.numpy as jnp
import numpy as np
from jax import lax
from jax.experimental import pallas as pl
from jax.experimental.pallas import tpu as pltpu

F32 = jnp.float32
BF16 = jnp.bfloat16
U32 = jnp.uint32
I32 = jnp.int32

D_MODEL = 1024
HALF = D_MODEL // 2
DEPTH = 4
HYENA_CH = D_MODEL // 2
CONF_CH = D_MODEL // 2
HYENA_ORDER = 2
HYENA_EMB_DIM = 33
HYENA_FILTER_DIM = 64
HYENA_SHORT_DECAY_PCT = 0.3
HYENA_LONG_DECAY_PCT = 1.5
HYENA_DECAY_TARGET = 1e-2
CONF_WIDTH = 31
EVEN_IN = 3 * HYENA_CH + 2 * CONF_CH
N_HEADS = 8
HEAD_DIM = 64
ATTN_W = N_HEADS * 2 * HEAD_DIM
N_EXPERTS = 32
TOP_K = 4
D_FF = D_MODEL
SWIGLU_LIMIT = 7.0
SWIGLU_ALPHA = 1.702
DEEPNORM_ALPHA = (2 * DEPTH) ** 0.25
LN_EPS = 1e-5

LANES = 128
SUBLANES = 8
VMEM_LIMIT_BYTES = 56 * 1024 * 1024
ROW_TILE = 512
EXPERT_ROWS = 512
SEG_ALIGN = 8
COPY_ROWS = (64, 32, 16, 8)
CONV_BLOCKS = 4
CONV_CH_TILE = 256
ATTN_Q_TILE = 1024
ATTN_ROW_CHUNK = 128
CONV_PAD = 16

_NT = (((1,), (1,)), ((), ()))


def _params(*sem):
    return pltpu.CompilerParams(dimension_semantics=sem, vmem_limit_bytes=VMEM_LIMIT_BYTES)


def _split_bf16(a):
    hi = a.astype(BF16)
    lo = (a - hi.astype(F32)).astype(BF16)
    return hi, lo


def _dot3(a, b):
    a_hi, a_lo = _split_bf16(a)
    b_hi, b_lo = _split_bf16(b)
    d = functools.partial(jnp.dot, preferred_element_type=F32)
    return d(a_hi, b_hi) + d(a_hi, b_lo) + d(a_lo, b_hi)


def _layer_norm_rows(y, g, b):
    mu = jnp.mean(y, axis=-1, keepdims=True)
    yc = y - mu
    var = jnp.mean(yc * yc, axis=-1, keepdims=True)
    return yc * lax.rsqrt(var + LN_EPS) * g + b


def _pack_halves(y):
    lo = lax.bitcast_convert_type(y[:, :HALF].astype(BF16).astype(F32), U32)
    hi = lax.bitcast_convert_type(y[:, HALF:].astype(BF16).astype(F32), U32)
    return hi | (lo >> 16)


def _unpack_halves(p):
    lo = lax.bitcast_convert_type(p << 16, F32)
    hi = lax.bitcast_convert_type(p & jnp.uint32(0xFFFF0000), F32)
    return lo, hi


def _proj_body(x_ref, w_ref, b_ref, o_ref, *, col_chunk):
    x = x_ref[...].astype(BF16)
    for j in range(0, o_ref.shape[1], col_chunk):
        acc = jnp.dot(x, w_ref[:, j:j + col_chunk], preferred_element_type=F32)
        o_ref[:, j:j + col_chunk] = (acc + b_ref[:, j:j + col_chunk]).astype(o_ref.dtype)


def _project(x, w, b):
    T, K = x.shape
    N = w.shape[1]
    return pl.pallas_call(
        functools.partial(_proj_body, col_chunk=512),
        grid=(T // ROW_TILE,),
        in_specs=[pl.BlockSpec((ROW_TILE, K), lambda i: (i, 0)),
                  pl.BlockSpec((K, N), lambda i: (0, 0)),
                  pl.BlockSpec((1, N), lambda i: (0, 0))],
        out_specs=pl.BlockSpec((ROW_TILE, N), lambda i: (i, 0)),
        out_shape=jax.ShapeDtypeStruct((T, N), BF16),
        compiler_params=_params("parallel"),
        name="project",
    )(x, w, b)


def _outproj_ln_body(a1_ref, a2_ref, w1_ref, w2_ref, b_ref, x_ref, g_ref, beta_ref, xo_ref):
    m = (jnp.dot(a1_ref[...], w1_ref[...], preferred_element_type=F32)
         + jnp.dot(a2_ref[...], w2_ref[...], preferred_element_type=F32) + b_ref[...])
    xo_ref[...] = _layer_norm_rows(DEEPNORM_ALPHA * x_ref[...] + m, g_ref[...], beta_ref[...])


def _outproj_ln(a1, a1_col, a2, a2_col, w, b, x, g, beta):
    T = x.shape[0]
    return pl.pallas_call(
        _outproj_ln_body,
        grid=(T // ROW_TILE,),
        in_specs=[pl.BlockSpec((ROW_TILE, HALF), lambda i: (i, a1_col)),
                  pl.BlockSpec((ROW_TILE, HALF), lambda i: (i, a2_col)),
                  pl.BlockSpec((HALF, D_MODEL), lambda i: (0, 0)),
                  pl.BlockSpec((HALF, D_MODEL), lambda i: (1, 0)),
                  pl.BlockSpec((1, D_MODEL), lambda i: (0, 0)),
                  pl.BlockSpec((ROW_TILE, D_MODEL), lambda i: (i, 0)),
                  pl.BlockSpec((1, D_MODEL), lambda i: (0, 0)),
                  pl.BlockSpec((1, D_MODEL), lambda i: (0, 0))],
        out_specs=pl.BlockSpec((ROW_TILE, D_MODEL), lambda i: (i, 0)),
        out_shape=jax.ShapeDtypeStruct((T, D_MODEL), F32),
        compiler_params=_params("parallel"),
        name="outproj_ln",
    )(a1, a2, w, w, b, x, g, beta)


def _short_conv_body(x_ref, w_ref, b_ref, o_ref):
    x = x_ref[...].astype(F32)
    S = x.shape[0]
    row = lax.broadcasted_iota(I32, x.shape, 0)
    prev = jnp.where(row == 0, 0.0, pltpu.roll(x, 1, 0))
    nxt = jnp.where(row == S - 1, 0.0, pltpu.roll(x, S - 1, 0))
    y = w_ref[0:1, :] * prev + w_ref[1:2, :] * x + w_ref[2:3, :] * nxt + b_ref[...]
    o_ref[...] = y.astype(o_ref.dtype)


def _short_conv(proj, w, b, B, S):
    T = B * S
    C = HYENA_CH
    return pl.pallas_call(
        _short_conv_body,
        grid=(B, 3),
        in_specs=[pl.BlockSpec((S, C), lambda bi, j: (bi, j)),
                  pl.BlockSpec((3, C), lambda bi, j: (0, j)),
                  pl.BlockSpec((1, C), lambda bi, j: (0, j))],
        out_specs=pl.BlockSpec((S, C), lambda bi, j: (bi, j)),
        out_shape=jax.ShapeDtypeStruct((T, 3 * C), BF16),
        compiler_params=_params("parallel", "parallel"),
        name="hyena_short_conv",
    )(proj, w, b)


def _conformer_body(a_ref, g_ref, w_ref, b_ref, lg_ref, lb_ref, o_ref, pad_ref, sh_ref, *, row_chunk):
    S, C = a_ref.shape
    zeros = jnp.zeros((CONV_PAD, C), F32)
    pad_ref[0:CONV_PAD, :] = zeros
    pad_ref[CONV_PAD + S:CONV_PAD + S + CONV_PAD, :] = zeros
    a = a_ref[...].astype(F32)
    g = g_ref[...].astype(F32)
    pad_ref[CONV_PAD:CONV_PAD + S, :] = a * jax.nn.sigmoid(g)
    first = CONV_PAD - CONF_WIDTH // 2
    span = sh_ref.shape[1]
    for r0 in range(0, S, row_chunk):
        for ph in range(1, SUBLANES):
            sh_ref[ph - 1] = pad_ref[r0 + ph:r0 + ph + span, :]
        acc = jnp.zeros((row_chunk, C), F32) + b_ref[...]
        for j in range(CONF_WIDTH):
            q, ph = divmod(first + j, SUBLANES)
            if ph == 0:
                win = pad_ref[r0 + q * SUBLANES:r0 + q * SUBLANES + row_chunk, :]
            else:
                win = sh_ref[ph - 1, q * SUBLANES:q * SUBLANES + row_chunk, :]
            acc = acc + w_ref[j:j + 1, :] * win
        y = _layer_norm_rows(acc, lg_ref[...], lb_ref[...])
        o_ref[r0:r0 + row_chunk, :] = (y * jax.nn.sigmoid(y)).astype(o_ref.dtype)


def _conformer(proj, w, b, lg, lb, B, S):
    T = B * S
    C = CONF_CH
    rc = min(S, 256)
    return pl.pallas_call(
        functools.partial(_conformer_body, row_chunk=rc),
        grid=(B,),
        in_specs=[pl.BlockSpec((S, C), lambda bi: (bi, 3)),
                  pl.BlockSpec((S, C), lambda bi: (bi, 4)),
                  pl.BlockSpec((CONF_WIDTH, C), lambda bi: (0, 0)),
                  pl.BlockSpec((1, C), lambda bi: (0, 0)),
                  pl.BlockSpec((1, C), lambda bi: (0, 0)),
                  pl.BlockSpec((1, C), lambda bi: (0, 0))],
        out_specs=pl.BlockSpec((S, C), lambda bi: (bi, 0)),
        out_shape=jax.ShapeDtypeStruct((T, C), BF16),
        scratch_shapes=[pltpu.VMEM((S + 2 * CONV_PAD, C), F32),
                        pltpu.VMEM((SUBLANES - 1, rc + _round_up(CONF_WIDTH, SUBLANES) - SUBLANES, C), F32)],
        compiler_params=_params("parallel"),
        name="conformer_conv",
    )(proj, proj, w, b, lg, lb)


def _filter_body(feat_ref, w1_ref, b1_ref, q1_ref, w2_ref, b2_ref, q2_ref, w3_ref, t_ref, delta_ref, o_ref):
    h = jnp.sin(q1_ref[...] * (_dot3(feat_ref[...], w1_ref[...]) + b1_ref[...]))
    h = jnp.sin(q2_ref[...] * (_dot3(h, w2_ref[...]) + b2_ref[...]))
    h = _dot3(h, w3_ref[...])
    o_ref[...] = h * jnp.exp(-t_ref[...] * delta_ref[...])


def _hyena_filters(feats, w1, b1, q1, w2, b2, q2, w3, tcol, deltas):
    S = feats.shape[0]
    C = HYENA_CH
    n = 2 * HYENA_ORDER
    fd = HYENA_FILTER_DIM
    return pl.pallas_call(
        _filter_body,
        grid=(n,),
        in_specs=[pl.BlockSpec((S, LANES), lambda j: (0, 0)),
                  pl.BlockSpec((LANES, fd), lambda j: (0, 0)),
                  pl.BlockSpec((1, fd), lambda j: (0, 0)),
                  pl.BlockSpec((1, fd), lambda j: (0, 0)),
                  pl.BlockSpec((fd, fd), lambda j: (0, 0)),
                  pl.BlockSpec((1, fd), lambda j: (0, 0)),
                  pl.BlockSpec((1, fd), lambda j: (0, 0)),
                  pl.BlockSpec((fd, C), lambda j: (0, j)),
                  pl.BlockSpec((S, 1), lambda j: (0, 0)),
                  pl.BlockSpec((1, C), lambda j: (0, 0))],
        out_specs=pl.BlockSpec((S, C), lambda j: (0, j)),
        out_shape=jax.ShapeDtypeStruct((S, n * C), F32),
        compiler_params=_params("parallel"),
        name="hyena_filter_mlp",
    )(feats, w1, b1, q1, w2, b2, q2, w3, tcol, deltas)


def _spectrum_body(fc_ref, fs_ref, a_ref, b_ref, hre_ref, him_ref):
    a = a_ref[...]
    row = lax.broadcasted_iota(I32, a.shape, 0)
    b = jnp.where(row == 0, 0.0, b_ref[...])
    hre_ref[...] = _dot3(fc_ref[...], a + b)
    him_ref[...] = _dot3(fs_ref[...], a - b)


def _filter_spectrum(fc32, fs32, taps, S):
    P = fc32.shape[0]
    n = S // P
    C = HYENA_CH
    oc = HYENA_ORDER * C
    fwd, bwd = taps[:, :oc], taps[:, oc:]
    zero = jnp.zeros((1, oc), F32)
    h_up = jnp.concatenate([zero, bwd[:0:-1], fwd], axis=0)
    h_down = jnp.concatenate([zero, h_up[:0:-1]], axis=0)
    spec = pl.BlockSpec((P, C), lambda di, o: (di, o))
    return pl.pallas_call(
        _spectrum_body,
        grid=(2 * n - 1, HYENA_ORDER),
        in_specs=[pl.BlockSpec((P, P), lambda di, o: (0, 0)),
                  pl.BlockSpec((P, P), lambda di, o: (0, 0)),
                  pl.BlockSpec((P, C), lambda di, o: (di + 1, o)),
                  pl.BlockSpec((P, C), lambda di, o: (2 * n - 1 - di, o))],
        out_specs=[spec, spec],
        out_shape=[jax.ShapeDtypeStruct(((2 * n - 1) * P, oc), F32)] * 2,
        compiler_params=_params("parallel", "parallel"),
        name="hyena_filter_spectrum",
    )(fc32, fs32, h_up, h_down)


def _long_conv_body(v_ref, gate_ref, fc_ref, fs_ref, hre_ref, him_ref, gc_ref, gs_ref, skip_ref, o_ref, *, n_blk):
    P = fc_ref.shape[0]
    fc, fs, gc, gs = fc_ref[...], fs_ref[...], gc_ref[...], gs_ref[...]
    vre, vim = [], []
    for j in range(n_blk):
        vj = v_ref[j * P:(j + 1) * P, :]
        vre.append(jnp.dot(fc, vj, preferred_element_type=F32))
        vim.append(jnp.dot(fs, vj, preferred_element_type=F32))
    for i in range(n_blk):
        yre = yim = None
        for j in range(n_blk):
            r0 = (i - j + n_blk - 1) * P
            hre = hre_ref[r0:r0 + P, :]
            him = him_ref[r0:r0 + P, :]
            tre = vre[j] * hre - vim[j] * him
            tim = vre[j] * him + vim[j] * hre
            yre = tre if yre is None else yre + tre
            yim = tim if yim is None else yim + tim
        y = (jnp.dot(gc, yre.astype(BF16), preferred_element_type=F32)
             + jnp.dot(gs, yim.astype(BF16), preferred_element_type=F32))
        rows = slice(i * P, (i + 1) * P)
        y = y + v_ref[rows, :].astype(F32) * skip_ref[...]
        o_ref[rows, :] = (gate_ref[rows, :].astype(F32) * y).astype(o_ref.dtype)


def _long_conv(v_arr, v_col, gate_arr, gate_col, tabs, hre, him, order, skip, B, S):
    fc, fs, gc, gs = tabs
    P = fc.shape[0]
    T = B * S
    C = HYENA_CH
    nc = C // CONV_CH_TILE
    cc = CONV_CH_TILE
    n_h = hre.shape[0]
    tab = pl.BlockSpec((P, P), lambda bi, c: (0, 0))
    return pl.pallas_call(
        functools.partial(_long_conv_body, n_blk=S // P),
        grid=(B, nc),
        in_specs=[pl.BlockSpec((S, cc), lambda bi, c: (bi, v_col * nc + c)),
                  pl.BlockSpec((S, cc), lambda bi, c: (bi, gate_col * nc + c)),
                  tab, tab,
                  pl.BlockSpec((n_h, cc), lambda bi, c: (0, order * nc + c)),
                  pl.BlockSpec((n_h, cc), lambda bi, c: (0, order * nc + c)),
                  tab, tab,
                  pl.BlockSpec((1, cc), lambda bi, c: (0, c))],
        out_specs=pl.BlockSpec((S, cc), lambda bi, c: (bi, c)),
        out_shape=jax.ShapeDtypeStruct((T, C), BF16),
        compiler_params=_params("parallel", "parallel"),
        name="hyena_long_conv",
    )(v_arr, gate_arr, fc, fs, hre, him, gc, gs, skip[order][None, :])


def _attn_body(slope_ref, lam_ref, q_ref, k_ref, v_ref, g_ref, o_ref, vaug_ref, *, lam_init, row_chunk):
    h = pl.program_id(1)
    qi = pl.program_id(2)
    tq = q_ref.shape[0]
    S = k_ref.shape[0]
    hw = 2 * HEAD_DIM
    k = k_ref[...]
    slope = slope_ref[h]
    kpos = lax.broadcasted_iota(I32, (1, S), 1).astype(F32) * slope

    @pl.when(qi == 0)
    def _():
        vaug_ref[:, :hw] = v_ref[...]
        vaug_ref[:, hw:] = jnp.where(lax.broadcasted_iota(I32, (S, hw), 1) == 0, 1.0, 0.0).astype(BF16)

    v_aug = vaug_ref[...]

    for r0 in range(0, tq, row_chunk):
        q = q_ref[r0:r0 + row_chunk, :]
        lane = lax.broadcasted_iota(I32, q.shape, 1)
        zero = jnp.zeros_like(q)
        qpos = (qi * tq + r0 + lax.broadcasted_iota(I32, (row_chunk, 1), 0)).astype(F32) * slope
        bias = lax.bitcast_convert_type(lax.bitcast_convert_type(qpos - kpos, U32) | jnp.uint32(0x80000000), F32)

        def weighted_values(qh):
            s = lax.dot_general(qh, k, _NT, preferred_element_type=F32) + bias
            e = jnp.exp((s - jnp.max(s, axis=-1, keepdims=True)).astype(BF16))
            return jnp.dot(e, v_aug, preferred_element_type=F32)

        o1 = weighted_values(jnp.where(lane < HEAD_DIM, q, zero))
        o2 = weighted_values(jnp.where(lane >= HEAD_DIM, q, zero))
        o = o1[:, :hw] * (1.0 / o1[:, hw:hw + 1]) - o2[:, :hw] * (lam_ref[0] / o2[:, hw:hw + 1])
        o = o * lax.rsqrt(jnp.mean(o * o, axis=-1, keepdims=True) + LN_EPS) * g_ref[...]
        o_ref[r0:r0 + row_chunk, :] = (o * (1.0 - lam_init)).astype(o_ref.dtype)


def _diff_attention(qkv, slopes, lam, subln_g, lam_init, B, S):
    T = B * S
    hw = 2 * HEAD_DIM
    tq = min(S, ATTN_Q_TILE)
    nq = S // tq
    smem = pl.BlockSpec(memory_space=pltpu.SMEM)
    return pl.pallas_call(
        functools.partial(_attn_body, lam_init=lam_init, row_chunk=min(tq, ATTN_ROW_CHUNK)),
        grid=(B, N_HEADS, nq),
        in_specs=[smem, smem,
                  pl.BlockSpec((tq, hw), lambda bi, h, qi: (bi * nq + qi, h)),
                  pl.BlockSpec((S, hw), lambda bi, h, qi: (bi, N_HEADS + h)),
                  pl.BlockSpec((S, hw), lambda bi, h, qi: (bi, 2 * N_HEADS + h)),
                  pl.BlockSpec((1, hw), lambda bi, h, qi: (0, 0))],
        out_specs=pl.BlockSpec((tq, hw), lambda bi, h, qi: (bi * nq + qi, h)),
        out_shape=jax.ShapeDtypeStruct((T, ATTN_W), BF16),
        scratch_shapes=[pltpu.VMEM((S, 2 * hw), BF16)],
        compiler_params=_params("parallel", "parallel", "arbitrary"),
        name="diff_attention",
    )(slopes, lam, qkv, qkv, qkv, subln_g)


def _router_body(x_ref, wt_ref, b_ref, idx_ref, gate_ref, rank_ref, cnt_ref):
    E = N_EXPERTS
    tm = x_ref.shape[0]
    x_hi, x_lo = _split_bf16(x_ref[...])
    w_hi, w_lo = _split_bf16(wt_ref[...])
    nt = functools.partial(lax.dot_general, dimension_numbers=_NT, preferred_element_type=F32)
    logits = nt(w_hi, x_hi) + nt(w_lo, x_hi) + nt(w_hi, x_lo) + b_ref[...]

    eid = lax.broadcasted_iota(I32, (E, tm), 0).astype(F32)
    work = logits
    vals, idxs = [], []
    for _ in range(TOP_K):
        m = jnp.max(work, axis=0, keepdims=True)
        sel = jnp.min(jnp.where(work == m, eid, float(E)), axis=0, keepdims=True)
        vals.append(m)
        idxs.append(sel)
        work = jnp.where(eid == sel, -jnp.inf, work)
    exps = [jnp.exp(v - vals[0]) for v in vals]
    denom = exps[0] + exps[1] + exps[2] + exps[3]

    chosen = jnp.zeros((E, tm), F32)
    for sel in idxs:
        chosen = chosen + jnp.where(eid == sel, 1.0, 0.0)
    earlier = jnp.where(lax.broadcasted_iota(I32, (tm, tm), 0) < lax.broadcasted_iota(I32, (tm, tm), 1), 1.0, 0.0)
    before = jnp.dot(chosen.astype(BF16), earlier.astype(BF16), preferred_element_type=F32)
    for k in range(TOP_K):
        gate_ref[k:k + 1, :] = exps[k] / denom
        idx_ref[k:k + 1, :] = idxs[k].astype(I32)
        rank_ref[k:k + 1, :] = jnp.sum(jnp.where(eid == idxs[k], before, 0.0), axis=0, keepdims=True).astype(I32)
    cnt_ref[...] = jnp.broadcast_to(jnp.sum(chosen, axis=1, keepdims=True), cnt_ref.shape)


def _router(x, w_rt, b_r):
    T = x.shape[0]
    E = N_EXPERTS
    tm = ROW_TILE
    tok = pl.BlockSpec((TOP_K, tm), lambda i: (0, i))
    return pl.pallas_call(
        _router_body,
        grid=(T // tm,),
        in_specs=[pl.BlockSpec((tm, D_MODEL), lambda i: (i, 0)),
                  pl.BlockSpec((E, D_MODEL), lambda i: (0, 0)),
                  pl.BlockSpec((E, 1), lambda i: (0, 0))],
        out_specs=[tok, tok, tok, pl.BlockSpec((None, E, LANES), lambda i: (i, 0, 0))],
        out_shape=[jax.ShapeDtypeStruct((TOP_K, T), I32), jax.ShapeDtypeStruct((TOP_K, T), F32),
                   jax.ShapeDtypeStruct((TOP_K, T), I32), jax.ShapeDtypeStruct((T // tm, E, LANES), F32)],
        compiler_params=_params("parallel"),
        name="moe_router",
    )(x, w_rt, b_r)


def _segment_copies(seg_ref, make_copy, slot, wait):
    big = COPY_ROWS[0]
    big_shift = (big // SEG_ALIGN).bit_length() - 1

    def go(copy):
        if wait:
            copy.wait()
        else:
            copy.start()

    def per_expert(e, carry):
        lrow = seg_ref[0, e]
        grow = seg_ref[0, N_EXPERTS + e]
        n = seg_ref[0, 2 * N_EXPERTS + e]
        n_big = lax.shift_right_logical(n, big_shift)

        def big_body(j, c):
            go(make_copy(slot, lrow + j * big, grow + j * big, big))
            return c
        lax.fori_loop(0, n_big, big_body, 0)
        off = n_big * big
        for rows in COPY_ROWS[1:]:
            has = (n & (rows // SEG_ALIGN)) != 0

            @pl.when(has)
            def _():
                go(make_copy(slot, lrow + off, grow + off, rows))
            off = off + jnp.where(has, rows, 0)
        return carry
    lax.fori_loop(0, N_EXPERTS, per_expert, 0)


def _dispatch_body(tail_ref, seg_ref, seg_prev_ref, x_ref, idx_ref, rank_ref, loff_ref, xs_hbm, lrow_ref,
                   buf_ref, zero_ref, sem, zsem, *, n_tok_blocks):
    b = pl.program_id(0)
    slot = b % 2
    tm = x_ref.shape[0]
    R = buf_ref.shape[1]

    def seg_copy(s, local_row, dst_row, rows):
        return pltpu.make_async_copy(buf_ref.at[s, pl.ds(pl.multiple_of(local_row, SEG_ALIGN), rows)],
                                     xs_hbm.at[pl.ds(pl.multiple_of(dst_row, SEG_ALIGN), rows)], sem.at[s])

    @pl.when(b == 0)
    def _():
        zero_ref[...] = jnp.zeros_like(zero_ref)
        for e in range(N_EXPERTS):
            fill = pltpu.make_async_copy(
                zero_ref, xs_hbm.at[pl.ds(pl.multiple_of(tail_ref[e], SEG_ALIGN), EXPERT_ROWS)], zsem)
            fill.start()
            fill.wait()

    eid = lax.broadcasted_iota(I32, (N_EXPERTS, tm), 0)
    loff = loff_ref[...].astype(F32)
    rid = lax.broadcasted_iota(I32, (R, tm), 0)
    sel_t = jnp.zeros((R, tm), F32)
    for k in range(TOP_K):
        base = jnp.sum(jnp.where(eid == idx_ref[k:k + 1, :], loff, 0.0), axis=0, keepdims=True).astype(I32)
        row = base + rank_ref[k:k + 1, :]
        lrow_ref[k:k + 1, :] = row
        sel_t = jnp.where(rid == row, 1.0, sel_t)
    sel_t = sel_t.astype(BF16)
    xb = x_ref[...].astype(BF16)
    lo = jnp.dot(sel_t, xb[:, :HALF], preferred_element_type=F32)
    hi = jnp.dot(sel_t, xb[:, HALF:], preferred_element_type=F32)
    packed = (lax.bitcast_convert_type(hi, U32) & jnp.uint32(0xFFFF0000)) | (lax.bitcast_convert_type(lo, U32) >> 16)
    buf_ref[slot] = packed

    @pl.when(b >= 1)
    def _():
        _segment_copies(seg_prev_ref, seg_copy, 1 - slot, wait=True)
    _segment_copies(seg_ref, seg_copy, slot, wait=False)

    @pl.when(b == n_tok_blocks - 1)
    def _():
        _segment_copies(seg_ref, seg_copy, slot, wait=True)


def _dispatch(x, idx, rank, loff, seg_table, tail_start, n_rows):
    T = x.shape[0]
    tm = ROW_TILE
    nb = T // tm
    R = TOP_K * tm + N_EXPERTS * SEG_ALIGN
    tok = pl.BlockSpec((TOP_K, tm), lambda b, tl: (0, b))
    seg_w = seg_table.shape[-1]
    return pl.pallas_call(
        functools.partial(_dispatch_body, n_tok_blocks=nb),
        grid_spec=pltpu.PrefetchScalarGridSpec(
            num_scalar_prefetch=1,
            grid=(nb,),
            in_specs=[pl.BlockSpec((None, 1, seg_w), lambda b, tl: (b, 0, 0), memory_space=pltpu.SMEM),
                      pl.BlockSpec((None, 1, seg_w), lambda b, tl: (jnp.maximum(b - 1, 0), 0, 0),
                                   memory_space=pltpu.SMEM),
                      pl.BlockSpec((tm, D_MODEL), lambda b, tl: (b, 0)),
                      tok, tok,
                      pl.BlockSpec((None, N_EXPERTS, 1), lambda b, tl: (b, 0, 0))],
            out_specs=[pl.BlockSpec(memory_space=pl.ANY), tok],
            scratch_shapes=[pltpu.VMEM((2, R, HALF), U32), pltpu.VMEM((EXPERT_ROWS, HALF), U32),
                            pltpu.SemaphoreType.DMA((2,)), pltpu.SemaphoreType.DMA(())]),
        out_shape=[jax.ShapeDtypeStruct((n_rows, HALF), U32), jax.ShapeDtypeStruct((TOP_K, T), I32)],
        compiler_params=_params("arbitrary"),
        name="moe_dispatch",
    )(tail_start, seg_table, seg_table, x, idx, rank, loff)


def _ffn_body(be_ref, nu_ref, xs_ref, w1_ref, b1_ref, w2_ref, b2_ref, ys_ref, w1b_ref, w2b_ref):
    i = pl.program_id(0)
    used = i < nu_ref[0]
    fresh = jnp.logical_or(i == 0, be_ref[i] != be_ref[jnp.maximum(i - 1, 0)])

    @pl.when(jnp.logical_and(used, fresh))
    def _():
        w1b_ref[...] = w1_ref[...].astype(BF16)
        w2b_ref[...] = w2_ref[...].astype(BF16)

    @pl.when(used)
    def _():
        lo, hi = _unpack_halves(xs_ref[...])
        h = (jnp.dot(lo.astype(BF16), w1b_ref[0:HALF, :], preferred_element_type=F32)
             + jnp.dot(hi.astype(BF16), w1b_ref[HALF:D_MODEL, :], preferred_element_type=F32) + b1_ref[...])
        hg = jnp.minimum(h[:, :D_FF], SWIGLU_LIMIT)
        hu = jnp.clip(h[:, D_FF:], -SWIGLU_LIMIT, SWIGLU_LIMIT)
        act = (hu + 1.0) * (hg * jax.nn.sigmoid(hg * SWIGLU_ALPHA))
        y = jnp.dot(act.astype(BF16), w2b_ref[...], preferred_element_type=F32) + b2_ref[...]
        ys_ref[...] = _pack_halves(y)


def _expert_ffn(xs, blk_expert, n_used, w1, b1, w2, b2, layer, n_blocks):
    rows = pl.BlockSpec((EXPERT_ROWS, HALF), lambda i, be, nu: (jnp.minimum(i, nu[0] - 1), 0))
    return pl.pallas_call(
        _ffn_body,
        grid_spec=pltpu.PrefetchScalarGridSpec(
            num_scalar_prefetch=2,
            grid=(n_blocks,),
            in_specs=[rows,
                      pl.BlockSpec((None, None, D_MODEL, 2 * D_FF), lambda i, be, nu: (layer, be[i], 0, 0)),
                      pl.BlockSpec((None, None, 1, 2 * D_FF), lambda i, be, nu: (layer, be[i], 0, 0)),
                      pl.BlockSpec((None, None, D_FF, D_MODEL), lambda i, be, nu: (layer, be[i], 0, 0)),
                      pl.BlockSpec((None, None, 1, D_MODEL), lambda i, be, nu: (layer, be[i], 0, 0))],
            out_specs=rows,
            scratch_shapes=[pltpu.VMEM((D_MODEL, 2 * D_FF), BF16), pltpu.VMEM((D_FF, D_MODEL), BF16)]),
        out_shape=jax.ShapeDtypeStruct((xs.shape[0], HALF), U32),
        compiler_params=_params("arbitrary"),
        name="moe_expert_ffn",
    )(blk_expert, n_used, xs, w1, b1, w2, b2)


def _combine_body(seg_ref, seg_next_ref, lrow_ref, gate_ref, x_ref, g_ref, beta_ref, ys_hbm, xo_ref,
                  buf_ref, sem, *, n_tok_blocks):
    b = pl.program_id(0)
    slot = b % 2
    tm = x_ref.shape[0]
    R = buf_ref.shape[1]

    def seg_copy(s, local_row, src_row, rows):
        return pltpu.make_async_copy(ys_hbm.at[pl.ds(pl.multiple_of(src_row, SEG_ALIGN), rows)],
                                     buf_ref.at[s, pl.ds(pl.multiple_of(local_row, SEG_ALIGN), rows)], sem.at[s])

    @pl.when(b == 0)
    def _():
        buf_ref[...] = jnp.zeros_like(buf_ref)
        _segment_copies(seg_ref, seg_copy, 0, wait=False)

    @pl.when(b + 1 < n_tok_blocks)
    def _():
        _segment_copies(seg_next_ref, seg_copy, 1 - slot, wait=False)

    _segment_copies(seg_ref, seg_copy, slot, wait=True)

    lo, hi = _unpack_halves(buf_ref[slot])
    cid = lax.broadcasted_iota(I32, (tm, R), 1)
    lrow = lrow_ref[...]
    gates = gate_ref[...]
    sel = jnp.zeros((tm, R), F32)
    for k in range(TOP_K):
        sel = jnp.where(cid == lrow[:, k:k + 1], gates[:, k:k + 1], sel)
    sel = sel.astype(BF16)
    f = jnp.concatenate([jnp.dot(sel, lo.astype(BF16), preferred_element_type=F32),
                         jnp.dot(sel, hi.astype(BF16), preferred_element_type=F32)], axis=1)
    xo_ref[...] = _layer_norm_rows(DEEPNORM_ALPHA * x_ref[...] + f, g_ref[...], beta_ref[...])


def _combine_ln(lrow_t, gate_t, x, g, beta, ys, seg_table):
    T = x.shape[0]
    tm = ROW_TILE
    nb = T // tm
    R = TOP_K * tm + N_EXPERTS * SEG_ALIGN
    seg_w = seg_table.shape[-1]
    return pl.pallas_call(
        functools.partial(_combine_body, n_tok_blocks=nb),
        grid=(nb,),
        in_specs=[pl.BlockSpec((None, 1, seg_w), lambda b: (b, 0, 0), memory_space=pltpu.SMEM),
                  pl.BlockSpec((None, 1, seg_w), lambda b: (jnp.minimum(b + 1, nb - 1), 0, 0),
                               memory_space=pltpu.SMEM),
                  pl.BlockSpec((tm, TOP_K), lambda b: (b, 0)),
                  pl.BlockSpec((tm, TOP_K), lambda b: (b, 0)),
                  pl.BlockSpec((tm, D_MODEL), lambda b: (b, 0)),
                  pl.BlockSpec((1, D_MODEL), lambda b: (0, 0)),
                  pl.BlockSpec((1, D_MODEL), lambda b: (0, 0)),
                  pl.BlockSpec(memory_space=pl.ANY)],
        out_specs=pl.BlockSpec((tm, D_MODEL), lambda b: (b, 0)),
        out_shape=jax.ShapeDtypeStruct((T, D_MODEL), F32),
        scratch_shapes=[pltpu.VMEM((2, R, HALF), U32), pltpu.SemaphoreType.DMA((2,))],
        compiler_params=_params("arbitrary"),
        name="moe_combine_ln",
    )(seg_table, seg_table, lrow_t, gate_t, x, g, beta, ys)


def _dft_tables(P):
    n2 = 4 * P
    k = jnp.arange(P, dtype=I32)
    m = ((2 * k[:, None] + 1) * k[None, :]) % n2
    ang = m.astype(F32) * F32(2.0 * math.pi / n2)
    fc32, fs32 = jnp.cos(ang), -jnp.sin(ang)
    scale = F32(1.0 / P)
    gc, gs = (fc32.T * scale).astype(BF16), (fs32.T * scale).astype(BF16)
    return fc32, fs32, (fc32.astype(BF16), fs32.astype(BF16), gc, gs)


def _hyena_positional(S):
    pos = jnp.arange(S, dtype=F32)
    t = jnp.linspace(0.0, 1.0, S, dtype=F32)[:, None]
    bands = (HYENA_EMB_DIM - 1) // 2
    f = jnp.linspace(1e-4, bands - 1, bands, dtype=F32)
    ang = (2.0 * math.pi / S) * pos[:, None] * f[None, :]
    feats = jnp.concatenate([t, jnp.cos(ang), -jnp.sin(ang)], axis=-1)
    feats = jnp.pad(feats, ((0, 0), (0, LANES - HYENA_EMB_DIM)))
    max_decay = math.log(HYENA_DECAY_TARGET) / HYENA_SHORT_DECAY_PCT
    min_decay = math.log(HYENA_DECAY_TARGET) / HYENA_LONG_DECAY_PCT
    deltas = jnp.abs(jnp.linspace(min_decay, max_decay, HYENA_CH, dtype=F32))[None, :]
    return feats, t, deltas


def _alibi_slopes():
    return jnp.asarray(np.array([2.0 ** (-8.0 * (i + 1) / N_HEADS) for i in range(N_HEADS)], dtype=np.float32))


def _even_mixer(x, xshape, tabs, w_in, b_in, short_w, short_b, f1_w, f1_b, f1_freq, f2_w, f2_b, f2_freq, f3_w,
                skip, dw_w, dw_b, cln_g, cln_b, w_out, b_out, ln_g, ln_b):
    B, S = xshape
    fc32, fs32, tabs16 = tabs
    proj = _project(x, w_in.astype(BF16), b_in[None, :])
    hy = _short_conv(proj, short_w, short_b[None, :], B, S)
    u = _conformer(proj, dw_w, dw_b[None, :], cln_g[None, :], cln_b[None, :], B, S)
    feats, tcol, deltas = _hyena_positional(S)
    f1_wp = jnp.pad(f1_w, ((0, LANES - HYENA_EMB_DIM), (0, 0)))
    taps = _hyena_filters(feats, f1_wp, f1_b[None, :], f1_freq[None, :], f2_w, f2_b[None, :], f2_freq[None, :],
                          f3_w, tcol, deltas)
    hre, him = _filter_spectrum(fc32, fs32, taps, S)
    z = _long_conv(hy, 2, hy, 0, tabs16, hre, him, 0, skip, B, S)
    z = _long_conv(z, 0, hy, 1, tabs16, hre, him, 1, skip, B, S)
    return _outproj_ln(z, 0, u, 0, w_out.astype(BF16), b_out[None, :], x, ln_g[None, :], ln_b[None, :])


def _odd_mixer(x, xshape, layer_idx, w_qkv, lq1, lk1, lq2, lk2, subln_g, w_out, ln_g, ln_b):
    B, S = xshape
    lam_init = 0.8 - 0.6 * math.exp(-0.3 * layer_idx)
    lam = (jnp.exp(jnp.sum(lq1 * lk1)) - jnp.exp(jnp.sum(lq2 * lk2)) + lam_init).reshape(1)
    q_scale = jnp.concatenate([jnp.full((ATTN_W,), HEAD_DIM ** -0.5, F32), jnp.ones((2 * ATTN_W,), F32)])
    w = (w_qkv * q_scale).astype(BF16)
    qkv = _project(x, w, jnp.zeros((1, 3 * ATTN_W), F32))
    o = _diff_attention(qkv, _alibi_slopes(), lam, subln_g[None, :], lam_init, B, S)
    return _outproj_ln(o, 0, o, 1, w_out.astype(BF16), jnp.zeros((1, D_MODEL), F32), x, ln_g[None, :], ln_b[None, :])


def _round_up(a, m):
    return (a + m - 1) // m * m


def _routing_tables(cnt_blocks, n_ffn_blocks):
    E = N_EXPERTS
    cnt8 = _round_up(cnt_blocks[:, :, 0].astype(I32), SEG_ALIGN)
    seg_end = jnp.cumsum(cnt8, axis=1)
    loff = seg_end - cnt8
    tot8 = jnp.sum(cnt8, axis=0)
    group = _round_up(tot8, EXPERT_ROWS)
    group_end = jnp.cumsum(group)
    group_start = group_end - group
    goff = group_start[None, :] + jnp.cumsum(cnt8, axis=0) - cnt8
    seg_table = jnp.concatenate([loff, goff, cnt8 // SEG_ALIGN], axis=1)
    starts = jnp.arange(n_ffn_blocks, dtype=I32) * EXPERT_ROWS
    blk_expert = jnp.minimum(jnp.sum((group_end[None, :] <= starts[:, None]).astype(I32), axis=1), E - 1)
    n_used = group_end[-1:] // EXPERT_ROWS
    tail_start = group_start + tot8
    return loff[:, :, None], seg_table[:, None, :], blk_expert, n_used, tail_start


def _moe_layer(x, layer, w_r, b_r, w1, b1, w2, b2, ln_g, ln_b):
    T = x.shape[0]
    nb = T // ROW_TILE
    n_rows = _round_up(T * TOP_K + nb * N_EXPERTS * (SEG_ALIGN - 1), EXPERT_ROWS) + N_EXPERTS * EXPERT_ROWS
    n_ffn_blocks = n_rows // EXPERT_ROWS
    idx, gate, rank, cnt = _router(x, w_r.T, b_r[:, None])
    loff, seg_table, blk_expert, n_used, tail_start = _routing_tables(cnt, n_ffn_blocks)
    xs, lrow = _dispatch(x, idx, rank, loff, seg_table, tail_start, n_rows + EXPERT_ROWS)
    ys = _expert_ffn(xs, blk_expert, n_used, w1, b1[:, :, None, :], w2, b2[:, :, None, :], layer, n_ffn_blocks)
    return _combine_ln(lrow.T, gate.T, x, ln_g[None, :], ln_b[None, :], ys, seg_table)


def kernel(x, hy_cf_w_in, hy_cf_b_in, hy_short_w, hy_short_b, hy_f1_w, hy_f1_b, hy_f1_freq, hy_f2_w, hy_f2_b, hy_f2_freq, hy_f3_w, hy_skip, cf_dw_w, cf_dw_b, cf_ln_g, cf_ln_b, even_w_out, even_b_out, attn_w_qkv, attn_lq1, attn_lk1, attn_lq2, attn_lk2, attn_subln_g, attn_w_out, ln1_g, ln1_b, ln2_g, ln2_b, moe_w_r, moe_b_r, moe_w1, moe_b1, moe_w2, moe_b2):
    B, S, D = x.shape
    assert D == D_MODEL and (B * S) % ROW_TILE == 0 and S % LANES == 0
    depth = ln1_g.shape[0]
    xf = x.reshape(B * S, D)
    tabs = _dft_tables(S // CONV_BLOCKS)
    for i in range(depth):
        j = i // 2
        if i % 2 == 0:
            xf = _even_mixer(xf, (B, S), tabs, hy_cf_w_in[j], hy_cf_b_in[j], hy_short_w[j], hy_short_b[j],
                                 hy_f1_w[j], hy_f1_b[j], hy_f1_freq[j], hy_f2_w[j], hy_f2_b[j], hy_f2_freq[j],
                                 hy_f3_w[j], hy_skip[j], cf_dw_w[j], cf_dw_b[j], cf_ln_g[j], cf_ln_b[j],
                                 even_w_out[j], even_b_out[j], ln1_g[i], ln1_b[i])
        else:
            xf = _odd_mixer(xf, (B, S), i, attn_w_qkv[j], attn_lq1[j], attn_lk1[j], attn_lq2[j], attn_lk2[j],
                            attn_subln_g[j], attn_w_out[j], ln1_g[i], ln1_b[i])
        xf = _moe_layer(xf, i, moe_w_r[i], moe_b_r[i], moe_w1, moe_b1, moe_w2, moe_b2, ln2_g[i], ln2_b[i])
    return xf.reshape(B, S, D)
```

```python
import functools
import math

import jax
import jax.numpy as jnp
import numpy as np
from jax import lax
from jax.experimental import pallas as pl
from jax.experimental.pallas import tpu as pltpu

F32 = jnp.float32
BF16 = jnp.bfloat16
U32 = jnp.uint32
I32 = jnp.int32

D_MODEL = 1024
HALF = D_MODEL // 2
DEPTH = 4
HYENA_CH = D_MODEL // 2
CONF_CH = D_MODEL // 2
HYENA_ORDER = 2
HYENA_EMB_DIM = 33
HYENA_FILTER_DIM = 64
HYENA_SHORT_DECAY_PCT = 0.3
HYENA_LONG_DECAY_PCT = 1.5
HYENA_DECAY_TARGET = 1e-2
CONF_WIDTH = 31
EVEN_IN = 3 * HYENA_CH + 2 * CONF_CH
N_HEADS = 8
HEAD_DIM = 64
ATTN_W = N_HEADS * 2 * HEAD_DIM
N_EXPERTS = 32
TOP_K = 4
D_FF = D_MODEL
SWIGLU_LIMIT = 7.0
SWIGLU_ALPHA = 1.702
DEEPNORM_ALPHA = (2 * DEPTH) ** 0.25
LN_EPS = 1e-5

LANES = 128
SUBLANES = 8
VMEM_LIMIT_BYTES = 56 * 1024 * 1024
ROW_TILE = 512
EXPERT_ROWS = 512
FFN_ROW_CHUNK = 256
SEG_ALIGN = 8
COPY_ROWS = (64, 32, 16, 8)
CONV_BLOCKS = 4
CONV_CH_TILE = 256
ATTN_Q_TILE = 2048
ATTN_ROW_CHUNK = 256
CONV_PAD = 16

_NT = (((1,), (1,)), ((), ()))


def _params(*sem):
    return pltpu.CompilerParams(dimension_semantics=sem, vmem_limit_bytes=VMEM_LIMIT_BYTES)


def _split_bf16(a):
    hi = a.astype(BF16)
    lo = (a - hi.astype(F32)).astype(BF16)
    return hi, lo


def _dot3(a, b):
    a_hi, a_lo = _split_bf16(a)
    b_hi, b_lo = _split_bf16(b)
    d = functools.partial(jnp.dot, preferred_element_type=F32)
    return d(a_hi, b_hi) + d(a_hi, b_lo) + d(a_lo, b_hi)


def _layer_norm_rows(y, g, b):
    mu = jnp.mean(y, axis=-1, keepdims=True)
    yc = y - mu
    var = jnp.mean(yc * yc, axis=-1, keepdims=True)
    return yc * lax.rsqrt(var + LN_EPS) * g + b


def _pack_halves(y):
    lo = lax.bitcast_convert_type(y[:, :HALF].astype(BF16).astype(F32), U32)
    hi = lax.bitcast_convert_type(y[:, HALF:].astype(BF16).astype(F32), U32)
    return hi | (lo >> 16)


def _unpack_halves(p):
    lo = lax.bitcast_convert_type(p << 16, F32)
    hi = lax.bitcast_convert_type(p & jnp.uint32(0xFFFF0000), F32)
    return lo, hi


def _proj_body(x_ref, w_ref, b_ref, o_ref, *, col_chunk):
    x = x_ref[...].astype(BF16)
    for j in range(0, o_ref.shape[1], col_chunk):
        acc = jnp.dot(x, w_ref[:, j:j + col_chunk], preferred_element_type=F32)
        o_ref[:, j:j + col_chunk] = (acc + b_ref[:, j:j + col_chunk]).astype(o_ref.dtype)


def _project(x, w, b):
    T, K = x.shape
    N = w.shape[1]
    return pl.pallas_call(
        functools.partial(_proj_body, col_chunk=512),
        grid=(T // ROW_TILE,),
        in_specs=[pl.BlockSpec((ROW_TILE, K), lambda i: (i, 0)),
                  pl.BlockSpec((K, N), lambda i: (0, 0)),
                  pl.BlockSpec((1, N), lambda i: (0, 0))],
        out_specs=pl.BlockSpec((ROW_TILE, N), lambda i: (i, 0)),
        out_shape=jax.ShapeDtypeStruct((T, N), BF16),
        compiler_params=_params("parallel"),
        name="project",
    )(x, w, b)


def _outproj_ln_body(a1_ref, a2_ref, w1_ref, w2_ref, b_ref, x_ref, g_ref, beta_ref, xo_ref):
    m = (jnp.dot(a1_ref[...], w1_ref[...], preferred_element_type=F32)
         + jnp.dot(a2_ref[...], w2_ref[...], preferred_element_type=F32) + b_ref[...])
    xo_ref[...] = _layer_norm_rows(DEEPNORM_ALPHA * x_ref[...] + m, g_ref[...], beta_ref[...])


def _outproj_ln(a1, a1_col, a2, a2_col, w, b, x, g, beta):
    T = x.shape[0]
    return pl.pallas_call(
        _outproj_ln_body,
        grid=(T // ROW_TILE,),
        in_specs=[pl.BlockSpec((ROW_TILE, HALF), lambda i: (i, a1_col)),
                  pl.BlockSpec((ROW_TILE, HALF), lambda i: (i, a2_col)),
                  pl.BlockSpec((HALF, D_MODEL), lambda i: (0, 0)),
                  pl.BlockSpec((HALF, D_MODEL), lambda i: (1, 0)),
                  pl.BlockSpec((1, D_MODEL), lambda i: (0, 0)),
                  pl.BlockSpec((ROW_TILE, D_MODEL), lambda i: (i, 0)),
                  pl.BlockSpec((1, D_MODEL), lambda i: (0, 0)),
                  pl.BlockSpec((1, D_MODEL), lambda i: (0, 0))],
        out_specs=pl.BlockSpec((ROW_TILE, D_MODEL), lambda i: (i, 0)),
        out_shape=jax.ShapeDtypeStruct((T, D_MODEL), F32),
        compiler_params=_params("parallel"),
        name="outproj_ln",
    )(a1, a2, w, w, b, x, g, beta)


def _short_conv_body(x_ref, w_ref, b_ref, o_ref):
    x = x_ref[...].astype(F32)
    S = x.shape[0]
    row = lax.broadcasted_iota(I32, x.shape, 0)
    prev = jnp.where(row == 0, 0.0, pltpu.roll(x, 1, 0))
    nxt = jnp.where(row == S - 1, 0.0, pltpu.roll(x, S - 1, 0))
    y = w_ref[0:1, :] * prev + w_ref[1:2, :] * x + w_ref[2:3, :] * nxt + b_ref[...]
    o_ref[...] = y.astype(o_ref.dtype)


def _short_conv(proj, w, b, B, S):
    T = B * S
    C = HYENA_CH
    return pl.pallas_call(
        _short_conv_body,
        grid=(B, 3),
        in_specs=[pl.BlockSpec((S, C), lambda bi, j: (bi, j)),
                  pl.BlockSpec((3, C), lambda bi, j: (0, j)),
                  pl.BlockSpec((1, C), lambda bi, j: (0, j))],
        out_specs=pl.BlockSpec((S, C), lambda bi, j: (bi, j)),
        out_shape=jax.ShapeDtypeStruct((T, 3 * C), BF16),
        compiler_params=_params("parallel", "parallel"),
        name="hyena_short_conv",
    )(proj, w, b)


def _conformer_body(a_ref, g_ref, w_ref, b_ref, lg_ref, lb_ref, o_ref, pad_ref, sh_ref, *, row_chunk):
    S, C = a_ref.shape
    zeros = jnp.zeros((CONV_PAD, C), F32)
    pad_ref[0:CONV_PAD, :] = zeros
    pad_ref[CONV_PAD + S:CONV_PAD + S + CONV_PAD, :] = zeros
    a = a_ref[...].astype(F32)
    g = g_ref[...].astype(F32)
    pad_ref[CONV_PAD:CONV_PAD + S, :] = a * jax.nn.sigmoid(g)
    first = CONV_PAD - CONF_WIDTH // 2
    span = sh_ref.shape[1]
    for r0 in range(0, S, row_chunk):
        for ph in range(1, SUBLANES):
            sh_ref[ph - 1] = pad_ref[r0 + ph:r0 + ph + span, :]
        acc = jnp.zeros((row_chunk, C), F32) + b_ref[...]
        for j in range(CONF_WIDTH):
            q, ph = divmod(first + j, SUBLANES)
            if ph == 0:
                win = pad_ref[r0 + q * SUBLANES:r0 + q * SUBLANES + row_chunk, :]
            else:
                win = sh_ref[ph - 1, q * SUBLANES:q * SUBLANES + row_chunk, :]
            acc = acc + w_ref[j:j + 1, :] * win
        y = _layer_norm_rows(acc, lg_ref[...], lb_ref[...])
        o_ref[r0:r0 + row_chunk, :] = (y * jax.nn.sigmoid(y)).astype(o_ref.dtype)


def _conformer(proj, w, b, lg, lb, B, S):
    T = B * S
    C = CONF_CH
    rc = min(S, 256)
    return pl.pallas_call(
        functools.partial(_conformer_body, row_chunk=rc),
        grid=(B,),
        in_specs=[pl.BlockSpec((S, C), lambda bi: (bi, 3)),
                  pl.BlockSpec((S, C), lambda bi: (bi, 4)),
                  pl.BlockSpec((CONF_WIDTH, C), lambda bi: (0, 0)),
                  pl.BlockSpec((1, C), lambda bi: (0, 0)),
                  pl.BlockSpec((1, C), lambda bi: (0, 0)),
                  pl.BlockSpec((1, C), lambda bi: (0, 0))],
        out_specs=pl.BlockSpec((S, C), lambda bi: (bi, 0)),
        out_shape=jax.ShapeDtypeStruct((T, C), BF16),
        scratch_shapes=[pltpu.VMEM((S + 2 * CONV_PAD, C), F32),
                        pltpu.VMEM((SUBLANES - 1, rc + _round_up(CONF_WIDTH, SUBLANES) - SUBLANES, C), F32)],
        compiler_params=_params("parallel"),
        name="conformer_conv",
    )(proj, proj, w, b, lg, lb)


def _filter_body(feat_ref, w1_ref, b1_ref, q1_ref, w2_ref, b2_ref, q2_ref, w3_ref, t_ref, delta_ref, o_ref):
    h = jnp.sin(q1_ref[...] * (_dot3(feat_ref[...], w1_ref[...]) + b1_ref[...]))
    h = jnp.sin(q2_ref[...] * (_dot3(h, w2_ref[...]) + b2_ref[...]))
    h = _dot3(h, w3_ref[...])
    o_ref[...] = h * jnp.exp(-t_ref[...] * delta_ref[...])


def _hyena_filters(feats, w1, b1, q1, w2, b2, q2, w3, tcol, deltas):
    S = feats.shape[0]
    C = HYENA_CH
    n = 2 * HYENA_ORDER
    fd = HYENA_FILTER_DIM
    return pl.pallas_call(
        _filter_body,
        grid=(n,),
        in_specs=[pl.BlockSpec((S, LANES), lambda j: (0, 0)),
                  pl.BlockSpec((LANES, fd), lambda j: (0, 0)),
                  pl.BlockSpec((1, fd), lambda j: (0, 0)),
                  pl.BlockSpec((1, fd), lambda j: (0, 0)),
                  pl.BlockSpec((fd, fd), lambda j: (0, 0)),
                  pl.BlockSpec((1, fd), lambda j: (0, 0)),
                  pl.BlockSpec((1, fd), lambda j: (0, 0)),
                  pl.BlockSpec((fd, C), lambda j: (0, j)),
                  pl.BlockSpec((S, 1), lambda j: (0, 0)),
                  pl.BlockSpec((1, C), lambda j: (0, 0))],
        out_specs=pl.BlockSpec((S, C), lambda j: (0, j)),
        out_shape=jax.ShapeDtypeStruct((S, n * C), F32),
        compiler_params=_params("parallel"),
        name="hyena_filter_mlp",
    )(feats, w1, b1, q1, w2, b2, q2, w3, tcol, deltas)


def _spectrum_body(fc_ref, fs_ref, a_ref, b_ref, hre_ref, him_ref):
    a = a_ref[...]
    row = lax.broadcasted_iota(I32, a.shape, 0)
    b = jnp.where(row == 0, 0.0, b_ref[...])
    hre_ref[...] = _dot3(fc_ref[...], a + b)
    him_ref[...] = _dot3(fs_ref[...], a - b)


def _filter_spectrum(fc32, fs32, taps, S):
    P = fc32.shape[0]
    n = S // P
    C = HYENA_CH
    oc = HYENA_ORDER * C
    fwd, bwd = taps[:, :oc], taps[:, oc:]
    zero = jnp.zeros((1, oc), F32)
    h_up = jnp.concatenate([zero, bwd[:0:-1], fwd], axis=0)
    h_down = jnp.concatenate([zero, h_up[:0:-1]], axis=0)
    spec = pl.BlockSpec((P, C), lambda di, o: (di, o))
    return pl.pallas_call(
        _spectrum_body,
        grid=(2 * n - 1, HYENA_ORDER),
        in_specs=[pl.BlockSpec((P, P), lambda di, o: (0, 0)),
                  pl.BlockSpec((P, P), lambda di, o: (0, 0)),
                  pl.BlockSpec((P, C), lambda di, o: (di + 1, o)),
                  pl.BlockSpec((P, C), lambda di, o: (2 * n - 1 - di, o))],
        out_specs=[spec, spec],
        out_shape=[jax.ShapeDtypeStruct(((2 * n - 1) * P, oc), F32)] * 2,
        compiler_params=_params("parallel", "parallel"),
        name="hyena_filter_spectrum",
    )(fc32, fs32, h_up, h_down)


def _long_conv_body(v_ref, gate_ref, fc_ref, fs_ref, hre_ref, him_ref, gc_ref, gs_ref, skip_ref, o_ref, *, n_blk):
    P = fc_ref.shape[0]
    fc, fs, gc, gs = fc_ref[...], fs_ref[...], gc_ref[...], gs_ref[...]
    vre, vim = [], []
    for j in range(n_blk):
        vj = v_ref[j * P:(j + 1) * P, :]
        vre.append(jnp.dot(fc, vj, preferred_element_type=F32))
        vim.append(jnp.dot(fs, vj, preferred_element_type=F32))
    for i in range(n_blk):
        yre = yim = None
        for j in range(n_blk):
            r0 = (i - j + n_blk - 1) * P
            hre = hre_ref[r0:r0 + P, :]
            him = him_ref[r0:r0 + P, :]
            tre = vre[j] * hre - vim[j] * him
            tim = vre[j] * him + vim[j] * hre
            yre = tre if yre is None else yre + tre
            yim = tim if yim is None else yim + tim
        y = (jnp.dot(gc, yre.astype(BF16), preferred_element_type=F32)
             + jnp.dot(gs, yim.astype(BF16), preferred_element_type=F32))
        rows = slice(i * P, (i + 1) * P)
        y = y + v_ref[rows, :].astype(F32) * skip_ref[...]
        o_ref[rows, :] = (gate_ref[rows, :].astype(F32) * y).astype(o_ref.dtype)


def _long_conv(v_arr, v_col, gate_arr, gate_col, tabs, hre, him, order, skip, B, S):
    fc, fs, gc, gs = tabs
    P = fc.shape[0]
    T = B * S
    C = HYENA_CH
    nc = C // CONV_CH_TILE
    cc = CONV_CH_TILE
    n_h = hre.shape[0]
    tab = pl.BlockSpec((P, P), lambda bi, c: (0, 0))
    return pl.pallas_call(
        functools.partial(_long_conv_body, n_blk=S // P),
        grid=(B, nc),
        in_specs=[pl.BlockSpec((S, cc), lambda bi, c: (bi, v_col * nc + c)),
                  pl.BlockSpec((S, cc), lambda bi, c: (bi, gate_col * nc + c)),
                  tab, tab,
                  pl.BlockSpec((n_h, cc), lambda bi, c: (0, order * nc + c)),
                  pl.BlockSpec((n_h, cc), lambda bi, c: (0, order * nc + c)),
                  tab, tab,
                  pl.BlockSpec((1, cc), lambda bi, c: (0, c))],
        out_specs=pl.BlockSpec((S, cc), lambda bi, c: (bi, c)),
        out_shape=jax.ShapeDtypeStruct((T, C), BF16),
        compiler_params=_params("parallel", "parallel"),
        name="hyena_long_conv",
    )(v_arr, gate_arr, fc, fs, hre, him, gc, gs, skip[order][None, :])


def _attn_body(slope_ref, lam_ref, q_ref, k_ref, v_ref, g_ref, o_ref, vaug_ref, *, lam_init, row_chunk):
    h = pl.program_id(1)
    qi = pl.program_id(2)
    tq = q_ref.shape[0]
    S = k_ref.shape[0]
    hw = 2 * HEAD_DIM
    k = k_ref[...]
    slope = slope_ref[h]
    kpos = lax.broadcasted_iota(I32, (1, S), 1).astype(F32) * slope

    @pl.when(qi == 0)
    def _():
        vaug_ref[:, :hw] = v_ref[...]
        vaug_ref[:, hw:] = jnp.where(lax.broadcasted_iota(I32, (S, hw), 1) == 0, 1.0, 0.0).astype(BF16)

    v_aug = vaug_ref[...]

    for r0 in range(0, tq, row_chunk):
        q = q_ref[r0:r0 + row_chunk, :]
        lane = lax.broadcasted_iota(I32, q.shape, 1)
        zero = jnp.zeros_like(q)
        qpos = (qi * tq + r0 + lax.broadcasted_iota(I32, (row_chunk, 1), 0)).astype(F32) * slope
        bias = lax.bitcast_convert_type(lax.bitcast_convert_type(qpos - kpos, U32) | jnp.uint32(0x80000000), F32)

        def weighted_values(qh):
            s = lax.dot_general(qh, k, _NT, preferred_element_type=F32) + bias
            e = jnp.exp((s - jnp.max(s, axis=-1, keepdims=True)).astype(BF16))
            return jnp.dot(e, v_aug, preferred_element_type=F32)

        o1 = weighted_values(jnp.where(lane < HEAD_DIM, q, zero))
        o2 = weighted_values(jnp.where(lane >= HEAD_DIM, q, zero))
        o = o1[:, :hw] * (1.0 / o1[:, hw:hw + 1]) - o2[:, :hw] * (lam_ref[0] / o2[:, hw:hw + 1])
        o = o * lax.rsqrt(jnp.mean(o * o, axis=-1, keepdims=True) + LN_EPS) * g_ref[...]
        o_ref[r0:r0 + row_chunk, :] = (o * (1.0 - lam_init)).astype(o_ref.dtype)


def _diff_attention(qkv, slopes, lam, subln_g, lam_init, B, S):
    T = B * S
    hw = 2 * HEAD_DIM
    tq = min(S, ATTN_Q_TILE)
    nq = S // tq
    smem = pl.BlockSpec(memory_space=pltpu.SMEM)
    return pl.pallas_call(
        functools.partial(_attn_body, lam_init=lam_init, row_chunk=min(tq, ATTN_ROW_CHUNK)),
        grid=(B, N_HEADS, nq),
        in_specs=[smem, smem,
                  pl.BlockSpec((tq, hw), lambda bi, h, qi: (bi * nq + qi, h)),
                  pl.BlockSpec((S, hw), lambda bi, h, qi: (bi, N_HEADS + h)),
                  pl.BlockSpec((S, hw), lambda bi, h, qi: (bi, 2 * N_HEADS + h)),
                  pl.BlockSpec((1, hw), lambda bi, h, qi: (0, 0))],
        out_specs=pl.BlockSpec((tq, hw), lambda bi, h, qi: (bi * nq + qi, h)),
        out_shape=jax.ShapeDtypeStruct((T, ATTN_W), BF16),
        scratch_shapes=[pltpu.VMEM((S, 2 * hw), BF16)],
        compiler_params=_params("parallel", "parallel", "arbitrary"),
        name="diff_attention",
    )(slopes, lam, qkv, qkv, qkv, subln_g)


def _router_body(x_ref, wt_ref, b_ref, idx_ref, gate_ref, rank_ref, cnt_ref):
    E = N_EXPERTS
    tm = x_ref.shape[0]
    x_hi, x_lo = _split_bf16(x_ref[...])
    w_hi, w_lo = _split_bf16(wt_ref[...])
    nt = functools.partial(lax.dot_general, dimension_numbers=_NT, preferred_element_type=F32)
    logits = nt(w_hi, x_hi) + nt(w_lo, x_hi) + nt(w_hi, x_lo) + b_ref[...]

    eid = lax.broadcasted_iota(I32, (E, tm), 0).astype(F32)
    work = logits
    vals, idxs = [], []
    for _ in range(TOP_K):
        m = jnp.max(work, axis=0, keepdims=True)
        sel = jnp.min(jnp.where(work == m, eid, float(E)), axis=0, keepdims=True)
        vals.append(m)
        idxs.append(sel)
        work = jnp.where(eid == sel, -jnp.inf, work)
    exps = [jnp.exp(v - vals[0]) for v in vals]
    denom = exps[0] + exps[1] + exps[2] + exps[3]

    chosen = jnp.zeros((E, tm), F32)
    for sel in idxs:
        chosen = chosen + jnp.where(eid == sel, 1.0, 0.0)
    earlier = jnp.where(lax.broadcasted_iota(I32, (tm, tm), 0) < lax.broadcasted_iota(I32, (tm, tm), 1), 1.0, 0.0)
    before = jnp.dot(chosen.astype(BF16), earlier.astype(BF16), preferred_element_type=F32)
    for k in range(TOP_K):
        gate_ref[k:k + 1, :] = exps[k] / denom
        idx_ref[k:k + 1, :] = idxs[k].astype(I32)
        rank_ref[k:k + 1, :] = jnp.sum(jnp.where(eid == idxs[k], before, 0.0), axis=0, keepdims=True).astype(I32)
    cnt_ref[...] = jnp.broadcast_to(jnp.sum(chosen, axis=1, keepdims=True), cnt_ref.shape)


def _router(x, w_rt, b_r):
    T = x.shape[0]
    E = N_EXPERTS
    tm = ROW_TILE
    tok = pl.BlockSpec((TOP_K, tm), lambda i: (0, i))
    return pl.pallas_call(
        _router_body,
        grid=(T // tm,),
        in_specs=[pl.BlockSpec((tm, D_MODEL), lambda i: (i, 0)),
                  pl.BlockSpec((E, D_MODEL), lambda i: (0, 0)),
                  pl.BlockSpec((E, 1), lambda i: (0, 0))],
        out_specs=[tok, tok, tok, pl.BlockSpec((None, E, LANES), lambda i: (i, 0, 0))],
        out_shape=[jax.ShapeDtypeStruct((TOP_K, T), I32), jax.ShapeDtypeStruct((TOP_K, T), F32),
                   jax.ShapeDtypeStruct((TOP_K, T), I32), jax.ShapeDtypeStruct((T // tm, E, LANES), F32)],
        compiler_params=_params("parallel"),
        name="moe_router",
    )(x, w_rt, b_r)


def _copy_caps():
    local_rows = TOP_K * ROW_TILE + N_EXPERTS * SEG_ALIGN
    return (local_rows // COPY_ROWS[0],) + (N_EXPERTS,) * (len(COPY_ROWS) - 1)


def _segment_copies(tab_ref, make_copy, slot, wait):
    base = len(COPY_ROWS)
    for ci, (rows, cap) in enumerate(zip(COPY_ROWS, _copy_caps())):
        def body(p, carry, base=base, rows=rows, cap=cap):
            copy = make_copy(slot, tab_ref[0, base + p], tab_ref[0, base + cap + p], rows)
            if wait:
                copy.wait()
            else:
                copy.start()
            return carry
        lax.fori_loop(0, tab_ref[0, ci], body, 0)
        base += 2 * cap


def _dispatch_body(tail_ref, seg_ref, seg_prev_ref, x_ref, idx_ref, rank_ref, loff_ref, xs_hbm, lrow_ref,
                   buf_ref, zero_ref, sem, zsem, *, n_tok_blocks):
    b = pl.program_id(0)
    slot = b % 2
    tm = x_ref.shape[0]
    R = buf_ref.shape[1]

    def seg_copy(s, local_row, dst_row, rows):
        return pltpu.make_async_copy(buf_ref.at[s, pl.ds(pl.multiple_of(local_row, SEG_ALIGN), rows)],
                                     xs_hbm.at[pl.ds(pl.multiple_of(dst_row, SEG_ALIGN), rows)], sem.at[s])

    @pl.when(b == 0)
    def _():
        zero_ref[...] = jnp.zeros_like(zero_ref)
        for e in range(N_EXPERTS):
            fill = pltpu.make_async_copy(
                zero_ref, xs_hbm.at[pl.ds(pl.multiple_of(tail_ref[e], SEG_ALIGN), EXPERT_ROWS)], zsem)
            fill.start()
            fill.wait()

    eid = lax.broadcasted_iota(I32, (N_EXPERTS, tm), 0)
    loff = loff_ref[...].astype(F32)
    rid = lax.broadcasted_iota(I32, (R, tm), 0)
    sel_t = jnp.zeros((R, tm), F32)
    for k in range(TOP_K):
        base = jnp.sum(jnp.where(eid == idx_ref[k:k + 1, :], loff, 0.0), axis=0, keepdims=True).astype(I32)
        row = base + rank_ref[k:k + 1, :]
        lrow_ref[k:k + 1, :] = row
        sel_t = jnp.where(rid == row, 1.0, sel_t)
    sel_t = sel_t.astype(BF16)
    xb = x_ref[...].astype(BF16)
    lo = jnp.dot(sel_t, xb[:, :HALF], preferred_element_type=F32)
    hi = jnp.dot(sel_t, xb[:, HALF:], preferred_element_type=F32)
    packed = (lax.bitcast_convert_type(hi, U32) & jnp.uint32(0xFFFF0000)) | (lax.bitcast_convert_type(lo, U32) >> 16)
    buf_ref[slot] = packed

    @pl.when(b >= 1)
    def _():
        _segment_copies(seg_prev_ref, seg_copy, 1 - slot, wait=True)
    _segment_copies(seg_ref, seg_copy, slot, wait=False)

    @pl.when(b == n_tok_blocks - 1)
    def _():
        _segment_copies(seg_ref, seg_copy, slot, wait=True)


def _dispatch(x, idx, rank, loff, seg_table, tail_start, n_rows):
    T = x.shape[0]
    tm = ROW_TILE
    nb = T // tm
    R = TOP_K * tm + N_EXPERTS * SEG_ALIGN
    tok = pl.BlockSpec((TOP_K, tm), lambda b, tl: (0, b))
    seg_w = seg_table.shape[-1]
    return pl.pallas_call(
        functools.partial(_dispatch_body, n_tok_blocks=nb),
        grid_spec=pltpu.PrefetchScalarGridSpec(
            num_scalar_prefetch=1,
            grid=(nb,),
            in_specs=[pl.BlockSpec((None, 1, seg_w), lambda b, tl: (b, 0, 0), memory_space=pltpu.SMEM),
                      pl.BlockSpec((None, 1, seg_w), lambda b, tl: (jnp.maximum(b - 1, 0), 0, 0),
                                   memory_space=pltpu.SMEM),
                      pl.BlockSpec((tm, D_MODEL), lambda b, tl: (b, 0)),
                      tok, tok,
                      pl.BlockSpec((None, N_EXPERTS, 1), lambda b, tl: (b, 0, 0))],
            out_specs=[pl.BlockSpec(memory_space=pl.ANY), tok],
            scratch_shapes=[pltpu.VMEM((2, R, HALF), U32), pltpu.VMEM((EXPERT_ROWS, HALF), U32),
                            pltpu.SemaphoreType.DMA((2,)), pltpu.SemaphoreType.DMA(())]),
        out_shape=[jax.ShapeDtypeStruct((n_rows, HALF), U32), jax.ShapeDtypeStruct((TOP_K, T), I32)],
        compiler_params=_params("arbitrary"),
        name="moe_dispatch",
    )(tail_start, seg_table, seg_table, x, idx, rank, loff)


def _ffn_body(be_ref, nu_ref, xs_ref, w1_ref, b1_ref, w2_ref, b2_ref, ys_ref, w1b_ref, w2b_ref):
    i = pl.program_id(0)
    used = i < nu_ref[0]
    fresh = jnp.logical_or(i == 0, be_ref[i] != be_ref[jnp.maximum(i - 1, 0)])

    @pl.when(jnp.logical_and(used, fresh))
    def _():
        w1b_ref[...] = w1_ref[...].astype(BF16)
        w2b_ref[...] = w2_ref[...].astype(BF16)

    @pl.when(used)
    def _():
        for r0 in range(0, EXPERT_ROWS, FFN_ROW_CHUNK):
            rows = slice(r0, r0 + FFN_ROW_CHUNK)
            lo, hi = _unpack_halves(xs_ref[rows, :])
            h = (jnp.dot(lo.astype(BF16), w1b_ref[0:HALF, :], preferred_element_type=F32)
                 + jnp.dot(hi.astype(BF16), w1b_ref[HALF:D_MODEL, :], preferred_element_type=F32) + b1_ref[...])
            hg = jnp.minimum(h[:, :D_FF], SWIGLU_LIMIT)
            hu = jnp.clip(h[:, D_FF:], -SWIGLU_LIMIT, SWIGLU_LIMIT)
            act = (hu + 1.0) * (hg * jax.nn.sigmoid(hg * SWIGLU_ALPHA))
            y = jnp.dot(act.astype(BF16), w2b_ref[...], preferred_element_type=F32) + b2_ref[...]
            ys_ref[rows, :] = _pack_halves(y)


def _expert_ffn(xs, blk_expert, n_used, w1, b1, w2, b2, layer, n_blocks):
    rows = pl.BlockSpec((EXPERT_ROWS, HALF), lambda i, be, nu: (jnp.minimum(i, nu[0] - 1), 0))
    return pl.pallas_call(
        _ffn_body,
        grid_spec=pltpu.PrefetchScalarGridSpec(
            num_scalar_prefetch=2,
            grid=(n_blocks,),
            in_specs=[rows,
                      pl.BlockSpec((None, None, D_MODEL, 2 * D_FF), lambda i, be, nu: (layer, be[i], 0, 0)),
                      pl.BlockSpec((None, None, 1, 2 * D_FF), lambda i, be, nu: (layer, be[i], 0, 0)),
                      pl.BlockSpec((None, None, D_FF, D_MODEL), lambda i, be, nu: (layer, be[i], 0, 0)),
                      pl.BlockSpec((None, None, 1, D_MODEL), lambda i, be, nu: (layer, be[i], 0, 0))],
            out_specs=rows,
            scratch_shapes=[pltpu.VMEM((D_MODEL, 2 * D_FF), BF16), pltpu.VMEM((D_FF, D_MODEL), BF16)]),
        out_shape=jax.ShapeDtypeStruct((xs.shape[0], HALF), U32),
        compiler_params=_params("arbitrary"),
        name="moe_expert_ffn",
    )(blk_expert, n_used, xs, w1, b1, w2, b2)


def _combine_body(seg_ref, seg_next_ref, lrow_ref, gate_ref, x_ref, g_ref, beta_ref, ys_hbm, xo_ref,
                  buf_ref, sem, *, n_tok_blocks):
    b = pl.program_id(0)
    slot = b % 2
    tm = x_ref.shape[0]
    R = buf_ref.shape[1]

    def seg_copy(s, local_row, src_row, rows):
        return pltpu.make_async_copy(ys_hbm.at[pl.ds(pl.multiple_of(src_row, SEG_ALIGN), rows)],
                                     buf_ref.at[s, pl.ds(pl.multiple_of(local_row, SEG_ALIGN), rows)], sem.at[s])

    @pl.when(b == 0)
    def _():
        buf_ref[...] = jnp.zeros_like(buf_ref)
        _segment_copies(seg_ref, seg_copy, 0, wait=False)

    @pl.when(b + 1 < n_tok_blocks)
    def _():
        _segment_copies(seg_next_ref, seg_copy, 1 - slot, wait=False)

    _segment_copies(seg_ref, seg_copy, slot, wait=True)

    lo, hi = _unpack_halves(buf_ref[slot])
    cid = lax.broadcasted_iota(I32, (tm, R), 1)
    lrow = lrow_ref[...]
    gates = gate_ref[...]
    sel = jnp.zeros((tm, R), F32)
    for k in range(TOP_K):
        sel = jnp.where(cid == lrow[:, k:k + 1], gates[:, k:k + 1], sel)
    sel = sel.astype(BF16)
    f = jnp.concatenate([jnp.dot(sel, lo.astype(BF16), preferred_element_type=F32),
                         jnp.dot(sel, hi.astype(BF16), preferred_element_type=F32)], axis=1)
    xo_ref[...] = _layer_norm_rows(DEEPNORM_ALPHA * x_ref[...] + f, g_ref[...], beta_ref[...])


def _combine_ln(lrow_t, gate_t, x, g, beta, ys, seg_table):
    T = x.shape[0]
    tm = ROW_TILE
    nb = T // tm
    R = TOP_K * tm + N_EXPERTS * SEG_ALIGN
    seg_w = seg_table.shape[-1]
    return pl.pallas_call(
        functools.partial(_combine_body, n_tok_blocks=nb),
        grid=(nb,),
        in_specs=[pl.BlockSpec((None, 1, seg_w), lambda b: (b, 0, 0), memory_space=pltpu.SMEM),
                  pl.BlockSpec((None, 1, seg_w), lambda b: (jnp.minimum(b + 1, nb - 1), 0, 0),
                               memory_space=pltpu.SMEM),
                  pl.BlockSpec((tm, TOP_K), lambda b: (b, 0)),
                  pl.BlockSpec((tm, TOP_K), lambda b: (b, 0)),
                  pl.BlockSpec((tm, D_MODEL), lambda b: (b, 0)),
                  pl.BlockSpec((1, D_MODEL), lambda b: (0, 0)),
                  pl.BlockSpec((1, D_MODEL), lambda b: (0, 0)),
                  pl.BlockSpec(memory_space=pl.ANY)],
        out_specs=pl.BlockSpec((tm, D_MODEL), lambda b: (b, 0)),
        out_shape=jax.ShapeDtypeStruct((T, D_MODEL), F32),
        scratch_shapes=[pltpu.VMEM((2, R, HALF), U32), pltpu.SemaphoreType.DMA((2,))],
        compiler_params=_params("arbitrary"),
        name="moe_combine_ln",
    )(seg_table, seg_table, lrow_t, gate_t, x, g, beta, ys)


def _dft_tables(P):
    n2 = 4 * P
    k = jnp.arange(P, dtype=I32)
    m = ((2 * k[:, None] + 1) * k[None, :]) % n2
    ang = m.astype(F32) * F32(2.0 * math.pi / n2)
    fc32, fs32 = jnp.cos(ang), -jnp.sin(ang)
    scale = F32(1.0 / P)
    gc, gs = (fc32.T * scale).astype(BF16), (fs32.T * scale).astype(BF16)
    return fc32, fs32, (fc32.astype(BF16), fs32.astype(BF16), gc, gs)


def _hyena_positional(S):
    pos = jnp.arange(S, dtype=F32)
    t = jnp.linspace(0.0, 1.0, S, dtype=F32)[:, None]
    bands = (HYENA_EMB_DIM - 1) // 2
    f = jnp.linspace(1e-4, bands - 1, bands, dtype=F32)
    ang = (2.0 * math.pi / S) * pos[:, None] * f[None, :]
    feats = jnp.concatenate([t, jnp.cos(ang), -jnp.sin(ang)], axis=-1)
    feats = jnp.pad(feats, ((0, 0), (0, LANES - HYENA_EMB_DIM)))
    max_decay = math.log(HYENA_DECAY_TARGET) / HYENA_SHORT_DECAY_PCT
    min_decay = math.log(HYENA_DECAY_TARGET) / HYENA_LONG_DECAY_PCT
    deltas = jnp.abs(jnp.linspace(min_decay, max_decay, HYENA_CH, dtype=F32))[None, :]
    return feats, t, deltas


def _alibi_slopes():
    return jnp.asarray(np.array([2.0 ** (-8.0 * (i + 1) / N_HEADS) for i in range(N_HEADS)], dtype=np.float32))


def _even_mixer(x, xshape, tabs, w_in, b_in, short_w, short_b, f1_w, f1_b, f1_freq, f2_w, f2_b, f2_freq, f3_w,
                skip, dw_w, dw_b, cln_g, cln_b, w_out, b_out, ln_g, ln_b):
    B, S = xshape
    fc32, fs32, tabs16 = tabs
    proj = _project(x, w_in.astype(BF16), b_in[None, :])
    hy = _short_conv(proj, short_w, short_b[None, :], B, S)
    u = _conformer(proj, dw_w, dw_b[None, :], cln_g[None, :], cln_b[None, :], B, S)
    feats, tcol, deltas = _hyena_positional(S)
    f1_wp = jnp.pad(f1_w, ((0, LANES - HYENA_EMB_DIM), (0, 0)))
    taps = _hyena_filters(feats, f1_wp, f1_b[None, :], f1_freq[None, :], f2_w, f2_b[None, :], f2_freq[None, :],
                          f3_w, tcol, deltas)
    hre, him = _filter_spectrum(fc32, fs32, taps, S)
    z = _long_conv(hy, 2, hy, 0, tabs16, hre, him, 0, skip, B, S)
    z = _long_conv(z, 0, hy, 1, tabs16, hre, him, 1, skip, B, S)
    return _outproj_ln(z, 0, u, 0, w_out.astype(BF16), b_out[None, :], x, ln_g[None, :], ln_b[None, :])


def _odd_mixer(x, xshape, layer_idx, w_qkv, lq1, lk1, lq2, lk2, subln_g, w_out, ln_g, ln_b):
    B, S = xshape
    lam_init = 0.8 - 0.6 * math.exp(-0.3 * layer_idx)
    lam = (jnp.exp(jnp.sum(lq1 * lk1)) - jnp.exp(jnp.sum(lq2 * lk2)) + lam_init).reshape(1)
    q_scale = jnp.concatenate([jnp.full((ATTN_W,), HEAD_DIM ** -0.5, F32), jnp.ones((2 * ATTN_W,), F32)])
    w = (w_qkv * q_scale).astype(BF16)
    qkv = _project(x, w, jnp.zeros((1, 3 * ATTN_W), F32))
    o = _diff_attention(qkv, _alibi_slopes(), lam, subln_g[None, :], lam_init, B, S)
    return _outproj_ln(o, 0, o, 1, w_out.astype(BF16), jnp.zeros((1, D_MODEL), F32), x, ln_g[None, :], ln_b[None, :])


def _round_up(a, m):
    return (a + m - 1) // m * m


def _copy_lists(loff, goff, units):
    E = N_EXPERTS
    caps = _copy_caps()
    big = COPY_ROWS[0]
    n_big = units // (big // SEG_ALIGN)
    cum = jnp.cumsum(n_big, axis=1)
    first = (cum - n_big)[:, None, :]
    p = jnp.arange(caps[0], dtype=I32)[None, :, None]
    mine = (first <= p) & (p < cum[:, None, :])
    within = (p - first) * big
    counts = [cum[:, -1]]
    cols = [jnp.sum(jnp.where(mine, loff[:, None, :] + within, 0), axis=2),
            jnp.sum(jnp.where(mine, goff[:, None, :] + within, 0), axis=2)]
    off = n_big * big
    p = jnp.arange(E, dtype=I32)[None, :, None]
    for rows in COPY_ROWS[1:]:
        has = (units & (rows // SEG_ALIGN)) != 0
        pos = jnp.cumsum(has.astype(I32), axis=1) - has.astype(I32)
        mine = has[:, None, :] & (pos[:, None, :] == p)
        counts.append(jnp.sum(has.astype(I32), axis=1))
        cols += [jnp.sum(jnp.where(mine, (loff + off)[:, None, :], 0), axis=2),
                 jnp.sum(jnp.where(mine, (goff + off)[:, None, :], 0), axis=2)]
        off = off + jnp.where(has, rows, 0)
    return jnp.concatenate([jnp.stack(counts, axis=1)] + cols, axis=1).astype(I32)


def _routing_tables(cnt_blocks, n_ffn_blocks):
    E = N_EXPERTS
    cnt8 = _round_up(cnt_blocks[:, :, 0].astype(I32), SEG_ALIGN)
    seg_end = jnp.cumsum(cnt8, axis=1)
    loff = seg_end - cnt8
    tot8 = jnp.sum(cnt8, axis=0)
    group = _round_up(tot8, EXPERT_ROWS)
    group_end = jnp.cumsum(group)
    group_start = group_end - group
    goff = group_start[None, :] + jnp.cumsum(cnt8, axis=0) - cnt8
    seg_table = _copy_lists(loff, goff, cnt8 // SEG_ALIGN)
    starts = jnp.arange(n_ffn_blocks, dtype=I32) * EXPERT_ROWS
    blk_expert = jnp.minimum(jnp.sum((group_end[None, :] <= starts[:, None]).astype(I32), axis=1), E - 1)
    n_used = group_end[-1:] // EXPERT_ROWS
    tail_start = group_start + tot8
    return loff[:, :, None], seg_table[:, None, :], blk_expert, n_used, tail_start


def _moe_layer(x, layer, w_r, b_r, w1, b1, w2, b2, ln_g, ln_b):
    T = x.shape[0]
    nb = T // ROW_TILE
    n_rows = _round_up(T * TOP_K + nb * N_EXPERTS * (SEG_ALIGN - 1), EXPERT_ROWS) + N_EXPERTS * EXPERT_ROWS
    n_ffn_blocks = n_rows // EXPERT_ROWS
    idx, gate, rank, cnt = _router(x, w_r.T, b_r[:, None])
    loff, seg_table, blk_expert, n_used, tail_start = _routing_tables(cnt, n_ffn_blocks)
    xs, lrow = _dispatch(x, idx, rank, loff, seg_table, tail_start, n_rows + EXPERT_ROWS)
    ys = _expert_ffn(xs, blk_expert, n_used, w1, b1[:, :, None, :], w2, b2[:, :, None, :], layer, n_ffn_blocks)
    return _combine_ln(lrow.T, gate.T, x, ln_g[None, :], ln_b[None, :], ys, seg_table)


def kernel(x, hy_cf_w_in, hy_cf_b_in, hy_short_w, hy_short_b, hy_f1_w, hy_f1_b, hy_f1_freq, hy_f2_w, hy_f2_b, hy_f2_freq, hy_f3_w, hy_skip, cf_dw_w, cf_dw_b, cf_ln_g, cf_ln_b, even_w_out, even_b_out, attn_w_qkv, attn_lq1, attn_lk1, attn_lq2, attn_lk2, attn_subln_g, attn_w_out, ln1_g, ln1_b, ln2_g, ln2_b, moe_w_r, moe_b_r, moe_w1, moe_b1, moe_w2, moe_b2):
    B, S, D = x.shape
    assert D == D_MODEL and (B * S) % ROW_TILE == 0 and S % LANES == 0
    depth = ln1_g.shape[0]
    xf = x.reshape(B * S, D)
    tabs = _dft_tables(S // CONV_BLOCKS)
    for i in range(depth):
        j = i // 2
        if i % 2 == 0:
            xf = _even_mixer(xf, (B, S), tabs, hy_cf_w_in[j], hy_cf_b_in[j], hy_short_w[j], hy_short_b[j],
                                 hy_f1_w[j], hy_f1_b[j], hy_f1_freq[j], hy_f2_w[j], hy_f2_b[j], hy_f2_freq[j],
                                 hy_f3_w[j], hy_skip[j], cf_dw_w[j], cf_dw_b[j], cf_ln_g[j], cf_ln_b[j],
                                 even_w_out[j], even_b_out[j], ln1_g[i], ln1_b[i])
        else:
            xf = _odd_mixer(xf, (B, S), i, attn_w_qkv[j], attn_lq1[j], attn_lk1[j], attn_lq2[j], attn_lk2[j],
                            attn_subln_g[j], attn_w_out[j], ln1_g[i], ln1_b[i])
        xf = _moe_layer(xf, i, moe_w_r[i], moe_b_r[i], moe_w1, moe_b1, moe_w2, moe_b2, ln2_g[i], ln2_b[i])
    return xf.reshape(B, S, D)
```

```python
import functools
import math

import jax
import jax.numpy as jnp
import numpy as np
from jax import lax
from jax.experimental import pallas as pl
from jax.experimental.pallas import tpu as pltpu

F32 = jnp.float32
BF16 = jnp.bfloat16
U32 = jnp.uint32
I32 = jnp.int32

D_MODEL = 1024
HALF = D_MODEL // 2
DEPTH = 4
HYENA_CH = D_MODEL // 2
CONF_CH = D_MODEL // 2
HYENA_ORDER = 2
HYENA_EMB_DIM = 33
HYENA_FILTER_DIM = 64
HYENA_SHORT_DECAY_PCT = 0.3
HYENA_LONG_DECAY_PCT = 1.5
HYENA_DECAY_TARGET = 1e-2
CONF_WIDTH = 31
EVEN_IN = 3 * HYENA_CH + 2 * CONF_CH
N_HEADS = 8
HEAD_DIM = 64
ATTN_W = N_HEADS * 2 * HEAD_DIM
N_EXPERTS = 32
TOP_K = 4
D_FF = D_MODEL
SWIGLU_LIMIT = 7.0
SWIGLU_ALPHA = 1.702
DEEPNORM_ALPHA = (2 * DEPTH) ** 0.25
LN_EPS = 1e-5

LANES = 128
SUBLANES = 8
VMEM_LIMIT_BYTES = 56 * 1024 * 1024
ROW_TILE = 512
EXPERT_ROWS = 512
FFN_ROW_CHUNK = 512
SEG_ALIGN = 8
COPY_ROWS = (64, 32, 16, 8)
CONV_BLOCKS = 4
CONV_CH_TILE = 256
ATTN_Q_TILE = 2048
ATTN_ROW_CHUNK = 256
CONV_PAD = 16

_NT = (((1,), (1,)), ((), ()))


def _params(*sem):
    return pltpu.CompilerParams(dimension_semantics=sem, vmem_limit_bytes=VMEM_LIMIT_BYTES)


def _split_bf16(a):
    hi = a.astype(BF16)
    lo = (a - hi.astype(F32)).astype(BF16)
    return hi, lo


def _dot3(a, b):
    a_hi, a_lo = _split_bf16(a)
    b_hi, b_lo = _split_bf16(b)
    d = functools.partial(jnp.dot, preferred_element_type=F32)
    return d(a_hi, b_hi) + d(a_hi, b_lo) + d(a_lo, b_hi)


def _layer_norm_rows(y, g, b):
    mu = jnp.mean(y, axis=-1, keepdims=True)
    yc = y - mu
    var = jnp.mean(yc * yc, axis=-1, keepdims=True)
    return yc * lax.rsqrt(var + LN_EPS) * g + b


def _pack_halves(y):
    lo = lax.bitcast_convert_type(y[:, :HALF].astype(BF16).astype(F32), U32)
    hi = lax.bitcast_convert_type(y[:, HALF:].astype(BF16).astype(F32), U32)
    return hi | (lo >> 16)


def _unpack_halves(p):
    lo = lax.bitcast_convert_type(p << 16, F32)
    hi = lax.bitcast_convert_type(p & jnp.uint32(0xFFFF0000), F32)
    return lo, hi


def _proj_body(x_ref, w_ref, b_ref, o_ref, *, col_chunk):
    x = x_ref[...].astype(BF16)
    for j in range(0, o_ref.shape[1], col_chunk):
        acc = jnp.dot(x, w_ref[:, j:j + col_chunk], preferred_element_type=F32)
        o_ref[:, j:j + col_chunk] = (acc + b_ref[:, j:j + col_chunk]).astype(o_ref.dtype)


def _project(x, w, b):
    T, K = x.shape
    N = w.shape[1]
    return pl.pallas_call(
        functools.partial(_proj_body, col_chunk=512),
        grid=(T // ROW_TILE,),
        in_specs=[pl.BlockSpec((ROW_TILE, K), lambda i: (i, 0)),
                  pl.BlockSpec((K, N), lambda i: (0, 0)),
                  pl.BlockSpec((1, N), lambda i: (0, 0))],
        out_specs=pl.BlockSpec((ROW_TILE, N), lambda i: (i, 0)),
        out_shape=jax.ShapeDtypeStruct((T, N), BF16),
        compiler_params=_params("parallel"),
        name="project",
    )(x, w, b)


def _outproj_ln_body(a1_ref, a2_ref, w1_ref, w2_ref, b_ref, x_ref, g_ref, beta_ref, xo_ref):
    m = (jnp.dot(a1_ref[...], w1_ref[...], preferred_element_type=F32)
         + jnp.dot(a2_ref[...], w2_ref[...], preferred_element_type=F32) + b_ref[...])
    xo_ref[...] = _layer_norm_rows(DEEPNORM_ALPHA * x_ref[...] + m, g_ref[...], beta_ref[...])


def _outproj_ln(a1, a1_col, a2, a2_col, w, b, x, g, beta):
    T = x.shape[0]
    return pl.pallas_call(
        _outproj_ln_body,
        grid=(T // ROW_TILE,),
        in_specs=[pl.BlockSpec((ROW_TILE, HALF), lambda i: (i, a1_col)),
                  pl.BlockSpec((ROW_TILE, HALF), lambda i: (i, a2_col)),
                  pl.BlockSpec((HALF, D_MODEL), lambda i: (0, 0)),
                  pl.BlockSpec((HALF, D_MODEL), lambda i: (1, 0)),
                  pl.BlockSpec((1, D_MODEL), lambda i: (0, 0)),
                  pl.BlockSpec((ROW_TILE, D_MODEL), lambda i: (i, 0)),
                  pl.BlockSpec((1, D_MODEL), lambda i: (0, 0)),
                  pl.BlockSpec((1, D_MODEL), lambda i: (0, 0))],
        out_specs=pl.BlockSpec((ROW_TILE, D_MODEL), lambda i: (i, 0)),
        out_shape=jax.ShapeDtypeStruct((T, D_MODEL), F32),
        compiler_params=_params("parallel"),
        name="outproj_ln",
    )(a1, a2, w, w, b, x, g, beta)


def _short_conv_body(x_ref, w_ref, b_ref, o_ref):
    x = x_ref[...].astype(F32)
    S = x.shape[0]
    row = lax.broadcasted_iota(I32, x.shape, 0)
    prev = jnp.where(row == 0, 0.0, pltpu.roll(x, 1, 0))
    nxt = jnp.where(row == S - 1, 0.0, pltpu.roll(x, S - 1, 0))
    y = w_ref[0:1, :] * prev + w_ref[1:2, :] * x + w_ref[2:3, :] * nxt + b_ref[...]
    o_ref[...] = y.astype(o_ref.dtype)


def _short_conv(proj, w, b, B, S):
    T = B * S
    C = HYENA_CH
    return pl.pallas_call(
        _short_conv_body,
        grid=(B, 3),
        in_specs=[pl.BlockSpec((S, C), lambda bi, j: (bi, j)),
                  pl.BlockSpec((3, C), lambda bi, j: (0, j)),
                  pl.BlockSpec((1, C), lambda bi, j: (0, j))],
        out_specs=pl.BlockSpec((S, C), lambda bi, j: (bi, j)),
        out_shape=jax.ShapeDtypeStruct((T, 3 * C), BF16),
        compiler_params=_params("parallel", "parallel"),
        name="hyena_short_conv",
    )(proj, w, b)


def _conformer_body(a_ref, g_ref, w_ref, b_ref, lg_ref, lb_ref, o_ref, pad_ref, sh_ref, *, row_chunk):
    S, C = a_ref.shape
    zeros = jnp.zeros((CONV_PAD, C), F32)
    pad_ref[0:CONV_PAD, :] = zeros
    pad_ref[CONV_PAD + S:CONV_PAD + S + CONV_PAD, :] = zeros
    a = a_ref[...].astype(F32)
    g = g_ref[...].astype(F32)
    pad_ref[CONV_PAD:CONV_PAD + S, :] = a * jax.nn.sigmoid(g)
    first = CONV_PAD - CONF_WIDTH // 2
    span = sh_ref.shape[1]
    for r0 in range(0, S, row_chunk):
        for ph in range(1, SUBLANES):
            sh_ref[ph - 1] = pad_ref[r0 + ph:r0 + ph + span, :]
        acc = jnp.zeros((row_chunk, C), F32) + b_ref[...]
        for j in range(CONF_WIDTH):
            q, ph = divmod(first + j, SUBLANES)
            if ph == 0:
                win = pad_ref[r0 + q * SUBLANES:r0 + q * SUBLANES + row_chunk, :]
            else:
                win = sh_ref[ph - 1, q * SUBLANES:q * SUBLANES + row_chunk, :]
            acc = acc + w_ref[j:j + 1, :] * win
        y = _layer_norm_rows(acc, lg_ref[...], lb_ref[...])
        o_ref[r0:r0 + row_chunk, :] = (y * jax.nn.sigmoid(y)).astype(o_ref.dtype)


def _conformer(proj, w, b, lg, lb, B, S):
    T = B * S
    C = CONF_CH
    rc = min(S, 256)
    return pl.pallas_call(
        functools.partial(_conformer_body, row_chunk=rc),
        grid=(B,),
        in_specs=[pl.BlockSpec((S, C), lambda bi: (bi, 3)),
                  pl.BlockSpec((S, C), lambda bi: (bi, 4)),
                  pl.BlockSpec((CONF_WIDTH, C), lambda bi: (0, 0)),
                  pl.BlockSpec((1, C), lambda bi: (0, 0)),
                  pl.BlockSpec((1, C), lambda bi: (0, 0)),
                  pl.BlockSpec((1, C), lambda bi: (0, 0))],
        out_specs=pl.BlockSpec((S, C), lambda bi: (bi, 0)),
        out_shape=jax.ShapeDtypeStruct((T, C), BF16),
        scratch_shapes=[pltpu.VMEM((S + 2 * CONV_PAD, C), F32),
                        pltpu.VMEM((SUBLANES - 1, rc + _round_up(CONF_WIDTH, SUBLANES) - SUBLANES, C), F32)],
        compiler_params=_params("parallel"),
        name="conformer_conv",
    )(proj, proj, w, b, lg, lb)


def _filter_body(feat_ref, w1_ref, b1_ref, q1_ref, w2_ref, b2_ref, q2_ref, w3_ref, t_ref, delta_ref, o_ref):
    h = jnp.sin(q1_ref[...] * (_dot3(feat_ref[...], w1_ref[...]) + b1_ref[...]))
    h = jnp.sin(q2_ref[...] * (_dot3(h, w2_ref[...]) + b2_ref[...]))
    h = _dot3(h, w3_ref[...])
    o_ref[...] = h * jnp.exp(-t_ref[...] * delta_ref[...])


def _hyena_filters(feats, w1, b1, q1, w2, b2, q2, w3, tcol, deltas):
    S = feats.shape[0]
    C = HYENA_CH
    n = 2 * HYENA_ORDER
    fd = HYENA_FILTER_DIM
    return pl.pallas_call(
        _filter_body,
        grid=(n,),
        in_specs=[pl.BlockSpec((S, LANES), lambda j: (0, 0)),
                  pl.BlockSpec((LANES, fd), lambda j: (0, 0)),
                  pl.BlockSpec((1, fd), lambda j: (0, 0)),
                  pl.BlockSpec((1, fd), lambda j: (0, 0)),
                  pl.BlockSpec((fd, fd), lambda j: (0, 0)),
                  pl.BlockSpec((1, fd), lambda j: (0, 0)),
                  pl.BlockSpec((1, fd), lambda j: (0, 0)),
                  pl.BlockSpec((fd, C), lambda j: (0, j)),
                  pl.BlockSpec((S, 1), lambda j: (0, 0)),
                  pl.BlockSpec((1, C), lambda j: (0, 0))],
        out_specs=pl.BlockSpec((S, C), lambda j: (0, j)),
        out_shape=jax.ShapeDtypeStruct((S, n * C), F32),
        compiler_params=_params("parallel"),
        name="hyena_filter_mlp",
    )(feats, w1, b1, q1, w2, b2, q2, w3, tcol, deltas)


def _spectrum_body(fc_ref, fs_ref, a_ref, b_ref, hre_ref, him_ref):
    a = a_ref[...]
    row = lax.broadcasted_iota(I32, a.shape, 0)
    b = jnp.where(row == 0, 0.0, b_ref[...])
    hre_ref[...] = _dot3(fc_ref[...], a + b)
    him_ref[...] = _dot3(fs_ref[...], a - b)


def _filter_spectrum(fc32, fs32, taps, taps_rev, S):
    P = fc32.shape[0]
    n = S // P
    C = HYENA_CH
    oc = HYENA_ORDER * C
    fwd, bwd = taps[:, :oc], taps[:, oc:]
    fwd_rev, bwd_rev = taps_rev[:, :oc], taps_rev[:, oc:]
    zero = jnp.zeros((1, oc), F32)
    h_up = jnp.concatenate([zero, bwd_rev[:S - 1], fwd], axis=0)
    h_down = jnp.concatenate([zero, fwd_rev, bwd[1:]], axis=0)
    spec = pl.BlockSpec((P, C), lambda di, o: (di, o))
    return pl.pallas_call(
        _spectrum_body,
        grid=(2 * n - 1, HYENA_ORDER),
        in_specs=[pl.BlockSpec((P, P), lambda di, o: (0, 0)),
                  pl.BlockSpec((P, P), lambda di, o: (0, 0)),
                  pl.BlockSpec((P, C), lambda di, o: (di + 1, o)),
                  pl.BlockSpec((P, C), lambda di, o: (2 * n - 1 - di, o))],
        out_specs=[spec, spec],
        out_shape=[jax.ShapeDtypeStruct(((2 * n - 1) * P, oc), F32)] * 2,
        compiler_params=_params("parallel", "parallel"),
        name="hyena_filter_spectrum",
    )(fc32, fs32, h_up, h_down)


def _long_conv_body(v_ref, gate_ref, fc_ref, fs_ref, hre_ref, him_ref, gc_ref, gs_ref, skip_ref, o_ref, *, n_blk):
    P = fc_ref.shape[0]
    fc, fs, gc, gs = fc_ref[...], fs_ref[...], gc_ref[...], gs_ref[...]
    vre, vim = [], []
    for j in range(n_blk):
        vj = v_ref[j * P:(j + 1) * P, :]
        vre.append(jnp.dot(fc, vj, preferred_element_type=F32))
        vim.append(jnp.dot(fs, vj, preferred_element_type=F32))
    for i in range(n_blk):
        yre = yim = None
        for j in range(n_blk):
            r0 = (i - j + n_blk - 1) * P
            hre = hre_ref[r0:r0 + P, :]
            him = him_ref[r0:r0 + P, :]
            tre = vre[j] * hre - vim[j] * him
            tim = vre[j] * him + vim[j] * hre
            yre = tre if yre is None else yre + tre
            yim = tim if yim is None else yim + tim
        y = (jnp.dot(gc, yre.astype(BF16), preferred_element_type=F32)
             + jnp.dot(gs, yim.astype(BF16), preferred_element_type=F32))
        rows = slice(i * P, (i + 1) * P)
        y = y + v_ref[rows, :].astype(F32) * skip_ref[...]
        o_ref[rows, :] = (gate_ref[rows, :].astype(F32) * y).astype(o_ref.dtype)


def _long_conv(v_arr, v_col, gate_arr, gate_col, tabs, hre, him, order, skip, B, S):
    fc, fs, gc, gs = tabs
    P = fc.shape[0]
    T = B * S
    C = HYENA_CH
    nc = C // CONV_CH_TILE
    cc = CONV_CH_TILE
    n_h = hre.shape[0]
    tab = pl.BlockSpec((P, P), lambda bi, c: (0, 0))
    return pl.pallas_call(
        functools.partial(_long_conv_body, n_blk=S // P),
        grid=(B, nc),
        in_specs=[pl.BlockSpec((S, cc), lambda bi, c: (bi, v_col * nc + c)),
                  pl.BlockSpec((S, cc), lambda bi, c: (bi, gate_col * nc + c)),
                  tab, tab,
                  pl.BlockSpec((n_h, cc), lambda bi, c: (0, order * nc + c)),
                  pl.BlockSpec((n_h, cc), lambda bi, c: (0, order * nc + c)),
                  tab, tab,
                  pl.BlockSpec((1, cc), lambda bi, c: (0, c))],
        out_specs=pl.BlockSpec((S, cc), lambda bi, c: (bi, c)),
        out_shape=jax.ShapeDtypeStruct((T, C), BF16),
        compiler_params=_params("parallel", "parallel"),
        name="hyena_long_conv",
    )(v_arr, gate_arr, fc, fs, hre, him, gc, gs, skip[order][None, :])


def _attn_body(slope_ref, lam_ref, q_ref, k_ref, v_ref, g_ref, o_ref, vaug_ref, *, lam_init, row_chunk):
    h = pl.program_id(1)
    qi = pl.program_id(2)
    tq = q_ref.shape[0]
    S = k_ref.shape[0]
    hw = 2 * HEAD_DIM
    k = k_ref[...]
    slope = slope_ref[h]
    kpos = lax.broadcasted_iota(I32, (1, S), 1).astype(F32) * slope

    @pl.when(qi == 0)
    def _():
        vaug_ref[:, :hw] = v_ref[...]
        vaug_ref[:, hw:] = jnp.where(lax.broadcasted_iota(I32, (S, hw), 1) == 0, 1.0, 0.0).astype(BF16)

    v_aug = vaug_ref[...]

    for r0 in range(0, tq, row_chunk):
        q = q_ref[r0:r0 + row_chunk, :]
        lane = lax.broadcasted_iota(I32, q.shape, 1)
        zero = jnp.zeros_like(q)
        qpos = (qi * tq + r0 + lax.broadcasted_iota(I32, (row_chunk, 1), 0)).astype(F32) * slope
        bias = lax.bitcast_convert_type(lax.bitcast_convert_type(qpos - kpos, U32) | jnp.uint32(0x80000000), F32)

        def weighted_values(qh):
            s = lax.dot_general(qh, k, _NT, preferred_element_type=F32) + bias
            e = jnp.exp((s - jnp.max(s, axis=-1, keepdims=True)).astype(BF16))
            return jnp.dot(e, v_aug, preferred_element_type=F32)

        o1 = weighted_values(jnp.where(lane < HEAD_DIM, q, zero))
        o2 = weighted_values(jnp.where(lane >= HEAD_DIM, q, zero))
        o = o1[:, :hw] * (1.0 / o1[:, hw:hw + 1]) - o2[:, :hw] * (lam_ref[0] / o2[:, hw:hw + 1])
        o = o * lax.rsqrt(jnp.mean(o * o, axis=-1, keepdims=True) + LN_EPS) * g_ref[...]
        o_ref[r0:r0 + row_chunk, :] = (o * (1.0 - lam_init)).astype(o_ref.dtype)


def _diff_attention(qkv, slopes, lam, subln_g, lam_init, B, S):
    T = B * S
    hw = 2 * HEAD_DIM
    tq = min(S, ATTN_Q_TILE)
    nq = S // tq
    smem = pl.BlockSpec(memory_space=pltpu.SMEM)
    return pl.pallas_call(
        functools.partial(_attn_body, lam_init=lam_init, row_chunk=min(tq, ATTN_ROW_CHUNK)),
        grid=(B, N_HEADS, nq),
        in_specs=[smem, smem,
                  pl.BlockSpec((tq, hw), lambda bi, h, qi: (bi * nq + qi, h)),
                  pl.BlockSpec((S, hw), lambda bi, h, qi: (bi, N_HEADS + h)),
                  pl.BlockSpec((S, hw), lambda bi, h, qi: (bi, 2 * N_HEADS + h)),
                  pl.BlockSpec((1, hw), lambda bi, h, qi: (0, 0))],
        out_specs=pl.BlockSpec((tq, hw), lambda bi, h, qi: (bi * nq + qi, h)),
        out_shape=jax.ShapeDtypeStruct((T, ATTN_W), BF16),
        scratch_shapes=[pltpu.VMEM((S, 2 * hw), BF16)],
        compiler_params=_params("parallel", "parallel", "arbitrary"),
        name="diff_attention",
    )(slopes, lam, qkv, qkv, qkv, subln_g)


def _router_body(x_ref, wt_ref, b_ref, idx_ref, gate_ref, rank_ref, cnt_ref):
    E = N_EXPERTS
    tm = x_ref.shape[0]
    x_hi, x_lo = _split_bf16(x_ref[...])
    w_hi, w_lo = _split_bf16(wt_ref[...])
    nt = functools.partial(lax.dot_general, dimension_numbers=_NT, preferred_element_type=F32)
    logits = nt(w_hi, x_hi) + nt(w_lo, x_hi) + nt(w_hi, x_lo) + b_ref[...]

    eid = lax.broadcasted_iota(I32, (E, tm), 0).astype(F32)
    work = logits
    vals, idxs = [], []
    for _ in range(TOP_K):
        m = jnp.max(work, axis=0, keepdims=True)
        sel = jnp.min(jnp.where(work == m, eid, float(E)), axis=0, keepdims=True)
        vals.append(m)
        idxs.append(sel)
        work = jnp.where(eid == sel, -jnp.inf, work)
    exps = [jnp.exp(v - vals[0]) for v in vals]
    denom = exps[0] + exps[1] + exps[2] + exps[3]

    chosen = jnp.zeros((E, tm), F32)
    for sel in idxs:
        chosen = chosen + jnp.where(eid == sel, 1.0, 0.0)
    earlier = jnp.where(lax.broadcasted_iota(I32, (tm, tm), 0) < lax.broadcasted_iota(I32, (tm, tm), 1), 1.0, 0.0)
    before = jnp.dot(chosen.astype(BF16), earlier.astype(BF16), preferred_element_type=F32)
    for k in range(TOP_K):
        gate_ref[k:k + 1, :] = exps[k] / denom
        idx_ref[k:k + 1, :] = idxs[k].astype(I32)
        rank_ref[k:k + 1, :] = jnp.sum(jnp.where(eid == idxs[k], before, 0.0), axis=0, keepdims=True).astype(I32)
    cnt_ref[...] = jnp.broadcast_to(jnp.sum(chosen, axis=1, keepdims=True), cnt_ref.shape)


def _router(x, w_rt, b_r):
    T = x.shape[0]
    E = N_EXPERTS
    tm = ROW_TILE
    tok = pl.BlockSpec((TOP_K, tm), lambda i: (0, i))
    return pl.pallas_call(
        _router_body,
        grid=(T // tm,),
        in_specs=[pl.BlockSpec((tm, D_MODEL), lambda i: (i, 0)),
                  pl.BlockSpec((E, D_MODEL), lambda i: (0, 0)),
                  pl.BlockSpec((E, 1), lambda i: (0, 0))],
        out_specs=[tok, tok, tok, pl.BlockSpec((None, E, LANES), lambda i: (i, 0, 0))],
        out_shape=[jax.ShapeDtypeStruct((TOP_K, T), I32), jax.ShapeDtypeStruct((TOP_K, T), F32),
                   jax.ShapeDtypeStruct((TOP_K, T), I32), jax.ShapeDtypeStruct((T // tm, E, LANES), F32)],
        compiler_params=_params("parallel"),
        name="moe_router",
    )(x, w_rt, b_r)


def _copy_caps():
    local_rows = TOP_K * ROW_TILE + N_EXPERTS * SEG_ALIGN
    return (local_rows // COPY_ROWS[0],) + (N_EXPERTS,) * (len(COPY_ROWS) - 1)


def _segment_copies(tab_ref, make_copy, slot, wait):
    base = len(COPY_ROWS)
    for ci, (rows, cap) in enumerate(zip(COPY_ROWS, _copy_caps())):
        def body(p, carry, base=base, rows=rows, cap=cap):
            copy = make_copy(slot, tab_ref[0, base + p], tab_ref[0, base + cap + p], rows)
            if wait:
                copy.wait()
            else:
                copy.start()
            return carry
        lax.fori_loop(0, tab_ref[0, ci], body, 0)
        base += 2 * cap


def _dispatch_body(tail_ref, seg_ref, seg_prev_ref, x_ref, idx_ref, rank_ref, loff_ref, xs_hbm, lrow_ref,
                   buf_ref, zero_ref, sem, zsem, *, n_tok_blocks):
    b = pl.program_id(0)
    slot = b % 2
    tm = x_ref.shape[0]
    R = buf_ref.shape[1]

    def seg_copy(s, local_row, dst_row, rows):
        return pltpu.make_async_copy(buf_ref.at[s, pl.ds(pl.multiple_of(local_row, SEG_ALIGN), rows)],
                                     xs_hbm.at[pl.ds(pl.multiple_of(dst_row, SEG_ALIGN), rows)], sem.at[s])

    @pl.when(b == 0)
    def _():
        zero_ref[...] = jnp.zeros_like(zero_ref)
        for e in range(N_EXPERTS):
            fill = pltpu.make_async_copy(
                zero_ref, xs_hbm.at[pl.ds(pl.multiple_of(tail_ref[e], SEG_ALIGN), EXPERT_ROWS)], zsem)
            fill.start()
            fill.wait()

    eid = lax.broadcasted_iota(I32, (N_EXPERTS, tm), 0)
    loff = loff_ref[...].astype(F32)
    rid = lax.broadcasted_iota(I32, (R, tm), 0).astype(jnp.int16)
    sel_t = jnp.zeros((R, tm), BF16)
    for k in range(TOP_K):
        base = jnp.sum(jnp.where(eid == idx_ref[k:k + 1, :], loff, 0.0), axis=0, keepdims=True).astype(I32)
        row = base + rank_ref[k:k + 1, :]
        lrow_ref[k:k + 1, :] = row
        sel_t = jnp.where(rid == row.astype(jnp.int16), jnp.ones((), BF16), sel_t)
    xb = x_ref[...].astype(BF16)
    lo = jnp.dot(sel_t, xb[:, :HALF], preferred_element_type=F32)
    hi = jnp.dot(sel_t, xb[:, HALF:], preferred_element_type=F32)
    packed = (lax.bitcast_convert_type(hi, U32) & jnp.uint32(0xFFFF0000)) | (lax.bitcast_convert_type(lo, U32) >> 16)
    buf_ref[slot] = packed

    @pl.when(b >= 1)
    def _():
        _segment_copies(seg_prev_ref, seg_copy, 1 - slot, wait=True)
    _segment_copies(seg_ref, seg_copy, slot, wait=False)

    @pl.when(b == n_tok_blocks - 1)
    def _():
        _segment_copies(seg_ref, seg_copy, slot, wait=True)


def _dispatch(x, idx, rank, loff, seg_table, tail_start, n_rows):
    T = x.shape[0]
    tm = ROW_TILE
    nb = T // tm
    R = TOP_K * tm + N_EXPERTS * SEG_ALIGN
    tok = pl.BlockSpec((TOP_K, tm), lambda b, tl: (0, b))
    seg_w = seg_table.shape[-1]
    return pl.pallas_call(
        functools.partial(_dispatch_body, n_tok_blocks=nb),
        grid_spec=pltpu.PrefetchScalarGridSpec(
            num_scalar_prefetch=1,
            grid=(nb,),
            in_specs=[pl.BlockSpec((None, 1, seg_w), lambda b, tl: (b, 0, 0), memory_space=pltpu.SMEM),
                      pl.BlockSpec((None, 1, seg_w), lambda b, tl: (jnp.maximum(b - 1, 0), 0, 0),
                                   memory_space=pltpu.SMEM),
                      pl.BlockSpec((tm, D_MODEL), lambda b, tl: (b, 0)),
                      tok, tok,
                      pl.BlockSpec((None, N_EXPERTS, 1), lambda b, tl: (b, 0, 0))],
            out_specs=[pl.BlockSpec(memory_space=pl.ANY), tok],
            scratch_shapes=[pltpu.VMEM((2, R, HALF), U32), pltpu.VMEM((EXPERT_ROWS, HALF), U32),
                            pltpu.SemaphoreType.DMA((2,)), pltpu.SemaphoreType.DMA(())]),
        out_shape=[jax.ShapeDtypeStruct((n_rows, HALF), U32), jax.ShapeDtypeStruct((TOP_K, T), I32)],
        compiler_params=_params("arbitrary"),
        name="moe_dispatch",
    )(tail_start, seg_table, seg_table, x, idx, rank, loff)


def _ffn_body(be_ref, nu_ref, xs_ref, w1_ref, b1_ref, w2_ref, b2_ref, ys_ref, w1b_ref, w2b_ref):
    i = pl.program_id(0)
    used = i < nu_ref[0]
    fresh = jnp.logical_or(i == 0, be_ref[i] != be_ref[jnp.maximum(i - 1, 0)])

    @pl.when(jnp.logical_and(used, fresh))
    def _():
        w1b_ref[...] = w1_ref[...].astype(BF16)
        w2b_ref[...] = w2_ref[...].astype(BF16)

    @pl.when(used)
    def _():
        for r0 in range(0, EXPERT_ROWS, FFN_ROW_CHUNK):
            rows = slice(r0, r0 + FFN_ROW_CHUNK)
            lo, hi = _unpack_halves(xs_ref[rows, :])
            h = (jnp.dot(lo.astype(BF16), w1b_ref[0:HALF, :], preferred_element_type=F32)
                 + jnp.dot(hi.astype(BF16), w1b_ref[HALF:D_MODEL, :], preferred_element_type=F32) + b1_ref[...])
            hg = jnp.minimum(h[:, :D_FF], SWIGLU_LIMIT)
            hu = jnp.clip(h[:, D_FF:], -SWIGLU_LIMIT, SWIGLU_LIMIT)
            act = (hu + 1.0) * (hg * jax.nn.sigmoid(hg * SWIGLU_ALPHA))
            y = jnp.dot(act.astype(BF16), w2b_ref[...], preferred_element_type=F32) + b2_ref[...]
            ys_ref[rows, :] = _pack_halves(y)


def _expert_ffn(xs, blk_expert, n_used, w1, b1, w2, b2, layer, n_blocks):
    rows = pl.BlockSpec((EXPERT_ROWS, HALF), lambda i, be, nu: (jnp.minimum(i, nu[0] - 1), 0))
    return pl.pallas_call(
        _ffn_body,
        grid_spec=pltpu.PrefetchScalarGridSpec(
            num_scalar_prefetch=2,
            grid=(n_blocks,),
            in_specs=[rows,
                      pl.BlockSpec((None, None, D_MODEL, 2 * D_FF), lambda i, be, nu: (layer, be[i], 0, 0)),
                      pl.BlockSpec((None, None, 1, 2 * D_FF), lambda i, be, nu: (layer, be[i], 0, 0)),
                      pl.BlockSpec((None, None, D_FF, D_MODEL), lambda i, be, nu: (layer, be[i], 0, 0)),
                      pl.BlockSpec((None, None, 1, D_MODEL), lambda i, be, nu: (layer, be[i], 0, 0))],
            out_specs=rows,
            scratch_shapes=[pltpu.VMEM((D_MODEL, 2 * D_FF), BF16), pltpu.VMEM((D_FF, D_MODEL), BF16)]),
        out_shape=jax.ShapeDtypeStruct((xs.shape[0], HALF), U32),
        compiler_params=_params("arbitrary"),
        name="moe_expert_ffn",
    )(blk_expert, n_used, xs, w1, b1, w2, b2)


def _combine_body(seg_ref, seg_next_ref, lrow_ref, gate_ref, x_ref, g_ref, beta_ref, ys_hbm, xo_ref,
                  buf_ref, sem, *, n_tok_blocks):
    b = pl.program_id(0)
    slot = b % 2
    tm = x_ref.shape[0]
    R = buf_ref.shape[1]

    def seg_copy(s, local_row, src_row, rows):
        return pltpu.make_async_copy(ys_hbm.at[pl.ds(pl.multiple_of(src_row, SEG_ALIGN), rows)],
                                     buf_ref.at[s, pl.ds(pl.multiple_of(local_row, SEG_ALIGN), rows)], sem.at[s])

    @pl.when(b == 0)
    def _():
        buf_ref[...] = jnp.zeros_like(buf_ref)
        _segment_copies(seg_ref, seg_copy, 0, wait=False)

    @pl.when(b + 1 < n_tok_blocks)
    def _():
        _segment_copies(seg_next_ref, seg_copy, 1 - slot, wait=False)

    _segment_copies(seg_ref, seg_copy, slot, wait=True)

    lo, hi = _unpack_halves(buf_ref[slot])
    lo = lo.astype(BF16)
    hi = hi.astype(BF16)
    tc = tm // 2
    cid = lax.broadcasted_iota(I32, (tc, R), 1).astype(jnp.int16)
    for t0 in range(0, tm, tc):
        lrow = lrow_ref[t0:t0 + tc, :].astype(jnp.int16)
        gates = gate_ref[t0:t0 + tc, :].astype(BF16)
        sel = jnp.zeros((tc, R), BF16)
        for k in range(TOP_K):
            sel = jnp.where(cid == lrow[:, k:k + 1], gates[:, k:k + 1], sel)
        f = jnp.concatenate([jnp.dot(sel, lo, preferred_element_type=F32),
                             jnp.dot(sel, hi, preferred_element_type=F32)], axis=1)
        xo_ref[t0:t0 + tc, :] = _layer_norm_rows(DEEPNORM_ALPHA * x_ref[t0:t0 + tc, :] + f, g_ref[...], beta_ref[...])


def _combine_ln(lrow_t, gate_t, x, g, beta, ys, seg_table):
    T = x.shape[0]
    tm = ROW_TILE
    nb = T // tm
    R = TOP_K * tm + N_EXPERTS * SEG_ALIGN
    seg_w = seg_table.shape[-1]
    return pl.pallas_call(
        functools.partial(_combine_body, n_tok_blocks=nb),
        grid=(nb,),
        in_specs=[pl.BlockSpec((None, 1, seg_w), lambda b: (b, 0, 0), memory_space=pltpu.SMEM),
                  pl.BlockSpec((None, 1, seg_w), lambda b: (jnp.minimum(b + 1, nb - 1), 0, 0),
                               memory_space=pltpu.SMEM),
                  pl.BlockSpec((tm, TOP_K), lambda b: (b, 0)),
                  pl.BlockSpec((tm, TOP_K), lambda b: (b, 0)),
                  pl.BlockSpec((tm, D_MODEL), lambda b: (b, 0)),
                  pl.BlockSpec((1, D_MODEL), lambda b: (0, 0)),
                  pl.BlockSpec((1, D_MODEL), lambda b: (0, 0)),
                  pl.BlockSpec(memory_space=pl.ANY)],
        out_specs=pl.BlockSpec((tm, D_MODEL), lambda b: (b, 0)),
        out_shape=jax.ShapeDtypeStruct((T, D_MODEL), F32),
        scratch_shapes=[pltpu.VMEM((2, R, HALF), U32), pltpu.SemaphoreType.DMA((2,))],
        compiler_params=_params("arbitrary"),
        name="moe_combine_ln",
    )(seg_table, seg_table, lrow_t, gate_t, x, g, beta, ys)


def _dft_tables(P):
    n2 = 4 * P
    k = jnp.arange(P, dtype=I32)
    m = ((2 * k[:, None] + 1) * k[None, :]) % n2
    ang = m.astype(F32) * F32(2.0 * math.pi / n2)
    fc32, fs32 = jnp.cos(ang), -jnp.sin(ang)
    scale = F32(1.0 / P)
    gc, gs = (fc32.T * scale).astype(BF16), (fs32.T * scale).astype(BF16)
    return fc32, fs32, (fc32.astype(BF16), fs32.astype(BF16), gc, gs)


def _hyena_positional(S):
    pos = jnp.arange(S, dtype=F32)
    t = jnp.linspace(0.0, 1.0, S, dtype=F32)[:, None]
    bands = (HYENA_EMB_DIM - 1) // 2
    f = jnp.linspace(1e-4, bands - 1, bands, dtype=F32)
    ang = (2.0 * math.pi / S) * pos[:, None] * f[None, :]
    feats = jnp.concatenate([t, jnp.cos(ang), -jnp.sin(ang)], axis=-1)
    feats = jnp.pad(feats, ((0, 0), (0, LANES - HYENA_EMB_DIM)))
    max_decay = math.log(HYENA_DECAY_TARGET) / HYENA_SHORT_DECAY_PCT
    min_decay = math.log(HYENA_DECAY_TARGET) / HYENA_LONG_DECAY_PCT
    deltas = jnp.abs(jnp.linspace(min_decay, max_decay, HYENA_CH, dtype=F32))[None, :]
    return feats, t, deltas


def _alibi_slopes():
    return jnp.asarray(np.array([2.0 ** (-8.0 * (i + 1) / N_HEADS) for i in range(N_HEADS)], dtype=np.float32))


def _even_mixer(x, xshape, tabs, w_in, b_in, short_w, short_b, f1_w, f1_b, f1_freq, f2_w, f2_b, f2_freq, f3_w,
                skip, dw_w, dw_b, cln_g, cln_b, w_out, b_out, ln_g, ln_b):
    B, S = xshape
    fc32, fs32, tabs16 = tabs
    proj = _project(x, w_in.astype(BF16), b_in[None, :])
    hy = _short_conv(proj, short_w, short_b[None, :], B, S)
    u = _conformer(proj, dw_w, dw_b[None, :], cln_g[None, :], cln_b[None, :], B, S)
    feats, tcol, deltas = _hyena_positional(S)
    f1_wp = jnp.pad(f1_w, ((0, LANES - HYENA_EMB_DIM), (0, 0)))
    mlp = (f1_wp, f1_b[None, :], f1_freq[None, :], f2_w, f2_b[None, :], f2_freq[None, :], f3_w)
    taps = _hyena_filters(feats, *mlp, tcol, deltas)
    taps_rev = _hyena_filters(feats[::-1], *mlp, tcol[::-1], deltas)
    hre, him = _filter_spectrum(fc32, fs32, taps, taps_rev, S)
    z = _long_conv(hy, 2, hy, 0, tabs16, hre, him, 0, skip, B, S)
    z = _long_conv(z, 0, hy, 1, tabs16, hre, him, 1, skip, B, S)
    return _outproj_ln(z, 0, u, 0, w_out.astype(BF16), b_out[None, :], x, ln_g[None, :], ln_b[None, :])


def _odd_mixer(x, xshape, layer_idx, w_qkv, lq1, lk1, lq2, lk2, subln_g, w_out, ln_g, ln_b):
    B, S = xshape
    lam_init = 0.8 - 0.6 * math.exp(-0.3 * layer_idx)
    lam = (jnp.exp(jnp.sum(lq1 * lk1)) - jnp.exp(jnp.sum(lq2 * lk2)) + lam_init).reshape(1)
    q_scale = jnp.concatenate([jnp.full((ATTN_W,), HEAD_DIM ** -0.5, F32), jnp.ones((2 * ATTN_W,), F32)])
    w = (w_qkv * q_scale).astype(BF16)
    qkv = _project(x, w, jnp.zeros((1, 3 * ATTN_W), F32))
    o = _diff_attention(qkv, _alibi_slopes(), lam, subln_g[None, :], lam_init, B, S)
    return _outproj_ln(o, 0, o, 1, w_out.astype(BF16), jnp.zeros((1, D_MODEL), F32), x, ln_g[None, :], ln_b[None, :])


def _round_up(a, m):
    return (a + m - 1) // m * m


def _copy_lists(loff, goff, units):
    E = N_EXPERTS
    caps = _copy_caps()
    big = COPY_ROWS[0]
    n_big = units // (big // SEG_ALIGN)
    cum = jnp.cumsum(n_big, axis=1)
    first = (cum - n_big)[:, None, :]
    p = jnp.arange(caps[0], dtype=I32)[None, :, None]
    mine = (first <= p) & (p < cum[:, None, :])
    within = (p - first) * big
    counts = [cum[:, -1]]
    cols = [jnp.sum(jnp.where(mine, loff[:, None, :] + within, 0), axis=2),
            jnp.sum(jnp.where(mine, goff[:, None, :] + within, 0), axis=2)]
    off = n_big * big
    p = jnp.arange(E, dtype=I32)[None, :, None]
    for rows in COPY_ROWS[1:]:
        has = (units & (rows // SEG_ALIGN)) != 0
        pos = jnp.cumsum(has.astype(I32), axis=1) - has.astype(I32)
        mine = has[:, None, :] & (pos[:, None, :] == p)
        counts.append(jnp.sum(has.astype(I32), axis=1))
        cols += [jnp.sum(jnp.where(mine, (loff + off)[:, None, :], 0), axis=2),
                 jnp.sum(jnp.where(mine, (goff + off)[:, None, :], 0), axis=2)]
        off = off + jnp.where(has, rows, 0)
    return jnp.concatenate([jnp.stack(counts, axis=1)] + cols, axis=1).astype(I32)


def _routing_tables(cnt_blocks, n_ffn_blocks):
    E = N_EXPERTS
    cnt8 = _round_up(cnt_blocks[:, :, 0].astype(I32), SEG_ALIGN)
    seg_end = jnp.cumsum(cnt8, axis=1)
    loff = seg_end - cnt8
    tot8 = jnp.sum(cnt8, axis=0)
    group = _round_up(tot8, EXPERT_ROWS)
    group_end = jnp.cumsum(group)
    group_start = group_end - group
    goff = group_start[None, :] + jnp.cumsum(cnt8, axis=0) - cnt8
    seg_table = _copy_lists(loff, goff, cnt8 // SEG_ALIGN)
    starts = jnp.arange(n_ffn_blocks, dtype=I32) * EXPERT_ROWS
    blk_expert = jnp.minimum(jnp.sum((group_end[None, :] <= starts[:, None]).astype(I32), axis=1), E - 1)
    n_used = group_end[-1:] // EXPERT_ROWS
    tail_start = group_start + tot8
    return loff[:, :, None], seg_table[:, None, :], blk_expert, n_used, tail_start


def _moe_layer(x, layer, w_r, b_r, w1, b1, w2, b2, ln_g, ln_b):
    T = x.shape[0]
    nb = T // ROW_TILE
    n_rows = _round_up(T * TOP_K + nb * N_EXPERTS * (SEG_ALIGN - 1), EXPERT_ROWS) + N_EXPERTS * EXPERT_ROWS
    n_ffn_blocks = n_rows // EXPERT_ROWS
    idx, gate, rank, cnt = _router(x, w_r.T, b_r[:, None])
    loff, seg_table, blk_expert, n_used, tail_start = _routing_tables(cnt, n_ffn_blocks)
    xs, lrow = _dispatch(x, idx, rank, loff, seg_table, tail_start, n_rows + EXPERT_ROWS)
    ys = _expert_ffn(xs, blk_expert, n_used, w1, b1[:, :, None, :], w2, b2[:, :, None, :], layer, n_ffn_blocks)
    return _combine_ln(lrow.T, gate.T, x, ln_g[None, :], ln_b[None, :], ys, seg_table)


def kernel(x, hy_cf_w_in, hy_cf_b_in, hy_short_w, hy_short_b, hy_f1_w, hy_f1_b, hy_f1_freq, hy_f2_w, hy_f2_b, hy_f2_freq, hy_f3_w, hy_skip, cf_dw_w, cf_dw_b, cf_ln_g, cf_ln_b, even_w_out, even_b_out, attn_w_qkv, attn_lq1, attn_lk1, attn_lq2, attn_lk2, attn_subln_g, attn_w_out, ln1_g, ln1_b, ln2_g, ln2_b, moe_w_r, moe_b_r, moe_w1, moe_b1, moe_w2, moe_b2):
    B, S, D = x.shape
    assert D == D_MODEL and (B * S) % ROW_TILE == 0 and S % LANES == 0
    depth = ln1_g.shape[0]
    xf = x.reshape(B * S, D)
    tabs = _dft_tables(S // CONV_BLOCKS)
    for i in range(depth):
        j = i // 2
        if i % 2 == 0:
            xf = _even_mixer(xf, (B, S), tabs, hy_cf_w_in[j], hy_cf_b_in[j], hy_short_w[j], hy_short_b[j],
                                 hy_f1_w[j], hy_f1_b[j], hy_f1_freq[j], hy_f2_w[j], hy_f2_b[j], hy_f2_freq[j],
                                 hy_f3_w[j], hy_skip[j], cf_dw_w[j], cf_dw_b[j], cf_ln_g[j], cf_ln_b[j],
                                 even_w_out[j], even_b_out[j], ln1_g[i], ln1_b[i])
        else:
            xf = _odd_mixer(xf, (B, S), i, attn_w_qkv[j], attn_lq1[j], attn_lk1[j], attn_lq2[j], attn_lk2[j],
                            attn_subln_g[j], attn_w_out[j], ln1_g[i], ln1_b[i])
        xf = _moe_layer(xf, i, moe_w_r[i], moe_b_r[i], moe_w1, moe_b1, moe_w2, moe_b2, ln2_g[i], ln2_b[i])
    return xf.reshape(B, S, D)
```

```python
import functools
import math

import jax
import jax.numpy as jnp
import numpy as np
from jax import lax
from jax.experimental import pallas as pl
from jax.experimental.pallas import tpu as pltpu

F32 = jnp.float32
BF16 = jnp.bfloat16
U32 = jnp.uint32
I32 = jnp.int32

D_MODEL = 1024
HALF = D_MODEL // 2
DEPTH = 4
HYENA_CH = D_MODEL // 2
CONF_CH = D_MODEL // 2
HYENA_ORDER = 2
HYENA_EMB_DIM = 33
HYENA_FILTER_DIM = 64
HYENA_SHORT_DECAY_PCT = 0.3
HYENA_LONG_DECAY_PCT = 1.5
HYENA_DECAY_TARGET = 1e-2
CONF_WIDTH = 31
EVEN_IN = 3 * HYENA_CH + 2 * CONF_CH
N_HEADS = 8
HEAD_DIM = 64
ATTN_W = N_HEADS * 2 * HEAD_DIM
N_EXPERTS = 32
TOP_K = 4
D_FF = D_MODEL
SWIGLU_LIMIT = 7.0
SWIGLU_ALPHA = 1.702
DEEPNORM_ALPHA = (2 * DEPTH) ** 0.25
LN_EPS = 1e-5

LANES = 128
SUBLANES = 8
VMEM_LIMIT_BYTES = 56 * 1024 * 1024
ROW_TILE = 512
DENSE_ROW_TILE = 1024
EXPERT_ROWS = 512
FFN_ROW_CHUNK = 512
SEG_ALIGN = 8
COPY_ROWS = (64, 32, 16, 8)
CONV_BLOCKS = 4
CONV_CH_TILE = 256
ATTN_Q_TILE = 2048
ATTN_HEADS_PER_STEP = 1
ATTN_ROW_CHUNK = 256
CONV_PAD = 16

_NT = (((1,), (1,)), ((), ()))


def _params(*sem):
    return pltpu.CompilerParams(dimension_semantics=sem, vmem_limit_bytes=VMEM_LIMIT_BYTES)


def _split_bf16(a):
    hi = a.astype(BF16)
    lo = (a - hi.astype(F32)).astype(BF16)
    return hi, lo


def _dot3(a, b):
    a_hi, a_lo = _split_bf16(a)
    b_hi, b_lo = _split_bf16(b)
    d = functools.partial(jnp.dot, preferred_element_type=F32)
    return d(a_hi, b_hi) + d(a_hi, b_lo) + d(a_lo, b_hi)


def _layer_norm_rows(y, g, b):
    mu = jnp.mean(y, axis=-1, keepdims=True)
    yc = y - mu
    var = jnp.mean(yc * yc, axis=-1, keepdims=True)
    return yc * lax.rsqrt(var + LN_EPS) * g + b


def _pack_halves(y):
    lo = lax.bitcast_convert_type(y[:, :HALF].astype(BF16).astype(F32), U32)
    hi = lax.bitcast_convert_type(y[:, HALF:].astype(BF16).astype(F32), U32)
    return hi | (lo >> 16)


def _unpack_halves(p):
    lo = lax.bitcast_convert_type(p << 16, F32)
    hi = lax.bitcast_convert_type(p & jnp.uint32(0xFFFF0000), F32)
    return lo, hi


def _proj_body(x_ref, w_ref, b_ref, o_ref, *, col_chunk):
    x = x_ref[...].astype(BF16)
    for j in range(0, o_ref.shape[1], col_chunk):
        acc = jnp.dot(x, w_ref[:, j:j + col_chunk], preferred_element_type=F32)
        o_ref[:, j:j + col_chunk] = (acc + b_ref[:, j:j + col_chunk]).astype(o_ref.dtype)


def _project(x, w, b):
    T, K = x.shape
    N = w.shape[1]
    tm = min(T, DENSE_ROW_TILE)
    return pl.pallas_call(
        functools.partial(_proj_body, col_chunk=512),
        grid=(T // tm,),
        in_specs=[pl.BlockSpec((tm, K), lambda i: (i, 0)),
                  pl.BlockSpec((K, N), lambda i: (0, 0)),
                  pl.BlockSpec((1, N), lambda i: (0, 0))],
        out_specs=pl.BlockSpec((tm, N), lambda i: (i, 0)),
        out_shape=jax.ShapeDtypeStruct((T, N), BF16),
        compiler_params=_params("parallel"),
        name="project",
    )(x, w, b)


def _outproj_ln_body(a1_ref, a2_ref, w1_ref, w2_ref, b_ref, x_ref, g_ref, beta_ref, wr_ref, br_ref,
                     xo_ref, idx_ref, gate_ref, rank_ref, cnt_ref):
    m = (jnp.dot(a1_ref[...], w1_ref[...], preferred_element_type=F32)
         + jnp.dot(a2_ref[...], w2_ref[...], preferred_element_type=F32) + b_ref[...])
    y = _layer_norm_rows(DEEPNORM_ALPHA * x_ref[...] + m, g_ref[...], beta_ref[...])
    xo_ref[...] = y
    w_hi, w_lo = _split_bf16(wr_ref[...])
    for s in range(y.shape[0] // ROW_TILE):
        cols = slice(s * ROW_TILE, (s + 1) * ROW_TILE)
        _route_block(y[cols, :], w_hi, w_lo, br_ref[...], idx_ref, gate_ref, rank_ref, cnt_ref.at[s], cols)


def _outproj_ln(a1, a1_col, a2, a2_col, w, b, x, g, beta, w_rt, b_r):
    T = x.shape[0]
    E = N_EXPERTS
    tm = min(T, DENSE_ROW_TILE)
    nsub = tm // ROW_TILE
    tok = pl.BlockSpec((TOP_K, tm), lambda i: (0, i))
    out = pl.pallas_call(
        _outproj_ln_body,
        grid=(T // tm,),
        in_specs=[pl.BlockSpec((tm, HALF), lambda i: (i, a1_col)),
                  pl.BlockSpec((tm, HALF), lambda i: (i, a2_col)),
                  pl.BlockSpec((HALF, D_MODEL), lambda i: (0, 0)),
                  pl.BlockSpec((HALF, D_MODEL), lambda i: (1, 0)),
                  pl.BlockSpec((1, D_MODEL), lambda i: (0, 0)),
                  pl.BlockSpec((tm, D_MODEL), lambda i: (i, 0)),
                  pl.BlockSpec((1, D_MODEL), lambda i: (0, 0)),
                  pl.BlockSpec((1, D_MODEL), lambda i: (0, 0)),
                  pl.BlockSpec((E, D_MODEL), lambda i: (0, 0)),
                  pl.BlockSpec((E, 1), lambda i: (0, 0))],
        out_specs=[pl.BlockSpec((tm, D_MODEL), lambda i: (i, 0)), tok, tok, tok,
                   pl.BlockSpec((nsub, E, LANES), lambda i: (i, 0, 0))],
        out_shape=[jax.ShapeDtypeStruct((T, D_MODEL), F32),
                   jax.ShapeDtypeStruct((TOP_K, T), I32), jax.ShapeDtypeStruct((TOP_K, T), F32),
                   jax.ShapeDtypeStruct((TOP_K, T), I32), jax.ShapeDtypeStruct((T // ROW_TILE, E, LANES), F32)],
        compiler_params=_params("parallel"),
        name="outproj_ln_route",
    )(a1, a2, w, w, b, x, g, beta, w_rt, b_r)
    return out[0], tuple(out[1:])


def _short_conv_body(x_ref, w_ref, b_ref, o_ref):
    x = x_ref[...].astype(F32)
    S = x.shape[0]
    row = lax.broadcasted_iota(I32, x.shape, 0)
    prev = jnp.where(row == 0, 0.0, pltpu.roll(x, 1, 0))
    nxt = jnp.where(row == S - 1, 0.0, pltpu.roll(x, S - 1, 0))
    y = w_ref[0:1, :] * prev + w_ref[1:2, :] * x + w_ref[2:3, :] * nxt + b_ref[...]
    o_ref[...] = y.astype(o_ref.dtype)


def _short_conv(proj, w, b, B, S):
    T = B * S
    C = HYENA_CH
    return pl.pallas_call(
        _short_conv_body,
        grid=(B, 3),
        in_specs=[pl.BlockSpec((S, C), lambda bi, j: (bi, j)),
                  pl.BlockSpec((3, C), lambda bi, j: (0, j)),
                  pl.BlockSpec((1, C), lambda bi, j: (0, j))],
        out_specs=pl.BlockSpec((S, C), lambda bi, j: (bi, j)),
        out_shape=jax.ShapeDtypeStruct((T, 3 * C), BF16),
        compiler_params=_params("parallel", "parallel"),
        name="hyena_short_conv",
    )(proj, w, b)


def _conformer_body(a_ref, g_ref, w_ref, b_ref, lg_ref, lb_ref, o_ref, pad_ref, sh_ref, *, row_chunk):
    S, C = a_ref.shape
    zeros = jnp.zeros((CONV_PAD, C), F32)
    pad_ref[0:CONV_PAD, :] = zeros
    pad_ref[CONV_PAD + S:CONV_PAD + S + CONV_PAD, :] = zeros
    a = a_ref[...].astype(F32)
    g = g_ref[...].astype(F32)
    pad_ref[CONV_PAD:CONV_PAD + S, :] = a * jax.nn.sigmoid(g)
    first = CONV_PAD - CONF_WIDTH // 2
    span = sh_ref.shape[1]
    for r0 in range(0, S, row_chunk):
        for ph in range(1, SUBLANES):
            sh_ref[ph - 1] = pad_ref[r0 + ph:r0 + ph + span, :]
        acc = jnp.zeros((row_chunk, C), F32) + b_ref[...]
        for j in range(CONF_WIDTH):
            q, ph = divmod(first + j, SUBLANES)
            if ph == 0:
                win = pad_ref[r0 + q * SUBLANES:r0 + q * SUBLANES + row_chunk, :]
            else:
                win = sh_ref[ph - 1, q * SUBLANES:q * SUBLANES + row_chunk, :]
            acc = acc + w_ref[j:j + 1, :] * win
        y = _layer_norm_rows(acc, lg_ref[...], lb_ref[...])
        o_ref[r0:r0 + row_chunk, :] = (y * jax.nn.sigmoid(y)).astype(o_ref.dtype)


def _conformer(proj, w, b, lg, lb, B, S):
    T = B * S
    C = CONF_CH
    rc = min(S, 256)
    return pl.pallas_call(
        functools.partial(_conformer_body, row_chunk=rc),
        grid=(B,),
        in_specs=[pl.BlockSpec((S, C), lambda bi: (bi, 3)),
                  pl.BlockSpec((S, C), lambda bi: (bi, 4)),
                  pl.BlockSpec((CONF_WIDTH, C), lambda bi: (0, 0)),
                  pl.BlockSpec((1, C), lambda bi: (0, 0)),
                  pl.BlockSpec((1, C), lambda bi: (0, 0)),
                  pl.BlockSpec((1, C), lambda bi: (0, 0))],
        out_specs=pl.BlockSpec((S, C), lambda bi: (bi, 0)),
        out_shape=jax.ShapeDtypeStruct((T, C), BF16),
        scratch_shapes=[pltpu.VMEM((S + 2 * CONV_PAD, C), F32),
                        pltpu.VMEM((SUBLANES - 1, rc + _round_up(CONF_WIDTH, SUBLANES) - SUBLANES, C), F32)],
        compiler_params=_params("parallel"),
        name="conformer_conv",
    )(proj, proj, w, b, lg, lb)


def _filter_body(feat_ref, w1_ref, b1_ref, q1_ref, w2_ref, b2_ref, q2_ref, w3_ref, t_ref, delta_ref, o_ref):
    h = jnp.sin(q1_ref[...] * (_dot3(feat_ref[...], w1_ref[...]) + b1_ref[...]))
    h = jnp.sin(q2_ref[...] * (_dot3(h, w2_ref[...]) + b2_ref[...]))
    h = _dot3(h, w3_ref[...])
    o_ref[...] = h * jnp.exp(-t_ref[...] * delta_ref[...])


def _hyena_filters(feats, w1, b1, q1, w2, b2, q2, w3, tcol, deltas):
    S = feats.shape[0]
    C = HYENA_CH
    n = 2 * HYENA_ORDER
    fd = HYENA_FILTER_DIM
    return pl.pallas_call(
        _filter_body,
        grid=(n,),
        in_specs=[pl.BlockSpec((S, LANES), lambda j: (0, 0)),
                  pl.BlockSpec((LANES, fd), lambda j: (0, 0)),
                  pl.BlockSpec((1, fd), lambda j: (0, 0)),
                  pl.BlockSpec((1, fd), lambda j: (0, 0)),
                  pl.BlockSpec((fd, fd), lambda j: (0, 0)),
                  pl.BlockSpec((1, fd), lambda j: (0, 0)),
                  pl.BlockSpec((1, fd), lambda j: (0, 0)),
                  pl.BlockSpec((fd, C), lambda j: (0, j)),
                  pl.BlockSpec((S, 1), lambda j: (0, 0)),
                  pl.BlockSpec((1, C), lambda j: (0, 0))],
        out_specs=pl.BlockSpec((S, C), lambda j: (0, j)),
        out_shape=jax.ShapeDtypeStruct((S, n * C), F32),
        compiler_params=_params("parallel"),
        name="hyena_filter_mlp",
    )(feats, w1, b1, q1, w2, b2, q2, w3, tcol, deltas)


def _spectrum_body(fc_ref, fs_ref, a_ref, b_ref, hre_ref, him_ref):
    a = a_ref[...]
    row = lax.broadcasted_iota(I32, a.shape, 0)
    b = jnp.where(row == 0, 0.0, b_ref[...])
    hre_ref[...] = _dot3(fc_ref[...], a + b)
    him_ref[...] = _dot3(fs_ref[...], a - b)


def _filter_spectrum(fc32, fs32, taps, taps_rev, S):
    P = fc32.shape[0]
    n = S // P
    C = HYENA_CH
    oc = HYENA_ORDER * C
    fwd, bwd = taps[:, :oc], taps[:, oc:]
    fwd_rev, bwd_rev = taps_rev[:, :oc], taps_rev[:, oc:]
    zero = jnp.zeros((1, oc), F32)
    h_up = jnp.concatenate([zero, bwd_rev[:S - 1], fwd], axis=0)
    h_down = jnp.concatenate([zero, fwd_rev, bwd[1:]], axis=0)
    spec = pl.BlockSpec((P, C), lambda di, o: (di, o))
    return pl.pallas_call(
        _spectrum_body,
        grid=(2 * n - 1, HYENA_ORDER),
        in_specs=[pl.BlockSpec((P, P), lambda di, o: (0, 0)),
                  pl.BlockSpec((P, P), lambda di, o: (0, 0)),
                  pl.BlockSpec((P, C), lambda di, o: (di + 1, o)),
                  pl.BlockSpec((P, C), lambda di, o: (2 * n - 1 - di, o))],
        out_specs=[spec, spec],
        out_shape=[jax.ShapeDtypeStruct(((2 * n - 1) * P, oc), F32)] * 2,
        compiler_params=_params("parallel", "parallel"),
        name="hyena_filter_spectrum",
    )(fc32, fs32, h_up, h_down)


def _long_conv_body(v_ref, gate_ref, fc_ref, fs_ref, hre_ref, him_ref, gc_ref, gs_ref, skip_ref, o_ref, *, n_blk):
    P = fc_ref.shape[0]
    fc, fs, gc, gs = fc_ref[...], fs_ref[...], gc_ref[...], gs_ref[...]
    vre, vim = [], []
    for j in range(n_blk):
        vj = v_ref[j * P:(j + 1) * P, :]
        vre.append(jnp.dot(fc, vj, preferred_element_type=F32))
        vim.append(jnp.dot(fs, vj, preferred_element_type=F32))
    for i in range(n_blk):
        yre = yim = None
        for j in range(n_blk):
            r0 = (i - j + n_blk - 1) * P
            hre = hre_ref[r0:r0 + P, :]
            him = him_ref[r0:r0 + P, :]
            tre = vre[j] * hre - vim[j] * him
            tim = vre[j] * him + vim[j] * hre
            yre = tre if yre is None else yre + tre
            yim = tim if yim is None else yim + tim
        y = (jnp.dot(gc, yre.astype(BF16), preferred_element_type=F32)
             + jnp.dot(gs, yim.astype(BF16), preferred_element_type=F32))
        rows = slice(i * P, (i + 1) * P)
        y = y + v_ref[rows, :].astype(F32) * skip_ref[...]
        o_ref[rows, :] = (gate_ref[rows, :].astype(F32) * y).astype(o_ref.dtype)


def _long_conv(v_arr, v_col, gate_arr, gate_col, tabs, hre, him, order, skip, B, S):
    fc, fs, gc, gs = tabs
    P = fc.shape[0]
    T = B * S
    C = HYENA_CH
    nc = C // CONV_CH_TILE
    cc = CONV_CH_TILE
    n_h = hre.shape[0]
    tab = pl.BlockSpec((P, P), lambda bi, c: (0, 0))
    return pl.pallas_call(
        functools.partial(_long_conv_body, n_blk=S // P),
        grid=(B, nc),
        in_specs=[pl.BlockSpec((S, cc), lambda bi, c: (bi, v_col * nc + c)),
                  pl.BlockSpec((S, cc), lambda bi, c: (bi, gate_col * nc + c)),
                  tab, tab,
                  pl.BlockSpec((n_h, cc), lambda bi, c: (0, order * nc + c)),
                  pl.BlockSpec((n_h, cc), lambda bi, c: (0, order * nc + c)),
                  tab, tab,
                  pl.BlockSpec((1, cc), lambda bi, c: (0, c))],
        out_specs=pl.BlockSpec((S, cc), lambda bi, c: (bi, c)),
        out_shape=jax.ShapeDtypeStruct((T, C), BF16),
        compiler_params=_params("parallel", "parallel"),
        name="hyena_long_conv",
    )(v_arr, gate_arr, fc, fs, hre, him, gc, gs, skip[order][None, :])


def _attn_body(slope_ref, lam_ref, q_ref, k_ref, v_ref, g_ref, o_ref, vaug_ref, *, lam_init, row_chunk, heads):
    hg = pl.program_id(1)
    qi = pl.program_id(2)
    tq = q_ref.shape[0]
    S = k_ref.shape[0]
    hw = 2 * HEAD_DIM

    @pl.when(qi == 0)
    def _():
        ones_col = jnp.where(lax.broadcasted_iota(I32, (S, hw), 1) == 0, 1.0, 0.0).astype(BF16)
        for hh in range(heads):
            vaug_ref[hh, :, :hw] = v_ref[:, hh * hw:(hh + 1) * hw]
            vaug_ref[hh, :, hw:] = ones_col

    for hh in range(heads):
        cols = slice(hh * hw, (hh + 1) * hw)
        k = k_ref[:, cols]
        v_aug = vaug_ref[hh]
        slope = slope_ref[hg * heads + hh]
        kpos = lax.broadcasted_iota(I32, (1, S), 1).astype(F32) * slope
        for r0 in range(0, tq, row_chunk):
            q = q_ref[r0:r0 + row_chunk, cols]
            lane = lax.broadcasted_iota(I32, q.shape, 1)
            zero = jnp.zeros_like(q)
            qpos = (qi * tq + r0 + lax.broadcasted_iota(I32, (row_chunk, 1), 0)).astype(F32) * slope
            bias = lax.bitcast_convert_type(lax.bitcast_convert_type(qpos - kpos, U32) | jnp.uint32(0x80000000), F32)

            def weighted_values(qh):
                s = lax.dot_general(qh, k, _NT, preferred_element_type=F32) + bias
                e = jnp.exp((s - jnp.max(s, axis=-1, keepdims=True)).astype(BF16))
                return jnp.dot(e, v_aug, preferred_element_type=F32)

            o1 = weighted_values(jnp.where(lane < HEAD_DIM, q, zero))
            o2 = weighted_values(jnp.where(lane >= HEAD_DIM, q, zero))
            o = o1[:, :hw] * (1.0 / o1[:, hw:hw + 1]) - o2[:, :hw] * (lam_ref[0] / o2[:, hw:hw + 1])
            o = o * lax.rsqrt(jnp.mean(o * o, axis=-1, keepdims=True) + LN_EPS) * g_ref[...]
            o_ref[r0:r0 + row_chunk, cols] = (o * (1.0 - lam_init)).astype(o_ref.dtype)


def _diff_attention(qkv, slopes, lam, subln_g, lam_init, B, S):
    T = B * S
    hw = 2 * HEAD_DIM
    hp = ATTN_HEADS_PER_STEP
    ng = N_HEADS // hp
    tq = min(S, ATTN_Q_TILE)
    nq = S // tq
    smem = pl.BlockSpec(memory_space=pltpu.SMEM)
    return pl.pallas_call(
        functools.partial(_attn_body, lam_init=lam_init, row_chunk=min(tq, ATTN_ROW_CHUNK), heads=hp),
        grid=(B, ng, nq),
        in_specs=[smem, smem,
                  pl.BlockSpec((tq, hp * hw), lambda bi, h, qi: (bi * nq + qi, h)),
                  pl.BlockSpec((S, hp * hw), lambda bi, h, qi: (bi, ng + h)),
                  pl.BlockSpec((S, hp * hw), lambda bi, h, qi: (bi, 2 * ng + h)),
                  pl.BlockSpec((1, hw), lambda bi, h, qi: (0, 0))],
        out_specs=pl.BlockSpec((tq, hp * hw), lambda bi, h, qi: (bi * nq + qi, h)),
        out_shape=jax.ShapeDtypeStruct((T, ATTN_W), BF16),
        scratch_shapes=[pltpu.VMEM((hp, S, 2 * hw), BF16)],
        compiler_params=_params("parallel", "parallel", "arbitrary"),
        name="diff_attention",
    )(slopes, lam, qkv, qkv, qkv, subln_g)


def _route_block(x, w_hi, w_lo, bias, idx_ref, gate_ref, rank_ref, cnt_ref, cols):
    E = N_EXPERTS
    tm = x.shape[0]
    x_hi, x_lo = _split_bf16(x)
    nt = functools.partial(lax.dot_general, dimension_numbers=_NT, preferred_element_type=F32)
    logits = nt(w_hi, x_hi) + nt(w_lo, x_hi) + nt(w_hi, x_lo) + bias

    eid = lax.broadcasted_iota(I32, (E, tm), 0).astype(F32)
    work = logits
    vals, idxs = [], []
    for _ in range(TOP_K):
        m = jnp.max(work, axis=0, keepdims=True)
        sel = jnp.min(jnp.where(work == m, eid, float(E)), axis=0, keepdims=True)
        vals.append(m)
        idxs.append(sel)
        work = jnp.where(eid == sel, -jnp.inf, work)
    exps = [jnp.exp(v - vals[0]) for v in vals]
    denom = exps[0] + exps[1] + exps[2] + exps[3]

    chosen = jnp.zeros((E, tm), F32)
    for sel in idxs:
        chosen = chosen + jnp.where(eid == sel, 1.0, 0.0)
    earlier = jnp.where(lax.broadcasted_iota(I32, (tm, tm), 0) < lax.broadcasted_iota(I32, (tm, tm), 1), 1.0, 0.0)
    before = jnp.dot(chosen.astype(BF16), earlier.astype(BF16), preferred_element_type=F32)
    for k in range(TOP_K):
        gate_ref[k:k + 1, cols] = exps[k] / denom
        idx_ref[k:k + 1, cols] = idxs[k].astype(I32)
        rank_ref[k:k + 1, cols] = jnp.sum(jnp.where(eid == idxs[k], before, 0.0), axis=0, keepdims=True).astype(I32)
    cnt_ref[...] = jnp.broadcast_to(jnp.sum(chosen, axis=1, keepdims=True), cnt_ref.shape)


def _copy_caps():
    local_rows = TOP_K * ROW_TILE + N_EXPERTS * SEG_ALIGN
    return (local_rows // COPY_ROWS[0],) + (N_EXPERTS,) * (len(COPY_ROWS) - 1)


def _segment_copies(tab_ref, make_copy, slot, wait):
    base = len(COPY_ROWS)
    for ci, (rows, cap) in enumerate(zip(COPY_ROWS, _copy_caps())):
        def body(p, carry, base=base, rows=rows, cap=cap):
            copy = make_copy(slot, tab_ref[0, base + p], tab_ref[0, base + cap + p], rows)
            if wait:
                copy.wait()
            else:
                copy.start()
            return carry
        lax.fori_loop(0, tab_ref[0, ci], body, 0)
        base += 2 * cap


def _dispatch_body(tail_ref, seg_ref, seg_prev_ref, x_ref, idx_ref, rank_ref, loff_ref, xs_hbm, lrow_ref,
                   buf_ref, zero_ref, sem, zsem, *, n_tok_blocks):
    b = pl.program_id(0)
    slot = b % 2
    tm = x_ref.shape[0]
    R = buf_ref.shape[1]

    def seg_copy(s, local_row, dst_row, rows):
        return pltpu.make_async_copy(buf_ref.at[s, pl.ds(pl.multiple_of(local_row, SEG_ALIGN), rows)],
                                     xs_hbm.at[pl.ds(pl.multiple_of(dst_row, SEG_ALIGN), rows)], sem.at[s])

    @pl.when(b == 0)
    def _():
        zero_ref[...] = jnp.zeros_like(zero_ref)
        for e in range(N_EXPERTS):
            fill = pltpu.make_async_copy(
                zero_ref, xs_hbm.at[pl.ds(pl.multiple_of(tail_ref[e], SEG_ALIGN), EXPERT_ROWS)], zsem)
            fill.start()
            fill.wait()

    eid = lax.broadcasted_iota(I32, (N_EXPERTS, tm), 0)
    loff = loff_ref[...].astype(F32)
    rid = lax.broadcasted_iota(I32, (R, tm), 0).astype(jnp.int16)
    sel_t = jnp.zeros((R, tm), BF16)
    for k in range(TOP_K):
        base = jnp.sum(jnp.where(eid == idx_ref[k:k + 1, :], loff, 0.0), axis=0, keepdims=True).astype(I32)
        row = base + rank_ref[k:k + 1, :]
        lrow_ref[k:k + 1, :] = row
        sel_t = jnp.where(rid == row.astype(jnp.int16), jnp.ones((), BF16), sel_t)
    xb = x_ref[...].astype(BF16)
    lo = jnp.dot(sel_t, xb[:, :HALF], preferred_element_type=F32)
    hi = jnp.dot(sel_t, xb[:, HALF:], preferred_element_type=F32)
    packed = (lax.bitcast_convert_type(hi, U32) & jnp.uint32(0xFFFF0000)) | (lax.bitcast_convert_type(lo, U32) >> 16)
    buf_ref[slot] = packed

    @pl.when(b >= 1)
    def _():
        _segment_copies(seg_prev_ref, seg_copy, 1 - slot, wait=True)
    _segment_copies(seg_ref, seg_copy, slot, wait=False)

    @pl.when(b == n_tok_blocks - 1)
    def _():
        _segment_copies(seg_ref, seg_copy, slot, wait=True)


def _dispatch(x, idx, rank, loff, seg_table, tail_start, n_rows):
    T = x.shape[0]
    tm = ROW_TILE
    nb = T // tm
    R = TOP_K * tm + N_EXPERTS * SEG_ALIGN
    tok = pl.BlockSpec((TOP_K, tm), lambda b, tl: (0, b))
    seg_w = seg_table.shape[-1]
    return pl.pallas_call(
        functools.partial(_dispatch_body, n_tok_blocks=nb),
        grid_spec=pltpu.PrefetchScalarGridSpec(
            num_scalar_prefetch=1,
            grid=(nb,),
            in_specs=[pl.BlockSpec((None, 1, seg_w), lambda b, tl: (b, 0, 0), memory_space=pltpu.SMEM),
                      pl.BlockSpec((None, 1, seg_w), lambda b, tl: (jnp.maximum(b - 1, 0), 0, 0),
                                   memory_space=pltpu.SMEM),
                      pl.BlockSpec((tm, D_MODEL), lambda b, tl: (b, 0)),
                      tok, tok,
                      pl.BlockSpec((None, N_EXPERTS, 1), lambda b, tl: (b, 0, 0))],
            out_specs=[pl.BlockSpec(memory_space=pl.ANY), tok],
            scratch_shapes=[pltpu.VMEM((2, R, HALF), U32), pltpu.VMEM((EXPERT_ROWS, HALF), U32),
                            pltpu.SemaphoreType.DMA((2,)), pltpu.SemaphoreType.DMA(())]),
        out_shape=[jax.ShapeDtypeStruct((n_rows, HALF), U32), jax.ShapeDtypeStruct((TOP_K, T), I32)],
        compiler_params=_params("arbitrary"),
        name="moe_dispatch",
    )(tail_start, seg_table, seg_table, x, idx, rank, loff)


def _ffn_body(be_ref, nu_ref, xs_ref, w1_ref, b1_ref, w2_ref, b2_ref, ys_ref, w1b_ref, w2b_ref):
    i = pl.program_id(0)
    used = i < nu_ref[0]
    fresh = jnp.logical_or(i == 0, be_ref[i] != be_ref[jnp.maximum(i - 1, 0)])

    @pl.when(jnp.logical_and(used, fresh))
    def _():
        w1b_ref[...] = w1_ref[...].astype(BF16)
        w2b_ref[...] = w2_ref[...].astype(BF16)

    @pl.when(used)
    def _():
        for r0 in range(0, EXPERT_ROWS, FFN_ROW_CHUNK):
            rows = slice(r0, r0 + FFN_ROW_CHUNK)
            lo, hi = _unpack_halves(xs_ref[rows, :])
            h = (jnp.dot(lo.astype(BF16), w1b_ref[0:HALF, :], preferred_element_type=F32)
                 + jnp.dot(hi.astype(BF16), w1b_ref[HALF:D_MODEL, :], preferred_element_type=F32) + b1_ref[...])
            hg = jnp.minimum(h[:, :D_FF], SWIGLU_LIMIT)
            hu = jnp.clip(h[:, D_FF:], -SWIGLU_LIMIT, SWIGLU_LIMIT)
            act = (hu + 1.0) * (hg * jax.nn.sigmoid(hg * SWIGLU_ALPHA))
            y = jnp.dot(act.astype(BF16), w2b_ref[...], preferred_element_type=F32) + b2_ref[...]
            ys_ref[rows, :] = _pack_halves(y)


def _expert_ffn(xs, blk_expert, n_used, w1, b1, w2, b2, layer, n_blocks):
    rows = pl.BlockSpec((EXPERT_ROWS, HALF), lambda i, be, nu: (jnp.minimum(i, nu[0] - 1), 0))
    return pl.pallas_call(
        _ffn_body,
        grid_spec=pltpu.PrefetchScalarGridSpec(
            num_scalar_prefetch=2,
            grid=(n_blocks,),
            in_specs=[rows,
                      pl.BlockSpec((None, None, D_MODEL, 2 * D_FF), lambda i, be, nu: (layer, be[i], 0, 0)),
                      pl.BlockSpec((None, None, 1, 2 * D_FF), lambda i, be, nu: (layer, be[i], 0, 0)),
                      pl.BlockSpec((None, None, D_FF, D_MODEL), lambda i, be, nu: (layer, be[i], 0, 0)),
                      pl.BlockSpec((None, None, 1, D_MODEL), lambda i, be, nu: (layer, be[i], 0, 0))],
            out_specs=rows,
            scratch_shapes=[pltpu.VMEM((D_MODEL, 2 * D_FF), BF16), pltpu.VMEM((D_FF, D_MODEL), BF16)]),
        out_shape=jax.ShapeDtypeStruct((xs.shape[0], HALF), U32),
        compiler_params=_params("arbitrary"),
        name="moe_expert_ffn",
    )(blk_expert, n_used, xs, w1, b1, w2, b2)


def _combine_body(seg_ref, seg_next_ref, lrow_ref, gate_ref, x_ref, g_ref, beta_ref, ys_hbm, xo_ref,
                  buf_ref, sem, *, n_tok_blocks):
    b = pl.program_id(0)
    slot = b % 2
    tm = x_ref.shape[0]
    R = buf_ref.shape[1]

    def seg_copy(s, local_row, src_row, rows):
        return pltpu.make_async_copy(ys_hbm.at[pl.ds(pl.multiple_of(src_row, SEG_ALIGN), rows)],
                                     buf_ref.at[s, pl.ds(pl.multiple_of(local_row, SEG_ALIGN), rows)], sem.at[s])

    @pl.when(b == 0)
    def _():
        buf_ref[...] = jnp.zeros_like(buf_ref)
        _segment_copies(seg_ref, seg_copy, 0, wait=False)

    @pl.when(b + 1 < n_tok_blocks)
    def _():
        _segment_copies(seg_next_ref, seg_copy, 1 - slot, wait=False)

    _segment_copies(seg_ref, seg_copy, slot, wait=True)

    lo, hi = _unpack_halves(buf_ref[slot])
    lo = lo.astype(BF16)
    hi = hi.astype(BF16)
    tc = tm // 2
    cid = lax.broadcasted_iota(I32, (tc, R), 1).astype(jnp.int16)
    for t0 in range(0, tm, tc):
        lrow = lrow_ref[t0:t0 + tc, :].astype(jnp.int16)
        gates = gate_ref[t0:t0 + tc, :].astype(BF16)
        sel = jnp.zeros((tc, R), BF16)
        for k in range(TOP_K):
            sel = jnp.where(cid == lrow[:, k:k + 1], gates[:, k:k + 1], sel)
        f = jnp.concatenate([jnp.dot(sel, lo, preferred_element_type=F32),
                             jnp.dot(sel, hi, preferred_element_type=F32)], axis=1)
        xo_ref[t0:t0 + tc, :] = _layer_norm_rows(DEEPNORM_ALPHA * x_ref[t0:t0 + tc, :] + f, g_ref[...], beta_ref[...])


def _combine_ln(lrow_t, gate_t, x, g, beta, ys, seg_table):
    T = x.shape[0]
    tm = ROW_TILE
    nb = T // tm
    R = TOP_K * tm + N_EXPERTS * SEG_ALIGN
    seg_w = seg_table.shape[-1]
    return pl.pallas_call(
        functools.partial(_combine_body, n_tok_blocks=nb),
        grid=(nb,),
        in_specs=[pl.BlockSpec((None, 1, seg_w), lambda b: (b, 0, 0), memory_space=pltpu.SMEM),
                  pl.BlockSpec((None, 1, seg_w), lambda b: (jnp.minimum(b + 1, nb - 1), 0, 0),
                               memory_space=pltpu.SMEM),
                  pl.BlockSpec((tm, TOP_K), lambda b: (b, 0)),
                  pl.BlockSpec((tm, TOP_K), lambda b: (b, 0)),
                  pl.BlockSpec((tm, D_MODEL), lambda b: (b, 0)),
                  pl.BlockSpec((1, D_MODEL), lambda b: (0, 0)),
                  pl.BlockSpec((1, D_MODEL), lambda b: (0, 0)),
                  pl.BlockSpec(memory_space=pl.ANY)],
        out_specs=pl.BlockSpec((tm, D_MODEL), lambda b: (b, 0)),
        out_shape=jax.ShapeDtypeStruct((T, D_MODEL), F32),
        scratch_shapes=[pltpu.VMEM((2, R, HALF), U32), pltpu.SemaphoreType.DMA((2,))],
        compiler_params=_params("arbitrary"),
        name="moe_combine_ln",
    )(seg_table, seg_table, lrow_t, gate_t, x, g, beta, ys)


def _dft_tables(P):
    n2 = 4 * P
    k = jnp.arange(P, dtype=I32)
    m = ((2 * k[:, None] + 1) * k[None, :]) % n2
    ang = m.astype(F32) * F32(2.0 * math.pi / n2)
    fc32, fs32 = jnp.cos(ang), -jnp.sin(ang)
    scale = F32(1.0 / P)
    gc, gs = (fc32.T * scale).astype(BF16), (fs32.T * scale).astype(BF16)
    return fc32, fs32, (fc32.astype(BF16), fs32.astype(BF16), gc, gs)


def _hyena_positional(S):
    pos = jnp.arange(S, dtype=F32)
    t = jnp.linspace(0.0, 1.0, S, dtype=F32)[:, None]
    bands = (HYENA_EMB_DIM - 1) // 2
    f = jnp.linspace(1e-4, bands - 1, bands, dtype=F32)
    ang = (2.0 * math.pi / S) * pos[:, None] * f[None, :]
    feats = jnp.concatenate([t, jnp.cos(ang), -jnp.sin(ang)], axis=-1)
    feats = jnp.pad(feats, ((0, 0), (0, LANES - HYENA_EMB_DIM)))
    max_decay = math.log(HYENA_DECAY_TARGET) / HYENA_SHORT_DECAY_PCT
    min_decay = math.log(HYENA_DECAY_TARGET) / HYENA_LONG_DECAY_PCT
    deltas = jnp.abs(jnp.linspace(min_decay, max_decay, HYENA_CH, dtype=F32))[None, :]
    return feats, t, deltas


def _alibi_slopes():
    return jnp.asarray(np.array([2.0 ** (-8.0 * (i + 1) / N_HEADS) for i in range(N_HEADS)], dtype=np.float32))


def _even_mixer(x, xshape, tabs, w_in, b_in, short_w, short_b, f1_w, f1_b, f1_freq, f2_w, f2_b, f2_freq, f3_w,
                skip, dw_w, dw_b, cln_g, cln_b, w_out, b_out, ln_g, ln_b, w_r, b_r):
    B, S = xshape
    fc32, fs32, tabs16 = tabs
    proj = _project(x, w_in.astype(BF16), b_in[None, :])
    hy = _short_conv(proj, short_w, short_b[None, :], B, S)
    u = _conformer(proj, dw_w, dw_b[None, :], cln_g[None, :], cln_b[None, :], B, S)
    feats, tcol, deltas = _hyena_positional(S)
    f1_wp = jnp.pad(f1_w, ((0, LANES - HYENA_EMB_DIM), (0, 0)))
    mlp = (f1_wp, f1_b[None, :], f1_freq[None, :], f2_w, f2_b[None, :], f2_freq[None, :], f3_w)
    taps = _hyena_filters(feats, *mlp, tcol, deltas)
    taps_rev = _hyena_filters(feats[::-1], *mlp, tcol[::-1], deltas)
    hre, him = _filter_spectrum(fc32, fs32, taps, taps_rev, S)
    z = _long_conv(hy, 2, hy, 0, tabs16, hre, him, 0, skip, B, S)
    z = _long_conv(z, 0, hy, 1, tabs16, hre, him, 1, skip, B, S)
    return _outproj_ln(z, 0, u, 0, w_out.astype(BF16), b_out[None, :], x, ln_g[None, :], ln_b[None, :],
                       w_r.T, b_r[:, None])


def _odd_mixer(x, xshape, layer_idx, w_qkv, lq1, lk1, lq2, lk2, subln_g, w_out, ln_g, ln_b, w_r, b_r):
    B, S = xshape
    lam_init = 0.8 - 0.6 * math.exp(-0.3 * layer_idx)
    lam = (jnp.exp(jnp.sum(lq1 * lk1)) - jnp.exp(jnp.sum(lq2 * lk2)) + lam_init).reshape(1)
    q_scale = jnp.concatenate([jnp.full((ATTN_W,), HEAD_DIM ** -0.5, F32), jnp.ones((2 * ATTN_W,), F32)])
    w = (w_qkv * q_scale).astype(BF16)
    qkv = _project(x, w, jnp.zeros((1, 3 * ATTN_W), F32))
    o = _diff_attention(qkv, _alibi_slopes(), lam, subln_g[None, :], lam_init, B, S)
    return _outproj_ln(o, 0, o, 1, w_out.astype(BF16), jnp.zeros((1, D_MODEL), F32), x, ln_g[None, :], ln_b[None, :],
                       w_r.T, b_r[:, None])


def _round_up(a, m):
    return (a + m - 1) // m * m


def _copy_lists(loff, goff, units):
    E = N_EXPERTS
    caps = _copy_caps()
    big = COPY_ROWS[0]
    n_big = units // (big // SEG_ALIGN)
    cum = jnp.cumsum(n_big, axis=1)
    first = (cum - n_big)[:, None, :]
    p = jnp.arange(caps[0], dtype=I32)[None, :, None]
    mine = (first <= p) & (p < cum[:, None, :])
    within = (p - first) * big
    counts = [cum[:, -1]]
    cols = [jnp.sum(jnp.where(mine, loff[:, None, :] + within, 0), axis=2),
            jnp.sum(jnp.where(mine, goff[:, None, :] + within, 0), axis=2)]
    off = n_big * big
    p = jnp.arange(E, dtype=I32)[None, :, None]
    for rows in COPY_ROWS[1:]:
        has = (units & (rows // SEG_ALIGN)) != 0
        pos = jnp.cumsum(has.astype(I32), axis=1) - has.astype(I32)
        mine = has[:, None, :] & (pos[:, None, :] == p)
        counts.append(jnp.sum(has.astype(I32), axis=1))
        cols += [jnp.sum(jnp.where(mine, (loff + off)[:, None, :], 0), axis=2),
                 jnp.sum(jnp.where(mine, (goff + off)[:, None, :], 0), axis=2)]
        off = off + jnp.where(has, rows, 0)
    return jnp.concatenate([jnp.stack(counts, axis=1)] + cols, axis=1).astype(I32)


def _routing_tables(cnt_blocks, n_ffn_blocks):
    E = N_EXPERTS
    cnt8 = _round_up(cnt_blocks[:, :, 0].astype(I32), SEG_ALIGN)
    seg_end = jnp.cumsum(cnt8, axis=1)
    loff = seg_end - cnt8
    tot8 = jnp.sum(cnt8, axis=0)
    group = _round_up(tot8, EXPERT_ROWS)
    group_end = jnp.cumsum(group)
    group_start = group_end - group
    goff = group_start[None, :] + jnp.cumsum(cnt8, axis=0) - cnt8
    seg_table = _copy_lists(loff, goff, cnt8 // SEG_ALIGN)
    starts = jnp.arange(n_ffn_blocks, dtype=I32) * EXPERT_ROWS
    blk_expert = jnp.minimum(jnp.sum((group_end[None, :] <= starts[:, None]).astype(I32), axis=1), E - 1)
    n_used = group_end[-1:] // EXPERT_ROWS
    tail_start = group_start + tot8
    return loff[:, :, None], seg_table[:, None, :], blk_expert, n_used, tail_start


def _moe_layer(x, routing, layer, w1, b1, w2, b2, ln_g, ln_b):
    T = x.shape[0]
    nb = T // ROW_TILE
    n_rows = _round_up(T * TOP_K + nb * N_EXPERTS * (SEG_ALIGN - 1), EXPERT_ROWS) + N_EXPERTS * EXPERT_ROWS
    n_ffn_blocks = n_rows // EXPERT_ROWS
    idx, gate, rank, cnt = routing
    loff, seg_table, blk_expert, n_used, tail_start = _routing_tables(cnt, n_ffn_blocks)
    xs, lrow = _dispatch(x, idx, rank, loff, seg_table, tail_start, n_rows + EXPERT_ROWS)
    ys = _expert_ffn(xs, blk_expert, n_used, w1, b1[:, :, None, :], w2, b2[:, :, None, :], layer, n_ffn_blocks)
    return _combine_ln(lrow.T, gate.T, x, ln_g[None, :], ln_b[None, :], ys, seg_table)


def kernel(x, hy_cf_w_in, hy_cf_b_in, hy_short_w, hy_short_b, hy_f1_w, hy_f1_b, hy_f1_freq, hy_f2_w, hy_f2_b, hy_f2_freq, hy_f3_w, hy_skip, cf_dw_w, cf_dw_b, cf_ln_g, cf_ln_b, even_w_out, even_b_out, attn_w_qkv, attn_lq1, attn_lk1, attn_lq2, attn_lk2, attn_subln_g, attn_w_out, ln1_g, ln1_b, ln2_g, ln2_b, moe_w_r, moe_b_r, moe_w1, moe_b1, moe_w2, moe_b2):
    B, S, D = x.shape
    assert D == D_MODEL and S % LANES == 0
    assert (B * S) % ROW_TILE == 0 and (B * S) % min(B * S, DENSE_ROW_TILE) == 0
    depth = ln1_g.shape[0]
    xf = x.reshape(B * S, D)
    tabs = _dft_tables(S // CONV_BLOCKS)
    for i in range(depth):
        j = i // 2
        if i % 2 == 0:
            xf, routing = _even_mixer(xf, (B, S), tabs, hy_cf_w_in[j], hy_cf_b_in[j], hy_short_w[j], hy_short_b[j],
                                      hy_f1_w[j], hy_f1_b[j], hy_f1_freq[j], hy_f2_w[j], hy_f2_b[j], hy_f2_freq[j],
                                      hy_f3_w[j], hy_skip[j], cf_dw_w[j], cf_dw_b[j], cf_ln_g[j], cf_ln_b[j],
                                      even_w_out[j], even_b_out[j], ln1_g[i], ln1_b[i], moe_w_r[i], moe_b_r[i])
        else:
            xf, routing = _odd_mixer(xf, (B, S), i, attn_w_qkv[j], attn_lq1[j], attn_lk1[j], attn_lq2[j],
                                     attn_lk2[j], attn_subln_g[j], attn_w_out[j], ln1_g[i], ln1_b[i],
                                     moe_w_r[i], moe_b_r[i])
        xf = _moe_layer(xf, routing, i, moe_w1, moe_b1, moe_w2, moe_b2, ln2_g[i], ln2_b[i])
    return xf.reshape(B, S, D)
```

```python
import functools
import math

import jax
import jax.numpy as jnp
import numpy as np
from jax import lax
from jax.experimental import pallas as pl
from jax.experimental.pallas import tpu as pltpu

F32 = jnp.float32
BF16 = jnp.bfloat16
U32 = jnp.uint32
I32 = jnp.int32

D_MODEL = 1024
HALF = D_MODEL // 2
DEPTH = 4
HYENA_CH = D_MODEL // 2
CONF_CH = D_MODEL // 2
HYENA_ORDER = 2
HYENA_EMB_DIM = 33
HYENA_FILTER_DIM = 64
HYENA_SHORT_DECAY_PCT = 0.3
HYENA_LONG_DECAY_PCT = 1.5
HYENA_DECAY_TARGET = 1e-2
CONF_WIDTH = 31
EVEN_IN = 3 * HYENA_CH + 2 * CONF_CH
N_HEADS = 8
HEAD_DIM = 64
ATTN_W = N_HEADS * 2 * HEAD_DIM
N_EXPERTS = 32
TOP_K = 4
D_FF = D_MODEL
SWIGLU_LIMIT = 7.0
SWIGLU_ALPHA = 1.702
DEEPNORM_ALPHA = (2 * DEPTH) ** 0.25
LN_EPS = 1e-5

LANES = 128
VMEM_LIMIT_BYTES = 56 * 1024 * 1024
ROW_TILE = 512
DENSE_ROW_TILE = 1024
EXPERT_ROWS = 512
FFN_ROW_CHUNK = 512
SEG_ALIGN = 8
COPY_ROWS = (64, 32, 16, 8)
CONV_BLOCKS = 4
CONV_CH_TILE = 256
ATTN_Q_TILE = 2048
ATTN_HEADS_PER_STEP = 1
ATTN_ROW_CHUNK = 256

_NT = (((1,), (1,)), ((), ()))


def _params(*sem):
    return pltpu.CompilerParams(dimension_semantics=sem, vmem_limit_bytes=VMEM_LIMIT_BYTES)


def _split_bf16(a):
    hi = a.astype(BF16)
    lo = (a - hi.astype(F32)).astype(BF16)
    return hi, lo


def _dot3(a, b):
    a_hi, a_lo = _split_bf16(a)
    b_hi, b_lo = _split_bf16(b)
    d = functools.partial(jnp.dot, preferred_element_type=F32)
    return d(a_hi, b_hi) + d(a_hi, b_lo) + d(a_lo, b_hi)


def _layer_norm_rows(y, g, b):
    mu = jnp.mean(y, axis=-1, keepdims=True)
    yc = y - mu
    var = jnp.mean(yc * yc, axis=-1, keepdims=True)
    return yc * lax.rsqrt(var + LN_EPS) * g + b


def _pack_halves(y):
    lo = lax.bitcast_convert_type(y[:, :HALF].astype(BF16).astype(F32), U32)
    hi = lax.bitcast_convert_type(y[:, HALF:].astype(BF16).astype(F32), U32)
    return hi | (lo >> 16)


def _unpack_halves(p):
    lo = lax.bitcast_convert_type(p << 16, F32)
    hi = lax.bitcast_convert_type(p & jnp.uint32(0xFFFF0000), F32)
    return lo, hi


def _proj_body(x_ref, w_ref, b_ref, o_ref, *, col_chunk):
    x = x_ref[...].astype(BF16)
    for j in range(0, o_ref.shape[1], col_chunk):
        acc = jnp.dot(x, w_ref[:, j:j + col_chunk], preferred_element_type=F32)
        o_ref[:, j:j + col_chunk] = (acc + b_ref[:, j:j + col_chunk]).astype(o_ref.dtype)


def _project(x, w, b):
    T, K = x.shape
    N = w.shape[1]
    tm = min(T, DENSE_ROW_TILE)
    return pl.pallas_call(
        functools.partial(_proj_body, col_chunk=512),
        grid=(T // tm,),
        in_specs=[pl.BlockSpec((tm, K), lambda i: (i, 0)),
                  pl.BlockSpec((K, N), lambda i: (0, 0)),
                  pl.BlockSpec((1, N), lambda i: (0, 0))],
        out_specs=pl.BlockSpec((tm, N), lambda i: (i, 0)),
        out_shape=jax.ShapeDtypeStruct((T, N), BF16),
        compiler_params=_params("parallel"),
        name="project",
    )(x, w, b)


def _outproj_ln_body(a1_ref, a2_ref, w1_ref, w2_ref, b_ref, x_ref, g_ref, beta_ref, wr_ref, br_ref,
                     xo_ref, idx_ref, gate_ref, rank_ref, cnt_ref):
    m = (jnp.dot(a1_ref[...], w1_ref[...], preferred_element_type=F32)
         + jnp.dot(a2_ref[...], w2_ref[...], preferred_element_type=F32) + b_ref[...])
    y = _layer_norm_rows(DEEPNORM_ALPHA * x_ref[...] + m, g_ref[...], beta_ref[...])
    xo_ref[...] = y
    w_hi, w_lo = _split_bf16(wr_ref[...])
    for s in range(y.shape[0] // ROW_TILE):
        cols = slice(s * ROW_TILE, (s + 1) * ROW_TILE)
        _route_block(y[cols, :], w_hi, w_lo, br_ref[...], idx_ref, gate_ref, rank_ref, cnt_ref.at[s], cols)


def _outproj_ln(a1, a1_col, a2, a2_col, w, b, x, g, beta, w_rt, b_r):
    T = x.shape[0]
    E = N_EXPERTS
    tm = min(T, DENSE_ROW_TILE)
    nsub = tm // ROW_TILE
    tok = pl.BlockSpec((TOP_K, tm), lambda i: (0, i))
    out = pl.pallas_call(
        _outproj_ln_body,
        grid=(T // tm,),
        in_specs=[pl.BlockSpec((tm, HALF), lambda i: (i, a1_col)),
                  pl.BlockSpec((tm, HALF), lambda i: (i, a2_col)),
                  pl.BlockSpec((HALF, D_MODEL), lambda i: (0, 0)),
                  pl.BlockSpec((HALF, D_MODEL), lambda i: (1, 0)),
                  pl.BlockSpec((1, D_MODEL), lambda i: (0, 0)),
                  pl.BlockSpec((tm, D_MODEL), lambda i: (i, 0)),
                  pl.BlockSpec((1, D_MODEL), lambda i: (0, 0)),
                  pl.BlockSpec((1, D_MODEL), lambda i: (0, 0)),
                  pl.BlockSpec((E, D_MODEL), lambda i: (0, 0)),
                  pl.BlockSpec((E, 1), lambda i: (0, 0))],
        out_specs=[pl.BlockSpec((tm, D_MODEL), lambda i: (i, 0)), tok, tok, tok,
                   pl.BlockSpec((nsub, E, LANES), lambda i: (i, 0, 0))],
        out_shape=[jax.ShapeDtypeStruct((T, D_MODEL), F32),
                   jax.ShapeDtypeStruct((TOP_K, T), I32), jax.ShapeDtypeStruct((TOP_K, T), F32),
                   jax.ShapeDtypeStruct((TOP_K, T), I32), jax.ShapeDtypeStruct((T // ROW_TILE, E, LANES), F32)],
        compiler_params=_params("parallel"),
        name="outproj_ln_route",
    )(a1, a2, w, w, b, x, g, beta, w_rt, b_r)
    return out[0], tuple(out[1:])


def _short_conv_body(x_ref, w_ref, b_ref, o_ref):
    x = x_ref[...].astype(F32)
    S = x.shape[0]
    row = lax.broadcasted_iota(I32, x.shape, 0)
    prev = jnp.where(row == 0, 0.0, pltpu.roll(x, 1, 0))
    nxt = jnp.where(row == S - 1, 0.0, pltpu.roll(x, S - 1, 0))
    y = w_ref[0:1, :] * prev + w_ref[1:2, :] * x + w_ref[2:3, :] * nxt + b_ref[...]
    o_ref[...] = y.astype(o_ref.dtype)


def _short_conv(proj, w, b, B, S):
    T = B * S
    C = HYENA_CH
    return pl.pallas_call(
        _short_conv_body,
        grid=(B, 3),
        in_specs=[pl.BlockSpec((S, C), lambda bi, j: (bi, j)),
                  pl.BlockSpec((3, C), lambda bi, j: (0, j)),
                  pl.BlockSpec((1, C), lambda bi, j: (0, j))],
        out_specs=pl.BlockSpec((S, C), lambda bi, j: (bi, j)),
        out_shape=jax.ShapeDtypeStruct((T, 3 * C), BF16),
        compiler_params=_params("parallel", "parallel"),
        name="hyena_short_conv",
    )(proj, w, b)


def _conformer_body(a_ref, g_ref, fc_ref, fs_ref, hre_ref, him_ref, gc_ref, gs_ref, b_ref, lg_ref, lb_ref, o_ref,
                    *, n_blk):
    P = fc_ref.shape[0]
    fc, fs, gc, gs = fc_ref[...], fs_ref[...], gc_ref[...], gs_ref[...]
    vre, vim = [], []
    for j in range(n_blk):
        rows = slice(j * P, (j + 1) * P)
        u = (a_ref[rows, :].astype(F32) * jax.nn.sigmoid(g_ref[rows, :].astype(F32))).astype(BF16)
        vre.append(jnp.dot(fc, u, preferred_element_type=F32))
        vim.append(jnp.dot(fs, u, preferred_element_type=F32))
    for i in range(n_blk):
        yre, yim = _mix_block_lags(vre, vim, hre_ref, him_ref, i, 1)
        y = jnp.dot(gc, yre, preferred_element_type=F32) + jnp.dot(gs, yim, preferred_element_type=F32) + b_ref[...]
        y = _layer_norm_rows(y, lg_ref[...], lb_ref[...])
        o_ref[i * P:(i + 1) * P, :] = (y * jax.nn.sigmoid(y)).astype(o_ref.dtype)


def _conformer(proj, tabs, w, b, lg, lb, B, S):
    fc32, fs32, (fc, fs, gc, gs) = tabs
    P = fc.shape[0]
    T = B * S
    C = CONF_CH
    half = CONF_WIDTH // 2
    assert half < P
    span = ((S - half, S - half - 1), (0, 0))
    hre, him = _lag_spectra(fc32, fs32, jnp.pad(w[::-1], span), jnp.pad(w, span), 1)
    tab = pl.BlockSpec((P, P), lambda bi: (0, 0))
    spec = pl.BlockSpec((3 * P, C), lambda bi: (0, 0))
    vec = pl.BlockSpec((1, C), lambda bi: (0, 0))
    return pl.pallas_call(
        functools.partial(_conformer_body, n_blk=S // P),
        grid=(B,),
        in_specs=[pl.BlockSpec((S, C), lambda bi: (bi, 3)),
                  pl.BlockSpec((S, C), lambda bi: (bi, 4)),
                  tab, tab, spec, spec, tab, tab, vec, vec, vec],
        out_specs=pl.BlockSpec((S, C), lambda bi: (bi, 0)),
        out_shape=jax.ShapeDtypeStruct((T, C), BF16),
        compiler_params=_params("parallel"),
        name="conformer_conv",
    )(proj, proj, fc, fs, hre, him, gc, gs, b, lg, lb)


def _filter_body(feat_ref, w1_ref, b1_ref, q1_ref, w2_ref, b2_ref, q2_ref, w3_ref, t_ref, delta_ref, o_ref):
    h = jnp.sin(q1_ref[...] * (_dot3(feat_ref[...], w1_ref[...]) + b1_ref[...]))
    h = jnp.sin(q2_ref[...] * (_dot3(h, w2_ref[...]) + b2_ref[...]))
    h = _dot3(h, w3_ref[...])
    o_ref[...] = h * jnp.exp(-t_ref[...] * delta_ref[...])


def _hyena_filters(feats, w1, b1, q1, w2, b2, q2, w3, tcol, deltas):
    S = feats.shape[0]
    C = HYENA_CH
    n = 2 * HYENA_ORDER
    fd = HYENA_FILTER_DIM
    return pl.pallas_call(
        _filter_body,
        grid=(n,),
        in_specs=[pl.BlockSpec((S, LANES), lambda j: (0, 0)),
                  pl.BlockSpec((LANES, fd), lambda j: (0, 0)),
                  pl.BlockSpec((1, fd), lambda j: (0, 0)),
                  pl.BlockSpec((1, fd), lambda j: (0, 0)),
                  pl.BlockSpec((fd, fd), lambda j: (0, 0)),
                  pl.BlockSpec((1, fd), lambda j: (0, 0)),
                  pl.BlockSpec((1, fd), lambda j: (0, 0)),
                  pl.BlockSpec((fd, C), lambda j: (0, j)),
                  pl.BlockSpec((S, 1), lambda j: (0, 0)),
                  pl.BlockSpec((1, C), lambda j: (0, 0))],
        out_specs=pl.BlockSpec((S, C), lambda j: (0, j)),
        out_shape=jax.ShapeDtypeStruct((S, n * C), F32),
        compiler_params=_params("parallel"),
        name="hyena_filter_mlp",
    )(feats, w1, b1, q1, w2, b2, q2, w3, tcol, deltas)


def _spectrum_body(fc_ref, fs_ref, a_ref, b_ref, hre_ref, him_ref):
    a = a_ref[...]
    row = lax.broadcasted_iota(I32, a.shape, 0)
    b = jnp.where(row == 0, 0.0, b_ref[...])
    hre_ref[...] = _dot3(fc_ref[...], a + b)
    him_ref[...] = _dot3(fs_ref[...], a - b)


def _lag_spectra(fc32, fs32, h_up, h_down, d_max):
    P = fc32.shape[0]
    n = h_up.shape[0] // (2 * P)
    n_ch = h_up.shape[1]
    C = HYENA_CH
    nd = 2 * d_max + 1
    spec = pl.BlockSpec((P, C), lambda di, o: (di, o))
    return pl.pallas_call(
        _spectrum_body,
        grid=(nd, n_ch // C),
        in_specs=[pl.BlockSpec((P, P), lambda di, o: (0, 0)),
                  pl.BlockSpec((P, P), lambda di, o: (0, 0)),
                  pl.BlockSpec((P, C), lambda di, o: (n - d_max + di, o)),
                  pl.BlockSpec((P, C), lambda di, o: (n + d_max - di, o))],
        out_specs=[spec, spec],
        out_shape=[jax.ShapeDtypeStruct((nd * P, n_ch), F32)] * 2,
        compiler_params=_params("parallel", "parallel"),
        name="block_lag_spectra",
    )(fc32, fs32, h_up, h_down)


def _hyena_spectra(fc32, fs32, taps, taps_rev, S):
    oc = HYENA_ORDER * HYENA_CH
    fwd, bwd = taps[:, :oc], taps[:, oc:]
    fwd_rev, bwd_rev = taps_rev[:, :oc], taps_rev[:, oc:]
    zero = jnp.zeros((1, oc), F32)
    h_up = jnp.concatenate([zero, bwd_rev[:S - 1], fwd], axis=0)
    h_down = jnp.concatenate([zero, fwd_rev, bwd[1:]], axis=0)
    return _lag_spectra(fc32, fs32, h_up, h_down, S // fc32.shape[0] - 1)


def _mix_block_lags(vre, vim, hre_ref, him_ref, i, d_max):
    P = vre[0].shape[0]
    yre = yim = None
    for j in range(max(0, i - d_max), min(len(vre), i + d_max + 1)):
        r0 = (i - j + d_max) * P
        hre = hre_ref[r0:r0 + P, :]
        him = him_ref[r0:r0 + P, :]
        tre = vre[j] * hre - vim[j] * him
        tim = vre[j] * him + vim[j] * hre
        yre = tre if yre is None else yre + tre
        yim = tim if yim is None else yim + tim
    return yre.astype(BF16), yim.astype(BF16)


def _long_conv_body(v_ref, gate_ref, fc_ref, fs_ref, hre_ref, him_ref, gc_ref, gs_ref, skip_ref, o_ref, *, n_blk):
    P = fc_ref.shape[0]
    fc, fs, gc, gs = fc_ref[...], fs_ref[...], gc_ref[...], gs_ref[...]
    vre, vim = [], []
    for j in range(n_blk):
        vj = v_ref[j * P:(j + 1) * P, :]
        vre.append(jnp.dot(fc, vj, preferred_element_type=F32))
        vim.append(jnp.dot(fs, vj, preferred_element_type=F32))
    for i in range(n_blk):
        yre, yim = _mix_block_lags(vre, vim, hre_ref, him_ref, i, n_blk - 1)
        y = jnp.dot(gc, yre, preferred_element_type=F32) + jnp.dot(gs, yim, preferred_element_type=F32)
        rows = slice(i * P, (i + 1) * P)
        y = y + v_ref[rows, :].astype(F32) * skip_ref[...]
        o_ref[rows, :] = (gate_ref[rows, :].astype(F32) * y).astype(o_ref.dtype)


def _long_conv(v_arr, v_col, gate_arr, gate_col, tabs, hre, him, order, skip, B, S):
    fc, fs, gc, gs = tabs
    P = fc.shape[0]
    T = B * S
    C = HYENA_CH
    nc = C // CONV_CH_TILE
    cc = CONV_CH_TILE
    n_h = hre.shape[0]
    tab = pl.BlockSpec((P, P), lambda bi, c: (0, 0))
    return pl.pallas_call(
        functools.partial(_long_conv_body, n_blk=S // P),
        grid=(B, nc),
        in_specs=[pl.BlockSpec((S, cc), lambda bi, c: (bi, v_col * nc + c)),
                  pl.BlockSpec((S, cc), lambda bi, c: (bi, gate_col * nc + c)),
                  tab, tab,
                  pl.BlockSpec((n_h, cc), lambda bi, c: (0, order * nc + c)),
                  pl.BlockSpec((n_h, cc), lambda bi, c: (0, order * nc + c)),
                  tab, tab,
                  pl.BlockSpec((1, cc), lambda bi, c: (0, c))],
        out_specs=pl.BlockSpec((S, cc), lambda bi, c: (bi, c)),
        out_shape=jax.ShapeDtypeStruct((T, C), BF16),
        compiler_params=_params("parallel", "parallel"),
        name="hyena_long_conv",
    )(v_arr, gate_arr, fc, fs, hre, him, gc, gs, skip[order][None, :])


def _attn_body(slope_ref, lam_ref, q_ref, k_ref, v_ref, g_ref, o_ref, vaug_ref, *, lam_init, row_chunk, heads):
    hg = pl.program_id(1)
    qi = pl.program_id(2)
    tq = q_ref.shape[0]
    S = k_ref.shape[0]
    hw = 2 * HEAD_DIM

    @pl.when(qi == 0)
    def _():
        ones_col = jnp.where(lax.broadcasted_iota(I32, (S, hw), 1) == 0, 1.0, 0.0).astype(BF16)
        for hh in range(heads):
            vaug_ref[hh, :, :hw] = v_ref[:, hh * hw:(hh + 1) * hw]
            vaug_ref[hh, :, hw:] = ones_col

    for hh in range(heads):
        cols = slice(hh * hw, (hh + 1) * hw)
        k = k_ref[:, cols]
        v_aug = vaug_ref[hh]
        slope = slope_ref[hg * heads + hh]
        kpos = lax.broadcasted_iota(I32, (1, S), 1).astype(F32) * slope
        for r0 in range(0, tq, row_chunk):
            q = q_ref[r0:r0 + row_chunk, cols]
            lane = lax.broadcasted_iota(I32, q.shape, 1)
            zero = jnp.zeros_like(q)
            qpos = (qi * tq + r0 + lax.broadcasted_iota(I32, (row_chunk, 1), 0)).astype(F32) * slope
            bias = lax.bitcast_convert_type(lax.bitcast_convert_type(qpos - kpos, U32) | jnp.uint32(0x80000000), F32)

            def weighted_values(qh):
                s = lax.dot_general(qh, k, _NT, preferred_element_type=F32) + bias
                e = jnp.exp((s - jnp.max(s, axis=-1, keepdims=True)).astype(BF16))
                return jnp.dot(e, v_aug, preferred_element_type=F32)

            o1 = weighted_values(jnp.where(lane < HEAD_DIM, q, zero))
            o2 = weighted_values(jnp.where(lane >= HEAD_DIM, q, zero))
            o = o1[:, :hw] * (1.0 / o1[:, hw:hw + 1]) - o2[:, :hw] * (lam_ref[0] / o2[:, hw:hw + 1])
            o = o * lax.rsqrt(jnp.mean(o * o, axis=-1, keepdims=True) + LN_EPS) * g_ref[...]
            o_ref[r0:r0 + row_chunk, cols] = (o * (1.0 - lam_init)).astype(o_ref.dtype)


def _diff_attention(qkv, slopes, lam, subln_g, lam_init, B, S):
    T = B * S
    hw = 2 * HEAD_DIM
    hp = ATTN_HEADS_PER_STEP
    ng = N_HEADS // hp
    tq = min(S, ATTN_Q_TILE)
    nq = S // tq
    smem = pl.BlockSpec(memory_space=pltpu.SMEM)
    return pl.pallas_call(
        functools.partial(_attn_body, lam_init=lam_init, row_chunk=min(tq, ATTN_ROW_CHUNK), heads=hp),
        grid=(B, ng, nq),
        in_specs=[smem, smem,
                  pl.BlockSpec((tq, hp * hw), lambda bi, h, qi: (bi * nq + qi, h)),
                  pl.BlockSpec((S, hp * hw), lambda bi, h, qi: (bi, ng + h)),
                  pl.BlockSpec((S, hp * hw), lambda bi, h, qi: (bi, 2 * ng + h)),
                  pl.BlockSpec((1, hw), lambda bi, h, qi: (0, 0))],
        out_specs=pl.BlockSpec((tq, hp * hw), lambda bi, h, qi: (bi * nq + qi, h)),
        out_shape=jax.ShapeDtypeStruct((T, ATTN_W), BF16),
        scratch_shapes=[pltpu.VMEM((hp, S, 2 * hw), BF16)],
        compiler_params=_params("parallel", "parallel", "arbitrary"),
        name="diff_attention",
    )(slopes, lam, qkv, qkv, qkv, subln_g)


def _route_block(x, w_hi, w_lo, bias, idx_ref, gate_ref, rank_ref, cnt_ref, cols):
    E = N_EXPERTS
    tm = x.shape[0]
    x_hi, x_lo = _split_bf16(x)
    nt = functools.partial(lax.dot_general, dimension_numbers=_NT, preferred_element_type=F32)
    logits = nt(w_hi, x_hi) + nt(w_lo, x_hi) + nt(w_hi, x_lo) + bias

    eid = lax.broadcasted_iota(I32, (E, tm), 0).astype(F32)
    work = logits
    vals, idxs = [], []
    for _ in range(TOP_K):
        m = jnp.max(work, axis=0, keepdims=True)
        sel = jnp.min(jnp.where(work == m, eid, float(E)), axis=0, keepdims=True)
        vals.append(m)
        idxs.append(sel)
        work = jnp.where(eid == sel, -jnp.inf, work)
    exps = [jnp.exp(v - vals[0]) for v in vals]
    denom = exps[0] + exps[1] + exps[2] + exps[3]

    chosen = jnp.zeros((E, tm), F32)
    for sel in idxs:
        chosen = chosen + jnp.where(eid == sel, 1.0, 0.0)
    earlier = jnp.where(lax.broadcasted_iota(I32, (tm, tm), 0) < lax.broadcasted_iota(I32, (tm, tm), 1), 1.0, 0.0)
    before = jnp.dot(chosen.astype(BF16), earlier.astype(BF16), preferred_element_type=F32)
    for k in range(TOP_K):
        gate_ref[k:k + 1, cols] = exps[k] / denom
        idx_ref[k:k + 1, cols] = idxs[k].astype(I32)
        rank_ref[k:k + 1, cols] = jnp.sum(jnp.where(eid == idxs[k], before, 0.0), axis=0, keepdims=True).astype(I32)
    cnt_ref[...] = jnp.broadcast_to(jnp.sum(chosen, axis=1, keepdims=True), cnt_ref.shape)


def _copy_caps():
    local_rows = TOP_K * ROW_TILE + N_EXPERTS * SEG_ALIGN
    return (local_rows // COPY_ROWS[0],) + (N_EXPERTS,) * (len(COPY_ROWS) - 1)


def _segment_copies(tab_ref, make_copy, slot, wait):
    base = len(COPY_ROWS)
    for ci, (rows, cap) in enumerate(zip(COPY_ROWS, _copy_caps())):
        def body(p, carry, base=base, rows=rows, cap=cap):
            copy = make_copy(slot, tab_ref[0, base + p], tab_ref[0, base + cap + p], rows)
            if wait:
                copy.wait()
            else:
                copy.start()
            return carry
        lax.fori_loop(0, tab_ref[0, ci], body, 0)
        base += 2 * cap


def _dispatch_body(tail_ref, seg_ref, seg_prev_ref, x_ref, idx_ref, rank_ref, loff_ref, xs_hbm, lrow_ref,
                   buf_ref, zero_ref, sem, zsem, *, n_tok_blocks):
    b = pl.program_id(0)
    slot = b % 2
    tm = x_ref.shape[0]
    R = buf_ref.shape[1]

    def seg_copy(s, local_row, dst_row, rows):
        return pltpu.make_async_copy(buf_ref.at[s, pl.ds(pl.multiple_of(local_row, SEG_ALIGN), rows)],
                                     xs_hbm.at[pl.ds(pl.multiple_of(dst_row, SEG_ALIGN), rows)], sem.at[s])

    @pl.when(b == 0)
    def _():
        zero_ref[...] = jnp.zeros_like(zero_ref)
        for e in range(N_EXPERTS):
            fill = pltpu.make_async_copy(
                zero_ref, xs_hbm.at[pl.ds(pl.multiple_of(tail_ref[e], SEG_ALIGN), EXPERT_ROWS)], zsem)
            fill.start()
            fill.wait()

    eid = lax.broadcasted_iota(I32, (N_EXPERTS, tm), 0)
    loff = loff_ref[...].astype(F32)
    rid = lax.broadcasted_iota(I32, (R, tm), 0).astype(jnp.int16)
    sel_t = jnp.zeros((R, tm), BF16)
    for k in range(TOP_K):
        base = jnp.sum(jnp.where(eid == idx_ref[k:k + 1, :], loff, 0.0), axis=0, keepdims=True).astype(I32)
        row = base + rank_ref[k:k + 1, :]
        lrow_ref[k:k + 1, :] = row
        sel_t = jnp.where(rid == row.astype(jnp.int16), jnp.ones((), BF16), sel_t)
    xb = x_ref[...].astype(BF16)
    lo = jnp.dot(sel_t, xb[:, :HALF], preferred_element_type=F32)
    hi = jnp.dot(sel_t, xb[:, HALF:], preferred_element_type=F32)
    packed = (lax.bitcast_convert_type(hi, U32) & jnp.uint32(0xFFFF0000)) | (lax.bitcast_convert_type(lo, U32) >> 16)
    buf_ref[slot] = packed

    @pl.when(b >= 1)
    def _():
        _segment_copies(seg_prev_ref, seg_copy, 1 - slot, wait=True)
    _segment_copies(seg_ref, seg_copy, slot, wait=False)

    @pl.when(b == n_tok_blocks - 1)
    def _():
        _segment_copies(seg_ref, seg_copy, slot, wait=True)


def _dispatch(x, idx, rank, loff, seg_table, tail_start, n_rows):
    T = x.shape[0]
    tm = ROW_TILE
    nb = T // tm
    R = TOP_K * tm + N_EXPERTS * SEG_ALIGN
    tok = pl.BlockSpec((TOP_K, tm), lambda b, tl: (0, b))
    seg_w = seg_table.shape[-1]
    return pl.pallas_call(
        functools.partial(_dispatch_body, n_tok_blocks=nb),
        grid_spec=pltpu.PrefetchScalarGridSpec(
            num_scalar_prefetch=1,
            grid=(nb,),
            in_specs=[pl.BlockSpec((None, 1, seg_w), lambda b, tl: (b, 0, 0), memory_space=pltpu.SMEM),
                      pl.BlockSpec((None, 1, seg_w), lambda b, tl: (jnp.maximum(b - 1, 0), 0, 0),
                                   memory_space=pltpu.SMEM),
                      pl.BlockSpec((tm, D_MODEL), lambda b, tl: (b, 0)),
                      tok, tok,
                      pl.BlockSpec((None, N_EXPERTS, 1), lambda b, tl: (b, 0, 0))],
            out_specs=[pl.BlockSpec(memory_space=pl.ANY), tok],
            scratch_shapes=[pltpu.VMEM((2, R, HALF), U32), pltpu.VMEM((EXPERT_ROWS, HALF), U32),
                            pltpu.SemaphoreType.DMA((2,)), pltpu.SemaphoreType.DMA(())]),
        out_shape=[jax.ShapeDtypeStruct((n_rows, HALF), U32), jax.ShapeDtypeStruct((TOP_K, T), I32)],
        compiler_params=_params("arbitrary"),
        name="moe_dispatch",
    )(tail_start, seg_table, seg_table, x, idx, rank, loff)


def _ffn_body(be_ref, nu_ref, xs_ref, w1_ref, b1_ref, w2_ref, b2_ref, ys_ref, w1b_ref, w2b_ref):
    i = pl.program_id(0)
    used = i < nu_ref[0]
    fresh = jnp.logical_or(i == 0, be_ref[i] != be_ref[jnp.maximum(i - 1, 0)])

    @pl.when(jnp.logical_and(used, fresh))
    def _():
        w1b_ref[...] = w1_ref[...].astype(BF16)
        w2b_ref[...] = w2_ref[...].astype(BF16)

    @pl.when(used)
    def _():
        for r0 in range(0, EXPERT_ROWS, FFN_ROW_CHUNK):
            rows = slice(r0, r0 + FFN_ROW_CHUNK)
            lo, hi = _unpack_halves(xs_ref[rows, :])
            h = (jnp.dot(lo.astype(BF16), w1b_ref[0:HALF, :], preferred_element_type=F32)
                 + jnp.dot(hi.astype(BF16), w1b_ref[HALF:D_MODEL, :], preferred_element_type=F32) + b1_ref[...])
            hg = jnp.minimum(h[:, :D_FF], SWIGLU_LIMIT)
            hu = jnp.clip(h[:, D_FF:], -SWIGLU_LIMIT, SWIGLU_LIMIT)
            act = (hu + 1.0) * (hg * jax.nn.sigmoid(hg * SWIGLU_ALPHA))
            y = jnp.dot(act.astype(BF16), w2b_ref[...], preferred_element_type=F32) + b2_ref[...]
            ys_ref[rows, :] = _pack_halves(y)


def _expert_ffn(xs, blk_expert, n_used, w1, b1, w2, b2, layer, n_blocks):
    rows = pl.BlockSpec((EXPERT_ROWS, HALF), lambda i, be, nu: (jnp.minimum(i, nu[0] - 1), 0))
    return pl.pallas_call(
        _ffn_body,
        grid_spec=pltpu.PrefetchScalarGridSpec(
            num_scalar_prefetch=2,
            grid=(n_blocks,),
            in_specs=[rows,
                      pl.BlockSpec((None, None, D_MODEL, 2 * D_FF), lambda i, be, nu: (layer, be[i], 0, 0)),
                      pl.BlockSpec((None, None, 1, 2 * D_FF), lambda i, be, nu: (layer, be[i], 0, 0)),
                      pl.BlockSpec((None, None, D_FF, D_MODEL), lambda i, be, nu: (layer, be[i], 0, 0)),
                      pl.BlockSpec((None, None, 1, D_MODEL), lambda i, be, nu: (layer, be[i], 0, 0))],
            out_specs=rows,
            scratch_shapes=[pltpu.VMEM((D_MODEL, 2 * D_FF), BF16), pltpu.VMEM((D_FF, D_MODEL), BF16)]),
        out_shape=jax.ShapeDtypeStruct((xs.shape[0], HALF), U32),
        compiler_params=_params("arbitrary"),
        name="moe_expert_ffn",
    )(blk_expert, n_used, xs, w1, b1, w2, b2)


def _combine_body(seg_ref, seg_next_ref, lrow_ref, gate_ref, x_ref, g_ref, beta_ref, ys_hbm, xo_ref,
                  buf_ref, sem, *, n_tok_blocks):
    b = pl.program_id(0)
    slot = b % 2
    tm = x_ref.shape[0]
    R = buf_ref.shape[1]

    def seg_copy(s, local_row, src_row, rows):
        return pltpu.make_async_copy(ys_hbm.at[pl.ds(pl.multiple_of(src_row, SEG_ALIGN), rows)],
                                     buf_ref.at[s, pl.ds(pl.multiple_of(local_row, SEG_ALIGN), rows)], sem.at[s])

    @pl.when(b == 0)
    def _():
        buf_ref[...] = jnp.zeros_like(buf_ref)
        _segment_copies(seg_ref, seg_copy, 0, wait=False)

    @pl.when(b + 1 < n_tok_blocks)
    def _():
        _segment_copies(seg_next_ref, seg_copy, 1 - slot, wait=False)

    _segment_copies(seg_ref, seg_copy, slot, wait=True)

    lo, hi = _unpack_halves(buf_ref[slot])
    lo = lo.astype(BF16)
    hi = hi.astype(BF16)
    tc = tm // 2
    cid = lax.broadcasted_iota(I32, (tc, R), 1).astype(jnp.int16)
    for t0 in range(0, tm, tc):
        lrow = lrow_ref[t0:t0 + tc, :].astype(jnp.int16)
        gates = gate_ref[t0:t0 + tc, :].astype(BF16)
        sel = jnp.zeros((tc, R), BF16)
        for k in range(TOP_K):
            sel = jnp.where(cid == lrow[:, k:k + 1], gates[:, k:k + 1], sel)
        f = jnp.concatenate([jnp.dot(sel, lo, preferred_element_type=F32),
                             jnp.dot(sel, hi, preferred_element_type=F32)], axis=1)
        xo_ref[t0:t0 + tc, :] = _layer_norm_rows(DEEPNORM_ALPHA * x_ref[t0:t0 + tc, :] + f, g_ref[...], beta_ref[...])


def _combine_ln(lrow_t, gate_t, x, g, beta, ys, seg_table):
    T = x.shape[0]
    tm = ROW_TILE
    nb = T // tm
    R = TOP_K * tm + N_EXPERTS * SEG_ALIGN
    seg_w = seg_table.shape[-1]
    return pl.pallas_call(
        functools.partial(_combine_body, n_tok_blocks=nb),
        grid=(nb,),
        in_specs=[pl.BlockSpec((None, 1, seg_w), lambda b: (b, 0, 0), memory_space=pltpu.SMEM),
                  pl.BlockSpec((None, 1, seg_w), lambda b: (jnp.minimum(b + 1, nb - 1), 0, 0),
                               memory_space=pltpu.SMEM),
                  pl.BlockSpec((tm, TOP_K), lambda b: (b, 0)),
                  pl.BlockSpec((tm, TOP_K), lambda b: (b, 0)),
                  pl.BlockSpec((tm, D_MODEL), lambda b: (b, 0)),
                  pl.BlockSpec((1, D_MODEL), lambda b: (0, 0)),
                  pl.BlockSpec((1, D_MODEL), lambda b: (0, 0)),
                  pl.BlockSpec(memory_space=pl.ANY)],
        out_specs=pl.BlockSpec((tm, D_MODEL), lambda b: (b, 0)),
        out_shape=jax.ShapeDtypeStruct((T, D_MODEL), F32),
        scratch_shapes=[pltpu.VMEM((2, R, HALF), U32), pltpu.SemaphoreType.DMA((2,))],
        compiler_params=_params("arbitrary"),
        name="moe_combine_ln",
    )(seg_table, seg_table, lrow_t, gate_t, x, g, beta, ys)


def _dft_tables(P):
    n2 = 4 * P
    k = jnp.arange(P, dtype=I32)
    m = ((2 * k[:, None] + 1) * k[None, :]) % n2
    ang = m.astype(F32) * F32(2.0 * math.pi / n2)
    fc32, fs32 = jnp.cos(ang), -jnp.sin(ang)
    scale = F32(1.0 / P)
    gc, gs = (fc32.T * scale).astype(BF16), (fs32.T * scale).astype(BF16)
    return fc32, fs32, (fc32.astype(BF16), fs32.astype(BF16), gc, gs)


def _hyena_positional(S):
    pos = jnp.arange(S, dtype=F32)
    t = jnp.linspace(0.0, 1.0, S, dtype=F32)[:, None]
    bands = (HYENA_EMB_DIM - 1) // 2
    f = jnp.linspace(1e-4, bands - 1, bands, dtype=F32)
    ang = (2.0 * math.pi / S) * pos[:, None] * f[None, :]
    feats = jnp.concatenate([t, jnp.cos(ang), -jnp.sin(ang)], axis=-1)
    feats = jnp.pad(feats, ((0, 0), (0, LANES - HYENA_EMB_DIM)))
    max_decay = math.log(HYENA_DECAY_TARGET) / HYENA_SHORT_DECAY_PCT
    min_decay = math.log(HYENA_DECAY_TARGET) / HYENA_LONG_DECAY_PCT
    deltas = jnp.abs(jnp.linspace(min_decay, max_decay, HYENA_CH, dtype=F32))[None, :]
    return feats, t, deltas


def _alibi_slopes():
    return jnp.asarray(np.array([2.0 ** (-8.0 * (i + 1) / N_HEADS) for i in range(N_HEADS)], dtype=np.float32))


def _even_mixer(x, xshape, tabs, w_in, b_in, short_w, short_b, f1_w, f1_b, f1_freq, f2_w, f2_b, f2_freq, f3_w,
                skip, dw_w, dw_b, cln_g, cln_b, w_out, b_out, ln_g, ln_b, w_r, b_r):
    B, S = xshape
    fc32, fs32, tabs16 = tabs
    proj = _project(x, w_in.astype(BF16), b_in[None, :])
    hy = _short_conv(proj, short_w, short_b[None, :], B, S)
    u = _conformer(proj, tabs, dw_w, dw_b[None, :], cln_g[None, :], cln_b[None, :], B, S)
    feats, tcol, deltas = _hyena_positional(S)
    f1_wp = jnp.pad(f1_w, ((0, LANES - HYENA_EMB_DIM), (0, 0)))
    mlp = (f1_wp, f1_b[None, :], f1_freq[None, :], f2_w, f2_b[None, :], f2_freq[None, :], f3_w)
    taps = _hyena_filters(feats, *mlp, tcol, deltas)
    taps_rev = _hyena_filters(feats[::-1], *mlp, tcol[::-1], deltas)
    hre, him = _hyena_spectra(fc32, fs32, taps, taps_rev, S)
    z = _long_conv(hy, 2, hy, 0, tabs16, hre, him, 0, skip, B, S)
    z = _long_conv(z, 0, hy, 1, tabs16, hre, him, 1, skip, B, S)
    return _outproj_ln(z, 0, u, 0, w_out.astype(BF16), b_out[None, :], x, ln_g[None, :], ln_b[None, :],
                       w_r.T, b_r[:, None])


def _odd_mixer(x, xshape, layer_idx, w_qkv, lq1, lk1, lq2, lk2, subln_g, w_out, ln_g, ln_b, w_r, b_r):
    B, S = xshape
    lam_init = 0.8 - 0.6 * math.exp(-0.3 * layer_idx)
    lam = (jnp.exp(jnp.sum(lq1 * lk1)) - jnp.exp(jnp.sum(lq2 * lk2)) + lam_init).reshape(1)
    q_scale = jnp.concatenate([jnp.full((ATTN_W,), HEAD_DIM ** -0.5, F32), jnp.ones((2 * ATTN_W,), F32)])
    w = (w_qkv * q_scale).astype(BF16)
    qkv = _project(x, w, jnp.zeros((1, 3 * ATTN_W), F32))
    o = _diff_attention(qkv, _alibi_slopes(), lam, subln_g[None, :], lam_init, B, S)
    return _outproj_ln(o, 0, o, 1, w_out.astype(BF16), jnp.zeros((1, D_MODEL), F32), x, ln_g[None, :], ln_b[None, :],
                       w_r.T, b_r[:, None])


def _round_up(a, m):
    return (a + m - 1) // m * m


def _copy_lists(loff, goff, units):
    E = N_EXPERTS
    caps = _copy_caps()
    big = COPY_ROWS[0]
    n_big = units // (big // SEG_ALIGN)
    cum = jnp.cumsum(n_big, axis=1)
    first = (cum - n_big)[:, None, :]
    p = jnp.arange(caps[0], dtype=I32)[None, :, None]
    mine = (first <= p) & (p < cum[:, None, :])
    within = (p - first) * big
    counts = [cum[:, -1]]
    cols = [jnp.sum(jnp.where(mine, loff[:, None, :] + within, 0), axis=2),
            jnp.sum(jnp.where(mine, goff[:, None, :] + within, 0), axis=2)]
    off = n_big * big
    p = jnp.arange(E, dtype=I32)[None, :, None]
    for rows in COPY_ROWS[1:]:
        has = (units & (rows // SEG_ALIGN)) != 0
        pos = jnp.cumsum(has.astype(I32), axis=1) - has.astype(I32)
        mine = has[:, None, :] & (pos[:, None, :] == p)
        counts.append(jnp.sum(has.astype(I32), axis=1))
        cols += [jnp.sum(jnp.where(mine, (loff + off)[:, None, :], 0), axis=2),
                 jnp.sum(jnp.where(mine, (goff + off)[:, None, :], 0), axis=2)]
        off = off + jnp.where(has, rows, 0)
    return jnp.concatenate([jnp.stack(counts, axis=1)] + cols, axis=1).astype(I32)


def _routing_tables(cnt_blocks, n_ffn_blocks):
    E = N_EXPERTS
    cnt8 = _round_up(cnt_blocks[:, :, 0].astype(I32), SEG_ALIGN)
    seg_end = jnp.cumsum(cnt8, axis=1)
    loff = seg_end - cnt8
    tot8 = jnp.sum(cnt8, axis=0)
    group = _round_up(tot8, EXPERT_ROWS)
    group_end = jnp.cumsum(group)
    group_start = group_end - group
    goff = group_start[None, :] + jnp.cumsum(cnt8, axis=0) - cnt8
    seg_table = _copy_lists(loff, goff, cnt8 // SEG_ALIGN)
    starts = jnp.arange(n_ffn_blocks, dtype=I32) * EXPERT_ROWS
    blk_expert = jnp.minimum(jnp.sum((group_end[None, :] <= starts[:, None]).astype(I32), axis=1), E - 1)
    n_used = group_end[-1:] // EXPERT_ROWS
    tail_start = group_start + tot8
    return loff[:, :, None], seg_table[:, None, :], blk_expert, n_used, tail_start


def _moe_layer(x, routing, layer, w1, b1, w2, b2, ln_g, ln_b):
    T = x.shape[0]
    nb = T // ROW_TILE
    n_rows = _round_up(T * TOP_K + nb * N_EXPERTS * (SEG_ALIGN - 1), EXPERT_ROWS) + N_EXPERTS * EXPERT_ROWS
    n_ffn_blocks = n_rows // EXPERT_ROWS
    idx, gate, rank, cnt = routing
    loff, seg_table, blk_expert, n_used, tail_start = _routing_tables(cnt, n_ffn_blocks)
    xs, lrow = _dispatch(x, idx, rank, loff, seg_table, tail_start, n_rows + EXPERT_ROWS)
    ys = _expert_ffn(xs, blk_expert, n_used, w1, b1[:, :, None, :], w2, b2[:, :, None, :], layer, n_ffn_blocks)
    return _combine_ln(lrow.T, gate.T, x, ln_g[None, :], ln_b[None, :], ys, seg_table)


def kernel(x, hy_cf_w_in, hy_cf_b_in, hy_short_w, hy_short_b, hy_f1_w, hy_f1_b, hy_f1_freq, hy_f2_w, hy_f2_b, hy_f2_freq, hy_f3_w, hy_skip, cf_dw_w, cf_dw_b, cf_ln_g, cf_ln_b, even_w_out, even_b_out, attn_w_qkv, attn_lq1, attn_lk1, attn_lq2, attn_lk2, attn_subln_g, attn_w_out, ln1_g, ln1_b, ln2_g, ln2_b, moe_w_r, moe_b_r, moe_w1, moe_b1, moe_w2, moe_b2):
    B, S, D = x.shape
    assert D == D_MODEL and S % LANES == 0
    assert (B * S) % ROW_TILE == 0 and (B * S) % min(B * S, DENSE_ROW_TILE) == 0
    depth = ln1_g.shape[0]
    xf = x.reshape(B * S, D)
    tabs = _dft_tables(S // CONV_BLOCKS)
    for i in range(depth):
        j = i // 2
        if i % 2 == 0:
            xf, routing = _even_mixer(xf, (B, S), tabs, hy_cf_w_in[j], hy_cf_b_in[j], hy_short_w[j], hy_short_b[j],
                                      hy_f1_w[j], hy_f1_b[j], hy_f1_freq[j], hy_f2_w[j], hy_f2_b[j], hy_f2_freq[j],
                                      hy_f3_w[j], hy_skip[j], cf_dw_w[j], cf_dw_b[j], cf_ln_g[j], cf_ln_b[j],
                                      even_w_out[j], even_b_out[j], ln1_g[i], ln1_b[i], moe_w_r[i], moe_b_r[i])
        else:
            xf, routing = _odd_mixer(xf, (B, S), i, attn_w_qkv[j], attn_lq1[j], attn_lk1[j], attn_lq2[j],
                                     attn_lk2[j], attn_subln_g[j], attn_w_out[j], ln1_g[i], ln1_b[i],
                                     moe_w_r[i], moe_b_r[i])
        xf = _moe_layer(xf, routing, i, moe_w1, moe_b1, moe_w2, moe_b2, ln2_g[i], ln2_b[i])
    return xf.reshape(B, S, D)
```

```python
import functools
import math

import jax
import jax.numpy as jnp
import numpy as np
from jax import lax
from jax.experimental import pallas as pl
from jax.experimental.pallas import tpu as pltpu

F32 = jnp.float32
BF16 = jnp.bfloat16
U32 = jnp.uint32
I32 = jnp.int32

D_MODEL = 1024
HALF = D_MODEL // 2
DEPTH = 4
HYENA_CH = D_MODEL // 2
CONF_CH = D_MODEL // 2
HYENA_ORDER = 2
HYENA_EMB_DIM = 33
HYENA_FILTER_DIM = 64
HYENA_SHORT_DECAY_PCT = 0.3
HYENA_LONG_DECAY_PCT = 1.5
HYENA_DECAY_TARGET = 1e-2
CONF_WIDTH = 31
EVEN_IN = 3 * HYENA_CH + 2 * CONF_CH
N_HEADS = 8
HEAD_DIM = 64
ATTN_W = N_HEADS * 2 * HEAD_DIM
N_EXPERTS = 32
TOP_K = 4
D_FF = D_MODEL
SWIGLU_LIMIT = 7.0
SWIGLU_ALPHA = 1.702
DEEPNORM_ALPHA = (2 * DEPTH) ** 0.25
LN_EPS = 1e-5

LANES = 128
VMEM_LIMIT_BYTES = 56 * 1024 * 1024
ROW_TILE = 512
DENSE_ROW_TILE = 1024
EXPERT_ROWS = 512
FFN_ROW_CHUNK = 512
SEG_ALIGN = 8
COPY_ROWS = (64, 32, 16, 8)
CONV_BLOCKS = 4
CONV_CH_TILE = 256
ATTN_Q_TILE = 2048
ATTN_HEADS_PER_STEP = 1
ATTN_ROW_CHUNK = 256

_NT = (((1,), (1,)), ((), ()))


def _params(*sem):
    return pltpu.CompilerParams(dimension_semantics=sem, vmem_limit_bytes=VMEM_LIMIT_BYTES)


def _split_bf16(a):
    hi = a.astype(BF16)
    lo = (a - hi.astype(F32)).astype(BF16)
    return hi, lo


def _dot3(a, b):
    a_hi, a_lo = _split_bf16(a)
    b_hi, b_lo = _split_bf16(b)
    d = functools.partial(jnp.dot, preferred_element_type=F32)
    return d(a_hi, b_hi) + d(a_hi, b_lo) + d(a_lo, b_hi)


def _layer_norm_rows(y, g, b):
    mu = jnp.mean(y, axis=-1, keepdims=True)
    yc = y - mu
    var = jnp.mean(yc * yc, axis=-1, keepdims=True)
    return yc * lax.rsqrt(var + LN_EPS) * g + b


def _pack_halves(y):
    lo = lax.bitcast_convert_type(y[:, :HALF].astype(BF16).astype(F32), U32)
    hi = lax.bitcast_convert_type(y[:, HALF:].astype(BF16).astype(F32), U32)
    return hi | (lo >> 16)


def _unpack_halves(p):
    lo = lax.bitcast_convert_type(p << 16, F32)
    hi = lax.bitcast_convert_type(p & jnp.uint32(0xFFFF0000), F32)
    return lo, hi


def _proj_body(x_ref, w_ref, b_ref, o_ref, *, col_chunk):
    x = x_ref[...].astype(BF16)
    for j in range(0, o_ref.shape[1], col_chunk):
        acc = jnp.dot(x, w_ref[:, j:j + col_chunk], preferred_element_type=F32)
        o_ref[:, j:j + col_chunk] = (acc + b_ref[:, j:j + col_chunk]).astype(o_ref.dtype)


def _project(x, w, b):
    T, K = x.shape
    N = w.shape[1]
    tm = min(T, DENSE_ROW_TILE)
    return pl.pallas_call(
        functools.partial(_proj_body, col_chunk=512),
        grid=(T // tm,),
        in_specs=[pl.BlockSpec((tm, K), lambda i: (i, 0)),
                  pl.BlockSpec((K, N), lambda i: (0, 0)),
                  pl.BlockSpec((1, N), lambda i: (0, 0))],
        out_specs=pl.BlockSpec((tm, N), lambda i: (i, 0)),
        out_shape=jax.ShapeDtypeStruct((T, N), BF16),
        compiler_params=_params("parallel"),
        name="project",
    )(x, w, b)


def _outproj_ln_body(a1_ref, a2_ref, w1_ref, w2_ref, b_ref, x_ref, g_ref, beta_ref, wr_ref, br_ref,
                     xo_ref, idx_ref, gate_ref, rank_ref, cnt_ref):
    m = (jnp.dot(a1_ref[...], w1_ref[...], preferred_element_type=F32)
         + jnp.dot(a2_ref[...], w2_ref[...], preferred_element_type=F32) + b_ref[...])
    y = _layer_norm_rows(DEEPNORM_ALPHA * x_ref[...] + m, g_ref[...], beta_ref[...])
    xo_ref[...] = y
    w_hi, w_lo = _split_bf16(wr_ref[...])
    for s in range(y.shape[0] // ROW_TILE):
        cols = slice(s * ROW_TILE, (s + 1) * ROW_TILE)
        _route_block(y[cols, :], w_hi, w_lo, br_ref[...], idx_ref, gate_ref, rank_ref, cnt_ref.at[s], cols)


def _outproj_ln(a1, a1_col, a2, a2_col, w, b, x, g, beta, w_rt, b_r):
    T = x.shape[0]
    E = N_EXPERTS
    tm = min(T, DENSE_ROW_TILE)
    nsub = tm // ROW_TILE
    tok = pl.BlockSpec((TOP_K, tm), lambda i: (0, i))
    out = pl.pallas_call(
        _outproj_ln_body,
        grid=(T // tm,),
        in_specs=[pl.BlockSpec((tm, HALF), lambda i: (i, a1_col)),
                  pl.BlockSpec((tm, HALF), lambda i: (i, a2_col)),
                  pl.BlockSpec((HALF, D_MODEL), lambda i: (0, 0)),
                  pl.BlockSpec((HALF, D_MODEL), lambda i: (1, 0)),
                  pl.BlockSpec((1, D_MODEL), lambda i: (0, 0)),
                  pl.BlockSpec((tm, D_MODEL), lambda i: (i, 0)),
                  pl.BlockSpec((1, D_MODEL), lambda i: (0, 0)),
                  pl.BlockSpec((1, D_MODEL), lambda i: (0, 0)),
                  pl.BlockSpec((E, D_MODEL), lambda i: (0, 0)),
                  pl.BlockSpec((E, 1), lambda i: (0, 0))],
        out_specs=[pl.BlockSpec((tm, D_MODEL), lambda i: (i, 0)), tok, tok, tok,
                   pl.BlockSpec((nsub, E, LANES), lambda i: (i, 0, 0))],
        out_shape=[jax.ShapeDtypeStruct((T, D_MODEL), F32),
                   jax.ShapeDtypeStruct((TOP_K, T), I32), jax.ShapeDtypeStruct((TOP_K, T), F32),
                   jax.ShapeDtypeStruct((TOP_K, T), I32), jax.ShapeDtypeStruct((T // ROW_TILE, E, LANES), F32)],
        compiler_params=_params("parallel"),
        name="outproj_ln_route",
    )(a1, a2, w, w, b, x, g, beta, w_rt, b_r)
    return out[0], tuple(out[1:])


def _short_conv_body(x_ref, w_ref, b_ref, o_ref):
    x = x_ref[...].astype(F32)
    S = x.shape[0]
    row = lax.broadcasted_iota(I32, x.shape, 0)
    prev = jnp.where(row == 0, 0.0, pltpu.roll(x, 1, 0))
    nxt = jnp.where(row == S - 1, 0.0, pltpu.roll(x, S - 1, 0))
    y = w_ref[0:1, :] * prev + w_ref[1:2, :] * x + w_ref[2:3, :] * nxt + b_ref[...]
    o_ref[...] = y.astype(o_ref.dtype)


def _short_conv(proj, w, b, B, S):
    T = B * S
    C = HYENA_CH
    return pl.pallas_call(
        _short_conv_body,
        grid=(B, 3),
        in_specs=[pl.BlockSpec((S, C), lambda bi, j: (bi, j)),
                  pl.BlockSpec((3, C), lambda bi, j: (0, j)),
                  pl.BlockSpec((1, C), lambda bi, j: (0, j))],
        out_specs=pl.BlockSpec((S, C), lambda bi, j: (bi, j)),
        out_shape=jax.ShapeDtypeStruct((T, 3 * C), BF16),
        compiler_params=_params("parallel", "parallel"),
        name="hyena_short_conv",
    )(proj, w, b)


def _conformer_body(a_ref, g_ref, fc_ref, fs_ref, hre_ref, him_ref, gc_ref, gs_ref, b_ref, lg_ref, lb_ref, o_ref,
                    *, n_blk):
    P = fc_ref.shape[0]
    fc, fs, gc, gs = fc_ref[...], fs_ref[...], gc_ref[...], gs_ref[...]
    vre, vim = [], []
    for j in range(n_blk):
        rows = slice(j * P, (j + 1) * P)
        u = (a_ref[rows, :].astype(F32) * jax.nn.sigmoid(g_ref[rows, :].astype(F32))).astype(BF16)
        vre.append(jnp.dot(fc, u, preferred_element_type=F32))
        vim.append(jnp.dot(fs, u, preferred_element_type=F32))
    for i in range(n_blk):
        yre, yim = _mix_block_lags(vre, vim, hre_ref, him_ref, i, 1)
        y = jnp.dot(gc, yre, preferred_element_type=F32) + jnp.dot(gs, yim, preferred_element_type=F32) + b_ref[...]
        y = _layer_norm_rows(y, lg_ref[...], lb_ref[...])
        o_ref[i * P:(i + 1) * P, :] = (y * jax.nn.sigmoid(y)).astype(o_ref.dtype)


def _conformer(proj, tabs, w, b, lg, lb, B, S):
    fc32, fs32, (fc, fs, gc, gs) = tabs
    P = fc.shape[0]
    T = B * S
    C = CONF_CH
    half = CONF_WIDTH // 2
    assert half < P
    span = ((S - half, S - half - 1), (0, 0))
    hre, him = _lag_spectra(fc32, fs32, jnp.pad(w[::-1], span), jnp.pad(w, span), 1)
    tab = pl.BlockSpec((P, P), lambda bi: (0, 0))
    spec = pl.BlockSpec((3 * P, C), lambda bi: (0, 0))
    vec = pl.BlockSpec((1, C), lambda bi: (0, 0))
    return pl.pallas_call(
        functools.partial(_conformer_body, n_blk=S // P),
        grid=(B,),
        in_specs=[pl.BlockSpec((S, C), lambda bi: (bi, 3)),
                  pl.BlockSpec((S, C), lambda bi: (bi, 4)),
                  tab, tab, spec, spec, tab, tab, vec, vec, vec],
        out_specs=pl.BlockSpec((S, C), lambda bi: (bi, 0)),
        out_shape=jax.ShapeDtypeStruct((T, C), BF16),
        compiler_params=_params("parallel"),
        name="conformer_conv",
    )(proj, proj, fc, fs, hre, him, gc, gs, b, lg, lb)


def _filter_body(feat_ref, w1_ref, b1_ref, q1_ref, w2_ref, b2_ref, q2_ref, w3_ref, t_ref, delta_ref, o_ref):
    h = jnp.sin(q1_ref[...] * (_dot3(feat_ref[...], w1_ref[...]) + b1_ref[...]))
    h = jnp.sin(q2_ref[...] * (_dot3(h, w2_ref[...]) + b2_ref[...]))
    h = _dot3(h, w3_ref[...])
    o_ref[...] = h * jnp.exp(-t_ref[...] * delta_ref[...])


def _hyena_filters(feats2, w1, b1, q1, w2, b2, q2, w3, tcol2, deltas):
    S = feats2.shape[1]
    C = HYENA_CH
    fd = HYENA_FILTER_DIM
    fixed = lambda p, d, o: (0, 0)
    return pl.pallas_call(
        _filter_body,
        grid=(2, 2, HYENA_ORDER),
        in_specs=[pl.BlockSpec((None, S, LANES), lambda p, d, o: (p, 0, 0)),
                  pl.BlockSpec((LANES, fd), fixed),
                  pl.BlockSpec((1, fd), fixed),
                  pl.BlockSpec((1, fd), fixed),
                  pl.BlockSpec((fd, fd), fixed),
                  pl.BlockSpec((1, fd), fixed),
                  pl.BlockSpec((1, fd), fixed),
                  pl.BlockSpec((fd, C), lambda p, d, o: (0, d * HYENA_ORDER + o)),
                  pl.BlockSpec((None, S, 1), lambda p, d, o: (p, 0, 0)),
                  pl.BlockSpec((1, C), fixed)],
        out_specs=pl.BlockSpec((None, S, C), lambda p, d, o: ((d + 1 - p) % 2, p, o)),
        out_shape=jax.ShapeDtypeStruct((2, 2 * S, HYENA_ORDER * C), F32),
        compiler_params=_params("parallel", "parallel", "parallel"),
        name="hyena_filter_mlp",
    )(feats2, w1, b1, q1, w2, b2, q2, w3, tcol2, deltas)


def _spectrum_body(fc_ref, fs_ref, a_ref, b_ref, hre_ref, him_ref):
    a = a_ref[...]
    row = lax.broadcasted_iota(I32, a.shape, 0)
    b = jnp.where(row == 0, 0.0, b_ref[...])
    hre_ref[...] = _dot3(fc_ref[...], a + b)
    him_ref[...] = _dot3(fs_ref[...], a - b)


def _lag_spectra(fc32, fs32, h_up, h_down, d_max):
    P = fc32.shape[0]
    n = h_up.shape[0] // (2 * P)
    n_ch = h_up.shape[1]
    C = HYENA_CH
    nd = 2 * d_max + 1
    spec = pl.BlockSpec((P, C), lambda di, o: (di, o))
    return pl.pallas_call(
        _spectrum_body,
        grid=(nd, n_ch // C),
        in_specs=[pl.BlockSpec((P, P), lambda di, o: (0, 0)),
                  pl.BlockSpec((P, P), lambda di, o: (0, 0)),
                  pl.BlockSpec((P, C), lambda di, o: (n - d_max + di, o)),
                  pl.BlockSpec((P, C), lambda di, o: (n + d_max - di, o))],
        out_specs=[spec, spec],
        out_shape=[jax.ShapeDtypeStruct((nd * P, n_ch), F32)] * 2,
        compiler_params=_params("parallel", "parallel"),
        name="block_lag_spectra",
    )(fc32, fs32, h_up, h_down)


def _mix_block_lags(vre, vim, hre_ref, him_ref, i, d_max):
    P = vre[0].shape[0]
    yre = yim = None
    for j in range(max(0, i - d_max), min(len(vre), i + d_max + 1)):
        r0 = (i - j + d_max) * P
        hre = hre_ref[r0:r0 + P, :]
        him = him_ref[r0:r0 + P, :]
        tre = vre[j] * hre - vim[j] * him
        tim = vre[j] * him + vim[j] * hre
        yre = tre if yre is None else yre + tre
        yim = tim if yim is None else yim + tim
    return yre.astype(BF16), yim.astype(BF16)


def _long_conv_body(v_ref, gate_ref, fc_ref, fs_ref, hre_ref, him_ref, gc_ref, gs_ref, skip_ref, o_ref, *, n_blk):
    P = fc_ref.shape[0]
    fc, fs, gc, gs = fc_ref[...], fs_ref[...], gc_ref[...], gs_ref[...]
    vre, vim = [], []
    for j in range(n_blk):
        vj = v_ref[j * P:(j + 1) * P, :]
        vre.append(jnp.dot(fc, vj, preferred_element_type=F32))
        vim.append(jnp.dot(fs, vj, preferred_element_type=F32))
    for i in range(n_blk):
        yre, yim = _mix_block_lags(vre, vim, hre_ref, him_ref, i, n_blk - 1)
        y = jnp.dot(gc, yre, preferred_element_type=F32) + jnp.dot(gs, yim, preferred_element_type=F32)
        rows = slice(i * P, (i + 1) * P)
        y = y + v_ref[rows, :].astype(F32) * skip_ref[...]
        o_ref[rows, :] = (gate_ref[rows, :].astype(F32) * y).astype(o_ref.dtype)


def _long_conv(v_arr, v_col, gate_arr, gate_col, tabs, hre, him, order, skip, B, S):
    fc, fs, gc, gs = tabs
    P = fc.shape[0]
    T = B * S
    C = HYENA_CH
    nc = C // CONV_CH_TILE
    cc = CONV_CH_TILE
    n_h = hre.shape[0]
    tab = pl.BlockSpec((P, P), lambda bi, c: (0, 0))
    return pl.pallas_call(
        functools.partial(_long_conv_body, n_blk=S // P),
        grid=(B, nc),
        in_specs=[pl.BlockSpec((S, cc), lambda bi, c: (bi, v_col * nc + c)),
                  pl.BlockSpec((S, cc), lambda bi, c: (bi, gate_col * nc + c)),
                  tab, tab,
                  pl.BlockSpec((n_h, cc), lambda bi, c: (0, order * nc + c)),
                  pl.BlockSpec((n_h, cc), lambda bi, c: (0, order * nc + c)),
                  tab, tab,
                  pl.BlockSpec((1, cc), lambda bi, c: (0, c))],
        out_specs=pl.BlockSpec((S, cc), lambda bi, c: (bi, c)),
        out_shape=jax.ShapeDtypeStruct((T, C), BF16),
        compiler_params=_params("parallel", "parallel"),
        name="hyena_long_conv",
    )(v_arr, gate_arr, fc, fs, hre, him, gc, gs, skip[order][None, :])


def _attn_body(slope_ref, lam_ref, q_ref, k_ref, v_ref, g_ref, o_ref, vaug_ref, *, lam_init, row_chunk, heads):
    hg = pl.program_id(1)
    qi = pl.program_id(2)
    tq = q_ref.shape[0]
    S = k_ref.shape[0]
    hw = 2 * HEAD_DIM

    @pl.when(qi == 0)
    def _():
        ones_col = jnp.where(lax.broadcasted_iota(I32, (S, hw), 1) == 0, 1.0, 0.0).astype(BF16)
        for hh in range(heads):
            vaug_ref[hh, :, :hw] = v_ref[:, hh * hw:(hh + 1) * hw]
            vaug_ref[hh, :, hw:] = ones_col

    for hh in range(heads):
        cols = slice(hh * hw, (hh + 1) * hw)
        k = k_ref[:, cols]
        v_aug = vaug_ref[hh]
        slope = slope_ref[hg * heads + hh]
        kpos = lax.broadcasted_iota(I32, (1, S), 1).astype(F32) * slope
        for r0 in range(0, tq, row_chunk):
            q = q_ref[r0:r0 + row_chunk, cols]
            lane = lax.broadcasted_iota(I32, q.shape, 1)
            zero = jnp.zeros_like(q)
            qpos = (qi * tq + r0 + lax.broadcasted_iota(I32, (row_chunk, 1), 0)).astype(F32) * slope
            bias = lax.bitcast_convert_type(lax.bitcast_convert_type(qpos - kpos, U32) | jnp.uint32(0x80000000), F32)

            def weighted_values(qh):
                s = lax.dot_general(qh, k, _NT, preferred_element_type=F32) + bias
                e = jnp.exp((s - jnp.max(s, axis=-1, keepdims=True)).astype(BF16))
                return jnp.dot(e, v_aug, preferred_element_type=F32)

            o1 = weighted_values(jnp.where(lane < HEAD_DIM, q, zero))
            o2 = weighted_values(jnp.where(lane >= HEAD_DIM, q, zero))
            o = o1[:, :hw] * (1.0 / o1[:, hw:hw + 1]) - o2[:, :hw] * (lam_ref[0] / o2[:, hw:hw + 1])
            o = o * lax.rsqrt(jnp.mean(o * o, axis=-1, keepdims=True) + LN_EPS) * g_ref[...]
            o_ref[r0:r0 + row_chunk, cols] = (o * (1.0 - lam_init)).astype(o_ref.dtype)


def _diff_attention(qkv, slopes, lam, subln_g, lam_init, B, S):
    T = B * S
    hw = 2 * HEAD_DIM
    hp = ATTN_HEADS_PER_STEP
    ng = N_HEADS // hp
    tq = min(S, ATTN_Q_TILE)
    nq = S // tq
    smem = pl.BlockSpec(memory_space=pltpu.SMEM)
    return pl.pallas_call(
        functools.partial(_attn_body, lam_init=lam_init, row_chunk=min(tq, ATTN_ROW_CHUNK), heads=hp),
        grid=(B, ng, nq),
        in_specs=[smem, smem,
                  pl.BlockSpec((tq, hp * hw), lambda bi, h, qi: (bi * nq + qi, h)),
                  pl.BlockSpec((S, hp * hw), lambda bi, h, qi: (bi, ng + h)),
                  pl.BlockSpec((S, hp * hw), lambda bi, h, qi: (bi, 2 * ng + h)),
                  pl.BlockSpec((1, hw), lambda bi, h, qi: (0, 0))],
        out_specs=pl.BlockSpec((tq, hp * hw), lambda bi, h, qi: (bi * nq + qi, h)),
        out_shape=jax.ShapeDtypeStruct((T, ATTN_W), BF16),
        scratch_shapes=[pltpu.VMEM((hp, S, 2 * hw), BF16)],
        compiler_params=_params("parallel", "parallel", "arbitrary"),
        name="diff_attention",
    )(slopes, lam, qkv, qkv, qkv, subln_g)


def _route_block(x, w_hi, w_lo, bias, idx_ref, gate_ref, rank_ref, cnt_ref, cols):
    E = N_EXPERTS
    tm = x.shape[0]
    x_hi, x_lo = _split_bf16(x)
    nt = functools.partial(lax.dot_general, dimension_numbers=_NT, preferred_element_type=F32)
    logits = nt(w_hi, x_hi) + nt(w_lo, x_hi) + nt(w_hi, x_lo) + bias

    eid = lax.broadcasted_iota(I32, (E, tm), 0).astype(F32)
    work = logits
    vals, idxs = [], []
    for _ in range(TOP_K):
        m = jnp.max(work, axis=0, keepdims=True)
        sel = jnp.min(jnp.where(work == m, eid, float(E)), axis=0, keepdims=True)
        vals.append(m)
        idxs.append(sel)
        work = jnp.where(eid == sel, -jnp.inf, work)
    exps = [jnp.exp(v - vals[0]) for v in vals]
    denom = exps[0] + exps[1] + exps[2] + exps[3]

    chosen = jnp.zeros((E, tm), F32)
    for sel in idxs:
        chosen = chosen + jnp.where(eid == sel, 1.0, 0.0)
    earlier = jnp.where(lax.broadcasted_iota(I32, (tm, tm), 0) < lax.broadcasted_iota(I32, (tm, tm), 1), 1.0, 0.0)
    before = jnp.dot(chosen.astype(BF16), earlier.astype(BF16), preferred_element_type=F32)
    for k in range(TOP_K):
        gate_ref[k:k + 1, cols] = exps[k] / denom
        idx_ref[k:k + 1, cols] = idxs[k].astype(I32)
        rank_ref[k:k + 1, cols] = jnp.sum(jnp.where(eid == idxs[k], before, 0.0), axis=0, keepdims=True).astype(I32)
    cnt_ref[...] = jnp.broadcast_to(jnp.sum(chosen, axis=1, keepdims=True), cnt_ref.shape)


def _copy_caps():
    local_rows = TOP_K * ROW_TILE + N_EXPERTS * SEG_ALIGN
    return (local_rows // COPY_ROWS[0],) + (N_EXPERTS,) * (len(COPY_ROWS) - 1)


def _segment_copies(tab_ref, make_copy, slot, wait):
    base = len(COPY_ROWS)
    for ci, (rows, cap) in enumerate(zip(COPY_ROWS, _copy_caps())):
        def body(p, carry, base=base, rows=rows, cap=cap):
            copy = make_copy(slot, tab_ref[0, base + p], tab_ref[0, base + cap + p], rows)
            if wait:
                copy.wait()
            else:
                copy.start()
            return carry
        lax.fori_loop(0, tab_ref[0, ci], body, 0)
        base += 2 * cap


def _dispatch_body(tail_ref, seg_ref, seg_prev_ref, x_ref, idx_ref, rank_ref, loff_ref, xs_hbm, lrow_ref,
                   buf_ref, zero_ref, sem, zsem, *, n_tok_blocks):
    b = pl.program_id(0)
    slot = b % 2
    tm = x_ref.shape[0]
    R = buf_ref.shape[1]

    def seg_copy(s, local_row, dst_row, rows):
        return pltpu.make_async_copy(buf_ref.at[s, pl.ds(pl.multiple_of(local_row, SEG_ALIGN), rows)],
                                     xs_hbm.at[pl.ds(pl.multiple_of(dst_row, SEG_ALIGN), rows)], sem.at[s])

    @pl.when(b == 0)
    def _():
        zero_ref[...] = jnp.zeros_like(zero_ref)
        for e in range(N_EXPERTS):
            fill = pltpu.make_async_copy(
                zero_ref, xs_hbm.at[pl.ds(pl.multiple_of(tail_ref[e], SEG_ALIGN), EXPERT_ROWS)], zsem)
            fill.start()
            fill.wait()

    eid = lax.broadcasted_iota(I32, (N_EXPERTS, tm), 0)
    loff = loff_ref[...].astype(F32)
    rid = lax.broadcasted_iota(I32, (R, tm), 0).astype(jnp.int16)
    sel_t = jnp.zeros((R, tm), BF16)
    for k in range(TOP_K):
        base = jnp.sum(jnp.where(eid == idx_ref[k:k + 1, :], loff, 0.0), axis=0, keepdims=True).astype(I32)
        row = base + rank_ref[k:k + 1, :]
        lrow_ref[k:k + 1, :] = row
        sel_t = jnp.where(rid == row.astype(jnp.int16), jnp.ones((), BF16), sel_t)
    xb = x_ref[...].astype(BF16)
    lo = jnp.dot(sel_t, xb[:, :HALF], preferred_element_type=F32)
    hi = jnp.dot(sel_t, xb[:, HALF:], preferred_element_type=F32)
    packed = (lax.bitcast_convert_type(hi, U32) & jnp.uint32(0xFFFF0000)) | (lax.bitcast_convert_type(lo, U32) >> 16)
    buf_ref[slot] = packed

    @pl.when(b >= 1)
    def _():
        _segment_copies(seg_prev_ref, seg_copy, 1 - slot, wait=True)
    _segment_copies(seg_ref, seg_copy, slot, wait=False)

    @pl.when(b == n_tok_blocks - 1)
    def _():
        _segment_copies(seg_ref, seg_copy, slot, wait=True)


def _dispatch(x, idx, rank, loff, seg_table, tail_start, n_rows):
    T = x.shape[0]
    tm = ROW_TILE
    nb = T // tm
    R = TOP_K * tm + N_EXPERTS * SEG_ALIGN
    tok = pl.BlockSpec((TOP_K, tm), lambda b, tl: (0, b))
    seg_w = seg_table.shape[-1]
    return pl.pallas_call(
        functools.partial(_dispatch_body, n_tok_blocks=nb),
        grid_spec=pltpu.PrefetchScalarGridSpec(
            num_scalar_prefetch=1,
            grid=(nb,),
            in_specs=[pl.BlockSpec((None, 1, seg_w), lambda b, tl: (b, 0, 0), memory_space=pltpu.SMEM),
                      pl.BlockSpec((None, 1, seg_w), lambda b, tl: (jnp.maximum(b - 1, 0), 0, 0),
                                   memory_space=pltpu.SMEM),
                      pl.BlockSpec((tm, D_MODEL), lambda b, tl: (b, 0)),
                      tok, tok,
                      pl.BlockSpec((None, N_EXPERTS, 1), lambda b, tl: (b, 0, 0))],
            out_specs=[pl.BlockSpec(memory_space=pl.ANY), tok],
            scratch_shapes=[pltpu.VMEM((2, R, HALF), U32), pltpu.VMEM((EXPERT_ROWS, HALF), U32),
                            pltpu.SemaphoreType.DMA((2,)), pltpu.SemaphoreType.DMA(())]),
        out_shape=[jax.ShapeDtypeStruct((n_rows, HALF), U32), jax.ShapeDtypeStruct((TOP_K, T), I32)],
        compiler_params=_params("arbitrary"),
        name="moe_dispatch",
    )(tail_start, seg_table, seg_table, x, idx, rank, loff)


def _ffn_body(be_ref, nu_ref, xs_ref, w1_ref, b1_ref, w2_ref, b2_ref, ys_ref, w1b_ref, w2b_ref):
    i = pl.program_id(0)
    used = i < nu_ref[0]
    fresh = jnp.logical_or(i == 0, be_ref[i] != be_ref[jnp.maximum(i - 1, 0)])

    @pl.when(jnp.logical_and(used, fresh))
    def _():
        w1b_ref[...] = w1_ref[...].astype(BF16)
        w2b_ref[...] = w2_ref[...].astype(BF16)

    @pl.when(used)
    def _():
        for r0 in range(0, EXPERT_ROWS, FFN_ROW_CHUNK):
            rows = slice(r0, r0 + FFN_ROW_CHUNK)
            lo, hi = _unpack_halves(xs_ref[rows, :])
            h = (jnp.dot(lo.astype(BF16), w1b_ref[0:HALF, :], preferred_element_type=F32)
                 + jnp.dot(hi.astype(BF16), w1b_ref[HALF:D_MODEL, :], preferred_element_type=F32) + b1_ref[...])
            hg = jnp.minimum(h[:, :D_FF], SWIGLU_LIMIT)
            hu = jnp.clip(h[:, D_FF:], -SWIGLU_LIMIT, SWIGLU_LIMIT)
            act = (hu + 1.0) * (hg * jax.nn.sigmoid(hg * SWIGLU_ALPHA))
            y = jnp.dot(act.astype(BF16), w2b_ref[...], preferred_element_type=F32) + b2_ref[...]
            ys_ref[rows, :] = _pack_halves(y)


def _expert_ffn(xs, blk_expert, n_used, w1, b1, w2, b2, layer, n_blocks):
    rows = pl.BlockSpec((EXPERT_ROWS, HALF), lambda i, be, nu: (jnp.minimum(i, nu[0] - 1), 0))
    return pl.pallas_call(
        _ffn_body,
        grid_spec=pltpu.PrefetchScalarGridSpec(
            num_scalar_prefetch=2,
            grid=(n_blocks,),
            in_specs=[rows,
                      pl.BlockSpec((None, None, D_MODEL, 2 * D_FF), lambda i, be, nu: (layer, be[i], 0, 0)),
                      pl.BlockSpec((None, None, 1, 2 * D_FF), lambda i, be, nu: (layer, be[i], 0, 0)),
                      pl.BlockSpec((None, None, D_FF, D_MODEL), lambda i, be, nu: (layer, be[i], 0, 0)),
                      pl.BlockSpec((None, None, 1, D_MODEL), lambda i, be, nu: (layer, be[i], 0, 0))],
            out_specs=rows,
            scratch_shapes=[pltpu.VMEM((D_MODEL, 2 * D_FF), BF16), pltpu.VMEM((D_FF, D_MODEL), BF16)]),
        out_shape=jax.ShapeDtypeStruct((xs.shape[0], HALF), U32),
        compiler_params=_params("arbitrary"),
        name="moe_expert_ffn",
    )(blk_expert, n_used, xs, w1, b1, w2, b2)


def _combine_body(seg_ref, seg_next_ref, lrow_ref, gate_ref, x_ref, g_ref, beta_ref, ys_hbm, xo_ref,
                  buf_ref, sem, *, n_tok_blocks):
    b = pl.program_id(0)
    slot = b % 2
    tm = x_ref.shape[0]
    R = buf_ref.shape[1]

    def seg_copy(s, local_row, src_row, rows):
        return pltpu.make_async_copy(ys_hbm.at[pl.ds(pl.multiple_of(src_row, SEG_ALIGN), rows)],
                                     buf_ref.at[s, pl.ds(pl.multiple_of(local_row, SEG_ALIGN), rows)], sem.at[s])

    @pl.when(b == 0)
    def _():
        buf_ref[...] = jnp.zeros_like(buf_ref)
        _segment_copies(seg_ref, seg_copy, 0, wait=False)

    @pl.when(b + 1 < n_tok_blocks)
    def _():
        _segment_copies(seg_next_ref, seg_copy, 1 - slot, wait=False)

    _segment_copies(seg_ref, seg_copy, slot, wait=True)

    lo, hi = _unpack_halves(buf_ref[slot])
    lo = lo.astype(BF16)
    hi = hi.astype(BF16)
    tc = tm // 2
    cid = lax.broadcasted_iota(I32, (tc, R), 1).astype(jnp.int16)
    for t0 in range(0, tm, tc):
        lrow = lrow_ref[t0:t0 + tc, :].astype(jnp.int16)
        gates = gate_ref[t0:t0 + tc, :].astype(BF16)
        sel = jnp.zeros((tc, R), BF16)
        for k in range(TOP_K):
            sel = jnp.where(cid == lrow[:, k:k + 1], gates[:, k:k + 1], sel)
        f = jnp.concatenate([jnp.dot(sel, lo, preferred_element_type=F32),
                             jnp.dot(sel, hi, preferred_element_type=F32)], axis=1)
        xo_ref[t0:t0 + tc, :] = _layer_norm_rows(DEEPNORM_ALPHA * x_ref[t0:t0 + tc, :] + f, g_ref[...], beta_ref[...])


def _combine_ln(lrow_t, gate_t, x, g, beta, ys, seg_table):
    T = x.shape[0]
    tm = ROW_TILE
    nb = T // tm
    R = TOP_K * tm + N_EXPERTS * SEG_ALIGN
    seg_w = seg_table.shape[-1]
    return pl.pallas_call(
        functools.partial(_combine_body, n_tok_blocks=nb),
        grid=(nb,),
        in_specs=[pl.BlockSpec((None, 1, seg_w), lambda b: (b, 0, 0), memory_space=pltpu.SMEM),
                  pl.BlockSpec((None, 1, seg_w), lambda b: (jnp.minimum(b + 1, nb - 1), 0, 0),
                               memory_space=pltpu.SMEM),
                  pl.BlockSpec((tm, TOP_K), lambda b: (b, 0)),
                  pl.BlockSpec((tm, TOP_K), lambda b: (b, 0)),
                  pl.BlockSpec((tm, D_MODEL), lambda b: (b, 0)),
                  pl.BlockSpec((1, D_MODEL), lambda b: (0, 0)),
                  pl.BlockSpec((1, D_MODEL), lambda b: (0, 0)),
                  pl.BlockSpec(memory_space=pl.ANY)],
        out_specs=pl.BlockSpec((tm, D_MODEL), lambda b: (b, 0)),
        out_shape=jax.ShapeDtypeStruct((T, D_MODEL), F32),
        scratch_shapes=[pltpu.VMEM((2, R, HALF), U32), pltpu.SemaphoreType.DMA((2,))],
        compiler_params=_params("arbitrary"),
        name="moe_combine_ln",
    )(seg_table, seg_table, lrow_t, gate_t, x, g, beta, ys)


def _dft_tables(P):
    n2 = 4 * P
    k = jnp.arange(P, dtype=I32)
    m = ((2 * k[:, None] + 1) * k[None, :]) % n2
    ang = m.astype(F32) * F32(2.0 * math.pi / n2)
    fc32, fs32 = jnp.cos(ang), -jnp.sin(ang)
    scale = F32(1.0 / P)
    gc, gs = (fc32.T * scale).astype(BF16), (fs32.T * scale).astype(BF16)
    return fc32, fs32, (fc32.astype(BF16), fs32.astype(BF16), gc, gs)


def _hyena_positional(S):
    pos = jnp.arange(S, dtype=F32)
    t = jnp.linspace(0.0, 1.0, S, dtype=F32)[:, None]
    bands = (HYENA_EMB_DIM - 1) // 2
    f = jnp.linspace(1e-4, bands - 1, bands, dtype=F32)
    ang = (2.0 * math.pi / S) * pos[:, None] * f[None, :]
    feats = jnp.concatenate([t, jnp.cos(ang), -jnp.sin(ang)], axis=-1)
    feats = jnp.pad(feats, ((0, 0), (0, LANES - HYENA_EMB_DIM)))
    max_decay = math.log(HYENA_DECAY_TARGET) / HYENA_SHORT_DECAY_PCT
    min_decay = math.log(HYENA_DECAY_TARGET) / HYENA_LONG_DECAY_PCT
    deltas = jnp.abs(jnp.linspace(min_decay, max_decay, HYENA_CH, dtype=F32))[None, :]
    return feats, t, deltas


def _alibi_slopes():
    return jnp.asarray(np.array([2.0 ** (-8.0 * (i + 1) / N_HEADS) for i in range(N_HEADS)], dtype=np.float32))


def _even_mixer(x, xshape, tabs, w_in, b_in, short_w, short_b, f1_w, f1_b, f1_freq, f2_w, f2_b, f2_freq, f3_w,
                skip, dw_w, dw_b, cln_g, cln_b, w_out, b_out, ln_g, ln_b, w_r, b_r):
    B, S = xshape
    fc32, fs32, tabs16 = tabs
    proj = _project(x, w_in.astype(BF16), b_in[None, :])
    hy = _short_conv(proj, short_w, short_b[None, :], B, S)
    u = _conformer(proj, tabs, dw_w, dw_b[None, :], cln_g[None, :], cln_b[None, :], B, S)
    feats, tcol, deltas = _hyena_positional(S)
    f1_wp = jnp.pad(f1_w, ((0, LANES - HYENA_EMB_DIM), (0, 0)))
    flip = lambda a: jnp.concatenate([a[:1], a[:0:-1]], axis=0)
    h2 = _hyena_filters(jnp.stack([flip(feats), feats]), f1_wp, f1_b[None, :], f1_freq[None, :], f2_w, f2_b[None, :],
                        f2_freq[None, :], f3_w, jnp.stack([flip(tcol), tcol]), deltas)
    hre, him = _lag_spectra(fc32, fs32, h2[0], h2[1], S // fc32.shape[0] - 1)
    z = _long_conv(hy, 2, hy, 0, tabs16, hre, him, 0, skip, B, S)
    z = _long_conv(z, 0, hy, 1, tabs16, hre, him, 1, skip, B, S)
    return _outproj_ln(z, 0, u, 0, w_out.astype(BF16), b_out[None, :], x, ln_g[None, :], ln_b[None, :],
                       w_r.T, b_r[:, None])


def _odd_mixer(x, xshape, layer_idx, w_qkv, lq1, lk1, lq2, lk2, subln_g, w_out, ln_g, ln_b, w_r, b_r):
    B, S = xshape
    lam_init = 0.8 - 0.6 * math.exp(-0.3 * layer_idx)
    lam = (jnp.exp(jnp.sum(lq1 * lk1)) - jnp.exp(jnp.sum(lq2 * lk2)) + lam_init).reshape(1)
    q_scale = jnp.concatenate([jnp.full((ATTN_W,), HEAD_DIM ** -0.5, F32), jnp.ones((2 * ATTN_W,), F32)])
    w = (w_qkv * q_scale).astype(BF16)
    qkv = _project(x, w, jnp.zeros((1, 3 * ATTN_W), F32))
    o = _diff_attention(qkv, _alibi_slopes(), lam, subln_g[None, :], lam_init, B, S)
    return _outproj_ln(o, 0, o, 1, w_out.astype(BF16), jnp.zeros((1, D_MODEL), F32), x, ln_g[None, :], ln_b[None, :],
                       w_r.T, b_r[:, None])


def _round_up(a, m):
    return (a + m - 1) // m * m


def _copy_lists(loff, goff, units):
    E = N_EXPERTS
    caps = _copy_caps()
    big = COPY_ROWS[0]
    n_big = units // (big // SEG_ALIGN)
    cum = jnp.cumsum(n_big, axis=1)
    first = (cum - n_big)[:, None, :]
    p = jnp.arange(caps[0], dtype=I32)[None, :, None]
    mine = (first <= p) & (p < cum[:, None, :])
    within = (p - first) * big
    counts = [cum[:, -1]]
    cols = [jnp.sum(jnp.where(mine, loff[:, None, :] + within, 0), axis=2),
            jnp.sum(jnp.where(mine, goff[:, None, :] + within, 0), axis=2)]
    off = n_big * big
    p = jnp.arange(E, dtype=I32)[None, :, None]
    for rows in COPY_ROWS[1:]:
        has = (units & (rows // SEG_ALIGN)) != 0
        pos = jnp.cumsum(has.astype(I32), axis=1) - has.astype(I32)
        mine = has[:, None, :] & (pos[:, None, :] == p)
        counts.append(jnp.sum(has.astype(I32), axis=1))
        cols += [jnp.sum(jnp.where(mine, (loff + off)[:, None, :], 0), axis=2),
                 jnp.sum(jnp.where(mine, (goff + off)[:, None, :], 0), axis=2)]
        off = off + jnp.where(has, rows, 0)
    return jnp.concatenate([jnp.stack(counts, axis=1)] + cols, axis=1).astype(I32)


def _routing_tables(cnt_blocks, n_ffn_blocks):
    E = N_EXPERTS
    cnt8 = _round_up(cnt_blocks[:, :, 0].astype(I32), SEG_ALIGN)
    seg_end = jnp.cumsum(cnt8, axis=1)
    loff = seg_end - cnt8
    tot8 = jnp.sum(cnt8, axis=0)
    group = _round_up(tot8, EXPERT_ROWS)
    group_end = jnp.cumsum(group)
    group_start = group_end - group
    goff = group_start[None, :] + jnp.cumsum(cnt8, axis=0) - cnt8
    seg_table = _copy_lists(loff, goff, cnt8 // SEG_ALIGN)
    starts = jnp.arange(n_ffn_blocks, dtype=I32) * EXPERT_ROWS
    blk_expert = jnp.minimum(jnp.sum((group_end[None, :] <= starts[:, None]).astype(I32), axis=1), E - 1)
    n_used = group_end[-1:] // EXPERT_ROWS
    tail_start = group_start + tot8
    return loff[:, :, None], seg_table[:, None, :], blk_expert, n_used, tail_start


def _moe_layer(x, routing, layer, w1, b1, w2, b2, ln_g, ln_b):
    T = x.shape[0]
    nb = T // ROW_TILE
    n_rows = _round_up(T * TOP_K + nb * N_EXPERTS * (SEG_ALIGN - 1), EXPERT_ROWS) + N_EXPERTS * EXPERT_ROWS
    n_ffn_blocks = n_rows // EXPERT_ROWS
    idx, gate, rank, cnt = routing
    loff, seg_table, blk_expert, n_used, tail_start = _routing_tables(cnt, n_ffn_blocks)
    xs, lrow = _dispatch(x, idx, rank, loff, seg_table, tail_start, n_rows + EXPERT_ROWS)
    ys = _expert_ffn(xs, blk_expert, n_used, w1, b1[:, :, None, :], w2, b2[:, :, None, :], layer, n_ffn_blocks)
    return _combine_ln(lrow.T, gate.T, x, ln_g[None, :], ln_b[None, :], ys, seg_table)


def kernel(x, hy_cf_w_in, hy_cf_b_in, hy_short_w, hy_short_b, hy_f1_w, hy_f1_b, hy_f1_freq, hy_f2_w, hy_f2_b, hy_f2_freq, hy_f3_w, hy_skip, cf_dw_w, cf_dw_b, cf_ln_g, cf_ln_b, even_w_out, even_b_out, attn_w_qkv, attn_lq1, attn_lk1, attn_lq2, attn_lk2, attn_subln_g, attn_w_out, ln1_g, ln1_b, ln2_g, ln2_b, moe_w_r, moe_b_r, moe_w1, moe_b1, moe_w2, moe_b2):
    B, S, D = x.shape
    assert D == D_MODEL and S % LANES == 0
    assert (B * S) % ROW_TILE == 0 and (B * S) % min(B * S, DENSE_ROW_TILE) == 0
    depth = ln1_g.shape[0]
    xf = x.reshape(B * S, D)
    tabs = _dft_tables(S // CONV_BLOCKS)
    for i in range(depth):
        j = i // 2
        if i % 2 == 0:
            xf, routing = _even_mixer(xf, (B, S), tabs, hy_cf_w_in[j], hy_cf_b_in[j], hy_short_w[j], hy_short_b[j],
                                      hy_f1_w[j], hy_f1_b[j], hy_f1_freq[j], hy_f2_w[j], hy_f2_b[j], hy_f2_freq[j],
                                      hy_f3_w[j], hy_skip[j], cf_dw_w[j], cf_dw_b[j], cf_ln_g[j], cf_ln_b[j],
                                      even_w_out[j], even_b_out[j], ln1_g[i], ln1_b[i], moe_w_r[i], moe_b_r[i])
        else:
            xf, routing = _odd_mixer(xf, (B, S), i, attn_w_qkv[j], attn_lq1[j], attn_lk1[j], attn_lq2[j],
                                     attn_lk2[j], attn_subln_g[j], attn_w_out[j], ln1_g[i], ln1_b[i],
                                     moe_w_r[i], moe_b_r[i])
        xf = _moe_layer(xf, routing, i, moe_w1, moe_b1, moe_w2, moe_b2, ln2_g[i], ln2_b[i])
    return xf.reshape(B, S, D)
```

```python
import functools
import math

import jax
import jax.numpy as jnp
import numpy as np
from jax import lax
from jax.experimental import pallas as pl
from jax.experimental.pallas import tpu as pltpu

F32 = jnp.float32
BF16 = jnp.bfloat16
U32 = jnp.uint32
I32 = jnp.int32

D_MODEL = 1024
HALF = D_MODEL // 2
DEPTH = 4
HYENA_CH = D_MODEL // 2
CONF_CH = D_MODEL // 2
HYENA_ORDER = 2
HYENA_EMB_DIM = 33
HYENA_FILTER_DIM = 64
HYENA_SHORT_DECAY_PCT = 0.3
HYENA_LONG_DECAY_PCT = 1.5
HYENA_DECAY_TARGET = 1e-2
CONF_WIDTH = 31
EVEN_IN = 3 * HYENA_CH + 2 * CONF_CH
N_HEADS = 8
HEAD_DIM = 64
ATTN_W = N_HEADS * 2 * HEAD_DIM
N_EXPERTS = 32
TOP_K = 4
D_FF = D_MODEL
SWIGLU_LIMIT = 7.0
SWIGLU_ALPHA = 1.702
DEEPNORM_ALPHA = (2 * DEPTH) ** 0.25
LN_EPS = 1e-5

LANES = 128
VMEM_LIMIT_BYTES = 56 * 1024 * 1024
ROW_TILE = 512
DENSE_ROW_TILE = 1024
EXPERT_ROWS = 512
FFN_ROW_CHUNK = 512
SEG_ALIGN = 8
COPY_ROWS = (64, 32, 16, 8)
CONV_BLOCKS = 4
CONV_CH_TILE = 256
ATTN_Q_TILE = 2048
ATTN_HEADS_PER_STEP = 1
ATTN_ROW_CHUNK = 256

_NT = (((1,), (1,)), ((), ()))


def _params(*sem):
    return pltpu.CompilerParams(dimension_semantics=sem, vmem_limit_bytes=VMEM_LIMIT_BYTES)


def _split_bf16(a):
    hi = a.astype(BF16)
    lo = (a - hi.astype(F32)).astype(BF16)
    return hi, lo


def _dot3(a, b):
    a_hi, a_lo = _split_bf16(a)
    b_hi, b_lo = _split_bf16(b)
    d = functools.partial(jnp.dot, preferred_element_type=F32)
    return d(a_hi, b_hi) + d(a_hi, b_lo) + d(a_lo, b_hi)


def _layer_norm_rows(y, g, b):
    mu = jnp.mean(y, axis=-1, keepdims=True)
    yc = y - mu
    var = jnp.mean(yc * yc, axis=-1, keepdims=True)
    return yc * lax.rsqrt(var + LN_EPS) * g + b


def _pack_halves(y):
    lo = lax.bitcast_convert_type(y[:, :HALF].astype(BF16).astype(F32), U32)
    hi = lax.bitcast_convert_type(y[:, HALF:].astype(BF16).astype(F32), U32)
    return hi | (lo >> 16)


def _unpack_halves(p):
    lo = lax.bitcast_convert_type(p << 16, F32)
    hi = lax.bitcast_convert_type(p & jnp.uint32(0xFFFF0000), F32)
    return lo, hi


def _proj_body(x_ref, w_ref, b_ref, o_ref, *, col_chunk):
    x = x_ref[...].astype(BF16)
    for j in range(0, o_ref.shape[1], col_chunk):
        acc = jnp.dot(x, w_ref[:, j:j + col_chunk], preferred_element_type=F32)
        o_ref[:, j:j + col_chunk] = (acc + b_ref[:, j:j + col_chunk]).astype(o_ref.dtype)


def _project(x, w, b):
    T, K = x.shape
    N = w.shape[1]
    tm = min(T, DENSE_ROW_TILE)
    return pl.pallas_call(
        functools.partial(_proj_body, col_chunk=512),
        grid=(T // tm,),
        in_specs=[pl.BlockSpec((tm, K), lambda i: (i, 0)),
                  pl.BlockSpec((K, N), lambda i: (0, 0)),
                  pl.BlockSpec((1, N), lambda i: (0, 0))],
        out_specs=pl.BlockSpec((tm, N), lambda i: (i, 0)),
        out_shape=jax.ShapeDtypeStruct((T, N), BF16),
        compiler_params=_params("parallel"),
        name="project",
    )(x, w, b)


def _outproj_ln_body(a1_ref, a2_ref, w1_ref, w2_ref, b_ref, x_ref, g_ref, beta_ref, wr_ref, br_ref,
                     xo_ref, idx_ref, gate_ref, rank_ref, cnt_ref):
    m = (jnp.dot(a1_ref[...], w1_ref[...], preferred_element_type=F32)
         + jnp.dot(a2_ref[...], w2_ref[...], preferred_element_type=F32) + b_ref[...])
    y = _layer_norm_rows(DEEPNORM_ALPHA * x_ref[...] + m, g_ref[...], beta_ref[...])
    xo_ref[...] = y
    w_hi, w_lo = _split_bf16(wr_ref[...])
    for s in range(y.shape[0] // ROW_TILE):
        cols = slice(s * ROW_TILE, (s + 1) * ROW_TILE)
        _route_block(y[cols, :], w_hi, w_lo, br_ref[...], idx_ref, gate_ref, rank_ref, cnt_ref.at[s], cols)


def _outproj_ln(a1, a1_col, a2, a2_col, w, b, x, g, beta, w_rt, b_r):
    T = x.shape[0]
    E = N_EXPERTS
    tm = min(T, DENSE_ROW_TILE)
    nsub = tm // ROW_TILE
    tok = pl.BlockSpec((TOP_K, tm), lambda i: (0, i))
    out = pl.pallas_call(
        _outproj_ln_body,
        grid=(T // tm,),
        in_specs=[pl.BlockSpec((tm, HALF), lambda i: (i, a1_col)),
                  pl.BlockSpec((tm, HALF), lambda i: (i, a2_col)),
                  pl.BlockSpec((HALF, D_MODEL), lambda i: (0, 0)),
                  pl.BlockSpec((HALF, D_MODEL), lambda i: (1, 0)),
                  pl.BlockSpec((1, D_MODEL), lambda i: (0, 0)),
                  pl.BlockSpec((tm, D_MODEL), lambda i: (i, 0)),
                  pl.BlockSpec((1, D_MODEL), lambda i: (0, 0)),
                  pl.BlockSpec((1, D_MODEL), lambda i: (0, 0)),
                  pl.BlockSpec((E, D_MODEL), lambda i: (0, 0)),
                  pl.BlockSpec((E, 1), lambda i: (0, 0))],
        out_specs=[pl.BlockSpec((tm, D_MODEL), lambda i: (i, 0)), tok, tok, tok,
                   pl.BlockSpec((nsub, E, LANES), lambda i: (i, 0, 0))],
        out_shape=[jax.ShapeDtypeStruct((T, D_MODEL), F32),
                   jax.ShapeDtypeStruct((TOP_K, T), I32), jax.ShapeDtypeStruct((TOP_K, T), F32),
                   jax.ShapeDtypeStruct((TOP_K, T), I32), jax.ShapeDtypeStruct((T // ROW_TILE, E, LANES), F32)],
        compiler_params=_params("parallel"),
        name="outproj_ln_route",
    )(a1, a2, w, w, b, x, g, beta, w_rt, b_r)
    return out[0], tuple(out[1:])


def _short_conv_body(x_ref, w_ref, b_ref, o_ref):
    x = x_ref[...].astype(F32)
    S = x.shape[0]
    row = lax.broadcasted_iota(I32, x.shape, 0)
    prev = jnp.where(row == 0, 0.0, pltpu.roll(x, 1, 0))
    nxt = jnp.where(row == S - 1, 0.0, pltpu.roll(x, S - 1, 0))
    y = w_ref[0:1, :] * prev + w_ref[1:2, :] * x + w_ref[2:3, :] * nxt + b_ref[...]
    o_ref[...] = y.astype(o_ref.dtype)


def _short_conv(proj, w, b, B, S):
    T = B * S
    C = HYENA_CH
    return pl.pallas_call(
        _short_conv_body,
        grid=(B, 3),
        in_specs=[pl.BlockSpec((S, C), lambda bi, j: (bi, j)),
                  pl.BlockSpec((3, C), lambda bi, j: (0, j)),
                  pl.BlockSpec((1, C), lambda bi, j: (0, j))],
        out_specs=pl.BlockSpec((S, C), lambda bi, j: (bi, j)),
        out_shape=jax.ShapeDtypeStruct((T, 3 * C), BF16),
        compiler_params=_params("parallel", "parallel"),
        name="hyena_short_conv",
    )(proj, w, b)


def _conformer_body(a_ref, g_ref, fc_ref, fs_ref, hre_ref, him_ref, gc_ref, gs_ref, b_ref, lg_ref, lb_ref, o_ref,
                    *, n_blk):
    P = fc_ref.shape[0]
    fc, fs, gc, gs = fc_ref[...], fs_ref[...], gc_ref[...], gs_ref[...]
    vre, vim = [], []
    for j in range(n_blk):
        rows = slice(j * P, (j + 1) * P)
        u = (a_ref[rows, :].astype(F32) * jax.nn.sigmoid(g_ref[rows, :].astype(F32))).astype(BF16)
        vre.append(jnp.dot(fc, u, preferred_element_type=F32))
        vim.append(jnp.dot(fs, u, preferred_element_type=F32))
    for i in range(n_blk):
        yre, yim = _mix_block_lags(vre, vim, hre_ref, him_ref, i, 1)
        y = jnp.dot(gc, yre, preferred_element_type=F32) + jnp.dot(gs, yim, preferred_element_type=F32) + b_ref[...]
        y = _layer_norm_rows(y, lg_ref[...], lb_ref[...])
        o_ref[i * P:(i + 1) * P, :] = (y * jax.nn.sigmoid(y)).astype(o_ref.dtype)


def _conformer(proj, tabs, w, b, lg, lb, B, S):
    fc32, fs32, (fc, fs, gc, gs) = tabs
    P = fc.shape[0]
    T = B * S
    C = CONF_CH
    half = CONF_WIDTH // 2
    assert half < P
    span = ((S - half, S - half - 1), (0, 0))
    hre, him = _lag_spectra(fc32, fs32, jnp.stack([jnp.pad(w[::-1], span), jnp.pad(w, span)]), 1)
    tab = pl.BlockSpec((P, P), lambda bi: (0, 0))
    spec = pl.BlockSpec((3 * P, C), lambda bi: (0, 0))
    vec = pl.BlockSpec((1, C), lambda bi: (0, 0))
    return pl.pallas_call(
        functools.partial(_conformer_body, n_blk=S // P),
        grid=(B,),
        in_specs=[pl.BlockSpec((S, C), lambda bi: (bi, 3)),
                  pl.BlockSpec((S, C), lambda bi: (bi, 4)),
                  tab, tab, spec, spec, tab, tab, vec, vec, vec],
        out_specs=pl.BlockSpec((S, C), lambda bi: (bi, 0)),
        out_shape=jax.ShapeDtypeStruct((T, C), BF16),
        compiler_params=_params("parallel"),
        name="conformer_conv",
    )(proj, proj, fc, fs, hre, him, gc, gs, b, lg, lb)


def _filter_body(feat_ref, w1_ref, b1_ref, q1_ref, w2_ref, b2_ref, q2_ref, w3_ref, t_ref, delta_ref, o_ref):
    h = jnp.sin(q1_ref[...] * (_dot3(feat_ref[...], w1_ref[...]) + b1_ref[...]))
    h = jnp.sin(q2_ref[...] * (_dot3(h, w2_ref[...]) + b2_ref[...]))
    h = _dot3(h, w3_ref[...])
    o_ref[...] = h * jnp.exp(-t_ref[...] * delta_ref[...])


def _hyena_filters(feats2, w1, b1, q1, w2, b2, q2, w3, tcol2, deltas):
    S = feats2.shape[1]
    C = HYENA_CH
    fd = HYENA_FILTER_DIM
    fixed = lambda p, d, o: (0, 0)
    return pl.pallas_call(
        _filter_body,
        grid=(2, 2, HYENA_ORDER),
        in_specs=[pl.BlockSpec((None, S, LANES), lambda p, d, o: (p, 0, 0)),
                  pl.BlockSpec((LANES, fd), fixed),
                  pl.BlockSpec((1, fd), fixed),
                  pl.BlockSpec((1, fd), fixed),
                  pl.BlockSpec((fd, fd), fixed),
                  pl.BlockSpec((1, fd), fixed),
                  pl.BlockSpec((1, fd), fixed),
                  pl.BlockSpec((fd, C), lambda p, d, o: (0, d * HYENA_ORDER + o)),
                  pl.BlockSpec((None, S, 1), lambda p, d, o: (p, 0, 0)),
                  pl.BlockSpec((1, C), fixed)],
        out_specs=pl.BlockSpec((None, S, C), lambda p, d, o: ((d + 1 - p) % 2, p, o)),
        out_shape=jax.ShapeDtypeStruct((2, 2 * S, HYENA_ORDER * C), F32),
        compiler_params=_params("parallel", "parallel", "parallel"),
        name="hyena_filter_mlp",
    )(feats2, w1, b1, q1, w2, b2, q2, w3, tcol2, deltas)


def _spectrum_body(fc_ref, fs_ref, a_ref, b_ref, hre_ref, him_ref):
    a = a_ref[...]
    row = lax.broadcasted_iota(I32, a.shape, 0)
    b = jnp.where(row == 0, 0.0, b_ref[...])
    hre_ref[...] = _dot3(fc_ref[...], a + b)
    him_ref[...] = _dot3(fs_ref[...], a - b)


def _lag_spectra(fc32, fs32, h2, d_max):
    P = fc32.shape[0]
    n = h2.shape[1] // (2 * P)
    n_ch = h2.shape[2]
    C = HYENA_CH
    nd = 2 * d_max + 1
    spec = pl.BlockSpec((P, C), lambda di, o: (di, o))
    return pl.pallas_call(
        _spectrum_body,
        grid=(nd, n_ch // C),
        in_specs=[pl.BlockSpec((P, P), lambda di, o: (0, 0)),
                  pl.BlockSpec((P, P), lambda di, o: (0, 0)),
                  pl.BlockSpec((None, P, C), lambda di, o: (0, n - d_max + di, o)),
                  pl.BlockSpec((None, P, C), lambda di, o: (1, n + d_max - di, o))],
        out_specs=[spec, spec],
        out_shape=[jax.ShapeDtypeStruct((nd * P, n_ch), F32)] * 2,
        compiler_params=_params("parallel", "parallel"),
        name="block_lag_spectra",
    )(fc32, fs32, h2, h2)


def _mix_block_lags(vre, vim, hre_ref, him_ref, i, d_max):
    P = vre[0].shape[0]
    yre = yim = None
    for j in range(max(0, i - d_max), min(len(vre), i + d_max + 1)):
        r0 = (i - j + d_max) * P
        hre = hre_ref[r0:r0 + P, :]
        him = him_ref[r0:r0 + P, :]
        tre = vre[j] * hre - vim[j] * him
        tim = vre[j] * him + vim[j] * hre
        yre = tre if yre is None else yre + tre
        yim = tim if yim is None else yim + tim
    return yre.astype(BF16), yim.astype(BF16)


def _long_conv_body(v_ref, gate_ref, fc_ref, fs_ref, hre_ref, him_ref, gc_ref, gs_ref, skip_ref, o_ref, *, n_blk):
    P = fc_ref.shape[0]
    fc, fs, gc, gs = fc_ref[...], fs_ref[...], gc_ref[...], gs_ref[...]
    vre, vim = [], []
    for j in range(n_blk):
        vj = v_ref[j * P:(j + 1) * P, :]
        vre.append(jnp.dot(fc, vj, preferred_element_type=F32))
        vim.append(jnp.dot(fs, vj, preferred_element_type=F32))
    for i in range(n_blk):
        yre, yim = _mix_block_lags(vre, vim, hre_ref, him_ref, i, n_blk - 1)
        y = jnp.dot(gc, yre, preferred_element_type=F32) + jnp.dot(gs, yim, preferred_element_type=F32)
        rows = slice(i * P, (i + 1) * P)
        y = y + v_ref[rows, :].astype(F32) * skip_ref[...]
        o_ref[rows, :] = (gate_ref[rows, :].astype(F32) * y).astype(o_ref.dtype)


def _long_conv(v_arr, v_col, gate_arr, gate_col, tabs, hre, him, order, skip, B, S):
    fc, fs, gc, gs = tabs
    P = fc.shape[0]
    T = B * S
    C = HYENA_CH
    nc = C // CONV_CH_TILE
    cc = CONV_CH_TILE
    n_h = hre.shape[0]
    tab = pl.BlockSpec((P, P), lambda bi, c: (0, 0))
    return pl.pallas_call(
        functools.partial(_long_conv_body, n_blk=S // P),
        grid=(B, nc),
        in_specs=[pl.BlockSpec((S, cc), lambda bi, c: (bi, v_col * nc + c)),
                  pl.BlockSpec((S, cc), lambda bi, c: (bi, gate_col * nc + c)),
                  tab, tab,
                  pl.BlockSpec((n_h, cc), lambda bi, c: (0, order * nc + c)),
                  pl.BlockSpec((n_h, cc), lambda bi, c: (0, order * nc + c)),
                  tab, tab,
                  pl.BlockSpec((1, cc), lambda bi, c: (0, c))],
        out_specs=pl.BlockSpec((S, cc), lambda bi, c: (bi, c)),
        out_shape=jax.ShapeDtypeStruct((T, C), BF16),
        compiler_params=_params("parallel", "parallel"),
        name="hyena_long_conv",
    )(v_arr, gate_arr, fc, fs, hre, him, gc, gs, skip[order][None, :])


def _attn_body(slope_ref, lam_ref, q_ref, k_ref, v_ref, g_ref, o_ref, vaug_ref, *, lam_init, row_chunk, heads):
    hg = pl.program_id(1)
    qi = pl.program_id(2)
    tq = q_ref.shape[0]
    S = k_ref.shape[0]
    hw = 2 * HEAD_DIM

    @pl.when(qi == 0)
    def _():
        ones_col = jnp.where(lax.broadcasted_iota(I32, (S, hw), 1) == 0, 1.0, 0.0).astype(BF16)
        for hh in range(heads):
            vaug_ref[hh, :, :hw] = v_ref[:, hh * hw:(hh + 1) * hw]
            vaug_ref[hh, :, hw:] = ones_col

    for hh in range(heads):
        cols = slice(hh * hw, (hh + 1) * hw)
        k = k_ref[:, cols]
        v_aug = vaug_ref[hh]
        slope = slope_ref[hg * heads + hh]
        kpos = lax.broadcasted_iota(I32, (1, S), 1).astype(F32) * slope
        for r0 in range(0, tq, row_chunk):
            q = q_ref[r0:r0 + row_chunk, cols]
            lane = lax.broadcasted_iota(I32, q.shape, 1)
            zero = jnp.zeros_like(q)
            qpos = (qi * tq + r0 + lax.broadcasted_iota(I32, (row_chunk, 1), 0)).astype(F32) * slope
            bias = lax.bitcast_convert_type(lax.bitcast_convert_type(qpos - kpos, U32) | jnp.uint32(0x80000000), F32)

            def weighted_values(qh):
                s = lax.dot_general(qh, k, _NT, preferred_element_type=F32) + bias
                e = jnp.exp((s - jnp.max(s, axis=-1, keepdims=True)).astype(BF16))
                return jnp.dot(e, v_aug, preferred_element_type=F32)

            o1 = weighted_values(jnp.where(lane < HEAD_DIM, q, zero))
            o2 = weighted_values(jnp.where(lane >= HEAD_DIM, q, zero))
            o = o1[:, :hw] * (1.0 / o1[:, hw:hw + 1]) - o2[:, :hw] * (lam_ref[0] / o2[:, hw:hw + 1])
            o = o * lax.rsqrt(jnp.mean(o * o, axis=-1, keepdims=True) + LN_EPS) * g_ref[...]
            o_ref[r0:r0 + row_chunk, cols] = (o * (1.0 - lam_init)).astype(o_ref.dtype)


def _diff_attention(qkv, slopes, lam, subln_g, lam_init, B, S):
    T = B * S
    hw = 2 * HEAD_DIM
    hp = ATTN_HEADS_PER_STEP
    ng = N_HEADS // hp
    tq = min(S, ATTN_Q_TILE)
    nq = S // tq
    smem = pl.BlockSpec(memory_space=pltpu.SMEM)
    return pl.pallas_call(
        functools.partial(_attn_body, lam_init=lam_init, row_chunk=min(tq, ATTN_ROW_CHUNK), heads=hp),
        grid=(B, ng, nq),
        in_specs=[smem, smem,
                  pl.BlockSpec((tq, hp * hw), lambda bi, h, qi: (bi * nq + qi, h)),
                  pl.BlockSpec((S, hp * hw), lambda bi, h, qi: (bi, ng + h)),
                  pl.BlockSpec((S, hp * hw), lambda bi, h, qi: (bi, 2 * ng + h)),
                  pl.BlockSpec((1, hw), lambda bi, h, qi: (0, 0))],
        out_specs=pl.BlockSpec((tq, hp * hw), lambda bi, h, qi: (bi * nq + qi, h)),
        out_shape=jax.ShapeDtypeStruct((T, ATTN_W), BF16),
        scratch_shapes=[pltpu.VMEM((hp, S, 2 * hw), BF16)],
        compiler_params=_params("parallel", "parallel", "arbitrary"),
        name="diff_attention",
    )(slopes, lam, qkv, qkv, qkv, subln_g)


def _route_block(x, w_hi, w_lo, bias, idx_ref, gate_ref, rank_ref, cnt_ref, cols):
    E = N_EXPERTS
    tm = x.shape[0]
    x_hi, x_lo = _split_bf16(x)
    nt = functools.partial(lax.dot_general, dimension_numbers=_NT, preferred_element_type=F32)
    logits = nt(w_hi, x_hi) + nt(w_lo, x_hi) + nt(w_hi, x_lo) + bias

    eid = lax.broadcasted_iota(I32, (E, tm), 0).astype(F32)
    work = logits
    vals, idxs = [], []
    for _ in range(TOP_K):
        m = jnp.max(work, axis=0, keepdims=True)
        sel = jnp.min(jnp.where(work == m, eid, float(E)), axis=0, keepdims=True)
        vals.append(m)
        idxs.append(sel)
        work = jnp.where(eid == sel, -jnp.inf, work)
    exps = [jnp.exp(v - vals[0]) for v in vals]
    denom = exps[0] + exps[1] + exps[2] + exps[3]

    chosen = jnp.zeros((E, tm), F32)
    for sel in idxs:
        chosen = chosen + jnp.where(eid == sel, 1.0, 0.0)
    earlier = jnp.where(lax.broadcasted_iota(I32, (tm, tm), 0) < lax.broadcasted_iota(I32, (tm, tm), 1), 1.0, 0.0)
    before = jnp.dot(chosen.astype(BF16), earlier.astype(BF16), preferred_element_type=F32)
    for k in range(TOP_K):
        gate_ref[k:k + 1, cols] = exps[k] / denom
        idx_ref[k:k + 1, cols] = idxs[k].astype(I32)
        rank_ref[k:k + 1, cols] = jnp.sum(jnp.where(eid == idxs[k], before, 0.0), axis=0, keepdims=True).astype(I32)
    cnt_ref[...] = jnp.broadcast_to(jnp.sum(chosen, axis=1, keepdims=True), cnt_ref.shape)


def _copy_caps():
    local_rows = TOP_K * ROW_TILE + N_EXPERTS * SEG_ALIGN
    return (local_rows // COPY_ROWS[0],) + (N_EXPERTS,) * (len(COPY_ROWS) - 1)


def _segment_copies(tab_ref, make_copy, slot, wait):
    base = len(COPY_ROWS)
    for ci, (rows, cap) in enumerate(zip(COPY_ROWS, _copy_caps())):
        def body(p, carry, base=base, rows=rows, cap=cap):
            copy = make_copy(slot, tab_ref[0, base + p], tab_ref[0, base + cap + p], rows)
            if wait:
                copy.wait()
            else:
                copy.start()
            return carry
        lax.fori_loop(0, tab_ref[0, ci], body, 0)
        base += 2 * cap


def _dispatch_body(tail_ref, seg_ref, seg_prev_ref, x_ref, idx_ref, rank_ref, loff_ref, xs_hbm, lrow_ref,
                   buf_ref, zero_ref, sem, zsem, *, n_tok_blocks):
    b = pl.program_id(0)
    slot = b % 2
    tm = x_ref.shape[0]
    R = buf_ref.shape[1]

    def seg_copy(s, local_row, dst_row, rows):
        return pltpu.make_async_copy(buf_ref.at[s, pl.ds(pl.multiple_of(local_row, SEG_ALIGN), rows)],
                                     xs_hbm.at[pl.ds(pl.multiple_of(dst_row, SEG_ALIGN), rows)], sem.at[s])

    @pl.when(b == 0)
    def _():
        zero_ref[...] = jnp.zeros_like(zero_ref)
        for e in range(N_EXPERTS):
            fill = pltpu.make_async_copy(
                zero_ref, xs_hbm.at[pl.ds(pl.multiple_of(tail_ref[e], SEG_ALIGN), EXPERT_ROWS)], zsem)
            fill.start()
            fill.wait()

    eid = lax.broadcasted_iota(I32, (N_EXPERTS, tm), 0)
    loff = loff_ref[...].astype(F32)
    rid = lax.broadcasted_iota(I32, (R, tm), 0).astype(jnp.int16)
    sel_t = jnp.zeros((R, tm), BF16)
    for k in range(TOP_K):
        base = jnp.sum(jnp.where(eid == idx_ref[k:k + 1, :], loff, 0.0), axis=0, keepdims=True).astype(I32)
        row = base + rank_ref[k:k + 1, :]
        lrow_ref[k:k + 1, :] = row
        sel_t = jnp.where(rid == row.astype(jnp.int16), jnp.ones((), BF16), sel_t)
    xb = x_ref[...].astype(BF16)
    lo = jnp.dot(sel_t, xb[:, :HALF], preferred_element_type=F32)
    hi = jnp.dot(sel_t, xb[:, HALF:], preferred_element_type=F32)
    packed = (lax.bitcast_convert_type(hi, U32) & jnp.uint32(0xFFFF0000)) | (lax.bitcast_convert_type(lo, U32) >> 16)
    buf_ref[slot] = packed

    @pl.when(b >= 1)
    def _():
        _segment_copies(seg_prev_ref, seg_copy, 1 - slot, wait=True)
    _segment_copies(seg_ref, seg_copy, slot, wait=False)

    @pl.when(b == n_tok_blocks - 1)
    def _():
        _segment_copies(seg_ref, seg_copy, slot, wait=True)


def _dispatch(x, idx, rank, loff, seg_table, tail_start, n_rows):
    T = x.shape[0]
    tm = ROW_TILE
    nb = T // tm
    R = TOP_K * tm + N_EXPERTS * SEG_ALIGN
    tok = pl.BlockSpec((TOP_K, tm), lambda b, tl: (0, b))
    seg_w = seg_table.shape[-1]
    return pl.pallas_call(
        functools.partial(_dispatch_body, n_tok_blocks=nb),
        grid_spec=pltpu.PrefetchScalarGridSpec(
            num_scalar_prefetch=1,
            grid=(nb,),
            in_specs=[pl.BlockSpec((None, 1, seg_w), lambda b, tl: (b, 0, 0), memory_space=pltpu.SMEM),
                      pl.BlockSpec((None, 1, seg_w), lambda b, tl: (jnp.maximum(b - 1, 0), 0, 0),
                                   memory_space=pltpu.SMEM),
                      pl.BlockSpec((tm, D_MODEL), lambda b, tl: (b, 0)),
                      tok, tok,
                      pl.BlockSpec((None, N_EXPERTS, 1), lambda b, tl: (b, 0, 0))],
            out_specs=[pl.BlockSpec(memory_space=pl.ANY), tok],
            scratch_shapes=[pltpu.VMEM((2, R, HALF), U32), pltpu.VMEM((EXPERT_ROWS, HALF), U32),
                            pltpu.SemaphoreType.DMA((2,)), pltpu.SemaphoreType.DMA(())]),
        out_shape=[jax.ShapeDtypeStruct((n_rows, HALF), U32), jax.ShapeDtypeStruct((TOP_K, T), I32)],
        compiler_params=_params("arbitrary"),
        name="moe_dispatch",
    )(tail_start, seg_table, seg_table, x, idx, rank, loff)


def _ffn_body(be_ref, nu_ref, xs_ref, w1_ref, b1_ref, w2_ref, b2_ref, ys_ref, w1b_ref, w2b_ref):
    i = pl.program_id(0)
    used = i < nu_ref[0]
    fresh = jnp.logical_or(i == 0, be_ref[i] != be_ref[jnp.maximum(i - 1, 0)])

    @pl.when(jnp.logical_and(used, fresh))
    def _():
        w1b_ref[...] = w1_ref[...].astype(BF16)
        w2b_ref[...] = w2_ref[...].astype(BF16)

    @pl.when(used)
    def _():
        for r0 in range(0, EXPERT_ROWS, FFN_ROW_CHUNK):
            rows = slice(r0, r0 + FFN_ROW_CHUNK)
            lo, hi = _unpack_halves(xs_ref[rows, :])
            h = (jnp.dot(lo.astype(BF16), w1b_ref[0:HALF, :], preferred_element_type=F32)
                 + jnp.dot(hi.astype(BF16), w1b_ref[HALF:D_MODEL, :], preferred_element_type=F32) + b1_ref[...])
            hg = jnp.minimum(h[:, :D_FF], SWIGLU_LIMIT)
            hu = jnp.clip(h[:, D_FF:], -SWIGLU_LIMIT, SWIGLU_LIMIT)
            act = (hu + 1.0) * (hg * jax.nn.sigmoid(hg * SWIGLU_ALPHA))
            y = jnp.dot(act.astype(BF16), w2b_ref[...], preferred_element_type=F32) + b2_ref[...]
            ys_ref[rows, :] = _pack_halves(y)


def _expert_ffn(xs, blk_expert, n_used, w1, b1, w2, b2, layer, n_blocks):
    rows = pl.BlockSpec((EXPERT_ROWS, HALF), lambda i, be, nu: (jnp.minimum(i, nu[0] - 1), 0))
    return pl.pallas_call(
        _ffn_body,
        grid_spec=pltpu.PrefetchScalarGridSpec(
            num_scalar_prefetch=2,
            grid=(n_blocks,),
            in_specs=[rows,
                      pl.BlockSpec((None, None, D_MODEL, 2 * D_FF), lambda i, be, nu: (layer, be[i], 0, 0)),
                      pl.BlockSpec((None, None, 1, 2 * D_FF), lambda i, be, nu: (layer, be[i], 0, 0)),
                      pl.BlockSpec((None, None, D_FF, D_MODEL), lambda i, be, nu: (layer, be[i], 0, 0)),
                      pl.BlockSpec((None, None, 1, D_MODEL), lambda i, be, nu: (layer, be[i], 0, 0))],
            out_specs=rows,
            scratch_shapes=[pltpu.VMEM((D_MODEL, 2 * D_FF), BF16), pltpu.VMEM((D_FF, D_MODEL), BF16)]),
        out_shape=jax.ShapeDtypeStruct((xs.shape[0], HALF), U32),
        compiler_params=_params("arbitrary"),
        name="moe_expert_ffn",
    )(blk_expert, n_used, xs, w1, b1, w2, b2)


def _combine_body(seg_ref, seg_next_ref, lrow_ref, gate_ref, x_ref, g_ref, beta_ref, ys_hbm, xo_ref,
                  buf_ref, sem, *, n_tok_blocks):
    b = pl.program_id(0)
    slot = b % 2
    tm = x_ref.shape[0]
    R = buf_ref.shape[1]

    def seg_copy(s, local_row, src_row, rows):
        return pltpu.make_async_copy(ys_hbm.at[pl.ds(pl.multiple_of(src_row, SEG_ALIGN), rows)],
                                     buf_ref.at[s, pl.ds(pl.multiple_of(local_row, SEG_ALIGN), rows)], sem.at[s])

    @pl.when(b == 0)
    def _():
        buf_ref[...] = jnp.zeros_like(buf_ref)
        _segment_copies(seg_ref, seg_copy, 0, wait=False)

    @pl.when(b + 1 < n_tok_blocks)
    def _():
        _segment_copies(seg_next_ref, seg_copy, 1 - slot, wait=False)

    _segment_copies(seg_ref, seg_copy, slot, wait=True)

    lo, hi = _unpack_halves(buf_ref[slot])
    lo = lo.astype(BF16)
    hi = hi.astype(BF16)
    tc = tm // 2
    cid = lax.broadcasted_iota(I32, (tc, R), 1).astype(jnp.int16)
    lrow_t = lrow_ref[...].astype(F32).T
    gate_t = gate_ref[...].T
    for t0 in range(0, tm, tc):
        lrow = lrow_t[t0:t0 + tc, :].astype(jnp.int16)
        gates = gate_t[t0:t0 + tc, :].astype(BF16)
        sel = jnp.zeros((tc, R), BF16)
        for k in range(TOP_K):
            sel = jnp.where(cid == lrow[:, k:k + 1], gates[:, k:k + 1], sel)
        f = jnp.concatenate([jnp.dot(sel, lo, preferred_element_type=F32),
                             jnp.dot(sel, hi, preferred_element_type=F32)], axis=1)
        xo_ref[t0:t0 + tc, :] = _layer_norm_rows(DEEPNORM_ALPHA * x_ref[t0:t0 + tc, :] + f, g_ref[...], beta_ref[...])


def _combine_ln(lrow, gate, x, g, beta, ys, seg_table):
    T = x.shape[0]
    tm = ROW_TILE
    nb = T // tm
    R = TOP_K * tm + N_EXPERTS * SEG_ALIGN
    seg_w = seg_table.shape[-1]
    return pl.pallas_call(
        functools.partial(_combine_body, n_tok_blocks=nb),
        grid=(nb,),
        in_specs=[pl.BlockSpec((None, 1, seg_w), lambda b: (b, 0, 0), memory_space=pltpu.SMEM),
                  pl.BlockSpec((None, 1, seg_w), lambda b: (jnp.minimum(b + 1, nb - 1), 0, 0),
                               memory_space=pltpu.SMEM),
                  pl.BlockSpec((TOP_K, tm), lambda b: (0, b)),
                  pl.BlockSpec((TOP_K, tm), lambda b: (0, b)),
                  pl.BlockSpec((tm, D_MODEL), lambda b: (b, 0)),
                  pl.BlockSpec((1, D_MODEL), lambda b: (0, 0)),
                  pl.BlockSpec((1, D_MODEL), lambda b: (0, 0)),
                  pl.BlockSpec(memory_space=pl.ANY)],
        out_specs=pl.BlockSpec((tm, D_MODEL), lambda b: (b, 0)),
        out_shape=jax.ShapeDtypeStruct((T, D_MODEL), F32),
        scratch_shapes=[pltpu.VMEM((2, R, HALF), U32), pltpu.SemaphoreType.DMA((2,))],
        compiler_params=_params("arbitrary"),
        name="moe_combine_ln",
    )(seg_table, seg_table, lrow, gate, x, g, beta, ys)


def _dft_tables(P):
    n2 = 4 * P
    k = jnp.arange(P, dtype=I32)
    m = ((2 * k[:, None] + 1) * k[None, :]) % n2
    ang = m.astype(F32) * F32(2.0 * math.pi / n2)
    fc32, fs32 = jnp.cos(ang), -jnp.sin(ang)
    scale = F32(1.0 / P)
    gc, gs = (fc32.T * scale).astype(BF16), (fs32.T * scale).astype(BF16)
    return fc32, fs32, (fc32.astype(BF16), fs32.astype(BF16), gc, gs)


def _hyena_positional(S):
    pos = jnp.arange(S, dtype=F32)
    t = jnp.linspace(0.0, 1.0, S, dtype=F32)[:, None]
    bands = (HYENA_EMB_DIM - 1) // 2
    f = jnp.linspace(1e-4, bands - 1, bands, dtype=F32)
    ang = (2.0 * math.pi / S) * pos[:, None] * f[None, :]
    feats = jnp.concatenate([t, jnp.cos(ang), -jnp.sin(ang)], axis=-1)
    feats = jnp.pad(feats, ((0, 0), (0, LANES - HYENA_EMB_DIM)))
    max_decay = math.log(HYENA_DECAY_TARGET) / HYENA_SHORT_DECAY_PCT
    min_decay = math.log(HYENA_DECAY_TARGET) / HYENA_LONG_DECAY_PCT
    deltas = jnp.abs(jnp.linspace(min_decay, max_decay, HYENA_CH, dtype=F32))[None, :]
    return feats, t, deltas


def _alibi_slopes():
    return jnp.asarray(np.array([2.0 ** (-8.0 * (i + 1) / N_HEADS) for i in range(N_HEADS)], dtype=np.float32))


def _even_mixer(x, xshape, tabs, w_in, b_in, short_w, short_b, f1_w, f1_b, f1_freq, f2_w, f2_b, f2_freq, f3_w,
                skip, dw_w, dw_b, cln_g, cln_b, w_out, b_out, ln_g, ln_b, w_r, b_r):
    B, S = xshape
    fc32, fs32, tabs16 = tabs
    proj = _project(x, w_in.astype(BF16), b_in[None, :])
    hy = _short_conv(proj, short_w, short_b[None, :], B, S)
    u = _conformer(proj, tabs, dw_w, dw_b[None, :], cln_g[None, :], cln_b[None, :], B, S)
    feats, tcol, deltas = _hyena_positional(S)
    f1_wp = jnp.pad(f1_w, ((0, LANES - HYENA_EMB_DIM), (0, 0)))
    flip = lambda a: jnp.concatenate([a[:1], a[:0:-1]], axis=0)
    h2 = _hyena_filters(jnp.stack([flip(feats), feats]), f1_wp, f1_b[None, :], f1_freq[None, :], f2_w, f2_b[None, :],
                        f2_freq[None, :], f3_w, jnp.stack([flip(tcol), tcol]), deltas)
    hre, him = _lag_spectra(fc32, fs32, h2, S // fc32.shape[0] - 1)
    z = _long_conv(hy, 2, hy, 0, tabs16, hre, him, 0, skip, B, S)
    z = _long_conv(z, 0, hy, 1, tabs16, hre, him, 1, skip, B, S)
    return _outproj_ln(z, 0, u, 0, w_out.astype(BF16), b_out[None, :], x, ln_g[None, :], ln_b[None, :],
                       w_r.T, b_r[:, None])


def _odd_mixer(x, xshape, layer_idx, w_qkv, lq1, lk1, lq2, lk2, subln_g, w_out, ln_g, ln_b, w_r, b_r):
    B, S = xshape
    lam_init = 0.8 - 0.6 * math.exp(-0.3 * layer_idx)
    lam = (jnp.exp(jnp.sum(lq1 * lk1)) - jnp.exp(jnp.sum(lq2 * lk2)) + lam_init).reshape(1)
    q_scale = jnp.concatenate([jnp.full((ATTN_W,), HEAD_DIM ** -0.5, F32), jnp.ones((2 * ATTN_W,), F32)])
    w = (w_qkv * q_scale).astype(BF16)
    qkv = _project(x, w, jnp.zeros((1, 3 * ATTN_W), F32))
    o = _diff_attention(qkv, _alibi_slopes(), lam, subln_g[None, :], lam_init, B, S)
    return _outproj_ln(o, 0, o, 1, w_out.astype(BF16), jnp.zeros((1, D_MODEL), F32), x, ln_g[None, :], ln_b[None, :],
                       w_r.T, b_r[:, None])


def _round_up(a, m):
    return (a + m - 1) // m * m


def _copy_lists(loff, goff, units):
    E = N_EXPERTS
    caps = _copy_caps()
    big = COPY_ROWS[0]
    n_big = units // (big // SEG_ALIGN)
    cum = jnp.cumsum(n_big, axis=1)
    first = (cum - n_big)[:, None, :]
    p = jnp.arange(caps[0], dtype=I32)[None, :, None]
    mine = (first <= p) & (p < cum[:, None, :])
    within = (p - first) * big
    counts = [cum[:, -1]]
    cols = [jnp.sum(jnp.where(mine, loff[:, None, :] + within, 0), axis=2),
            jnp.sum(jnp.where(mine, goff[:, None, :] + within, 0), axis=2)]
    off = n_big * big
    p = jnp.arange(E, dtype=I32)[None, :, None]
    for rows in COPY_ROWS[1:]:
        has = (units & (rows // SEG_ALIGN)) != 0
        pos = jnp.cumsum(has.astype(I32), axis=1) - has.astype(I32)
        mine = has[:, None, :] & (pos[:, None, :] == p)
        counts.append(jnp.sum(has.astype(I32), axis=1))
        cols += [jnp.sum(jnp.where(mine, (loff + off)[:, None, :], 0), axis=2),
                 jnp.sum(jnp.where(mine, (goff + off)[:, None, :], 0), axis=2)]
        off = off + jnp.where(has, rows, 0)
    return jnp.concatenate([jnp.stack(counts, axis=1)] + cols, axis=1).astype(I32)


def _routing_tables(cnt_blocks, n_ffn_blocks):
    E = N_EXPERTS
    cnt8 = _round_up(cnt_blocks[:, :, 0].astype(I32), SEG_ALIGN)
    seg_end = jnp.cumsum(cnt8, axis=1)
    loff = seg_end - cnt8
    tot8 = jnp.sum(cnt8, axis=0)
    group = _round_up(tot8, EXPERT_ROWS)
    group_end = jnp.cumsum(group)
    group_start = group_end - group
    goff = group_start[None, :] + jnp.cumsum(cnt8, axis=0) - cnt8
    seg_table = _copy_lists(loff, goff, cnt8 // SEG_ALIGN)
    starts = jnp.arange(n_ffn_blocks, dtype=I32) * EXPERT_ROWS
    blk_expert = jnp.minimum(jnp.sum((group_end[None, :] <= starts[:, None]).astype(I32), axis=1), E - 1)
    n_used = group_end[-1:] // EXPERT_ROWS
    tail_start = group_start + tot8
    return loff[:, :, None], seg_table[:, None, :], blk_expert, n_used, tail_start


def _moe_layer(x, routing, layer, w1, b1, w2, b2, ln_g, ln_b):
    T = x.shape[0]
    nb = T // ROW_TILE
    n_rows = _round_up(T * TOP_K + nb * N_EXPERTS * (SEG_ALIGN - 1), EXPERT_ROWS) + N_EXPERTS * EXPERT_ROWS
    n_ffn_blocks = n_rows // EXPERT_ROWS
    idx, gate, rank, cnt = routing
    loff, seg_table, blk_expert, n_used, tail_start = _routing_tables(cnt, n_ffn_blocks)
    xs, lrow = _dispatch(x, idx, rank, loff, seg_table, tail_start, n_rows + EXPERT_ROWS)
    ys = _expert_ffn(xs, blk_expert, n_used, w1, b1[:, :, None, :], w2, b2[:, :, None, :], layer, n_ffn_blocks)
    return _combine_ln(lrow, gate, x, ln_g[None, :], ln_b[None, :], ys, seg_table)


def kernel(x, hy_cf_w_in, hy_cf_b_in, hy_short_w, hy_short_b, hy_f1_w, hy_f1_b, hy_f1_freq, hy_f2_w, hy_f2_b, hy_f2_freq, hy_f3_w, hy_skip, cf_dw_w, cf_dw_b, cf_ln_g, cf_ln_b, even_w_out, even_b_out, attn_w_qkv, attn_lq1, attn_lk1, attn_lq2, attn_lk2, attn_subln_g, attn_w_out, ln1_g, ln1_b, ln2_g, ln2_b, moe_w_r, moe_b_r, moe_w1, moe_b1, moe_w2, moe_b2):
    B, S, D = x.shape
    assert D == D_MODEL and S % LANES == 0
    assert (B * S) % ROW_TILE == 0 and (B * S) % min(B * S, DENSE_ROW_TILE) == 0
    depth = ln1_g.shape[0]
    xf = x.reshape(B * S, D)
    tabs = _dft_tables(S // CONV_BLOCKS)
    for i in range(depth):
        j = i // 2
        if i % 2 == 0:
            xf, routing = _even_mixer(xf, (B, S), tabs, hy_cf_w_in[j], hy_cf_b_in[j], hy_short_w[j], hy_short_b[j],
                                      hy_f1_w[j], hy_f1_b[j], hy_f1_freq[j], hy_f2_w[j], hy_f2_b[j], hy_f2_freq[j],
                                      hy_f3_w[j], hy_skip[j], cf_dw_w[j], cf_dw_b[j], cf_ln_g[j], cf_ln_b[j],
                                      even_w_out[j], even_b_out[j], ln1_g[i], ln1_b[i], moe_w_r[i], moe_b_r[i])
        else:
            xf, routing = _odd_mixer(xf, (B, S), i, attn_w_qkv[j], attn_lq1[j], attn_lk1[j], attn_lq2[j],
                                     attn_lk2[j], attn_subln_g[j], attn_w_out[j], ln1_g[i], ln1_b[i],
                                     moe_w_r[i], moe_b_r[i])
        xf = _moe_layer(xf, routing, i, moe_w1, moe_b1, moe_w2, moe_b2, ln2_g[i], ln2_b[i])
    return xf.reshape(B, S, D)
```

```python
import functools
import math

import jax
import jax.numpy as jnp
import numpy as np
from jax import lax
from jax.experimental import pallas as pl
from jax.experimental.pallas import tpu as pltpu

F32 = jnp.float32
BF16 = jnp.bfloat16
U32 = jnp.uint32
I32 = jnp.int32

D_MODEL = 1024
HALF = D_MODEL // 2
DEPTH = 4
HYENA_CH = D_MODEL // 2
CONF_CH = D_MODEL // 2
HYENA_ORDER = 2
HYENA_EMB_DIM = 33
HYENA_FILTER_DIM = 64
HYENA_SHORT_DECAY_PCT = 0.3
HYENA_LONG_DECAY_PCT = 1.5
HYENA_DECAY_TARGET = 1e-2
CONF_WIDTH = 31
EVEN_IN = 3 * HYENA_CH + 2 * CONF_CH
HYENA_SHORT_GROUPS = 3
N_HEADS = 8
HEAD_DIM = 64
ATTN_W = N_HEADS * 2 * HEAD_DIM
N_EXPERTS = 32
TOP_K = 4
D_FF = D_MODEL
SWIGLU_LIMIT = 7.0
SWIGLU_ALPHA = 1.702
DEEPNORM_ALPHA = (2 * DEPTH) ** 0.25
LN_EPS = 1e-5

LANES = 128
VMEM_LIMIT_BYTES = 56 * 1024 * 1024
ROW_TILE = 512
DENSE_ROW_TILE = 1024
EXPERT_ROWS = 512
SEG_ALIGN = 8
COPY_ROWS = (64, 32, 16, 8)
CONV_BLOCKS = 4
CONV_CH_TILE = 256
ATTN_Q_TILE = 2048
ATTN_ROW_CHUNK = 256

_NT = (((1,), (1,)), ((), ()))


def _params(*sem):
    return pltpu.CompilerParams(dimension_semantics=sem, vmem_limit_bytes=VMEM_LIMIT_BYTES)


def _split_bf16(a):
    hi = a.astype(BF16)
    lo = (a - hi.astype(F32)).astype(BF16)
    return hi, lo


def _dot3(a, b):
    a_hi, a_lo = _split_bf16(a)
    b_hi, b_lo = _split_bf16(b)
    d = functools.partial(jnp.dot, preferred_element_type=F32)
    return d(a_hi, b_hi) + d(a_hi, b_lo) + d(a_lo, b_hi)


def _layer_norm_rows(y, g, b):
    mu = jnp.mean(y, axis=-1, keepdims=True)
    yc = y - mu
    var = jnp.mean(yc * yc, axis=-1, keepdims=True)
    return yc * lax.rsqrt(var + LN_EPS) * g + b


def _pack_halves(y):
    lo = lax.bitcast_convert_type(y[:, :HALF].astype(BF16).astype(F32), U32)
    hi = lax.bitcast_convert_type(y[:, HALF:].astype(BF16).astype(F32), U32)
    return hi | (lo >> 16)


def _unpack_halves(p):
    lo = lax.bitcast_convert_type(p << 16, F32)
    hi = lax.bitcast_convert_type(p & jnp.uint32(0xFFFF0000), F32)
    return lo, hi


def _proj_body(x_ref, w_ref, b_ref, o_ref, *, col_chunk):
    x = x_ref[...].astype(BF16)
    for j in range(0, o_ref.shape[1], col_chunk):
        acc = jnp.dot(x, w_ref[:, j:j + col_chunk], preferred_element_type=F32)
        o_ref[:, j:j + col_chunk] = (acc + b_ref[:, j:j + col_chunk]).astype(o_ref.dtype)


def _project(x, w, b):
    T, K = x.shape
    N = w.shape[1]
    tm = min(T, DENSE_ROW_TILE)
    return pl.pallas_call(
        functools.partial(_proj_body, col_chunk=512),
        grid=(T // tm,),
        in_specs=[pl.BlockSpec((tm, K), lambda i: (i, 0)),
                  pl.BlockSpec((K, N), lambda i: (0, 0)),
                  pl.BlockSpec((1, N), lambda i: (0, 0))],
        out_specs=pl.BlockSpec((tm, N), lambda i: (i, 0)),
        out_shape=jax.ShapeDtypeStruct((T, N), BF16),
        compiler_params=_params("parallel"),
        name="project",
    )(x, w, b)


def _outproj_ln_body(a1_ref, a2_ref, w1_ref, w2_ref, b_ref, x_ref, g_ref, beta_ref, wr_ref, br_ref,
                     xo_ref, idx_ref, gate_ref, rank_ref, cnt_ref):
    m = (jnp.dot(a1_ref[...], w1_ref[...], preferred_element_type=F32)
         + jnp.dot(a2_ref[...], w2_ref[...], preferred_element_type=F32) + b_ref[...])
    y = _layer_norm_rows(DEEPNORM_ALPHA * x_ref[...] + m, g_ref[...], beta_ref[...])
    xo_ref[...] = y
    w_hi, w_lo = _split_bf16(wr_ref[...])
    for s in range(y.shape[0] // ROW_TILE):
        cols = slice(s * ROW_TILE, (s + 1) * ROW_TILE)
        _route_block(y[cols, :], w_hi, w_lo, br_ref[...], idx_ref, gate_ref, rank_ref, cnt_ref.at[s], cols)


def _outproj_ln(a1, a1_col, a2, a2_col, w, b, x, g, beta, w_rt, b_r):
    T = x.shape[0]
    E = N_EXPERTS
    tm = min(T, DENSE_ROW_TILE)
    nsub = tm // ROW_TILE
    tok = pl.BlockSpec((TOP_K, tm), lambda i: (0, i))
    out = pl.pallas_call(
        _outproj_ln_body,
        grid=(T // tm,),
        in_specs=[pl.BlockSpec((tm, HALF), lambda i: (i, a1_col)),
                  pl.BlockSpec((tm, HALF), lambda i: (i, a2_col)),
                  pl.BlockSpec((HALF, D_MODEL), lambda i: (0, 0)),
                  pl.BlockSpec((HALF, D_MODEL), lambda i: (1, 0)),
                  pl.BlockSpec((1, D_MODEL), lambda i: (0, 0)),
                  pl.BlockSpec((tm, D_MODEL), lambda i: (i, 0)),
                  pl.BlockSpec((1, D_MODEL), lambda i: (0, 0)),
                  pl.BlockSpec((1, D_MODEL), lambda i: (0, 0)),
                  pl.BlockSpec((E, D_MODEL), lambda i: (0, 0)),
                  pl.BlockSpec((E, 1), lambda i: (0, 0))],
        out_specs=[pl.BlockSpec((tm, D_MODEL), lambda i: (i, 0)), tok, tok, tok,
                   pl.BlockSpec((nsub, E, LANES), lambda i: (i, 0, 0))],
        out_shape=[jax.ShapeDtypeStruct((T, D_MODEL), F32),
                   jax.ShapeDtypeStruct((TOP_K, T), I32), jax.ShapeDtypeStruct((TOP_K, T), F32),
                   jax.ShapeDtypeStruct((TOP_K, T), I32), jax.ShapeDtypeStruct((T // ROW_TILE, E, LANES), F32)],
        compiler_params=_params("parallel"),
        name="outproj_ln_route",
    )(a1, a2, w, w, b, x, g, beta, w_rt, b_r)
    return out[0], tuple(out[1:])


def _proj_even_body(x_ref, w_ref, b_ref, sw_ref, sb_ref, o_ref, xb_ref):
    j = pl.program_id(1)

    @pl.when(j == 0)
    def _():
        xb_ref[...] = x_ref[...].astype(BF16)

    S, C = o_ref.shape
    half = C // 2

    def project(c0):
        cols = slice(c0, c0 + half)
        return jnp.dot(xb_ref[...], w_ref[:, cols], preferred_element_type=F32) + b_ref[:, cols]

    @pl.when(j < HYENA_SHORT_GROUPS)
    def _():
        row = lax.broadcasted_iota(I32, (S, half), 0)
        for c0 in range(0, C, half):
            cols = slice(c0, c0 + half)
            y = project(c0)
            prev = jnp.where(row == 0, 0.0, pltpu.roll(y, 1, 0))
            nxt = jnp.where(row == S - 1, 0.0, pltpu.roll(y, S - 1, 0))
            conv = sw_ref[0:1, cols] * prev + sw_ref[1:2, cols] * y + sw_ref[2:3, cols] * nxt + sb_ref[:, cols]
            o_ref[:, cols] = conv.astype(o_ref.dtype)

    @pl.when(j >= HYENA_SHORT_GROUPS)
    def _():
        for c0 in range(0, C, half):
            o_ref[:, c0:c0 + half] = project(c0).astype(o_ref.dtype)


def _project_even(x, w, b, short_w, short_b, B, S):
    T, K = x.shape
    C = HYENA_CH
    last = HYENA_SHORT_GROUPS - 1
    return pl.pallas_call(
        _proj_even_body,
        grid=(B, EVEN_IN // C),
        in_specs=[pl.BlockSpec((S, K), lambda bi, j: (bi, 0)),
                  pl.BlockSpec((K, C), lambda bi, j: (0, j)),
                  pl.BlockSpec((1, C), lambda bi, j: (0, j)),
                  pl.BlockSpec((3, C), lambda bi, j: (0, jnp.minimum(j, last))),
                  pl.BlockSpec((1, C), lambda bi, j: (0, jnp.minimum(j, last)))],
        out_specs=pl.BlockSpec((S, C), lambda bi, j: (bi, j)),
        out_shape=jax.ShapeDtypeStruct((T, EVEN_IN), BF16),
        scratch_shapes=[pltpu.VMEM((S, K), BF16)],
        compiler_params=_params("parallel", "arbitrary"),
        name="project_even_short_conv",
    )(x, w, b, short_w, short_b)


def _conformer_body(a_ref, g_ref, fc_ref, fs_ref, hre_ref, him_ref, gc_ref, gs_ref, b_ref, lg_ref, lb_ref, o_ref,
                    *, n_blk):
    P = fc_ref.shape[0]
    fc, fs, gc, gs = fc_ref[...], fs_ref[...], gc_ref[...], gs_ref[...]
    vre, vim = [], []
    for j in range(n_blk):
        rows = slice(j * P, (j + 1) * P)
        u = (a_ref[rows, :].astype(F32) * jax.nn.sigmoid(g_ref[rows, :].astype(F32))).astype(BF16)
        vre.append(jnp.dot(fc, u, preferred_element_type=F32))
        vim.append(jnp.dot(fs, u, preferred_element_type=F32))
    for i in range(n_blk):
        yre, yim = _mix_block_lags(vre, vim, hre_ref, him_ref, i, 1)
        y = jnp.dot(gc, yre, preferred_element_type=F32) + jnp.dot(gs, yim, preferred_element_type=F32) + b_ref[...]
        y = _layer_norm_rows(y, lg_ref[...], lb_ref[...])
        o_ref[i * P:(i + 1) * P, :] = (y * jax.nn.sigmoid(y)).astype(o_ref.dtype)


def _conformer(proj, tabs, w, b, lg, lb, B, S):
    fc32, fs32, (fc, fs, gc, gs) = tabs
    P = fc.shape[0]
    T = B * S
    C = CONF_CH
    half = CONF_WIDTH // 2
    assert half < P
    span = ((S - half, S - half - 1), (0, 0))
    hre, him = _lag_spectra(fc32, fs32, jnp.stack([jnp.pad(w[::-1], span), jnp.pad(w, span)]), 1)
    tab = pl.BlockSpec((P, P), lambda bi: (0, 0))
    spec = pl.BlockSpec((3 * P, C), lambda bi: (0, 0))
    vec = pl.BlockSpec((1, C), lambda bi: (0, 0))
    return pl.pallas_call(
        functools.partial(_conformer_body, n_blk=S // P),
        grid=(B,),
        in_specs=[pl.BlockSpec((S, C), lambda bi: (bi, 3)),
                  pl.BlockSpec((S, C), lambda bi: (bi, 4)),
                  tab, tab, spec, spec, tab, tab, vec, vec, vec],
        out_specs=pl.BlockSpec((S, C), lambda bi: (bi, 0)),
        out_shape=jax.ShapeDtypeStruct((T, C), BF16),
        compiler_params=_params("parallel"),
        name="conformer_conv",
    )(proj, proj, fc, fs, hre, him, gc, gs, b, lg, lb)


def _filter_body(feat_ref, w1_ref, b1_ref, q1_ref, w2_ref, b2_ref, q2_ref, w3_ref, t_ref, delta_ref, o_ref):
    h = jnp.sin(q1_ref[...] * (_dot3(feat_ref[...], w1_ref[...]) + b1_ref[...]))
    h = jnp.sin(q2_ref[...] * (_dot3(h, w2_ref[...]) + b2_ref[...]))
    h = _dot3(h, w3_ref[...])
    o_ref[...] = h * jnp.exp(-t_ref[...] * delta_ref[...])


def _hyena_filters(feats2, w1, b1, q1, w2, b2, q2, w3, tcol2, deltas):
    S = feats2.shape[1]
    C = HYENA_CH
    fd = HYENA_FILTER_DIM
    fixed = lambda p, d, o: (0, 0)
    return pl.pallas_call(
        _filter_body,
        grid=(2, 2, HYENA_ORDER),
        in_specs=[pl.BlockSpec((None, S, LANES), lambda p, d, o: (p, 0, 0)),
                  pl.BlockSpec((LANES, fd), fixed),
                  pl.BlockSpec((1, fd), fixed),
                  pl.BlockSpec((1, fd), fixed),
                  pl.BlockSpec((fd, fd), fixed),
                  pl.BlockSpec((1, fd), fixed),
                  pl.BlockSpec((1, fd), fixed),
                  pl.BlockSpec((fd, C), lambda p, d, o: (0, d * HYENA_ORDER + o)),
                  pl.BlockSpec((None, S, 1), lambda p, d, o: (p, 0, 0)),
                  pl.BlockSpec((1, C), fixed)],
        out_specs=pl.BlockSpec((None, S, C), lambda p, d, o: ((d + 1 - p) % 2, p, o)),
        out_shape=jax.ShapeDtypeStruct((2, 2 * S, HYENA_ORDER * C), F32),
        compiler_params=_params("parallel", "parallel", "parallel"),
        name="hyena_filter_mlp",
    )(feats2, w1, b1, q1, w2, b2, q2, w3, tcol2, deltas)


def _spectrum_body(fc_ref, fs_ref, a_ref, b_ref, hre_ref, him_ref):
    a = a_ref[...]
    row = lax.broadcasted_iota(I32, a.shape, 0)
    b = jnp.where(row == 0, 0.0, b_ref[...])
    hre_ref[...] = _dot3(fc_ref[...], a + b)
    him_ref[...] = _dot3(fs_ref[...], a - b)


def _lag_spectra(fc32, fs32, h2, d_max):
    P = fc32.shape[0]
    n = h2.shape[1] // (2 * P)
    n_ch = h2.shape[2]
    C = HYENA_CH
    nd = 2 * d_max + 1
    spec = pl.BlockSpec((P, C), lambda di, o: (di, o))
    return pl.pallas_call(
        _spectrum_body,
        grid=(nd, n_ch // C),
        in_specs=[pl.BlockSpec((P, P), lambda di, o: (0, 0)),
                  pl.BlockSpec((P, P), lambda di, o: (0, 0)),
                  pl.BlockSpec((None, P, C), lambda di, o: (0, n - d_max + di, o)),
                  pl.BlockSpec((None, P, C), lambda di, o: (1, n + d_max - di, o))],
        out_specs=[spec, spec],
        out_shape=[jax.ShapeDtypeStruct((nd * P, n_ch), F32)] * 2,
        compiler_params=_params("parallel", "parallel"),
        name="block_lag_spectra",
    )(fc32, fs32, h2, h2)


def _mix_block_lags(vre, vim, hre_ref, him_ref, i, d_max):
    P = vre[0].shape[0]
    yre = yim = None
    for j in range(max(0, i - d_max), min(len(vre), i + d_max + 1)):
        r0 = (i - j + d_max) * P
        hre = hre_ref[r0:r0 + P, :]
        him = him_ref[r0:r0 + P, :]
        tre = vre[j] * hre - vim[j] * him
        tim = vre[j] * him + vim[j] * hre
        yre = tre if yre is None else yre + tre
        yim = tim if yim is None else yim + tim
    return yre.astype(BF16), yim.astype(BF16)


def _long_conv_body(v_ref, gate_ref, fc_ref, fs_ref, hre_ref, him_ref, gc_ref, gs_ref, skip_ref, o_ref, *, n_blk):
    P = fc_ref.shape[0]
    fc, fs, gc, gs = fc_ref[...], fs_ref[...], gc_ref[...], gs_ref[...]
    vre, vim = [], []
    for j in range(n_blk):
        vj = v_ref[j * P:(j + 1) * P, :]
        vre.append(jnp.dot(fc, vj, preferred_element_type=F32))
        vim.append(jnp.dot(fs, vj, preferred_element_type=F32))
    for i in range(n_blk):
        yre, yim = _mix_block_lags(vre, vim, hre_ref, him_ref, i, n_blk - 1)
        y = jnp.dot(gc, yre, preferred_element_type=F32) + jnp.dot(gs, yim, preferred_element_type=F32)
        rows = slice(i * P, (i + 1) * P)
        y = y + v_ref[rows, :].astype(F32) * skip_ref[...]
        o_ref[rows, :] = (gate_ref[rows, :].astype(F32) * y).astype(o_ref.dtype)


def _long_conv(v_arr, v_col, gate_arr, gate_col, tabs, hre, him, order, skip, B, S):
    fc, fs, gc, gs = tabs
    P = fc.shape[0]
    T = B * S
    C = HYENA_CH
    nc = C // CONV_CH_TILE
    cc = CONV_CH_TILE
    n_h = hre.shape[0]
    tab = pl.BlockSpec((P, P), lambda bi, c: (0, 0))
    return pl.pallas_call(
        functools.partial(_long_conv_body, n_blk=S // P),
        grid=(B, nc),
        in_specs=[pl.BlockSpec((S, cc), lambda bi, c: (bi, v_col * nc + c)),
                  pl.BlockSpec((S, cc), lambda bi, c: (bi, gate_col * nc + c)),
                  tab, tab,
                  pl.BlockSpec((n_h, cc), lambda bi, c: (0, order * nc + c)),
                  pl.BlockSpec((n_h, cc), lambda bi, c: (0, order * nc + c)),
                  tab, tab,
                  pl.BlockSpec((1, cc), lambda bi, c: (0, c))],
        out_specs=pl.BlockSpec((S, cc), lambda bi, c: (bi, c)),
        out_shape=jax.ShapeDtypeStruct((T, C), BF16),
        compiler_params=_params("parallel", "parallel"),
        name="hyena_long_conv",
    )(v_arr, gate_arr, fc, fs, hre, him, gc, gs, skip[order][None, :])


def _attn_body(slope_ref, lam_ref, q_ref, k_ref, v_ref, g_ref, o_ref, vaug_ref, *, lam_init, row_chunk):
    h = pl.program_id(1)
    qi = pl.program_id(2)
    tq = q_ref.shape[0]
    S = k_ref.shape[0]
    hw = 2 * HEAD_DIM
    k = k_ref[...]
    slope = slope_ref[h]
    kpos = lax.broadcasted_iota(I32, (1, S), 1).astype(F32) * slope

    @pl.when(qi == 0)
    def _():
        vaug_ref[:, :hw] = v_ref[...]
        vaug_ref[:, hw:] = jnp.where(lax.broadcasted_iota(I32, (S, hw), 1) == 0, 1.0, 0.0).astype(BF16)

    v_aug = vaug_ref[...]

    for r0 in range(0, tq, row_chunk):
        q = q_ref[r0:r0 + row_chunk, :]
        lane = lax.broadcasted_iota(I32, q.shape, 1)
        zero = jnp.zeros_like(q)
        qpos = (qi * tq + r0 + lax.broadcasted_iota(I32, (row_chunk, 1), 0)).astype(F32) * slope
        bias = lax.bitcast_convert_type(lax.bitcast_convert_type(qpos - kpos, U32) | jnp.uint32(0x80000000), F32)

        def weighted_values(qh):
            s = lax.dot_general(qh, k, _NT, preferred_element_type=F32) + bias
            e = jnp.exp((s - jnp.max(s, axis=-1, keepdims=True)).astype(BF16))
            return jnp.dot(e, v_aug, preferred_element_type=F32)

        o1 = weighted_values(jnp.where(lane < HEAD_DIM, q, zero))
        o2 = weighted_values(jnp.where(lane >= HEAD_DIM, q, zero))
        o = o1[:, :hw] * (1.0 / o1[:, hw:hw + 1]) - o2[:, :hw] * (lam_ref[0] / o2[:, hw:hw + 1])
        o = o * lax.rsqrt(jnp.mean(o * o, axis=-1, keepdims=True) + LN_EPS) * g_ref[...]
        o_ref[r0:r0 + row_chunk, :] = (o * (1.0 - lam_init)).astype(o_ref.dtype)


def _diff_attention(qkv, slopes, lam, subln_g, lam_init, B, S):
    T = B * S
    hw = 2 * HEAD_DIM
    tq = min(S, ATTN_Q_TILE)
    nq = S // tq
    smem = pl.BlockSpec(memory_space=pltpu.SMEM)
    return pl.pallas_call(
        functools.partial(_attn_body, lam_init=lam_init, row_chunk=min(tq, ATTN_ROW_CHUNK)),
        grid=(B, N_HEADS, nq),
        in_specs=[smem, smem,
                  pl.BlockSpec((tq, hw), lambda bi, h, qi: (bi * nq + qi, h)),
                  pl.BlockSpec((S, hw), lambda bi, h, qi: (bi, N_HEADS + h)),
                  pl.BlockSpec((S, hw), lambda bi, h, qi: (bi, 2 * N_HEADS + h)),
                  pl.BlockSpec((1, hw), lambda bi, h, qi: (0, 0))],
        out_specs=pl.BlockSpec((tq, hw), lambda bi, h, qi: (bi * nq + qi, h)),
        out_shape=jax.ShapeDtypeStruct((T, ATTN_W), BF16),
        scratch_shapes=[pltpu.VMEM((S, 2 * hw), BF16)],
        compiler_params=_params("parallel", "parallel", "arbitrary"),
        name="diff_attention",
    )(slopes, lam, qkv, qkv, qkv, subln_g)


def _route_block(x, w_hi, w_lo, bias, idx_ref, gate_ref, rank_ref, cnt_ref, cols):
    E = N_EXPERTS
    tm = x.shape[0]
    x_hi, x_lo = _split_bf16(x)
    nt = functools.partial(lax.dot_general, dimension_numbers=_NT, preferred_element_type=F32)
    logits = nt(w_hi, x_hi) + nt(w_lo, x_hi) + nt(w_hi, x_lo) + bias

    eid = lax.broadcasted_iota(I32, (E, tm), 0).astype(F32)
    work = logits
    vals, idxs = [], []
    for _ in range(TOP_K):
        m = jnp.max(work, axis=0, keepdims=True)
        sel = jnp.min(jnp.where(work == m, eid, float(E)), axis=0, keepdims=True)
        vals.append(m)
        idxs.append(sel)
        work = jnp.where(eid == sel, -jnp.inf, work)
    exps = [jnp.exp(v - vals[0]) for v in vals]
    denom = exps[0] + exps[1] + exps[2] + exps[3]

    chosen = jnp.zeros((E, tm), F32)
    for sel in idxs:
        chosen = chosen + jnp.where(eid == sel, 1.0, 0.0)
    earlier = jnp.where(lax.broadcasted_iota(I32, (tm, tm), 0) < lax.broadcasted_iota(I32, (tm, tm), 1), 1.0, 0.0)
    before = jnp.dot(chosen.astype(BF16), earlier.astype(BF16), preferred_element_type=F32)
    for k in range(TOP_K):
        gate_ref[k:k + 1, cols] = exps[k] / denom
        idx_ref[k:k + 1, cols] = idxs[k].astype(I32)
        rank_ref[k:k + 1, cols] = jnp.sum(jnp.where(eid == idxs[k], before, 0.0), axis=0, keepdims=True).astype(I32)
    cnt_ref[...] = jnp.broadcast_to(jnp.sum(chosen, axis=1, keepdims=True), cnt_ref.shape)


def _copy_caps():
    local_rows = TOP_K * ROW_TILE + N_EXPERTS * SEG_ALIGN
    return (local_rows // COPY_ROWS[0],) + (N_EXPERTS,) * (len(COPY_ROWS) - 1)


def _segment_copies(tab_ref, make_copy, slot, wait):
    base = len(COPY_ROWS)
    for ci, (rows, cap) in enumerate(zip(COPY_ROWS, _copy_caps())):
        def body(p, carry, base=base, rows=rows, cap=cap):
            copy = make_copy(slot, tab_ref[0, base + p], tab_ref[0, base + cap + p], rows)
            if wait:
                copy.wait()
            else:
                copy.start()
            return carry
        lax.fori_loop(0, tab_ref[0, ci], body, 0)
        base += 2 * cap


def _dispatch_body(tail_ref, seg_ref, seg_prev_ref, x_ref, idx_ref, rank_ref, loff_ref, xs_hbm, lrow_ref,
                   buf_ref, zero_ref, sem, zsem, *, n_tok_blocks):
    b = pl.program_id(0)
    slot = b % 2
    tm = x_ref.shape[0]
    R = buf_ref.shape[1]

    def seg_copy(s, local_row, dst_row, rows):
        return pltpu.make_async_copy(buf_ref.at[s, pl.ds(pl.multiple_of(local_row, SEG_ALIGN), rows)],
                                     xs_hbm.at[pl.ds(pl.multiple_of(dst_row, SEG_ALIGN), rows)], sem.at[s])

    @pl.when(b == 0)
    def _():
        zero_ref[...] = jnp.zeros_like(zero_ref)
        for e in range(N_EXPERTS):
            fill = pltpu.make_async_copy(
                zero_ref, xs_hbm.at[pl.ds(pl.multiple_of(tail_ref[e], SEG_ALIGN), EXPERT_ROWS)], zsem)
            fill.start()
            fill.wait()

    eid = lax.broadcasted_iota(I32, (N_EXPERTS, tm), 0)
    loff = loff_ref[...].astype(F32)
    rid = lax.broadcasted_iota(I32, (R, tm), 0).astype(jnp.int16)
    sel_t = jnp.zeros((R, tm), BF16)
    for k in range(TOP_K):
        base = jnp.sum(jnp.where(eid == idx_ref[k:k + 1, :], loff, 0.0), axis=0, keepdims=True).astype(I32)
        row = base + rank_ref[k:k + 1, :]
        lrow_ref[k:k + 1, :] = row
        sel_t = jnp.where(rid == row.astype(jnp.int16), jnp.ones((), BF16), sel_t)
    xb = x_ref[...].astype(BF16)
    lo = jnp.dot(sel_t, xb[:, :HALF], preferred_element_type=F32)
    hi = jnp.dot(sel_t, xb[:, HALF:], preferred_element_type=F32)
    packed = (lax.bitcast_convert_type(hi, U32) & jnp.uint32(0xFFFF0000)) | (lax.bitcast_convert_type(lo, U32) >> 16)
    buf_ref[slot] = packed

    @pl.when(b >= 1)
    def _():
        _segment_copies(seg_prev_ref, seg_copy, 1 - slot, wait=True)
    _segment_copies(seg_ref, seg_copy, slot, wait=False)

    @pl.when(b == n_tok_blocks - 1)
    def _():
        _segment_copies(seg_ref, seg_copy, slot, wait=True)


def _dispatch(x, idx, rank, loff, seg_table, tail_start, n_rows):
    T = x.shape[0]
    tm = ROW_TILE
    nb = T // tm
    R = TOP_K * tm + N_EXPERTS * SEG_ALIGN
    tok = pl.BlockSpec((TOP_K, tm), lambda b, tl: (0, b))
    seg_w = seg_table.shape[-1]
    return pl.pallas_call(
        functools.partial(_dispatch_body, n_tok_blocks=nb),
        grid_spec=pltpu.PrefetchScalarGridSpec(
            num_scalar_prefetch=1,
            grid=(nb,),
            in_specs=[pl.BlockSpec((None, 1, seg_w), lambda b, tl: (b, 0, 0), memory_space=pltpu.SMEM),
                      pl.BlockSpec((None, 1, seg_w), lambda b, tl: (jnp.maximum(b - 1, 0), 0, 0),
                                   memory_space=pltpu.SMEM),
                      pl.BlockSpec((tm, D_MODEL), lambda b, tl: (b, 0)),
                      tok, tok,
                      pl.BlockSpec((None, N_EXPERTS, 1), lambda b, tl: (b, 0, 0))],
            out_specs=[pl.BlockSpec(memory_space=pl.ANY), tok],
            scratch_shapes=[pltpu.VMEM((2, R, HALF), U32), pltpu.VMEM((EXPERT_ROWS, HALF), U32),
                            pltpu.SemaphoreType.DMA((2,)), pltpu.SemaphoreType.DMA(())]),
        out_shape=[jax.ShapeDtypeStruct((n_rows, HALF), U32), jax.ShapeDtypeStruct((TOP_K, T), I32)],
        compiler_params=_params("arbitrary"),
        name="moe_dispatch",
    )(tail_start, seg_table, seg_table, x, idx, rank, loff)


def _ffn_body(be_ref, nu_ref, xs_ref, w1_ref, b1_ref, w2_ref, b2_ref, ys_ref, w1b_ref, w2b_ref):
    i = pl.program_id(0)
    used = i < nu_ref[0]
    fresh = jnp.logical_or(i == 0, be_ref[i] != be_ref[jnp.maximum(i - 1, 0)])

    @pl.when(jnp.logical_and(used, fresh))
    def _():
        w1b_ref[...] = w1_ref[...].astype(BF16)
        w2b_ref[...] = w2_ref[...].astype(BF16)

    @pl.when(used)
    def _():
        lo, hi = _unpack_halves(xs_ref[...])
        h = (jnp.dot(lo.astype(BF16), w1b_ref[0:HALF, :], preferred_element_type=F32)
             + jnp.dot(hi.astype(BF16), w1b_ref[HALF:D_MODEL, :], preferred_element_type=F32) + b1_ref[...])
        hg = jnp.minimum(h[:, :D_FF], SWIGLU_LIMIT)
        hu = jnp.clip(h[:, D_FF:], -SWIGLU_LIMIT, SWIGLU_LIMIT)
        act = (hu + 1.0) * (hg * jax.nn.sigmoid(hg * SWIGLU_ALPHA))
        y = jnp.dot(act.astype(BF16), w2b_ref[...], preferred_element_type=F32) + b2_ref[...]
        ys_ref[...] = _pack_halves(y)


def _expert_ffn(xs, blk_expert, n_used, w1, b1, w2, b2, layer, n_blocks):
    rows = pl.BlockSpec((EXPERT_ROWS, HALF), lambda i, be, nu: (jnp.minimum(i, nu[0] - 1), 0))
    return pl.pallas_call(
        _ffn_body,
        grid_spec=pltpu.PrefetchScalarGridSpec(
            num_scalar_prefetch=2,
            grid=(n_blocks,),
            in_specs=[rows,
                      pl.BlockSpec((None, None, D_MODEL, 2 * D_FF), lambda i, be, nu: (layer, be[i], 0, 0)),
                      pl.BlockSpec((None, None, 1, 2 * D_FF), lambda i, be, nu: (layer, be[i], 0, 0)),
                      pl.BlockSpec((None, None, D_FF, D_MODEL), lambda i, be, nu: (layer, be[i], 0, 0)),
                      pl.BlockSpec((None, None, 1, D_MODEL), lambda i, be, nu: (layer, be[i], 0, 0))],
            out_specs=rows,
            scratch_shapes=[pltpu.VMEM((D_MODEL, 2 * D_FF), BF16), pltpu.VMEM((D_FF, D_MODEL), BF16)]),
        out_shape=jax.ShapeDtypeStruct((xs.shape[0], HALF), U32),
        compiler_params=_params("arbitrary"),
        name="moe_expert_ffn",
    )(blk_expert, n_used, xs, w1, b1, w2, b2)


def _combine_body(seg_ref, seg_next_ref, lrow_ref, gate_ref, x_ref, g_ref, beta_ref, ys_hbm, xo_ref,
                  buf_ref, sem, *, n_tok_blocks):
    b = pl.program_id(0)
    slot = b % 2
    tm = x_ref.shape[0]
    R = buf_ref.shape[1]

    def seg_copy(s, local_row, src_row, rows):
        return pltpu.make_async_copy(ys_hbm.at[pl.ds(pl.multiple_of(src_row, SEG_ALIGN), rows)],
                                     buf_ref.at[s, pl.ds(pl.multiple_of(local_row, SEG_ALIGN), rows)], sem.at[s])

    @pl.when(b == 0)
    def _():
        buf_ref[...] = jnp.zeros_like(buf_ref)
        _segment_copies(seg_ref, seg_copy, 0, wait=False)

    @pl.when(b + 1 < n_tok_blocks)
    def _():
        _segment_copies(seg_next_ref, seg_copy, 1 - slot, wait=False)

    _segment_copies(seg_ref, seg_copy, slot, wait=True)

    lo, hi = _unpack_halves(buf_ref[slot])
    lo = lo.astype(BF16)
    hi = hi.astype(BF16)
    tc = tm // 2
    cid = lax.broadcasted_iota(I32, (tc, R), 1).astype(jnp.int16)
    lrow_t = lrow_ref[...].astype(F32).T
    gate_t = gate_ref[...].T
    for t0 in range(0, tm, tc):
        lrow = lrow_t[t0:t0 + tc, :].astype(jnp.int16)
        gates = gate_t[t0:t0 + tc, :].astype(BF16)
        sel = jnp.zeros((tc, R), BF16)
        for k in range(TOP_K):
            sel = jnp.where(cid == lrow[:, k:k + 1], gates[:, k:k + 1], sel)
        f = jnp.concatenate([jnp.dot(sel, lo, preferred_element_type=F32),
                             jnp.dot(sel, hi, preferred_element_type=F32)], axis=1)
        xo_ref[t0:t0 + tc, :] = _layer_norm_rows(DEEPNORM_ALPHA * x_ref[t0:t0 + tc, :] + f, g_ref[...], beta_ref[...])


def _combine_ln(lrow, gate, x, g, beta, ys, seg_table):
    T = x.shape[0]
    tm = ROW_TILE
    nb = T // tm
    R = TOP_K * tm + N_EXPERTS * SEG_ALIGN
    seg_w = seg_table.shape[-1]
    return pl.pallas_call(
        functools.partial(_combine_body, n_tok_blocks=nb),
        grid=(nb,),
        in_specs=[pl.BlockSpec((None, 1, seg_w), lambda b: (b, 0, 0), memory_space=pltpu.SMEM),
                  pl.BlockSpec((None, 1, seg_w), lambda b: (jnp.minimum(b + 1, nb - 1), 0, 0),
                               memory_space=pltpu.SMEM),
                  pl.BlockSpec((TOP_K, tm), lambda b: (0, b)),
                  pl.BlockSpec((TOP_K, tm), lambda b: (0, b)),
                  pl.BlockSpec((tm, D_MODEL), lambda b: (b, 0)),
                  pl.BlockSpec((1, D_MODEL), lambda b: (0, 0)),
                  pl.BlockSpec((1, D_MODEL), lambda b: (0, 0)),
                  pl.BlockSpec(memory_space=pl.ANY)],
        out_specs=pl.BlockSpec((tm, D_MODEL), lambda b: (b, 0)),
        out_shape=jax.ShapeDtypeStruct((T, D_MODEL), F32),
        scratch_shapes=[pltpu.VMEM((2, R, HALF), U32), pltpu.SemaphoreType.DMA((2,))],
        compiler_params=_params("arbitrary"),
        name="moe_combine_ln",
    )(seg_table, seg_table, lrow, gate, x, g, beta, ys)


def _dft_tables(P):
    n2 = 4 * P
    k = jnp.arange(P, dtype=I32)
    m = ((2 * k[:, None] + 1) * k[None, :]) % n2
    ang = m.astype(F32) * F32(2.0 * math.pi / n2)
    fc32, fs32 = jnp.cos(ang), -jnp.sin(ang)
    scale = F32(1.0 / P)
    gc, gs = (fc32.T * scale).astype(BF16), (fs32.T * scale).astype(BF16)
    return fc32, fs32, (fc32.astype(BF16), fs32.astype(BF16), gc, gs)


def _hyena_positional(S):
    pos = jnp.arange(S, dtype=F32)
    t = jnp.linspace(0.0, 1.0, S, dtype=F32)[:, None]
    bands = (HYENA_EMB_DIM - 1) // 2
    f = jnp.linspace(1e-4, bands - 1, bands, dtype=F32)
    ang = (2.0 * math.pi / S) * pos[:, None] * f[None, :]
    feats = jnp.concatenate([t, jnp.cos(ang), -jnp.sin(ang)], axis=-1)
    feats = jnp.pad(feats, ((0, 0), (0, LANES - HYENA_EMB_DIM)))
    max_decay = math.log(HYENA_DECAY_TARGET) / HYENA_SHORT_DECAY_PCT
    min_decay = math.log(HYENA_DECAY_TARGET) / HYENA_LONG_DECAY_PCT
    deltas = jnp.abs(jnp.linspace(min_decay, max_decay, HYENA_CH, dtype=F32))[None, :]
    return feats, t, deltas


def _alibi_slopes():
    return jnp.asarray(np.array([2.0 ** (-8.0 * (i + 1) / N_HEADS) for i in range(N_HEADS)], dtype=np.float32))


def _even_mixer(x, xshape, tabs, w_in, b_in, short_w, short_b, f1_w, f1_b, f1_freq, f2_w, f2_b, f2_freq, f3_w,
                skip, dw_w, dw_b, cln_g, cln_b, w_out, b_out, ln_g, ln_b, w_r, b_r):
    B, S = xshape
    fc32, fs32, tabs16 = tabs
    proj = _project_even(x, w_in.astype(BF16), b_in[None, :], short_w, short_b[None, :], B, S)
    u = _conformer(proj, tabs, dw_w, dw_b[None, :], cln_g[None, :], cln_b[None, :], B, S)
    feats, tcol, deltas = _hyena_positional(S)
    f1_wp = jnp.pad(f1_w, ((0, LANES - HYENA_EMB_DIM), (0, 0)))
    flip = lambda a: jnp.concatenate([a[:1], a[:0:-1]], axis=0)
    h2 = _hyena_filters(jnp.stack([flip(feats), feats]), f1_wp, f1_b[None, :], f1_freq[None, :], f2_w, f2_b[None, :],
                        f2_freq[None, :], f3_w, jnp.stack([flip(tcol), tcol]), deltas)
    hre, him = _lag_spectra(fc32, fs32, h2, S // fc32.shape[0] - 1)
    z = _long_conv(proj, 2, proj, 0, tabs16, hre, him, 0, skip, B, S)
    z = _long_conv(z, 0, proj, 1, tabs16, hre, him, 1, skip, B, S)
    return _outproj_ln(z, 0, u, 0, w_out.astype(BF16), b_out[None, :], x, ln_g[None, :], ln_b[None, :],
                       w_r.T, b_r[:, None])


def _odd_mixer(x, xshape, layer_idx, w_qkv, lq1, lk1, lq2, lk2, subln_g, w_out, ln_g, ln_b, w_r, b_r):
    B, S = xshape
    lam_init = 0.8 - 0.6 * math.exp(-0.3 * layer_idx)
    lam = (jnp.exp(jnp.sum(lq1 * lk1)) - jnp.exp(jnp.sum(lq2 * lk2)) + lam_init).reshape(1)
    q_scale = jnp.concatenate([jnp.full((ATTN_W,), HEAD_DIM ** -0.5, F32), jnp.ones((2 * ATTN_W,), F32)])
    w = (w_qkv * q_scale).astype(BF16)
    qkv = _project(x, w, jnp.zeros((1, 3 * ATTN_W), F32))
    o = _diff_attention(qkv, _alibi_slopes(), lam, subln_g[None, :], lam_init, B, S)
    return _outproj_ln(o, 0, o, 1, w_out.astype(BF16), jnp.zeros((1, D_MODEL), F32), x, ln_g[None, :], ln_b[None, :],
                       w_r.T, b_r[:, None])


def _round_up(a, m):
    return (a + m - 1) // m * m


def _copy_lists(loff, goff, units):
    E = N_EXPERTS
    caps = _copy_caps()
    big = COPY_ROWS[0]
    n_big = units // (big // SEG_ALIGN)
    cum = jnp.cumsum(n_big, axis=1)
    first = (cum - n_big)[:, None, :]
    p = jnp.arange(caps[0], dtype=I32)[None, :, None]
    mine = (first <= p) & (p < cum[:, None, :])
    within = (p - first) * big
    counts = [cum[:, -1]]
    cols = [jnp.sum(jnp.where(mine, loff[:, None, :] + within, 0), axis=2),
            jnp.sum(jnp.where(mine, goff[:, None, :] + within, 0), axis=2)]
    off = n_big * big
    p = jnp.arange(E, dtype=I32)[None, :, None]
    for rows in COPY_ROWS[1:]:
        has = (units & (rows // SEG_ALIGN)) != 0
        pos = jnp.cumsum(has.astype(I32), axis=1) - has.astype(I32)
        mine = has[:, None, :] & (pos[:, None, :] == p)
        counts.append(jnp.sum(has.astype(I32), axis=1))
        cols += [jnp.sum(jnp.where(mine, (loff + off)[:, None, :], 0), axis=2),
                 jnp.sum(jnp.where(mine, (goff + off)[:, None, :], 0), axis=2)]
        off = off + jnp.where(has, rows, 0)
    return jnp.concatenate([jnp.stack(counts, axis=1)] + cols, axis=1).astype(I32)


def _routing_tables(cnt_blocks, n_ffn_blocks):
    E = N_EXPERTS
    cnt8 = _round_up(cnt_blocks[:, :, 0].astype(I32), SEG_ALIGN)
    seg_end = jnp.cumsum(cnt8, axis=1)
    loff = seg_end - cnt8
    tot8 = jnp.sum(cnt8, axis=0)
    group = _round_up(tot8, EXPERT_ROWS)
    group_end = jnp.cumsum(group)
    group_start = group_end - group
    goff = group_start[None, :] + jnp.cumsum(cnt8, axis=0) - cnt8
    seg_table = _copy_lists(loff, goff, cnt8 // SEG_ALIGN)
    starts = jnp.arange(n_ffn_blocks, dtype=I32) * EXPERT_ROWS
    blk_expert = jnp.minimum(jnp.sum((group_end[None, :] <= starts[:, None]).astype(I32), axis=1), E - 1)
    n_used = group_end[-1:] // EXPERT_ROWS
    tail_start = group_start + tot8
    return loff[:, :, None], seg_table[:, None, :], blk_expert, n_used, tail_start


def _moe_layer(x, routing, layer, w1, b1, w2, b2, ln_g, ln_b):
    T = x.shape[0]
    nb = T // ROW_TILE
    n_rows = _round_up(T * TOP_K + nb * N_EXPERTS * (SEG_ALIGN - 1), EXPERT_ROWS) + N_EXPERTS * EXPERT_ROWS
    n_ffn_blocks = n_rows // EXPERT_ROWS
    idx, gate, rank, cnt = routing
    loff, seg_table, blk_expert, n_used, tail_start = _routing_tables(cnt, n_ffn_blocks)
    xs, lrow = _dispatch(x, idx, rank, loff, seg_table, tail_start, n_rows + EXPERT_ROWS)
    ys = _expert_ffn(xs, blk_expert, n_used, w1, b1[:, :, None, :], w2, b2[:, :, None, :], layer, n_ffn_blocks)
    return _combine_ln(lrow, gate, x, ln_g[None, :], ln_b[None, :], ys, seg_table)


def kernel(x, hy_cf_w_in, hy_cf_b_in, hy_short_w, hy_short_b, hy_f1_w, hy_f1_b, hy_f1_freq, hy_f2_w, hy_f2_b, hy_f2_freq, hy_f3_w, hy_skip, cf_dw_w, cf_dw_b, cf_ln_g, cf_ln_b, even_w_out, even_b_out, attn_w_qkv, attn_lq1, attn_lk1, attn_lq2, attn_lk2, attn_subln_g, attn_w_out, ln1_g, ln1_b, ln2_g, ln2_b, moe_w_r, moe_b_r, moe_w1, moe_b1, moe_w2, moe_b2):
    B, S, D = x.shape
    assert D == D_MODEL and S % LANES == 0
    assert (B * S) % ROW_TILE == 0 and (B * S) % min(B * S, DENSE_ROW_TILE) == 0
    depth = ln1_g.shape[0]
    xf = x.reshape(B * S, D)
    tabs = _dft_tables(S // CONV_BLOCKS)
    for i in range(depth):
        j = i // 2
        if i % 2 == 0:
            xf, routing = _even_mixer(xf, (B, S), tabs, hy_cf_w_in[j], hy_cf_b_in[j], hy_short_w[j], hy_short_b[j],
                                      hy_f1_w[j], hy_f1_b[j], hy_f1_freq[j], hy_f2_w[j], hy_f2_b[j], hy_f2_freq[j],
                                      hy_f3_w[j], hy_skip[j], cf_dw_w[j], cf_dw_b[j], cf_ln_g[j], cf_ln_b[j],
                                      even_w_out[j], even_b_out[j], ln1_g[i], ln1_b[i], moe_w_r[i], moe_b_r[i])
        else:
            xf, routing = _odd_mixer(xf, (B, S), i, attn_w_qkv[j], attn_lq1[j], attn_lk1[j], attn_lq2[j],
                                     attn_lk2[j], attn_subln_g[j], attn_w_out[j], ln1_g[i], ln1_b[i],
                                     moe_w_r[i], moe_b_r[i])
        xf = _moe_layer(xf, routing, i, moe_w1, moe_b1, moe_w2, moe_b2, ln2_g[i], ln2_b[i])
    return xf.reshape(B, S, D)
```

```python
import functools
import math

import jax
import jax.numpy as jnp
import numpy as np
from jax import lax
from jax.experimental import pallas as pl
from jax.experimental.pallas import tpu as pltpu

F32 = jnp.float32
BF16 = jnp.bfloat16
U32 = jnp.uint32
I32 = jnp.int32

D_MODEL = 1024
HALF = D_MODEL // 2
DEPTH = 4
HYENA_CH = D_MODEL // 2
CONF_CH = D_MODEL // 2
HYENA_ORDER = 2
HYENA_EMB_DIM = 33
HYENA_FILTER_DIM = 64
HYENA_SHORT_DECAY_PCT = 0.3
HYENA_LONG_DECAY_PCT = 1.5
HYENA_DECAY_TARGET = 1e-2
CONF_WIDTH = 31
EVEN_IN = 3 * HYENA_CH + 2 * CONF_CH
N_HEADS = 8
HEAD_DIM = 64
ATTN_W = N_HEADS * 2 * HEAD_DIM
N_EXPERTS = 32
TOP_K = 4
D_FF = D_MODEL
SWIGLU_LIMIT = 7.0
SWIGLU_ALPHA = 1.702
DEEPNORM_ALPHA = (2 * DEPTH) ** 0.25
LN_EPS = 1e-5

LANES = 128
VMEM_LIMIT_BYTES = 56 * 1024 * 1024
ROW_TILE = 512
DENSE_ROW_TILE = 1024
EXPERT_ROWS = 512
SEG_ALIGN = 8
COPY_ROWS = (64, 32, 16, 8)
CONV_BLOCKS = 4
CONV_CH_TILE = 256
ATTN_Q_TILE = 2048
ATTN_ROW_CHUNK = 256

_NT = (((1,), (1,)), ((), ()))


def _params(*sem):
    return pltpu.CompilerParams(dimension_semantics=sem, vmem_limit_bytes=VMEM_LIMIT_BYTES)


def _split_bf16(a):
    hi = a.astype(BF16)
    lo = (a - hi.astype(F32)).astype(BF16)
    return hi, lo


def _dot3(a, b):
    a_hi, a_lo = _split_bf16(a)
    b_hi, b_lo = _split_bf16(b)
    d = functools.partial(jnp.dot, preferred_element_type=F32)
    return d(a_hi, b_hi) + d(a_hi, b_lo) + d(a_lo, b_hi)


def _layer_norm_rows(y, g, b):
    mu = jnp.mean(y, axis=-1, keepdims=True)
    yc = y - mu
    var = jnp.mean(yc * yc, axis=-1, keepdims=True)
    return yc * lax.rsqrt(var + LN_EPS) * g + b


def _pack_halves(y):
    lo = lax.bitcast_convert_type(y[:, :HALF].astype(BF16).astype(F32), U32)
    hi = lax.bitcast_convert_type(y[:, HALF:].astype(BF16).astype(F32), U32)
    return hi | (lo >> 16)


def _unpack_halves(p):
    lo = lax.bitcast_convert_type(p << 16, F32)
    hi = lax.bitcast_convert_type(p & jnp.uint32(0xFFFF0000), F32)
    return lo, hi


def _proj_body(x_ref, w_ref, b_ref, o_ref, *, col_chunk):
    x = x_ref[...].astype(BF16)
    for j in range(0, o_ref.shape[1], col_chunk):
        acc = jnp.dot(x, w_ref[:, j:j + col_chunk], preferred_element_type=F32)
        o_ref[:, j:j + col_chunk] = (acc + b_ref[:, j:j + col_chunk]).astype(o_ref.dtype)


def _project(x, w, b):
    T, K = x.shape
    N = w.shape[1]
    tm = min(T, DENSE_ROW_TILE)
    return pl.pallas_call(
        functools.partial(_proj_body, col_chunk=512),
        grid=(T // tm,),
        in_specs=[pl.BlockSpec((tm, K), lambda i: (i, 0)),
                  pl.BlockSpec((K, N), lambda i: (0, 0)),
                  pl.BlockSpec((1, N), lambda i: (0, 0))],
        out_specs=pl.BlockSpec((tm, N), lambda i: (i, 0)),
        out_shape=jax.ShapeDtypeStruct((T, N), BF16),
        compiler_params=_params("parallel"),
        name="project",
    )(x, w, b)


def _outproj_ln_body(a1_ref, a2_ref, w1_ref, w2_ref, b_ref, x_ref, g_ref, beta_ref, wr_ref, br_ref,
                     xo_ref, idx_ref, gate_ref, rank_ref, cnt_ref):
    m = (jnp.dot(a1_ref[...], w1_ref[...], preferred_element_type=F32)
         + jnp.dot(a2_ref[...], w2_ref[...], preferred_element_type=F32) + b_ref[...])
    y = _layer_norm_rows(DEEPNORM_ALPHA * x_ref[...] + m, g_ref[...], beta_ref[...])
    xo_ref[...] = y
    w_hi, w_lo = _split_bf16(wr_ref[...])
    for s in range(y.shape[0] // ROW_TILE):
        cols = slice(s * ROW_TILE, (s + 1) * ROW_TILE)
        _route_block(y[cols, :], w_hi, w_lo, br_ref[...], idx_ref, gate_ref, rank_ref, cnt_ref.at[s], cols)


def _outproj_ln(a1, a1_col, a2, a2_col, w, b, x, g, beta, w_rt, b_r):
    T = x.shape[0]
    E = N_EXPERTS
    tm = min(T, DENSE_ROW_TILE)
    nsub = tm // ROW_TILE
    tok = pl.BlockSpec((TOP_K, tm), lambda i: (0, i))
    out = pl.pallas_call(
        _outproj_ln_body,
        grid=(T // tm,),
        in_specs=[pl.BlockSpec((tm, HALF), lambda i: (i, a1_col)),
                  pl.BlockSpec((tm, HALF), lambda i: (i, a2_col)),
                  pl.BlockSpec((HALF, D_MODEL), lambda i: (0, 0)),
                  pl.BlockSpec((HALF, D_MODEL), lambda i: (1, 0)),
                  pl.BlockSpec((1, D_MODEL), lambda i: (0, 0)),
                  pl.BlockSpec((tm, D_MODEL), lambda i: (i, 0)),
                  pl.BlockSpec((1, D_MODEL), lambda i: (0, 0)),
                  pl.BlockSpec((1, D_MODEL), lambda i: (0, 0)),
                  pl.BlockSpec((E, D_MODEL), lambda i: (0, 0)),
                  pl.BlockSpec((E, 1), lambda i: (0, 0))],
        out_specs=[pl.BlockSpec((tm, D_MODEL), lambda i: (i, 0)), tok, tok, tok,
                   pl.BlockSpec((nsub, E, LANES), lambda i: (i, 0, 0))],
        out_shape=[jax.ShapeDtypeStruct((T, D_MODEL), F32),
                   jax.ShapeDtypeStruct((TOP_K, T), I32), jax.ShapeDtypeStruct((TOP_K, T), F32),
                   jax.ShapeDtypeStruct((TOP_K, T), I32), jax.ShapeDtypeStruct((T // ROW_TILE, E, LANES), F32)],
        compiler_params=_params("parallel"),
        name="outproj_ln_route",
    )(a1, a2, w, w, b, x, g, beta, w_rt, b_r)
    return out[0], tuple(out[1:])


def _short_conv_body(x_ref, w_ref, b_ref, o_ref):
    x = x_ref[...].astype(F32)
    S = x.shape[0]
    row = lax.broadcasted_iota(I32, x.shape, 0)
    prev = jnp.where(row == 0, 0.0, pltpu.roll(x, 1, 0))
    nxt = jnp.where(row == S - 1, 0.0, pltpu.roll(x, S - 1, 0))
    y = w_ref[0:1, :] * prev + w_ref[1:2, :] * x + w_ref[2:3, :] * nxt + b_ref[...]
    o_ref[...] = y.astype(o_ref.dtype)


def _short_conv(proj, w, b, B, S):
    T = B * S
    C = HYENA_CH
    return pl.pallas_call(
        _short_conv_body,
        grid=(B, 3),
        in_specs=[pl.BlockSpec((S, C), lambda bi, j: (bi, j)),
                  pl.BlockSpec((3, C), lambda bi, j: (0, j)),
                  pl.BlockSpec((1, C), lambda bi, j: (0, j))],
        out_specs=pl.BlockSpec((S, C), lambda bi, j: (bi, j)),
        out_shape=jax.ShapeDtypeStruct((T, 3 * C), BF16),
        compiler_params=_params("parallel", "parallel"),
        name="hyena_short_conv",
    )(proj, w, b)


def _conformer_body(a_ref, g_ref, fc_ref, fs_ref, hre_ref, him_ref, gc_ref, gs_ref, b_ref, lg_ref, lb_ref, o_ref,
                    *, n_blk):
    P = fc_ref.shape[0]
    fc, fs, gc, gs = fc_ref[...], fs_ref[...], gc_ref[...], gs_ref[...]
    vre, vim = [], []
    for j in range(n_blk):
        rows = slice(j * P, (j + 1) * P)
        u = (a_ref[rows, :].astype(F32) * jax.nn.sigmoid(g_ref[rows, :].astype(F32))).astype(BF16)
        vre.append(jnp.dot(fc, u, preferred_element_type=F32))
        vim.append(jnp.dot(fs, u, preferred_element_type=F32))
    for i in range(n_blk):
        yre, yim = _mix_block_lags(vre, vim, hre_ref, him_ref, i, 1)
        y = jnp.dot(gc, yre, preferred_element_type=F32) + jnp.dot(gs, yim, preferred_element_type=F32) + b_ref[...]
        y = _layer_norm_rows(y, lg_ref[...], lb_ref[...])
        o_ref[i * P:(i + 1) * P, :] = (y * jax.nn.sigmoid(y)).astype(o_ref.dtype)


def _conformer(proj, tabs, w, b, lg, lb, B, S):
    fc32, fs32, (fc, fs, gc, gs) = tabs
    P = fc.shape[0]
    T = B * S
    C = CONF_CH
    half = CONF_WIDTH // 2
    assert half < P
    span = ((S - half, S - half - 1), (0, 0))
    hre, him = _lag_spectra(fc32, fs32, jnp.stack([jnp.pad(w[::-1], span), jnp.pad(w, span)]), 1)
    tab = pl.BlockSpec((P, P), lambda bi: (0, 0))
    spec = pl.BlockSpec((3 * P, C), lambda bi: (0, 0))
    vec = pl.BlockSpec((1, C), lambda bi: (0, 0))
    return pl.pallas_call(
        functools.partial(_conformer_body, n_blk=S // P),
        grid=(B,),
        in_specs=[pl.BlockSpec((S, C), lambda bi: (bi, 3)),
                  pl.BlockSpec((S, C), lambda bi: (bi, 4)),
                  tab, tab, spec, spec, tab, tab, vec, vec, vec],
        out_specs=pl.BlockSpec((S, C), lambda bi: (bi, 0)),
        out_shape=jax.ShapeDtypeStruct((T, C), BF16),
        compiler_params=_params("parallel"),
        name="conformer_conv",
    )(proj, proj, fc, fs, hre, him, gc, gs, b, lg, lb)


def _filter_body(feat_ref, w1_ref, b1_ref, q1_ref, w2_ref, b2_ref, q2_ref, w3_ref, t_ref, delta_ref, o_ref):
    h = jnp.sin(q1_ref[...] * (_dot3(feat_ref[...], w1_ref[...]) + b1_ref[...]))
    h = jnp.sin(q2_ref[...] * (_dot3(h, w2_ref[...]) + b2_ref[...]))
    h = _dot3(h, w3_ref[...])
    o_ref[...] = h * jnp.exp(-t_ref[...] * delta_ref[...])


def _hyena_filters(feats2, w1, b1, q1, w2, b2, q2, w3, tcol2, deltas):
    S = feats2.shape[1]
    C = HYENA_CH
    fd = HYENA_FILTER_DIM
    fixed = lambda p, d, o: (0, 0)
    return pl.pallas_call(
        _filter_body,
        grid=(2, 2, HYENA_ORDER),
        in_specs=[pl.BlockSpec((None, S, LANES), lambda p, d, o: (p, 0, 0)),
                  pl.BlockSpec((LANES, fd), fixed),
                  pl.BlockSpec((1, fd), fixed),
                  pl.BlockSpec((1, fd), fixed),
                  pl.BlockSpec((fd, fd), fixed),
                  pl.BlockSpec((1, fd), fixed),
                  pl.BlockSpec((1, fd), fixed),
                  pl.BlockSpec((fd, C), lambda p, d, o: (0, d * HYENA_ORDER + o)),
                  pl.BlockSpec((None, S, 1), lambda p, d, o: (p, 0, 0)),
                  pl.BlockSpec((1, C), fixed)],
        out_specs=pl.BlockSpec((None, S, C), lambda p, d, o: ((d + 1 - p) % 2, p, o)),
        out_shape=jax.ShapeDtypeStruct((2, 2 * S, HYENA_ORDER * C), F32),
        compiler_params=_params("parallel", "parallel", "parallel"),
        name="hyena_filter_mlp",
    )(feats2, w1, b1, q1, w2, b2, q2, w3, tcol2, deltas)


def _spectrum_body(fc_ref, fs_ref, a_ref, b_ref, hre_ref, him_ref):
    a = a_ref[...]
    row = lax.broadcasted_iota(I32, a.shape, 0)
    b = jnp.where(row == 0, 0.0, b_ref[...])
    hre_ref[...] = _dot3(fc_ref[...], a + b)
    him_ref[...] = _dot3(fs_ref[...], a - b)


def _lag_spectra(fc32, fs32, h2, d_max):
    P = fc32.shape[0]
    n = h2.shape[1] // (2 * P)
    n_ch = h2.shape[2]
    C = HYENA_CH
    nd = 2 * d_max + 1
    spec = pl.BlockSpec((P, C), lambda di, o: (di, o))
    return pl.pallas_call(
        _spectrum_body,
        grid=(nd, n_ch // C),
        in_specs=[pl.BlockSpec((P, P), lambda di, o: (0, 0)),
                  pl.BlockSpec((P, P), lambda di, o: (0, 0)),
                  pl.BlockSpec((None, P, C), lambda di, o: (0, n - d_max + di, o)),
                  pl.BlockSpec((None, P, C), lambda di, o: (1, n + d_max - di, o))],
        out_specs=[spec, spec],
        out_shape=[jax.ShapeDtypeStruct((nd * P, n_ch), F32)] * 2,
        compiler_params=_params("parallel", "parallel"),
        name="block_lag_spectra",
    )(fc32, fs32, h2, h2)


def _mix_block_lags(vre, vim, hre_ref, him_ref, i, d_max):
    P = vre[0].shape[0]
    yre = yim = None
    for j in range(max(0, i - d_max), min(len(vre), i + d_max + 1)):
        r0 = (i - j + d_max) * P
        hre = hre_ref[r0:r0 + P, :]
        him = him_ref[r0:r0 + P, :]
        tre = vre[j] * hre - vim[j] * him
        tim = vre[j] * him + vim[j] * hre
        yre = tre if yre is None else yre + tre
        yim = tim if yim is None else yim + tim
    return yre.astype(BF16), yim.astype(BF16)


def _long_conv_body(v_ref, gate_ref, fc_ref, fs_ref, hre_ref, him_ref, gc_ref, gs_ref, skip_ref, o_ref, *, n_blk):
    P = fc_ref.shape[0]
    fc, fs, gc, gs = fc_ref[...], fs_ref[...], gc_ref[...], gs_ref[...]
    vre, vim = [], []
    for j in range(n_blk):
        vj = v_ref[j * P:(j + 1) * P, :]
        vre.append(jnp.dot(fc, vj, preferred_element_type=F32))
        vim.append(jnp.dot(fs, vj, preferred_element_type=F32))
    for i in range(n_blk):
        yre, yim = _mix_block_lags(vre, vim, hre_ref, him_ref, i, n_blk - 1)
        y = jnp.dot(gc, yre, preferred_element_type=F32) + jnp.dot(gs, yim, preferred_element_type=F32)
        rows = slice(i * P, (i + 1) * P)
        y = y + v_ref[rows, :].astype(F32) * skip_ref[...]
        o_ref[rows, :] = (gate_ref[rows, :].astype(F32) * y).astype(o_ref.dtype)


def _long_conv(v_arr, v_col, gate_arr, gate_col, tabs, hre, him, order, skip, B, S):
    fc, fs, gc, gs = tabs
    P = fc.shape[0]
    T = B * S
    C = HYENA_CH
    nc = C // CONV_CH_TILE
    cc = CONV_CH_TILE
    n_h = hre.shape[0]
    tab = pl.BlockSpec((P, P), lambda bi, c: (0, 0))
    return pl.pallas_call(
        functools.partial(_long_conv_body, n_blk=S // P),
        grid=(B, nc),
        in_specs=[pl.BlockSpec((S, cc), lambda bi, c: (bi, v_col * nc + c)),
                  pl.BlockSpec((S, cc), lambda bi, c: (bi, gate_col * nc + c)),
                  tab, tab,
                  pl.BlockSpec((n_h, cc), lambda bi, c: (0, order * nc + c)),
                  pl.BlockSpec((n_h, cc), lambda bi, c: (0, order * nc + c)),
                  tab, tab,
                  pl.BlockSpec((1, cc), lambda bi, c: (0, c))],
        out_specs=pl.BlockSpec((S, cc), lambda bi, c: (bi, c)),
        out_shape=jax.ShapeDtypeStruct((T, C), BF16),
        compiler_params=_params("parallel", "parallel"),
        name="hyena_long_conv",
    )(v_arr, gate_arr, fc, fs, hre, him, gc, gs, skip[order][None, :])


def _attn_body(slope_ref, lam_ref, q_ref, k_ref, v_ref, g_ref, o_ref, vaug_ref, *, lam_init, row_chunk):
    h = pl.program_id(1)
    qi = pl.program_id(2)
    tq = q_ref.shape[0]
    S = k_ref.shape[0]
    hw = 2 * HEAD_DIM
    k = k_ref[...]
    slope = slope_ref[h]
    kpos = lax.broadcasted_iota(I32, (1, S), 1).astype(F32) * slope

    @pl.when(qi == 0)
    def _():
        vaug_ref[:, :hw] = v_ref[...]
        vaug_ref[:, hw:] = jnp.where(lax.broadcasted_iota(I32, (S, hw), 1) == 0, 1.0, 0.0).astype(BF16)

    v_aug = vaug_ref[...]

    for r0 in range(0, tq, row_chunk):
        q = q_ref[r0:r0 + row_chunk, :]
        lane = lax.broadcasted_iota(I32, q.shape, 1)
        zero = jnp.zeros_like(q)
        qpos = (qi * tq + r0 + lax.broadcasted_iota(I32, (row_chunk, 1), 0)).astype(F32) * slope
        bias = lax.bitcast_convert_type(lax.bitcast_convert_type(qpos - kpos, U32) | jnp.uint32(0x80000000), F32)

        def weighted_values(qh):
            s = lax.dot_general(qh, k, _NT, preferred_element_type=F32) + bias
            e = jnp.exp((s - jnp.max(s, axis=-1, keepdims=True)).astype(BF16))
            return jnp.dot(e, v_aug, preferred_element_type=F32)

        o1 = weighted_values(jnp.where(lane < HEAD_DIM, q, zero))
        o2 = weighted_values(jnp.where(lane >= HEAD_DIM, q, zero))
        o = o1[:, :hw] * (1.0 / o1[:, hw:hw + 1]) - o2[:, :hw] * (lam_ref[0] / o2[:, hw:hw + 1])
        o = o * lax.rsqrt(jnp.mean(o * o, axis=-1, keepdims=True) + LN_EPS) * g_ref[...]
        o_ref[r0:r0 + row_chunk, :] = (o * (1.0 - lam_init)).astype(o_ref.dtype)


def _diff_attention(qkv, slopes, lam, subln_g, lam_init, B, S):
    T = B * S
    hw = 2 * HEAD_DIM
    tq = min(S, ATTN_Q_TILE)
    nq = S // tq
    smem = pl.BlockSpec(memory_space=pltpu.SMEM)
    return pl.pallas_call(
        functools.partial(_attn_body, lam_init=lam_init, row_chunk=min(tq, ATTN_ROW_CHUNK)),
        grid=(B, N_HEADS, nq),
        in_specs=[smem, smem,
                  pl.BlockSpec((tq, hw), lambda bi, h, qi: (bi * nq + qi, h)),
                  pl.BlockSpec((S, hw), lambda bi, h, qi: (bi, N_HEADS + h)),
                  pl.BlockSpec((S, hw), lambda bi, h, qi: (bi, 2 * N_HEADS + h)),
                  pl.BlockSpec((1, hw), lambda bi, h, qi: (0, 0))],
        out_specs=pl.BlockSpec((tq, hw), lambda bi, h, qi: (bi * nq + qi, h)),
        out_shape=jax.ShapeDtypeStruct((T, ATTN_W), BF16),
        scratch_shapes=[pltpu.VMEM((S, 2 * hw), BF16)],
        compiler_params=_params("parallel", "parallel", "arbitrary"),
        name="diff_attention",
    )(slopes, lam, qkv, qkv, qkv, subln_g)


def _route_block(x, w_hi, w_lo, bias, idx_ref, gate_ref, rank_ref, cnt_ref, cols):
    E = N_EXPERTS
    tm = x.shape[0]
    x_hi, x_lo = _split_bf16(x)
    nt = functools.partial(lax.dot_general, dimension_numbers=_NT, preferred_element_type=F32)
    logits = nt(w_hi, x_hi) + nt(w_lo, x_hi) + nt(w_hi, x_lo) + bias

    eid = lax.broadcasted_iota(I32, (E, tm), 0).astype(F32)
    work = logits
    vals, idxs = [], []
    for _ in range(TOP_K):
        m = jnp.max(work, axis=0, keepdims=True)
        sel = jnp.min(jnp.where(work == m, eid, float(E)), axis=0, keepdims=True)
        vals.append(m)
        idxs.append(sel)
        work = jnp.where(eid == sel, -jnp.inf, work)
    exps = [jnp.exp(v - vals[0]) for v in vals]
    denom = exps[0] + exps[1] + exps[2] + exps[3]

    chosen = jnp.zeros((E, tm), F32)
    for sel in idxs:
        chosen = chosen + jnp.where(eid == sel, 1.0, 0.0)
    earlier = jnp.where(lax.broadcasted_iota(I32, (tm, tm), 0) < lax.broadcasted_iota(I32, (tm, tm), 1), 1.0, 0.0)
    before = jnp.dot(chosen.astype(BF16), earlier.astype(BF16), preferred_element_type=F32)
    for k in range(TOP_K):
        gate_ref[k:k + 1, cols] = exps[k] / denom
        idx_ref[k:k + 1, cols] = idxs[k].astype(I32)
        rank_ref[k:k + 1, cols] = jnp.sum(jnp.where(eid == idxs[k], before, 0.0), axis=0, keepdims=True).astype(I32)
    cnt_ref[...] = jnp.broadcast_to(jnp.sum(chosen, axis=1, keepdims=True), cnt_ref.shape)


def _copy_caps():
    local_rows = TOP_K * ROW_TILE + N_EXPERTS * SEG_ALIGN
    return (local_rows // COPY_ROWS[0],) + (N_EXPERTS,) * (len(COPY_ROWS) - 1)


def _segment_copies(tab_ref, make_copy, slot, wait):
    base = len(COPY_ROWS)
    for ci, (rows, cap) in enumerate(zip(COPY_ROWS, _copy_caps())):
        def body(p, carry, base=base, rows=rows, cap=cap):
            copy = make_copy(slot, tab_ref[0, base + p], tab_ref[0, base + cap + p], rows)
            if wait:
                copy.wait()
            else:
                copy.start()
            return carry
        lax.fori_loop(0, tab_ref[0, ci], body, 0)
        base += 2 * cap


def _dispatch_body(tail_ref, seg_ref, seg_prev_ref, x_ref, idx_ref, rank_ref, loff_ref, xs_hbm, lrow_ref,
                   buf_ref, zero_ref, sem, zsem, *, n_tok_blocks):
    b = pl.program_id(0)
    slot = b % 2
    tm = x_ref.shape[0]
    R = buf_ref.shape[1]

    def seg_copy(s, local_row, dst_row, rows):
        return pltpu.make_async_copy(buf_ref.at[s, pl.ds(pl.multiple_of(local_row, SEG_ALIGN), rows)],
                                     xs_hbm.at[pl.ds(pl.multiple_of(dst_row, SEG_ALIGN), rows)], sem.at[s])

    @pl.when(b == 0)
    def _():
        zero_ref[...] = jnp.zeros_like(zero_ref)
        for e in range(N_EXPERTS):
            fill = pltpu.make_async_copy(
                zero_ref, xs_hbm.at[pl.ds(pl.multiple_of(tail_ref[e], SEG_ALIGN), EXPERT_ROWS)], zsem)
            fill.start()
            fill.wait()

    eid = lax.broadcasted_iota(I32, (N_EXPERTS, tm), 0)
    loff = loff_ref[...].astype(F32)
    rid = lax.broadcasted_iota(I32, (R, tm), 0).astype(jnp.int16)
    sel_t = jnp.zeros((R, tm), BF16)
    for k in range(TOP_K):
        base = jnp.sum(jnp.where(eid == idx_ref[k:k + 1, :], loff, 0.0), axis=0, keepdims=True).astype(I32)
        row = base + rank_ref[k:k + 1, :]
        lrow_ref[k:k + 1, :] = row
        sel_t = jnp.where(rid == row.astype(jnp.int16), jnp.ones((), BF16), sel_t)
    xb = x_ref[...].astype(BF16)
    lo = jnp.dot(sel_t, xb[:, :HALF], preferred_element_type=F32)
    hi = jnp.dot(sel_t, xb[:, HALF:], preferred_element_type=F32)
    packed = (lax.bitcast_convert_type(hi, U32) & jnp.uint32(0xFFFF0000)) | (lax.bitcast_convert_type(lo, U32) >> 16)
    buf_ref[slot] = packed

    @pl.when(b >= 1)
    def _():
        _segment_copies(seg_prev_ref, seg_copy, 1 - slot, wait=True)
    _segment_copies(seg_ref, seg_copy, slot, wait=False)

    @pl.when(b == n_tok_blocks - 1)
    def _():
        _segment_copies(seg_ref, seg_copy, slot, wait=True)


def _dispatch(x, idx, rank, loff, seg_table, tail_start, n_rows):
    T = x.shape[0]
    tm = ROW_TILE
    nb = T // tm
    R = TOP_K * tm + N_EXPERTS * SEG_ALIGN
    tok = pl.BlockSpec((TOP_K, tm), lambda b, tl: (0, b))
    seg_w = seg_table.shape[-1]
    return pl.pallas_call(
        functools.partial(_dispatch_body, n_tok_blocks=nb),
        grid_spec=pltpu.PrefetchScalarGridSpec(
            num_scalar_prefetch=1,
            grid=(nb,),
            in_specs=[pl.BlockSpec((None, 1, seg_w), lambda b, tl: (b, 0, 0), memory_space=pltpu.SMEM),
                      pl.BlockSpec((None, 1, seg_w), lambda b, tl: (jnp.maximum(b - 1, 0), 0, 0),
                                   memory_space=pltpu.SMEM),
                      pl.BlockSpec((tm, D_MODEL), lambda b, tl: (b, 0)),
                      tok, tok,
                      pl.BlockSpec((None, N_EXPERTS, 1), lambda b, tl: (b, 0, 0))],
            out_specs=[pl.BlockSpec(memory_space=pl.ANY), tok],
            scratch_shapes=[pltpu.VMEM((2, R, HALF), U32), pltpu.VMEM((EXPERT_ROWS, HALF), U32),
                            pltpu.SemaphoreType.DMA((2,)), pltpu.SemaphoreType.DMA(())]),
        out_shape=[jax.ShapeDtypeStruct((n_rows, HALF), U32), jax.ShapeDtypeStruct((TOP_K, T), I32)],
        compiler_params=_params("arbitrary"),
        name="moe_dispatch",
    )(tail_start, seg_table, seg_table, x, idx, rank, loff)


def _ffn_body(be_ref, nu_ref, xs_ref, w1_ref, b1_ref, w2_ref, b2_ref, ys_ref, w1b_ref, w2b_ref):
    i = pl.program_id(0)
    used = i < nu_ref[0]
    fresh = jnp.logical_or(i == 0, be_ref[i] != be_ref[jnp.maximum(i - 1, 0)])

    @pl.when(jnp.logical_and(used, fresh))
    def _():
        w1b_ref[...] = w1_ref[...].astype(BF16)
        w2b_ref[...] = w2_ref[...].astype(BF16)

    @pl.when(used)
    def _():
        lo, hi = _unpack_halves(xs_ref[...])
        h = (jnp.dot(lo.astype(BF16), w1b_ref[0:HALF, :], preferred_element_type=F32)
             + jnp.dot(hi.astype(BF16), w1b_ref[HALF:D_MODEL, :], preferred_element_type=F32) + b1_ref[...])
        hg = jnp.minimum(h[:, :D_FF], SWIGLU_LIMIT)
        hu = jnp.clip(h[:, D_FF:], -SWIGLU_LIMIT, SWIGLU_LIMIT)
        act = (hu + 1.0) * (hg * jax.nn.sigmoid(hg * SWIGLU_ALPHA))
        y = jnp.dot(act.astype(BF16), w2b_ref[...], preferred_element_type=F32) + b2_ref[...]
        ys_ref[...] = _pack_halves(y)


def _expert_ffn(xs, blk_expert, n_used, w1, b1, w2, b2, layer, n_blocks):
    rows = pl.BlockSpec((EXPERT_ROWS, HALF), lambda i, be, nu: (jnp.minimum(i, nu[0] - 1), 0))
    return pl.pallas_call(
        _ffn_body,
        grid_spec=pltpu.PrefetchScalarGridSpec(
            num_scalar_prefetch=2,
            grid=(n_blocks,),
            in_specs=[rows,
                      pl.BlockSpec((None, None, D_MODEL, 2 * D_FF), lambda i, be, nu: (layer, be[i], 0, 0)),
                      pl.BlockSpec((None, None, 1, 2 * D_FF), lambda i, be, nu: (layer, be[i], 0, 0)),
                      pl.BlockSpec((None, None, D_FF, D_MODEL), lambda i, be, nu: (layer, be[i], 0, 0)),
                      pl.BlockSpec((None, None, 1, D_MODEL), lambda i, be, nu: (layer, be[i], 0, 0))],
            out_specs=rows,
            scratch_shapes=[pltpu.VMEM((D_MODEL, 2 * D_FF), BF16), pltpu.VMEM((D_FF, D_MODEL), BF16)]),
        out_shape=jax.ShapeDtypeStruct((xs.shape[0], HALF), U32),
        compiler_params=_params("arbitrary"),
        name="moe_expert_ffn",
    )(blk_expert, n_used, xs, w1, b1, w2, b2)


def _combine_body(seg_ref, seg_next_ref, lrow_ref, gate_ref, x_ref, g_ref, beta_ref, ys_hbm, xo_ref,
                  buf_ref, sem, *, n_tok_blocks):
    b = pl.program_id(0)
    slot = b % 2
    tm = x_ref.shape[0]
    R = buf_ref.shape[1]

    def seg_copy(s, local_row, src_row, rows):
        return pltpu.make_async_copy(ys_hbm.at[pl.ds(pl.multiple_of(src_row, SEG_ALIGN), rows)],
                                     buf_ref.at[s, pl.ds(pl.multiple_of(local_row, SEG_ALIGN), rows)], sem.at[s])

    @pl.when(b == 0)
    def _():
        buf_ref[...] = jnp.zeros_like(buf_ref)
        _segment_copies(seg_ref, seg_copy, 0, wait=False)

    @pl.when(b + 1 < n_tok_blocks)
    def _():
        _segment_copies(seg_next_ref, seg_copy, 1 - slot, wait=False)

    _segment_copies(seg_ref, seg_copy, slot, wait=True)

    lo, hi = _unpack_halves(buf_ref[slot])
    lo = lo.astype(BF16)
    hi = hi.astype(BF16)
    tc = tm // 2
    cid = lax.broadcasted_iota(I32, (tc, R), 1).astype(jnp.int16)
    lrow_t = lrow_ref[...].astype(F32).T
    gate_t = gate_ref[...].T
    for t0 in range(0, tm, tc):
        lrow = lrow_t[t0:t0 + tc, :].astype(jnp.int16)
        gates = gate_t[t0:t0 + tc, :].astype(BF16)
        sel = jnp.zeros((tc, R), BF16)
        for k in range(TOP_K):
            sel = jnp.where(cid == lrow[:, k:k + 1], gates[:, k:k + 1], sel)
        f = jnp.concatenate([jnp.dot(sel, lo, preferred_element_type=F32),
                             jnp.dot(sel, hi, preferred_element_type=F32)], axis=1)
        xo_ref[t0:t0 + tc, :] = _layer_norm_rows(DEEPNORM_ALPHA * x_ref[t0:t0 + tc, :] + f, g_ref[...], beta_ref[...])


def _combine_ln(lrow, gate, x, g, beta, ys, seg_table):
    T = x.shape[0]
    tm = ROW_TILE
    nb = T // tm
    R = TOP_K * tm + N_EXPERTS * SEG_ALIGN
    seg_w = seg_table.shape[-1]
    return pl.pallas_call(
        functools.partial(_combine_body, n_tok_blocks=nb),
        grid=(nb,),
        in_specs=[pl.BlockSpec((None, 1, seg_w), lambda b: (b, 0, 0), memory_space=pltpu.SMEM),
                  pl.BlockSpec((None, 1, seg_w), lambda b: (jnp.minimum(b + 1, nb - 1), 0, 0),
                               memory_space=pltpu.SMEM),
                  pl.BlockSpec((TOP_K, tm), lambda b: (0, b)),
                  pl.BlockSpec((TOP_K, tm), lambda b: (0, b)),
                  pl.BlockSpec((tm, D_MODEL), lambda b: (b, 0)),
                  pl.BlockSpec((1, D_MODEL), lambda b: (0, 0)),
                  pl.BlockSpec((1, D_MODEL), lambda b: (0, 0)),
                  pl.BlockSpec(memory_space=pl.ANY)],
        out_specs=pl.BlockSpec((tm, D_MODEL), lambda b: (b, 0)),
        out_shape=jax.ShapeDtypeStruct((T, D_MODEL), F32),
        scratch_shapes=[pltpu.VMEM((2, R, HALF), U32), pltpu.SemaphoreType.DMA((2,))],
        compiler_params=_params("arbitrary"),
        name="moe_combine_ln",
    )(seg_table, seg_table, lrow, gate, x, g, beta, ys)


def _dft_tables(P):
    n2 = 4 * P
    k = jnp.arange(P, dtype=I32)
    m = ((2 * k[:, None] + 1) * k[None, :]) % n2
    ang = m.astype(F32) * F32(2.0 * math.pi / n2)
    fc32, fs32 = jnp.cos(ang), -jnp.sin(ang)
    scale = F32(1.0 / P)
    gc, gs = (fc32.T * scale).astype(BF16), (fs32.T * scale).astype(BF16)
    return fc32, fs32, (fc32.astype(BF16), fs32.astype(BF16), gc, gs)


def _hyena_positional(S):
    pos = jnp.arange(S, dtype=F32)
    t = jnp.linspace(0.0, 1.0, S, dtype=F32)[:, None]
    bands = (HYENA_EMB_DIM - 1) // 2
    f = jnp.linspace(1e-4, bands - 1, bands, dtype=F32)
    ang = (2.0 * math.pi / S) * pos[:, None] * f[None, :]
    feats = jnp.concatenate([t, jnp.cos(ang), -jnp.sin(ang)], axis=-1)
    feats = jnp.pad(feats, ((0, 0), (0, LANES - HYENA_EMB_DIM)))
    max_decay = math.log(HYENA_DECAY_TARGET) / HYENA_SHORT_DECAY_PCT
    min_decay = math.log(HYENA_DECAY_TARGET) / HYENA_LONG_DECAY_PCT
    deltas = jnp.abs(jnp.linspace(min_decay, max_decay, HYENA_CH, dtype=F32))[None, :]
    return feats, t, deltas


def _alibi_slopes():
    return jnp.asarray(np.array([2.0 ** (-8.0 * (i + 1) / N_HEADS) for i in range(N_HEADS)], dtype=np.float32))


def _even_mixer(x, xshape, tabs, w_in, b_in, short_w, short_b, f1_w, f1_b, f1_freq, f2_w, f2_b, f2_freq, f3_w,
                skip, dw_w, dw_b, cln_g, cln_b, w_out, b_out, ln_g, ln_b, w_r, b_r):
    B, S = xshape
    fc32, fs32, tabs16 = tabs
    proj = _project(x, w_in.astype(BF16), b_in[None, :])
    hy = _short_conv(proj, short_w, short_b[None, :], B, S)
    u = _conformer(proj, tabs, dw_w, dw_b[None, :], cln_g[None, :], cln_b[None, :], B, S)
    feats, tcol, deltas = _hyena_positional(S)
    f1_wp = jnp.pad(f1_w, ((0, LANES - HYENA_EMB_DIM), (0, 0)))
    flip = lambda a: jnp.concatenate([a[:1], a[:0:-1]], axis=0)
    h2 = _hyena_filters(jnp.stack([flip(feats), feats]), f1_wp, f1_b[None, :], f1_freq[None, :], f2_w, f2_b[None, :],
                        f2_freq[None, :], f3_w, jnp.stack([flip(tcol), tcol]), deltas)
    hre, him = _lag_spectra(fc32, fs32, h2, S // fc32.shape[0] - 1)
    z = _long_conv(hy, 2, hy, 0, tabs16, hre, him, 0, skip, B, S)
    z = _long_conv(z, 0, hy, 1, tabs16, hre, him, 1, skip, B, S)
    return _outproj_ln(z, 0, u, 0, w_out.astype(BF16), b_out[None, :], x, ln_g[None, :], ln_b[None, :],
                       w_r.T, b_r[:, None])


def _odd_mixer(x, xshape, layer_idx, w_qkv, lq1, lk1, lq2, lk2, subln_g, w_out, ln_g, ln_b, w_r, b_r):
    B, S = xshape
    lam_init = 0.8 - 0.6 * math.exp(-0.3 * layer_idx)
    lam = (jnp.exp(jnp.sum(lq1 * lk1)) - jnp.exp(jnp.sum(lq2 * lk2)) + lam_init).reshape(1)
    q_scale = jnp.concatenate([jnp.full((ATTN_W,), HEAD_DIM ** -0.5, F32), jnp.ones((2 * ATTN_W,), F32)])
    w = (w_qkv * q_scale).astype(BF16)
    qkv = _project(x, w, jnp.zeros((1, 3 * ATTN_W), F32))
    o = _diff_attention(qkv, _alibi_slopes(), lam, subln_g[None, :], lam_init, B, S)
    return _outproj_ln(o, 0, o, 1, w_out.astype(BF16), jnp.zeros((1, D_MODEL), F32), x, ln_g[None, :], ln_b[None, :],
                       w_r.T, b_r[:, None])


def _round_up(a, m):
    return (a + m - 1) // m * m


def _copy_lists(loff, goff, units):
    E = N_EXPERTS
    caps = _copy_caps()
    big = COPY_ROWS[0]
    n_big = units // (big // SEG_ALIGN)
    cum = jnp.cumsum(n_big, axis=1)
    first = (cum - n_big)[:, None, :]
    p = jnp.arange(caps[0], dtype=I32)[None, :, None]
    mine = (first <= p) & (p < cum[:, None, :])
    within = (p - first) * big
    counts = [cum[:, -1]]
    cols = [jnp.sum(jnp.where(mine, loff[:, None, :] + within, 0), axis=2),
            jnp.sum(jnp.where(mine, goff[:, None, :] + within, 0), axis=2)]
    off = n_big * big
    p = jnp.arange(E, dtype=I32)[None, :, None]
    for rows in COPY_ROWS[1:]:
        has = (units & (rows // SEG_ALIGN)) != 0
        pos = jnp.cumsum(has.astype(I32), axis=1) - has.astype(I32)
        mine = has[:, None, :] & (pos[:, None, :] == p)
        counts.append(jnp.sum(has.astype(I32), axis=1))
        cols += [jnp.sum(jnp.where(mine, (loff + off)[:, None, :], 0), axis=2),
                 jnp.sum(jnp.where(mine, (goff + off)[:, None, :], 0), axis=2)]
        off = off + jnp.where(has, rows, 0)
    return jnp.concatenate([jnp.stack(counts, axis=1)] + cols, axis=1).astype(I32)


def _routing_tables(cnt_blocks, n_ffn_blocks):
    E = N_EXPERTS
    cnt8 = _round_up(cnt_blocks[:, :, 0].astype(I32), SEG_ALIGN)
    seg_end = jnp.cumsum(cnt8, axis=1)
    loff = seg_end - cnt8
    tot8 = jnp.sum(cnt8, axis=0)
    group = _round_up(tot8, EXPERT_ROWS)
    group_end = jnp.cumsum(group)
    group_start = group_end - group
    goff = group_start[None, :] + jnp.cumsum(cnt8, axis=0) - cnt8
    seg_table = _copy_lists(loff, goff, cnt8 // SEG_ALIGN)
    starts = jnp.arange(n_ffn_blocks, dtype=I32) * EXPERT_ROWS
    blk_expert = jnp.minimum(jnp.sum((group_end[None, :] <= starts[:, None]).astype(I32), axis=1), E - 1)
    n_used = group_end[-1:] // EXPERT_ROWS
    tail_start = group_start + tot8
    return loff[:, :, None], seg_table[:, None, :], blk_expert, n_used, tail_start


def _moe_layer(x, routing, layer, w1, b1, w2, b2, ln_g, ln_b):
    T = x.shape[0]
    nb = T // ROW_TILE
    n_rows = _round_up(T * TOP_K + nb * N_EXPERTS * (SEG_ALIGN - 1), EXPERT_ROWS) + N_EXPERTS * EXPERT_ROWS
    n_ffn_blocks = n_rows // EXPERT_ROWS
    idx, gate, rank, cnt = routing
    loff, seg_table, blk_expert, n_used, tail_start = _routing_tables(cnt, n_ffn_blocks)
    xs, lrow = _dispatch(x, idx, rank, loff, seg_table, tail_start, n_rows + EXPERT_ROWS)
    ys = _expert_ffn(xs, blk_expert, n_used, w1, b1[:, :, None, :], w2, b2[:, :, None, :], layer, n_ffn_blocks)
    return _combine_ln(lrow, gate, x, ln_g[None, :], ln_b[None, :], ys, seg_table)


def kernel(x, hy_cf_w_in, hy_cf_b_in, hy_short_w, hy_short_b, hy_f1_w, hy_f1_b, hy_f1_freq, hy_f2_w, hy_f2_b, hy_f2_freq, hy_f3_w, hy_skip, cf_dw_w, cf_dw_b, cf_ln_g, cf_ln_b, even_w_out, even_b_out, attn_w_qkv, attn_lq1, attn_lk1, attn_lq2, attn_lk2, attn_subln_g, attn_w_out, ln1_g, ln1_b, ln2_g, ln2_b, moe_w_r, moe_b_r, moe_w1, moe_b1, moe_w2, moe_b2):
    B, S, D = x.shape
    assert D == D_MODEL and S % LANES == 0
    assert (B * S) % ROW_TILE == 0 and (B * S) % min(B * S, DENSE_ROW_TILE) == 0
    depth = ln1_g.shape[0]
    xf = x.reshape(B * S, D)
    tabs = _dft_tables(S // CONV_BLOCKS)
    for i in range(depth):
        j = i // 2
        if i % 2 == 0:
            xf, routing = _even_mixer(xf, (B, S), tabs, hy_cf_w_in[j], hy_cf_b_in[j], hy_short_w[j], hy_short_b[j],
                                      hy_f1_w[j], hy_f1_b[j], hy_f1_freq[j], hy_f2_w[j], hy_f2_b[j], hy_f2_freq[j],
                                      hy_f3_w[j], hy_skip[j], cf_dw_w[j], cf_dw_b[j], cf_ln_g[j], cf_ln_b[j],
                                      even_w_out[j], even_b_out[j], ln1_g[i], ln1_b[i], moe_w_r[i], moe_b_r[i])
        else:
            xf, routing = _odd_mixer(xf, (B, S), i, attn_w_qkv[j], attn_lq1[j], attn_lk1[j], attn_lq2[j],
                                     attn_lk2[j], attn_subln_g[j], attn_w_out[j], ln1_g[i], ln1_b[i],
                                     moe_w_r[i], moe_b_r[i])
        xf = _moe_layer(xf, routing, i, moe_w1, moe_b1, moe_w2, moe_b2, ln2_g[i], ln2_b[i])
    return xf.reshape(B, S, D)
```

```python
import functools
import math

import jax
import jax.numpy as jnp
import numpy as np
from jax import lax
from jax.experimental import pallas as pl
from jax.experimental.pallas import tpu as pltpu

F32 = jnp.float32
BF16 = jnp.bfloat16
U32 = jnp.uint32
I32 = jnp.int32

D_MODEL = 1024
HALF = D_MODEL // 2
DEPTH = 4
HYENA_CH = D_MODEL // 2
CONF_CH = D_MODEL // 2
HYENA_ORDER = 2
HYENA_EMB_DIM = 33
HYENA_FILTER_DIM = 64
HYENA_SHORT_DECAY_PCT = 0.3
HYENA_LONG_DECAY_PCT = 1.5
HYENA_DECAY_TARGET = 1e-2
CONF_WIDTH = 31
EVEN_IN = 3 * HYENA_CH + 2 * CONF_CH
N_HEADS = 8
HEAD_DIM = 64
ATTN_W = N_HEADS * 2 * HEAD_DIM
N_EXPERTS = 32
TOP_K = 4
D_FF = D_MODEL
SWIGLU_LIMIT = 7.0
SWIGLU_ALPHA = 1.702
DEEPNORM_ALPHA = (2 * DEPTH) ** 0.25
LN_EPS = 1e-5

LANES = 128
VMEM_LIMIT_BYTES = 56 * 1024 * 1024
ROW_TILE = 512
DENSE_ROW_TILE = 1024
EXPERT_ROWS = 512
SEG_ALIGN = 8
COPY_ROWS = (64, 32, 16, 8)
CONV_BLOCKS = 4
CONV_CH_TILE = 256
ATTN_Q_TILE = 2048
ATTN_ROW_CHUNK = 256

_NT = (((1,), (1,)), ((), ()))


def _params(*sem):
    return pltpu.CompilerParams(dimension_semantics=sem, vmem_limit_bytes=VMEM_LIMIT_BYTES)


def _split_bf16(a):
    hi = a.astype(BF16)
    lo = (a - hi.astype(F32)).astype(BF16)
    return hi, lo


def _dot3(a, b):
    a_hi, a_lo = _split_bf16(a)
    b_hi, b_lo = _split_bf16(b)
    d = functools.partial(jnp.dot, preferred_element_type=F32)
    return d(a_hi, b_hi) + d(a_hi, b_lo) + d(a_lo, b_hi)


def _layer_norm_rows(y, g, b):
    mu = jnp.mean(y, axis=-1, keepdims=True)
    yc = y - mu
    var = jnp.mean(yc * yc, axis=-1, keepdims=True)
    return yc * lax.rsqrt(var + LN_EPS) * g + b


def _pack_halves(y):
    lo = lax.bitcast_convert_type(y[:, :HALF].astype(BF16).astype(F32), U32)
    hi = lax.bitcast_convert_type(y[:, HALF:].astype(BF16).astype(F32), U32)
    return hi | (lo >> 16)


def _unpack_halves(p):
    lo = lax.bitcast_convert_type(p << 16, F32)
    hi = lax.bitcast_convert_type(p & jnp.uint32(0xFFFF0000), F32)
    return lo, hi


def _proj_body(x_ref, w_ref, b_ref, o_ref, *, col_chunk):
    x = x_ref[...].astype(BF16)
    for j in range(0, o_ref.shape[1], col_chunk):
        acc = jnp.dot(x, w_ref[:, j:j + col_chunk], preferred_element_type=F32)
        o_ref[:, j:j + col_chunk] = (acc + b_ref[:, j:j + col_chunk]).astype(o_ref.dtype)


def _project(x, w, b):
    T, K = x.shape
    N = w.shape[1]
    tm = min(T, DENSE_ROW_TILE)
    return pl.pallas_call(
        functools.partial(_proj_body, col_chunk=512),
        grid=(T // tm,),
        in_specs=[pl.BlockSpec((tm, K), lambda i: (i, 0)),
                  pl.BlockSpec((K, N), lambda i: (0, 0)),
                  pl.BlockSpec((1, N), lambda i: (0, 0))],
        out_specs=pl.BlockSpec((tm, N), lambda i: (i, 0)),
        out_shape=jax.ShapeDtypeStruct((T, N), BF16),
        compiler_params=_params("parallel"),
        name="project",
    )(x, w, b)


def _outproj_ln_body(a1_ref, a2_ref, w1_ref, w2_ref, b_ref, x_ref, g_ref, beta_ref, wr_ref, br_ref,
                     xo_ref, idx_ref, gate_ref, rank_ref, cnt_ref):
    m = (jnp.dot(a1_ref[...], w1_ref[...], preferred_element_type=F32)
         + jnp.dot(a2_ref[...], w2_ref[...], preferred_element_type=F32) + b_ref[...])
    y = _layer_norm_rows(DEEPNORM_ALPHA * x_ref[...] + m, g_ref[...], beta_ref[...])
    xo_ref[...] = y
    w_hi, w_lo = _split_bf16(wr_ref[...])
    for s in range(y.shape[0] // ROW_TILE):
        cols = slice(s * ROW_TILE, (s + 1) * ROW_TILE)
        _route_block(y[cols, :], w_hi, w_lo, br_ref[...], idx_ref, gate_ref, rank_ref, cnt_ref.at[s], cols)


def _outproj_ln(a1, a1_col, a2, a2_col, w, b, x, g, beta, w_rt, b_r):
    T = x.shape[0]
    E = N_EXPERTS
    tm = min(T, DENSE_ROW_TILE)
    nsub = tm // ROW_TILE
    tok = pl.BlockSpec((TOP_K, tm), lambda i: (0, i))
    out = pl.pallas_call(
        _outproj_ln_body,
        grid=(T // tm,),
        in_specs=[pl.BlockSpec((tm, HALF), lambda i: (i, a1_col)),
                  pl.BlockSpec((tm, HALF), lambda i: (i, a2_col)),
                  pl.BlockSpec((HALF, D_MODEL), lambda i: (0, 0)),
                  pl.BlockSpec((HALF, D_MODEL), lambda i: (1, 0)),
                  pl.BlockSpec((1, D_MODEL), lambda i: (0, 0)),
                  pl.BlockSpec((tm, D_MODEL), lambda i: (i, 0)),
                  pl.BlockSpec((1, D_MODEL), lambda i: (0, 0)),
                  pl.BlockSpec((1, D_MODEL), lambda i: (0, 0)),
                  pl.BlockSpec((E, D_MODEL), lambda i: (0, 0)),
                  pl.BlockSpec((E, 1), lambda i: (0, 0))],
        out_specs=[pl.BlockSpec((tm, D_MODEL), lambda i: (i, 0)), tok, tok, tok,
                   pl.BlockSpec((nsub, E, LANES), lambda i: (i, 0, 0))],
        out_shape=[jax.ShapeDtypeStruct((T, D_MODEL), F32),
                   jax.ShapeDtypeStruct((TOP_K, T), I32), jax.ShapeDtypeStruct((TOP_K, T), F32),
                   jax.ShapeDtypeStruct((TOP_K, T), I32), jax.ShapeDtypeStruct((T // ROW_TILE, E, LANES), F32)],
        compiler_params=_params("parallel"),
        name="outproj_ln_route",
    )(a1, a2, w, w, b, x, g, beta, w_rt, b_r)
    return out[0], tuple(out[1:])


def _short_conv_body(x_ref, w_ref, b_ref, o_ref):
    x = x_ref[...].astype(F32)
    S = x.shape[0]
    row = lax.broadcasted_iota(I32, x.shape, 0)
    prev = jnp.where(row == 0, 0.0, pltpu.roll(x, 1, 0))
    nxt = jnp.where(row == S - 1, 0.0, pltpu.roll(x, S - 1, 0))
    y = w_ref[0:1, :] * prev + w_ref[1:2, :] * x + w_ref[2:3, :] * nxt + b_ref[...]
    o_ref[...] = y.astype(o_ref.dtype)


def _short_conv(proj, w, b, B, S):
    T = B * S
    C = HYENA_CH
    return pl.pallas_call(
        _short_conv_body,
        grid=(B, 3),
        in_specs=[pl.BlockSpec((S, C), lambda bi, j: (bi, j)),
                  pl.BlockSpec((3, C), lambda bi, j: (0, j)),
                  pl.BlockSpec((1, C), lambda bi, j: (0, j))],
        out_specs=pl.BlockSpec((S, C), lambda bi, j: (bi, j)),
        out_shape=jax.ShapeDtypeStruct((T, 3 * C), BF16),
        compiler_params=_params("parallel", "parallel"),
        name="hyena_short_conv",
    )(proj, w, b)


def _conformer_body(a_ref, g_ref, fc_ref, fs_ref, hre_ref, him_ref, gc_ref, gs_ref, b_ref, lg_ref, lb_ref, o_ref,
                    *, n_blk):
    P = fc_ref.shape[0]
    fc, fs, gc, gs = fc_ref[...], fs_ref[...], gc_ref[...], gs_ref[...]
    vre, vim = [], []
    for j in range(n_blk):
        rows = slice(j * P, (j + 1) * P)
        u = (a_ref[rows, :].astype(F32) * jax.nn.sigmoid(g_ref[rows, :].astype(F32))).astype(BF16)
        vre.append(jnp.dot(fc, u, preferred_element_type=F32))
        vim.append(jnp.dot(fs, u, preferred_element_type=F32))
    for i in range(n_blk):
        yre, yim = _mix_block_lags(vre, vim, hre_ref, him_ref, i, 1)
        y = jnp.dot(gc, yre, preferred_element_type=F32) + jnp.dot(gs, yim, preferred_element_type=F32) + b_ref[...]
        y = _layer_norm_rows(y, lg_ref[...], lb_ref[...])
        o_ref[i * P:(i + 1) * P, :] = (y * jax.nn.sigmoid(y)).astype(o_ref.dtype)


def _conformer(proj, tabs, w, b, lg, lb, B, S):
    fc32, fs32, (fc, fs, gc, gs) = tabs
    P = fc.shape[0]
    T = B * S
    C = CONF_CH
    half = CONF_WIDTH // 2
    assert half < P
    span = ((S - half, S - half - 1), (0, 0))
    hre, him = _lag_spectra(fc32, fs32, jnp.stack([jnp.pad(w[::-1], span), jnp.pad(w, span)]), 1)
    tab = pl.BlockSpec((P, P), lambda bi: (0, 0))
    spec = pl.BlockSpec((3 * P, C), lambda bi: (0, 0))
    vec = pl.BlockSpec((1, C), lambda bi: (0, 0))
    return pl.pallas_call(
        functools.partial(_conformer_body, n_blk=S // P),
        grid=(B,),
        in_specs=[pl.BlockSpec((S, C), lambda bi: (bi, 3)),
                  pl.BlockSpec((S, C), lambda bi: (bi, 4)),
                  tab, tab, spec, spec, tab, tab, vec, vec, vec],
        out_specs=pl.BlockSpec((S, C), lambda bi: (bi, 0)),
        out_shape=jax.ShapeDtypeStruct((T, C), BF16),
        compiler_params=_params("parallel"),
        name="conformer_conv",
    )(proj, proj, fc, fs, hre, him, gc, gs, b, lg, lb)


def _filter_body(feat_ref, w1_ref, b1_ref, q1_ref, w2_ref, b2_ref, q2_ref, w3_ref, t_ref, delta_ref, o_ref):
    h = jnp.sin(q1_ref[...] * (_dot3(feat_ref[...], w1_ref[...]) + b1_ref[...]))
    h = jnp.sin(q2_ref[...] * (_dot3(h, w2_ref[...]) + b2_ref[...]))
    h = _dot3(h, w3_ref[...])
    o_ref[...] = h * jnp.exp(-t_ref[...] * delta_ref[...])


def _hyena_filters(feats2, w1, b1, q1, w2, b2, q2, w3, tcol2, deltas):
    S = feats2.shape[1]
    C = HYENA_CH
    fd = HYENA_FILTER_DIM
    fixed = lambda p, d, o: (0, 0)
    return pl.pallas_call(
        _filter_body,
        grid=(2, 2, HYENA_ORDER),
        in_specs=[pl.BlockSpec((None, S, LANES), lambda p, d, o: (p, 0, 0)),
                  pl.BlockSpec((LANES, fd), fixed),
                  pl.BlockSpec((1, fd), fixed),
                  pl.BlockSpec((1, fd), fixed),
                  pl.BlockSpec((fd, fd), fixed),
                  pl.BlockSpec((1, fd), fixed),
                  pl.BlockSpec((1, fd), fixed),
                  pl.BlockSpec((fd, C), lambda p, d, o: (0, d * HYENA_ORDER + o)),
                  pl.BlockSpec((None, S, 1), lambda p, d, o: (p, 0, 0)),
                  pl.BlockSpec((1, C), fixed)],
        out_specs=pl.BlockSpec((None, S, C), lambda p, d, o: ((d + 1 - p) % 2, p, o)),
        out_shape=jax.ShapeDtypeStruct((2, 2 * S, HYENA_ORDER * C), F32),
        compiler_params=_params("parallel", "parallel", "parallel"),
        name="hyena_filter_mlp",
    )(feats2, w1, b1, q1, w2, b2, q2, w3, tcol2, deltas)


def _spectrum_body(fc_ref, fs_ref, a_ref, b_ref, hre_ref, him_ref):
    a = a_ref[...]
    row = lax.broadcasted_iota(I32, a.shape, 0)
    b = jnp.where(row == 0, 0.0, b_ref[...])
    hre_ref[...] = _dot3(fc_ref[...], a + b)
    him_ref[...] = _dot3(fs_ref[...], a - b)


def _lag_spectra(fc32, fs32, h2, d_max):
    P = fc32.shape[0]
    n = h2.shape[1] // (2 * P)
    n_ch = h2.shape[2]
    C = HYENA_CH
    nd = 2 * d_max + 1
    spec = pl.BlockSpec((P, C), lambda di, o: (di, o))
    return pl.pallas_call(
        _spectrum_body,
        grid=(nd, n_ch // C),
        in_specs=[pl.BlockSpec((P, P), lambda di, o: (0, 0)),
                  pl.BlockSpec((P, P), lambda di, o: (0, 0)),
                  pl.BlockSpec((None, P, C), lambda di, o: (0, n - d_max + di, o)),
                  pl.BlockSpec((None, P, C), lambda di, o: (1, n + d_max - di, o))],
        out_specs=[spec, spec],
        out_shape=[jax.ShapeDtypeStruct((nd * P, n_ch), F32)] * 2,
        compiler_params=_params("parallel", "parallel"),
        name="block_lag_spectra",
    )(fc32, fs32, h2, h2)


def _mix_block_lags(vre, vim, hre_ref, him_ref, i, d_max):
    P = vre[0].shape[0]
    yre = yim = None
    for j in range(max(0, i - d_max), min(len(vre), i + d_max + 1)):
        r0 = (i - j + d_max) * P
        hre = hre_ref[r0:r0 + P, :]
        him = him_ref[r0:r0 + P, :]
        tre = vre[j] * hre - vim[j] * him
        tim = vre[j] * him + vim[j] * hre
        yre = tre if yre is None else yre + tre
        yim = tim if yim is None else yim + tim
    return yre.astype(BF16), yim.astype(BF16)


def _long_conv_body(v_ref, gate_ref, fc_ref, fs_ref, hre_ref, him_ref, gc_ref, gs_ref, skip_ref, o_ref, *, n_blk):
    P = fc_ref.shape[0]
    fc, fs, gc, gs = fc_ref[...], fs_ref[...], gc_ref[...], gs_ref[...]
    vre, vim = [], []
    for j in range(n_blk):
        vj = v_ref[j * P:(j + 1) * P, :]
        vre.append(jnp.dot(fc, vj, preferred_element_type=F32))
        vim.append(jnp.dot(fs, vj, preferred_element_type=F32))
    for i in range(n_blk):
        yre, yim = _mix_block_lags(vre, vim, hre_ref, him_ref, i, n_blk - 1)
        y = jnp.dot(gc, yre, preferred_element_type=F32) + jnp.dot(gs, yim, preferred_element_type=F32)
        rows = slice(i * P, (i + 1) * P)
        y = y + v_ref[rows, :].astype(F32) * skip_ref[...]
        o_ref[rows, :] = (gate_ref[rows, :].astype(F32) * y).astype(o_ref.dtype)


def _long_conv(v_arr, v_col, gate_arr, gate_col, tabs, hre, him, order, skip, B, S):
    fc, fs, gc, gs = tabs
    P = fc.shape[0]
    T = B * S
    C = HYENA_CH
    nc = C // CONV_CH_TILE
    cc = CONV_CH_TILE
    n_h = hre.shape[0]
    tab = pl.BlockSpec((P, P), lambda bi, c: (0, 0))
    return pl.pallas_call(
        functools.partial(_long_conv_body, n_blk=S // P),
        grid=(B, nc),
        in_specs=[pl.BlockSpec((S, cc), lambda bi, c: (bi, v_col * nc + c)),
                  pl.BlockSpec((S, cc), lambda bi, c: (bi, gate_col * nc + c)),
                  tab, tab,
                  pl.BlockSpec((n_h, cc), lambda bi, c: (0, order * nc + c)),
                  pl.BlockSpec((n_h, cc), lambda bi, c: (0, order * nc + c)),
                  tab, tab,
                  pl.BlockSpec((1, cc), lambda bi, c: (0, c))],
        out_specs=pl.BlockSpec((S, cc), lambda bi, c: (bi, c)),
        out_shape=jax.ShapeDtypeStruct((T, C), BF16),
        compiler_params=_params("parallel", "parallel"),
        name="hyena_long_conv",
    )(v_arr, gate_arr, fc, fs, hre, him, gc, gs, skip[order][None, :])


def _attn_body(slope_ref, lam_ref, q_ref, k_ref, v_ref, g_ref, o_ref, vaug_ref, kt_ref, *, lam_init, row_chunk):
    h = pl.program_id(1)
    qi = pl.program_id(2)
    tq = q_ref.shape[0]
    S = k_ref.shape[0]
    hw = 2 * HEAD_DIM
    slope = slope_ref[h]
    kpos = lax.broadcasted_iota(I32, (1, S), 1).astype(F32) * slope

    @pl.when(qi == 0)
    def _():
        vaug_ref[:, :hw] = v_ref[...]
        vaug_ref[:, hw:] = jnp.where(lax.broadcasted_iota(I32, (S, hw), 1) == 0, 1.0, 0.0).astype(BF16)
        kt_ref[...] = k_ref[...].astype(F32).T.astype(BF16)

    v_aug = vaug_ref[...]
    kt = kt_ref[...]

    for r0 in range(0, tq, row_chunk):
        q = q_ref[r0:r0 + row_chunk, :]
        lane = lax.broadcasted_iota(I32, q.shape, 1)
        zero = jnp.zeros_like(q)
        qpos = (qi * tq + r0 + lax.broadcasted_iota(I32, (row_chunk, 1), 0)).astype(F32) * slope
        bias = lax.bitcast_convert_type(lax.bitcast_convert_type(qpos - kpos, U32) | jnp.uint32(0x80000000), F32)

        def weighted_values(qh):
            s = jnp.dot(qh, kt, preferred_element_type=F32) + bias
            e = jnp.exp((s - jnp.max(s, axis=-1, keepdims=True)).astype(BF16))
            return jnp.dot(e, v_aug, preferred_element_type=F32)

        o1 = weighted_values(jnp.where(lane < HEAD_DIM, q, zero))
        o2 = weighted_values(jnp.where(lane >= HEAD_DIM, q, zero))
        o = o1[:, :hw] * (1.0 / o1[:, hw:hw + 1]) - o2[:, :hw] * (lam_ref[0] / o2[:, hw:hw + 1])
        o = o * lax.rsqrt(jnp.mean(o * o, axis=-1, keepdims=True) + LN_EPS) * g_ref[...]
        o_ref[r0:r0 + row_chunk, :] = (o * (1.0 - lam_init)).astype(o_ref.dtype)


def _diff_attention(qkv, slopes, lam, subln_g, lam_init, B, S):
    T = B * S
    hw = 2 * HEAD_DIM
    tq = min(S, ATTN_Q_TILE)
    nq = S // tq
    smem = pl.BlockSpec(memory_space=pltpu.SMEM)
    return pl.pallas_call(
        functools.partial(_attn_body, lam_init=lam_init, row_chunk=min(tq, ATTN_ROW_CHUNK)),
        grid=(B, N_HEADS, nq),
        in_specs=[smem, smem,
                  pl.BlockSpec((tq, hw), lambda bi, h, qi: (bi * nq + qi, h)),
                  pl.BlockSpec((S, hw), lambda bi, h, qi: (bi, N_HEADS + h)),
                  pl.BlockSpec((S, hw), lambda bi, h, qi: (bi, 2 * N_HEADS + h)),
                  pl.BlockSpec((1, hw), lambda bi, h, qi: (0, 0))],
        out_specs=pl.BlockSpec((tq, hw), lambda bi, h, qi: (bi * nq + qi, h)),
        out_shape=jax.ShapeDtypeStruct((T, ATTN_W), BF16),
        scratch_shapes=[pltpu.VMEM((S, 2 * hw), BF16), pltpu.VMEM((hw, S), BF16)],
        compiler_params=_params("parallel", "parallel", "arbitrary"),
        name="diff_attention",
    )(slopes, lam, qkv, qkv, qkv, subln_g)


def _route_block(x, w_hi, w_lo, bias, idx_ref, gate_ref, rank_ref, cnt_ref, cols):
    E = N_EXPERTS
    tm = x.shape[0]
    x_hi, x_lo = _split_bf16(x)
    nt = functools.partial(lax.dot_general, dimension_numbers=_NT, preferred_element_type=F32)
    logits = nt(w_hi, x_hi) + nt(w_lo, x_hi) + nt(w_hi, x_lo) + bias

    eid = lax.broadcasted_iota(I32, (E, tm), 0).astype(F32)
    work = logits
    vals, idxs = [], []
    for _ in range(TOP_K):
        m = jnp.max(work, axis=0, keepdims=True)
        sel = jnp.min(jnp.where(work == m, eid, float(E)), axis=0, keepdims=True)
        vals.append(m)
        idxs.append(sel)
        work = jnp.where(eid == sel, -jnp.inf, work)
    exps = [jnp.exp(v - vals[0]) for v in vals]
    denom = exps[0] + exps[1] + exps[2] + exps[3]

    chosen = jnp.zeros((E, tm), F32)
    for sel in idxs:
        chosen = chosen + jnp.where(eid == sel, 1.0, 0.0)
    earlier = jnp.where(lax.broadcasted_iota(I32, (tm, tm), 0) < lax.broadcasted_iota(I32, (tm, tm), 1), 1.0, 0.0)
    before = jnp.dot(chosen.astype(BF16), earlier.astype(BF16), preferred_element_type=F32)
    for k in range(TOP_K):
        gate_ref[k:k + 1, cols] = exps[k] / denom
        idx_ref[k:k + 1, cols] = idxs[k].astype(I32)
        rank_ref[k:k + 1, cols] = jnp.sum(jnp.where(eid == idxs[k], before, 0.0), axis=0, keepdims=True).astype(I32)
    cnt_ref[...] = jnp.broadcast_to(jnp.sum(chosen, axis=1, keepdims=True), cnt_ref.shape)


def _copy_caps():
    local_rows = TOP_K * ROW_TILE + N_EXPERTS * SEG_ALIGN
    return (local_rows // COPY_ROWS[0],) + (N_EXPERTS,) * (len(COPY_ROWS) - 1)


def _segment_copies(tab_ref, make_copy, slot, wait):
    base = len(COPY_ROWS)
    for ci, (rows, cap) in enumerate(zip(COPY_ROWS, _copy_caps())):
        def body(p, carry, base=base, rows=rows, cap=cap):
            copy = make_copy(slot, tab_ref[0, base + p], tab_ref[0, base + cap + p], rows)
            if wait:
                copy.wait()
            else:
                copy.start()
            return carry
        lax.fori_loop(0, tab_ref[0, ci], body, 0)
        base += 2 * cap


def _dispatch_body(tail_ref, seg_ref, seg_prev_ref, x_ref, idx_ref, rank_ref, loff_ref, xs_hbm, lrow_ref,
                   buf_ref, zero_ref, sem, zsem, *, n_tok_blocks):
    b = pl.program_id(0)
    slot = b % 2
    tm = x_ref.shape[0]
    R = buf_ref.shape[1]

    def seg_copy(s, local_row, dst_row, rows):
        return pltpu.make_async_copy(buf_ref.at[s, pl.ds(pl.multiple_of(local_row, SEG_ALIGN), rows)],
                                     xs_hbm.at[pl.ds(pl.multiple_of(dst_row, SEG_ALIGN), rows)], sem.at[s])

    @pl.when(b == 0)
    def _():
        zero_ref[...] = jnp.zeros_like(zero_ref)
        for e in range(N_EXPERTS):
            fill = pltpu.make_async_copy(
                zero_ref, xs_hbm.at[pl.ds(pl.multiple_of(tail_ref[e], SEG_ALIGN), EXPERT_ROWS)], zsem)
            fill.start()
            fill.wait()

    eid = lax.broadcasted_iota(I32, (N_EXPERTS, tm), 0)
    loff = loff_ref[...].astype(F32)
    rid = lax.broadcasted_iota(I32, (R, tm), 0).astype(jnp.int16)
    sel_t = jnp.zeros((R, tm), BF16)
    for k in range(TOP_K):
        base = jnp.sum(jnp.where(eid == idx_ref[k:k + 1, :], loff, 0.0), axis=0, keepdims=True).astype(I32)
        row = base + rank_ref[k:k + 1, :]
        lrow_ref[k:k + 1, :] = row
        sel_t = jnp.where(rid == row.astype(jnp.int16), jnp.ones((), BF16), sel_t)
    xb = x_ref[...].astype(BF16)
    lo = jnp.dot(sel_t, xb[:, :HALF], preferred_element_type=F32)
    hi = jnp.dot(sel_t, xb[:, HALF:], preferred_element_type=F32)
    packed = (lax.bitcast_convert_type(hi, U32) & jnp.uint32(0xFFFF0000)) | (lax.bitcast_convert_type(lo, U32) >> 16)
    buf_ref[slot] = packed

    @pl.when(b >= 1)
    def _():
        _segment_copies(seg_prev_ref, seg_copy, 1 - slot, wait=True)
    _segment_copies(seg_ref, seg_copy, slot, wait=False)

    @pl.when(b == n_tok_blocks - 1)
    def _():
        _segment_copies(seg_ref, seg_copy, slot, wait=True)


def _dispatch(x, idx, rank, loff, seg_table, tail_start, n_rows):
    T = x.shape[0]
    tm = ROW_TILE
    nb = T // tm
    R = TOP_K * tm + N_EXPERTS * SEG_ALIGN
    tok = pl.BlockSpec((TOP_K, tm), lambda b, tl: (0, b))
    seg_w = seg_table.shape[-1]
    return pl.pallas_call(
        functools.partial(_dispatch_body, n_tok_blocks=nb),
        grid_spec=pltpu.PrefetchScalarGridSpec(
            num_scalar_prefetch=1,
            grid=(nb,),
            in_specs=[pl.BlockSpec((None, 1, seg_w), lambda b, tl: (b, 0, 0), memory_space=pltpu.SMEM),
                      pl.BlockSpec((None, 1, seg_w), lambda b, tl: (jnp.maximum(b - 1, 0), 0, 0),
                                   memory_space=pltpu.SMEM),
                      pl.BlockSpec((tm, D_MODEL), lambda b, tl: (b, 0)),
                      tok, tok,
                      pl.BlockSpec((None, N_EXPERTS, 1), lambda b, tl: (b, 0, 0))],
            out_specs=[pl.BlockSpec(memory_space=pl.ANY), tok],
            scratch_shapes=[pltpu.VMEM((2, R, HALF), U32), pltpu.VMEM((EXPERT_ROWS, HALF), U32),
                            pltpu.SemaphoreType.DMA((2,)), pltpu.SemaphoreType.DMA(())]),
        out_shape=[jax.ShapeDtypeStruct((n_rows, HALF), U32), jax.ShapeDtypeStruct((TOP_K, T), I32)],
        compiler_params=_params("arbitrary"),
        name="moe_dispatch",
    )(tail_start, seg_table, seg_table, x, idx, rank, loff)


def _ffn_body(be_ref, nu_ref, xs_ref, w1_ref, b1_ref, w2_ref, b2_ref, ys_ref, w1b_ref, w2b_ref):
    i = pl.program_id(0)
    used = i < nu_ref[0]
    fresh = jnp.logical_or(i == 0, be_ref[i] != be_ref[jnp.maximum(i - 1, 0)])

    @pl.when(jnp.logical_and(used, fresh))
    def _():
        w1b_ref[...] = w1_ref[...].astype(BF16)
        w2b_ref[...] = w2_ref[...].astype(BF16)

    @pl.when(used)
    def _():
        lo, hi = _unpack_halves(xs_ref[...])
        h = (jnp.dot(lo.astype(BF16), w1b_ref[0:HALF, :], preferred_element_type=F32)
             + jnp.dot(hi.astype(BF16), w1b_ref[HALF:D_MODEL, :], preferred_element_type=F32) + b1_ref[...])
        hg = jnp.minimum(h[:, :D_FF], SWIGLU_LIMIT)
        hu = jnp.clip(h[:, D_FF:], -SWIGLU_LIMIT, SWIGLU_LIMIT)
        act = (hu + 1.0) * (hg * jax.nn.sigmoid(hg * SWIGLU_ALPHA))
        y = jnp.dot(act.astype(BF16), w2b_ref[...], preferred_element_type=F32) + b2_ref[...]
        ys_ref[...] = _pack_halves(y)


def _expert_ffn(xs, blk_expert, n_used, w1, b1, w2, b2, layer, n_blocks):
    rows = pl.BlockSpec((EXPERT_ROWS, HALF), lambda i, be, nu: (jnp.minimum(i, nu[0] - 1), 0))
    return pl.pallas_call(
        _ffn_body,
        grid_spec=pltpu.PrefetchScalarGridSpec(
            num_scalar_prefetch=2,
            grid=(n_blocks,),
            in_specs=[rows,
                      pl.BlockSpec((None, None, D_MODEL, 2 * D_FF), lambda i, be, nu: (layer, be[i], 0, 0)),
                      pl.BlockSpec((None, None, 1, 2 * D_FF), lambda i, be, nu: (layer, be[i], 0, 0)),
                      pl.BlockSpec((None, None, D_FF, D_MODEL), lambda i, be, nu: (layer, be[i], 0, 0)),
                      pl.BlockSpec((None, None, 1, D_MODEL), lambda i, be, nu: (layer, be[i], 0, 0))],
            out_specs=rows,
            scratch_shapes=[pltpu.VMEM((D_MODEL, 2 * D_FF), BF16), pltpu.VMEM((D_FF, D_MODEL), BF16)]),
        out_shape=jax.ShapeDtypeStruct((xs.shape[0], HALF), U32),
        compiler_params=_params("arbitrary"),
        name="moe_expert_ffn",
    )(blk_expert, n_used, xs, w1, b1, w2, b2)


def _combine_body(seg_ref, seg_next_ref, lrow_ref, gate_ref, x_ref, g_ref, beta_ref, ys_hbm, xo_ref,
                  buf_ref, sem, *, n_tok_blocks):
    b = pl.program_id(0)
    slot = b % 2
    tm = x_ref.shape[0]
    R = buf_ref.shape[1]

    def seg_copy(s, local_row, src_row, rows):
        return pltpu.make_async_copy(ys_hbm.at[pl.ds(pl.multiple_of(src_row, SEG_ALIGN), rows)],
                                     buf_ref.at[s, pl.ds(pl.multiple_of(local_row, SEG_ALIGN), rows)], sem.at[s])

    @pl.when(b == 0)
    def _():
        buf_ref[...] = jnp.zeros_like(buf_ref)
        _segment_copies(seg_ref, seg_copy, 0, wait=False)

    @pl.when(b + 1 < n_tok_blocks)
    def _():
        _segment_copies(seg_next_ref, seg_copy, 1 - slot, wait=False)

    _segment_copies(seg_ref, seg_copy, slot, wait=True)

    lo, hi = _unpack_halves(buf_ref[slot])
    lo = lo.astype(BF16)
    hi = hi.astype(BF16)
    tc = tm // 2
    cid = lax.broadcasted_iota(I32, (tc, R), 1).astype(jnp.int16)
    lrow_t = lrow_ref[...].astype(F32).T
    gate_t = gate_ref[...].T
    for t0 in range(0, tm, tc):
        lrow = lrow_t[t0:t0 + tc, :].astype(jnp.int16)
        gates = gate_t[t0:t0 + tc, :].astype(BF16)
        sel = jnp.zeros((tc, R), BF16)
        for k in range(TOP_K):
            sel = jnp.where(cid == lrow[:, k:k + 1], gates[:, k:k + 1], sel)
        f = jnp.concatenate([jnp.dot(sel, lo, preferred_element_type=F32),
                             jnp.dot(sel, hi, preferred_element_type=F32)], axis=1)
        xo_ref[t0:t0 + tc, :] = _layer_norm_rows(DEEPNORM_ALPHA * x_ref[t0:t0 + tc, :] + f, g_ref[...], beta_ref[...])


def _combine_ln(lrow, gate, x, g, beta, ys, seg_table):
    T = x.shape[0]
    tm = ROW_TILE
    nb = T // tm
    R = TOP_K * tm + N_EXPERTS * SEG_ALIGN
    seg_w = seg_table.shape[-1]
    return pl.pallas_call(
        functools.partial(_combine_body, n_tok_blocks=nb),
        grid=(nb,),
        in_specs=[pl.BlockSpec((None, 1, seg_w), lambda b: (b, 0, 0), memory_space=pltpu.SMEM),
                  pl.BlockSpec((None, 1, seg_w), lambda b: (jnp.minimum(b + 1, nb - 1), 0, 0),
                               memory_space=pltpu.SMEM),
                  pl.BlockSpec((TOP_K, tm), lambda b: (0, b)),
                  pl.BlockSpec((TOP_K, tm), lambda b: (0, b)),
                  pl.BlockSpec((tm, D_MODEL), lambda b: (b, 0)),
                  pl.BlockSpec((1, D_MODEL), lambda b: (0, 0)),
                  pl.BlockSpec((1, D_MODEL), lambda b: (0, 0)),
                  pl.BlockSpec(memory_space=pl.ANY)],
        out_specs=pl.BlockSpec((tm, D_MODEL), lambda b: (b, 0)),
        out_shape=jax.ShapeDtypeStruct((T, D_MODEL), F32),
        scratch_shapes=[pltpu.VMEM((2, R, HALF), U32), pltpu.SemaphoreType.DMA((2,))],
        compiler_params=_params("arbitrary"),
        name="moe_combine_ln",
    )(seg_table, seg_table, lrow, gate, x, g, beta, ys)


def _dft_tables(P):
    n2 = 4 * P
    k = jnp.arange(P, dtype=I32)
    m = ((2 * k[:, None] + 1) * k[None, :]) % n2
    ang = m.astype(F32) * F32(2.0 * math.pi / n2)
    fc32, fs32 = jnp.cos(ang), -jnp.sin(ang)
    scale = F32(1.0 / P)
    gc, gs = (fc32.T * scale).astype(BF16), (fs32.T * scale).astype(BF16)
    return fc32, fs32, (fc32.astype(BF16), fs32.astype(BF16), gc, gs)


def _hyena_positional(S):
    pos = jnp.arange(S, dtype=F32)
    t = jnp.linspace(0.0, 1.0, S, dtype=F32)[:, None]
    bands = (HYENA_EMB_DIM - 1) // 2
    f = jnp.linspace(1e-4, bands - 1, bands, dtype=F32)
    ang = (2.0 * math.pi / S) * pos[:, None] * f[None, :]
    feats = jnp.concatenate([t, jnp.cos(ang), -jnp.sin(ang)], axis=-1)
    feats = jnp.pad(feats, ((0, 0), (0, LANES - HYENA_EMB_DIM)))
    max_decay = math.log(HYENA_DECAY_TARGET) / HYENA_SHORT_DECAY_PCT
    min_decay = math.log(HYENA_DECAY_TARGET) / HYENA_LONG_DECAY_PCT
    deltas = jnp.abs(jnp.linspace(min_decay, max_decay, HYENA_CH, dtype=F32))[None, :]
    return feats, t, deltas


def _alibi_slopes():
    return jnp.asarray(np.array([2.0 ** (-8.0 * (i + 1) / N_HEADS) for i in range(N_HEADS)], dtype=np.float32))


def _even_mixer(x, xshape, tabs, w_in, b_in, short_w, short_b, f1_w, f1_b, f1_freq, f2_w, f2_b, f2_freq, f3_w,
                skip, dw_w, dw_b, cln_g, cln_b, w_out, b_out, ln_g, ln_b, w_r, b_r):
    B, S = xshape
    fc32, fs32, tabs16 = tabs
    proj = _project(x, w_in.astype(BF16), b_in[None, :])
    hy = _short_conv(proj, short_w, short_b[None, :], B, S)
    u = _conformer(proj, tabs, dw_w, dw_b[None, :], cln_g[None, :], cln_b[None, :], B, S)
    feats, tcol, deltas = _hyena_positional(S)
    f1_wp = jnp.pad(f1_w, ((0, LANES - HYENA_EMB_DIM), (0, 0)))
    flip = lambda a: jnp.concatenate([a[:1], a[:0:-1]], axis=0)
    h2 = _hyena_filters(jnp.stack([flip(feats), feats]), f1_wp, f1_b[None, :], f1_freq[None, :], f2_w, f2_b[None, :],
                        f2_freq[None, :], f3_w, jnp.stack([flip(tcol), tcol]), deltas)
    hre, him = _lag_spectra(fc32, fs32, h2, S // fc32.shape[0] - 1)
    z = _long_conv(hy, 2, hy, 0, tabs16, hre, him, 0, skip, B, S)
    z = _long_conv(z, 0, hy, 1, tabs16, hre, him, 1, skip, B, S)
    return _outproj_ln(z, 0, u, 0, w_out.astype(BF16), b_out[None, :], x, ln_g[None, :], ln_b[None, :],
                       w_r.T, b_r[:, None])


def _odd_mixer(x, xshape, layer_idx, w_qkv, lq1, lk1, lq2, lk2, subln_g, w_out, ln_g, ln_b, w_r, b_r):
    B, S = xshape
    lam_init = 0.8 - 0.6 * math.exp(-0.3 * layer_idx)
    lam = (jnp.exp(jnp.sum(lq1 * lk1)) - jnp.exp(jnp.sum(lq2 * lk2)) + lam_init).reshape(1)
    q_scale = jnp.concatenate([jnp.full((ATTN_W,), HEAD_DIM ** -0.5, F32), jnp.ones((2 * ATTN_W,), F32)])
    w = (w_qkv * q_scale).astype(BF16)
    qkv = _project(x, w, jnp.zeros((1, 3 * ATTN_W), F32))
    o = _diff_attention(qkv, _alibi_slopes(), lam, subln_g[None, :], lam_init, B, S)
    return _outproj_ln(o, 0, o, 1, w_out.astype(BF16), jnp.zeros((1, D_MODEL), F32), x, ln_g[None, :], ln_b[None, :],
                       w_r.T, b_r[:, None])


def _round_up(a, m):
    return (a + m - 1) // m * m


def _copy_lists(loff, goff, units):
    E = N_EXPERTS
    caps = _copy_caps()
    big = COPY_ROWS[0]
    n_big = units // (big // SEG_ALIGN)
    cum = jnp.cumsum(n_big, axis=1)
    first = (cum - n_big)[:, None, :]
    p = jnp.arange(caps[0], dtype=I32)[None, :, None]
    mine = (first <= p) & (p < cum[:, None, :])
    within = (p - first) * big
    counts = [cum[:, -1]]
    cols = [jnp.sum(jnp.where(mine, loff[:, None, :] + within, 0), axis=2),
            jnp.sum(jnp.where(mine, goff[:, None, :] + within, 0), axis=2)]
    off = n_big * big
    p = jnp.arange(E, dtype=I32)[None, :, None]
    for rows in COPY_ROWS[1:]:
        has = (units & (rows // SEG_ALIGN)) != 0
        pos = jnp.cumsum(has.astype(I32), axis=1) - has.astype(I32)
        mine = has[:, None, :] & (pos[:, None, :] == p)
        counts.append(jnp.sum(has.astype(I32), axis=1))
        cols += [jnp.sum(jnp.where(mine, (loff + off)[:, None, :], 0), axis=2),
                 jnp.sum(jnp.where(mine, (goff + off)[:, None, :], 0), axis=2)]
        off = off + jnp.where(has, rows, 0)
    return jnp.concatenate([jnp.stack(counts, axis=1)] + cols, axis=1).astype(I32)


def _routing_tables(cnt_blocks, n_ffn_blocks):
    E = N_EXPERTS
    cnt8 = _round_up(cnt_blocks[:, :, 0].astype(I32), SEG_ALIGN)
    seg_end = jnp.cumsum(cnt8, axis=1)
    loff = seg_end - cnt8
    tot8 = jnp.sum(cnt8, axis=0)
    group = _round_up(tot8, EXPERT_ROWS)
    group_end = jnp.cumsum(group)
    group_start = group_end - group
    goff = group_start[None, :] + jnp.cumsum(cnt8, axis=0) - cnt8
    seg_table = _copy_lists(loff, goff, cnt8 // SEG_ALIGN)
    starts = jnp.arange(n_ffn_blocks, dtype=I32) * EXPERT_ROWS
    blk_expert = jnp.minimum(jnp.sum((group_end[None, :] <= starts[:, None]).astype(I32), axis=1), E - 1)
    n_used = group_end[-1:] // EXPERT_ROWS
    tail_start = group_start + tot8
    return loff[:, :, None], seg_table[:, None, :], blk_expert, n_used, tail_start


def _moe_layer(x, routing, layer, w1, b1, w2, b2, ln_g, ln_b):
    T = x.shape[0]
    nb = T // ROW_TILE
    n_rows = _round_up(T * TOP_K + nb * N_EXPERTS * (SEG_ALIGN - 1), EXPERT_ROWS) + N_EXPERTS * EXPERT_ROWS
    n_ffn_blocks = n_rows // EXPERT_ROWS
    idx, gate, rank, cnt = routing
    loff, seg_table, blk_expert, n_used, tail_start = _routing_tables(cnt, n_ffn_blocks)
    xs, lrow = _dispatch(x, idx, rank, loff, seg_table, tail_start, n_rows + EXPERT_ROWS)
    ys = _expert_ffn(xs, blk_expert, n_used, w1, b1[:, :, None, :], w2, b2[:, :, None, :], layer, n_ffn_blocks)
    return _combine_ln(lrow, gate, x, ln_g[None, :], ln_b[None, :], ys, seg_table)


def kernel(x, hy_cf_w_in, hy_cf_b_in, hy_short_w, hy_short_b, hy_f1_w, hy_f1_b, hy_f1_freq, hy_f2_w, hy_f2_b, hy_f2_freq, hy_f3_w, hy_skip, cf_dw_w, cf_dw_b, cf_ln_g, cf_ln_b, even_w_out, even_b_out, attn_w_qkv, attn_lq1, attn_lk1, attn_lq2, attn_lk2, attn_subln_g, attn_w_out, ln1_g, ln1_b, ln2_g, ln2_b, moe_w_r, moe_b_r, moe_w1, moe_b1, moe_w2, moe_b2):
    B, S, D = x.shape
    assert D == D_MODEL and S % LANES == 0
    assert (B * S) % ROW_TILE == 0 and (B * S) % min(B * S, DENSE_ROW_TILE) == 0
    depth = ln1_g.shape[0]
    xf = x.reshape(B * S, D)
    tabs = _dft_tables(S // CONV_BLOCKS)
    for i in range(depth):
        j = i // 2
        if i % 2 == 0:
            xf, routing = _even_mixer(xf, (B, S), tabs, hy_cf_w_in[j], hy_cf_b_in[j], hy_short_w[j], hy_short_b[j],
                                      hy_f1_w[j], hy_f1_b[j], hy_f1_freq[j], hy_f2_w[j], hy_f2_b[j], hy_f2_freq[j],
                                      hy_f3_w[j], hy_skip[j], cf_dw_w[j], cf_dw_b[j], cf_ln_g[j], cf_ln_b[j],
                                      even_w_out[j], even_b_out[j], ln1_g[i], ln1_b[i], moe_w_r[i], moe_b_r[i])
        else:
            xf, routing = _odd_mixer(xf, (B, S), i, attn_w_qkv[j], attn_lq1[j], attn_lk1[j], attn_lq2[j],
                                     attn_lk2[j], attn_subln_g[j], attn_w_out[j], ln1_g[i], ln1_b[i],
                                     moe_w_r[i], moe_b_r[i])
        xf = _moe_layer(xf, routing, i, moe_w1, moe_b1, moe_w2, moe_b2, ln2_g[i], ln2_b[i])
    return xf.reshape(B, S, D)
```

```python
import functools
import math

import jax
import jax.numpy as jnp
import numpy as np
from jax import lax
from jax.experimental import pallas as pl
from jax.experimental.pallas import tpu as pltpu

F32 = jnp.float32
BF16 = jnp.bfloat16
U32 = jnp.uint32
I32 = jnp.int32

D_MODEL = 1024
HALF = D_MODEL // 2
DEPTH = 4
HYENA_CH = D_MODEL // 2
CONF_CH = D_MODEL // 2
HYENA_ORDER = 2
HYENA_EMB_DIM = 33
HYENA_FILTER_DIM = 64
HYENA_SHORT_DECAY_PCT = 0.3
HYENA_LONG_DECAY_PCT = 1.5
HYENA_DECAY_TARGET = 1e-2
CONF_WIDTH = 31
EVEN_IN = 3 * HYENA_CH + 2 * CONF_CH
N_HEADS = 8
HEAD_DIM = 64
ATTN_W = N_HEADS * 2 * HEAD_DIM
N_EXPERTS = 32
TOP_K = 4
D_FF = D_MODEL
SWIGLU_LIMIT = 7.0
SWIGLU_ALPHA = 1.702
DEEPNORM_ALPHA = (2 * DEPTH) ** 0.25
LN_EPS = 1e-5

LANES = 128
VMEM_LIMIT_BYTES = 56 * 1024 * 1024
ROW_TILE = 512
DENSE_ROW_TILE = 1024
EXPERT_ROWS = 512
FFN_ROW_CHUNK = 512
SEG_ALIGN = 8
COPY_ROWS = (64, 32, 16, 8)
CONV_BLOCKS = 4
CONV_CH_TILE = 256
ATTN_Q_TILE = 2048
ATTN_HEADS_PER_STEP = 1
ATTN_ROW_CHUNK = 256

_NT = (((1,), (1,)), ((), ()))


def _params(*sem):
    return pltpu.CompilerParams(dimension_semantics=sem, vmem_limit_bytes=VMEM_LIMIT_BYTES)


def _split_bf16(a):
    hi = a.astype(BF16)
    lo = (a - hi.astype(F32)).astype(BF16)
    return hi, lo


def _dot3(a, b):
    a_hi, a_lo = _split_bf16(a)
    b_hi, b_lo = _split_bf16(b)
    d = functools.partial(jnp.dot, preferred_element_type=F32)
    return d(a_hi, b_hi) + d(a_hi, b_lo) + d(a_lo, b_hi)


def _layer_norm_rows(y, g, b):
    mu = jnp.mean(y, axis=-1, keepdims=True)
    yc = y - mu
    var = jnp.mean(yc * yc, axis=-1, keepdims=True)
    return yc * lax.rsqrt(var + LN_EPS) * g + b


def _pack_halves(y):
    lo = lax.bitcast_convert_type(y[:, :HALF].astype(BF16).astype(F32), U32)
    hi = lax.bitcast_convert_type(y[:, HALF:].astype(BF16).astype(F32), U32)
    return hi | (lo >> 16)


def _unpack_halves(p):
    lo = lax.bitcast_convert_type(p << 16, F32)
    hi = lax.bitcast_convert_type(p & jnp.uint32(0xFFFF0000), F32)
    return lo, hi


def _proj_body(x_ref, w_ref, b_ref, o_ref, *, col_chunk):
    x = x_ref[...].astype(BF16)
    for j in range(0, o_ref.shape[1], col_chunk):
        acc = jnp.dot(x, w_ref[:, j:j + col_chunk], preferred_element_type=F32)
        o_ref[:, j:j + col_chunk] = (acc + b_ref[:, j:j + col_chunk]).astype(o_ref.dtype)


def _project(x, w, b):
    T, K = x.shape
    N = w.shape[1]
    tm = min(T, DENSE_ROW_TILE)
    return pl.pallas_call(
        functools.partial(_proj_body, col_chunk=512),
        grid=(T // tm,),
        in_specs=[pl.BlockSpec((tm, K), lambda i: (i, 0)),
                  pl.BlockSpec((K, N), lambda i: (0, 0)),
                  pl.BlockSpec((1, N), lambda i: (0, 0))],
        out_specs=pl.BlockSpec((tm, N), lambda i: (i, 0)),
        out_shape=jax.ShapeDtypeStruct((T, N), BF16),
        compiler_params=_params("parallel"),
        name="project",
    )(x, w, b)


def _outproj_ln_body(a1_ref, a2_ref, w1_ref, w2_ref, b_ref, x_ref, g_ref, beta_ref, wr_ref, br_ref,
                     xo_ref, idx_ref, gate_ref, rank_ref, cnt_ref):
    m = (jnp.dot(a1_ref[...], w1_ref[...], preferred_element_type=F32)
         + jnp.dot(a2_ref[...], w2_ref[...], preferred_element_type=F32) + b_ref[...])
    y = _layer_norm_rows(DEEPNORM_ALPHA * x_ref[...] + m, g_ref[...], beta_ref[...])
    xo_ref[...] = y
    w_hi, w_lo = _split_bf16(wr_ref[...])
    for s in range(y.shape[0] // ROW_TILE):
        cols = slice(s * ROW_TILE, (s + 1) * ROW_TILE)
        _route_block(y[cols, :], w_hi, w_lo, br_ref[...], idx_ref, gate_ref, rank_ref, cnt_ref.at[s], cols)


def _outproj_ln(a1, a1_col, a2, a2_col, w, b, x, g, beta, w_rt, b_r):
    T = x.shape[0]
    E = N_EXPERTS
    tm = min(T, DENSE_ROW_TILE)
    nsub = tm // ROW_TILE
    tok = pl.BlockSpec((TOP_K, tm), lambda i: (0, i))
    out = pl.pallas_call(
        _outproj_ln_body,
        grid=(T // tm,),
        in_specs=[pl.BlockSpec((tm, HALF), lambda i: (i, a1_col)),
                  pl.BlockSpec((tm, HALF), lambda i: (i, a2_col)),
                  pl.BlockSpec((HALF, D_MODEL), lambda i: (0, 0)),
                  pl.BlockSpec((HALF, D_MODEL), lambda i: (1, 0)),
                  pl.BlockSpec((1, D_MODEL), lambda i: (0, 0)),
                  pl.BlockSpec((tm, D_MODEL), lambda i: (i, 0)),
                  pl.BlockSpec((1, D_MODEL), lambda i: (0, 0)),
                  pl.BlockSpec((1, D_MODEL), lambda i: (0, 0)),
                  pl.BlockSpec((E, D_MODEL), lambda i: (0, 0)),
                  pl.BlockSpec((E, 1), lambda i: (0, 0))],
        out_specs=[pl.BlockSpec((tm, D_MODEL), lambda i: (i, 0)), tok, tok, tok,
                   pl.BlockSpec((nsub, E, LANES), lambda i: (i, 0, 0))],
        out_shape=[jax.ShapeDtypeStruct((T, D_MODEL), F32),
                   jax.ShapeDtypeStruct((TOP_K, T), I32), jax.ShapeDtypeStruct((TOP_K, T), F32),
                   jax.ShapeDtypeStruct((TOP_K, T), I32), jax.ShapeDtypeStruct((T // ROW_TILE, E, LANES), F32)],
        compiler_params=_params("parallel"),
        name="outproj_ln_route",
    )(a1, a2, w, w, b, x, g, beta, w_rt, b_r)
    return out[0], tuple(out[1:])


def _short_conv_body(x_ref, w_ref, b_ref, o_ref):
    x = x_ref[...].astype(F32)
    S = x.shape[0]
    row = lax.broadcasted_iota(I32, x.shape, 0)
    prev = jnp.where(row == 0, 0.0, pltpu.roll(x, 1, 0))
    nxt = jnp.where(row == S - 1, 0.0, pltpu.roll(x, S - 1, 0))
    y = w_ref[0:1, :] * prev + w_ref[1:2, :] * x + w_ref[2:3, :] * nxt + b_ref[...]
    o_ref[...] = y.astype(o_ref.dtype)


def _short_conv(proj, w, b, B, S):
    T = B * S
    C = HYENA_CH
    return pl.pallas_call(
        _short_conv_body,
        grid=(B, 3),
        in_specs=[pl.BlockSpec((S, C), lambda bi, j: (bi, j)),
                  pl.BlockSpec((3, C), lambda bi, j: (0, j)),
                  pl.BlockSpec((1, C), lambda bi, j: (0, j))],
        out_specs=pl.BlockSpec((S, C), lambda bi, j: (bi, j)),
        out_shape=jax.ShapeDtypeStruct((T, 3 * C), BF16),
        compiler_params=_params("parallel", "parallel"),
        name="hyena_short_conv",
    )(proj, w, b)


def _conformer_body(a_ref, g_ref, fc_ref, fs_ref, hre_ref, him_ref, gc_ref, gs_ref, b_ref, lg_ref, lb_ref, o_ref,
                    *, n_blk):
    P = fc_ref.shape[0]
    fc, fs, gc, gs = fc_ref[...], fs_ref[...], gc_ref[...], gs_ref[...]
    vre, vim = [], []
    for j in range(n_blk):
        rows = slice(j * P, (j + 1) * P)
        u = (a_ref[rows, :].astype(F32) * jax.nn.sigmoid(g_ref[rows, :].astype(F32))).astype(BF16)
        vre.append(jnp.dot(fc, u, preferred_element_type=F32))
        vim.append(jnp.dot(fs, u, preferred_element_type=F32))
    for i in range(n_blk):
        yre, yim = _mix_block_lags(vre, vim, hre_ref, him_ref, i, 1)
        y = jnp.dot(gc, yre, preferred_element_type=F32) + jnp.dot(gs, yim, preferred_element_type=F32) + b_ref[...]
        y = _layer_norm_rows(y, lg_ref[...], lb_ref[...])
        o_ref[i * P:(i + 1) * P, :] = (y * jax.nn.sigmoid(y)).astype(o_ref.dtype)


def _conformer(proj, tabs, w, b, lg, lb, B, S):
    fc32, fs32, (fc, fs, gc, gs) = tabs
    P = fc.shape[0]
    T = B * S
    C = CONF_CH
    half = CONF_WIDTH // 2
    assert half < P
    span = ((S - half, S - half - 1), (0, 0))
    hre, him = _lag_spectra(fc32, fs32, jnp.stack([jnp.pad(w[::-1], span), jnp.pad(w, span)]), 1)
    tab = pl.BlockSpec((P, P), lambda bi: (0, 0))
    spec = pl.BlockSpec((3 * P, C), lambda bi: (0, 0))
    vec = pl.BlockSpec((1, C), lambda bi: (0, 0))
    return pl.pallas_call(
        functools.partial(_conformer_body, n_blk=S // P),
        grid=(B,),
        in_specs=[pl.BlockSpec((S, C), lambda bi: (bi, 3)),
                  pl.BlockSpec((S, C), lambda bi: (bi, 4)),
                  tab, tab, spec, spec, tab, tab, vec, vec, vec],
        out_specs=pl.BlockSpec((S, C), lambda bi: (bi, 0)),
        out_shape=jax.ShapeDtypeStruct((T, C), BF16),
        compiler_params=_params("parallel"),
        name="conformer_conv",
    )(proj, proj, fc, fs, hre, him, gc, gs, b, lg, lb)


def _filter_body(feat_ref, w1_ref, b1_ref, q1_ref, w2_ref, b2_ref, q2_ref, w3_ref, t_ref, delta_ref, o_ref):
    h = jnp.sin(q1_ref[...] * (_dot3(feat_ref[...], w1_ref[...]) + b1_ref[...]))
    h = jnp.sin(q2_ref[...] * (_dot3(h, w2_ref[...]) + b2_ref[...]))
    h = _dot3(h, w3_ref[...])
    o_ref[...] = h * jnp.exp(-t_ref[...] * delta_ref[...])


def _hyena_filters(feats2, w1, b1, q1, w2, b2, q2, w3, tcol2, deltas):
    S = feats2.shape[1]
    C = HYENA_CH
    fd = HYENA_FILTER_DIM
    fixed = lambda p, d, o: (0, 0)
    return pl.pallas_call(
        _filter_body,
        grid=(2, 2, HYENA_ORDER),
        in_specs=[pl.BlockSpec((None, S, LANES), lambda p, d, o: (p, 0, 0)),
                  pl.BlockSpec((LANES, fd), fixed),
                  pl.BlockSpec((1, fd), fixed),
                  pl.BlockSpec((1, fd), fixed),
                  pl.BlockSpec((fd, fd), fixed),
                  pl.BlockSpec((1, fd), fixed),
                  pl.BlockSpec((1, fd), fixed),
                  pl.BlockSpec((fd, C), lambda p, d, o: (0, d * HYENA_ORDER + o)),
                  pl.BlockSpec((None, S, 1), lambda p, d, o: (p, 0, 0)),
                  pl.BlockSpec((1, C), fixed)],
        out_specs=pl.BlockSpec((None, S, C), lambda p, d, o: ((d + 1 - p) % 2, p, o)),
        out_shape=jax.ShapeDtypeStruct((2, 2 * S, HYENA_ORDER * C), F32),
        compiler_params=_params("parallel", "parallel", "parallel"),
        name="hyena_filter_mlp",
    )(feats2, w1, b1, q1, w2, b2, q2, w3, tcol2, deltas)


def _spectrum_body(fc_ref, fs_ref, a_ref, b_ref, hre_ref, him_ref):
    a = a_ref[...]
    row = lax.broadcasted_iota(I32, a.shape, 0)
    b = jnp.where(row == 0, 0.0, b_ref[...])
    hre_ref[...] = _dot3(fc_ref[...], a + b)
    him_ref[...] = _dot3(fs_ref[...], a - b)


def _lag_spectra(fc32, fs32, h2, d_max):
    P = fc32.shape[0]
    n = h2.shape[1] // (2 * P)
    n_ch = h2.shape[2]
    C = HYENA_CH
    nd = 2 * d_max + 1
    spec = pl.BlockSpec((P, C), lambda di, o: (di, o))
    return pl.pallas_call(
        _spectrum_body,
        grid=(nd, n_ch // C),
        in_specs=[pl.BlockSpec((P, P), lambda di, o: (0, 0)),
                  pl.BlockSpec((P, P), lambda di, o: (0, 0)),
                  pl.BlockSpec((None, P, C), lambda di, o: (0, n - d_max + di, o)),
                  pl.BlockSpec((None, P, C), lambda di, o: (1, n + d_max - di, o))],
        out_specs=[spec, spec],
        out_shape=[jax.ShapeDtypeStruct((nd * P, n_ch), F32)] * 2,
        compiler_params=_params("parallel", "parallel"),
        name="block_lag_spectra",
    )(fc32, fs32, h2, h2)


def _mix_block_lags(vre, vim, hre_ref, him_ref, i, d_max):
    P = vre[0].shape[0]
    yre = yim = None
    for j in range(max(0, i - d_max), min(len(vre), i + d_max + 1)):
        r0 = (i - j + d_max) * P
        hre = hre_ref[r0:r0 + P, :]
        him = him_ref[r0:r0 + P, :]
        tre = vre[j] * hre - vim[j] * him
        tim = vre[j] * him + vim[j] * hre
        yre = tre if yre is None else yre + tre
        yim = tim if yim is None else yim + tim
    return yre.astype(BF16), yim.astype(BF16)


def _long_conv_body(v_ref, gate_ref, fc_ref, fs_ref, hre_ref, him_ref, gc_ref, gs_ref, skip_ref, o_ref, *, n_blk):
    P = fc_ref.shape[0]
    fc, fs, gc, gs = fc_ref[...], fs_ref[...], gc_ref[...], gs_ref[...]
    vre, vim = [], []
    for j in range(n_blk):
        vj = v_ref[j * P:(j + 1) * P, :]
        vre.append(jnp.dot(fc, vj, preferred_element_type=F32))
        vim.append(jnp.dot(fs, vj, preferred_element_type=F32))
    for i in range(n_blk):
        yre, yim = _mix_block_lags(vre, vim, hre_ref, him_ref, i, n_blk - 1)
        y = jnp.dot(gc, yre, preferred_element_type=F32) + jnp.dot(gs, yim, preferred_element_type=F32)
        rows = slice(i * P, (i + 1) * P)
        y = y + v_ref[rows, :].astype(F32) * skip_ref[...]
        o_ref[rows, :] = (gate_ref[rows, :].astype(F32) * y).astype(o_ref.dtype)


def _long_conv(v_arr, v_col, gate_arr, gate_col, tabs, hre, him, order, skip, B, S):
    fc, fs, gc, gs = tabs
    P = fc.shape[0]
    T = B * S
    C = HYENA_CH
    nc = C // CONV_CH_TILE
    cc = CONV_CH_TILE
    n_h = hre.shape[0]
    tab = pl.BlockSpec((P, P), lambda bi, c: (0, 0))
    return pl.pallas_call(
        functools.partial(_long_conv_body, n_blk=S // P),
        grid=(B, nc),
        in_specs=[pl.BlockSpec((S, cc), lambda bi, c: (bi, v_col * nc + c)),
                  pl.BlockSpec((S, cc), lambda bi, c: (bi, gate_col * nc + c)),
                  tab, tab,
                  pl.BlockSpec((n_h, cc), lambda bi, c: (0, order * nc + c)),
                  pl.BlockSpec((n_h, cc), lambda bi, c: (0, order * nc + c)),
                  tab, tab,
                  pl.BlockSpec((1, cc), lambda bi, c: (0, c))],
        out_specs=pl.BlockSpec((S, cc), lambda bi, c: (bi, c)),
        out_shape=jax.ShapeDtypeStruct((T, C), BF16),
        compiler_params=_params("parallel", "parallel"),
        name="hyena_long_conv",
    )(v_arr, gate_arr, fc, fs, hre, him, gc, gs, skip[order][None, :])


def _attn_body(slope_ref, lam_ref, q_ref, k_ref, v_ref, g_ref, o_ref, vaug_ref, *, lam_init, row_chunk, heads):
    hg = pl.program_id(1)
    qi = pl.program_id(2)
    tq = q_ref.shape[0]
    S = k_ref.shape[0]
    hw = 2 * HEAD_DIM

    @pl.when(qi == 0)
    def _():
        ones_col = jnp.where(lax.broadcasted_iota(I32, (S, hw), 1) == 0, 1.0, 0.0).astype(BF16)
        for hh in range(heads):
            vaug_ref[hh, :, :hw] = v_ref[:, hh * hw:(hh + 1) * hw]
            vaug_ref[hh, :, hw:] = ones_col

    for hh in range(heads):
        cols = slice(hh * hw, (hh + 1) * hw)
        k = k_ref[:, cols]
        v_aug = vaug_ref[hh]
        slope = slope_ref[hg * heads + hh]
        kpos = lax.broadcasted_iota(I32, (1, S), 1).astype(F32) * slope
        for r0 in range(0, tq, row_chunk):
            q = q_ref[r0:r0 + row_chunk, cols]
            lane = lax.broadcasted_iota(I32, q.shape, 1)
            zero = jnp.zeros_like(q)
            qpos = (qi * tq + r0 + lax.broadcasted_iota(I32, (row_chunk, 1), 0)).astype(F32) * slope
            bias = lax.bitcast_convert_type(lax.bitcast_convert_type(qpos - kpos, U32) | jnp.uint32(0x80000000), F32)

            def weighted_values(qh):
                s = lax.dot_general(qh, k, _NT, preferred_element_type=F32) + bias
                e = jnp.exp((s - jnp.max(s, axis=-1, keepdims=True)).astype(BF16))
                return jnp.dot(e, v_aug, preferred_element_type=F32)

            o1 = weighted_values(jnp.where(lane < HEAD_DIM, q, zero))
            o2 = weighted_values(jnp.where(lane >= HEAD_DIM, q, zero))
            o = o1[:, :hw] * (1.0 / o1[:, hw:hw + 1]) - o2[:, :hw] * (lam_ref[0] / o2[:, hw:hw + 1])
            o = o * lax.rsqrt(jnp.mean(o * o, axis=-1, keepdims=True) + LN_EPS) * g_ref[...]
            o_ref[r0:r0 + row_chunk, cols] = (o * (1.0 - lam_init)).astype(o_ref.dtype)


def _diff_attention(qkv, slopes, lam, subln_g, lam_init, B, S):
    T = B * S
    hw = 2 * HEAD_DIM
    hp = ATTN_HEADS_PER_STEP
    ng = N_HEADS // hp
    tq = min(S, ATTN_Q_TILE)
    nq = S // tq
    smem = pl.BlockSpec(memory_space=pltpu.SMEM)
    return pl.pallas_call(
        functools.partial(_attn_body, lam_init=lam_init, row_chunk=min(tq, ATTN_ROW_CHUNK), heads=hp),
        grid=(B, ng, nq),
        in_specs=[smem, smem,
                  pl.BlockSpec((tq, hp * hw), lambda bi, h, qi: (bi * nq + qi, h)),
                  pl.BlockSpec((S, hp * hw), lambda bi, h, qi: (bi, ng + h)),
                  pl.BlockSpec((S, hp * hw), lambda bi, h, qi: (bi, 2 * ng + h)),
                  pl.BlockSpec((1, hw), lambda bi, h, qi: (0, 0))],
        out_specs=pl.BlockSpec((tq, hp * hw), lambda bi, h, qi: (bi * nq + qi, h)),
        out_shape=jax.ShapeDtypeStruct((T, ATTN_W), BF16),
        scratch_shapes=[pltpu.VMEM((hp, S, 2 * hw), BF16)],
        compiler_params=_params("parallel", "parallel", "arbitrary"),
        name="diff_attention",
    )(slopes, lam, qkv, qkv, qkv, subln_g)


def _route_block(x, w_hi, w_lo, bias, idx_ref, gate_ref, rank_ref, cnt_ref, cols):
    E = N_EXPERTS
    tm = x.shape[0]
    x_hi, x_lo = _split_bf16(x)
    nt = functools.partial(lax.dot_general, dimension_numbers=_NT, preferred_element_type=F32)
    logits = nt(w_hi, x_hi) + nt(w_lo, x_hi) + nt(w_hi, x_lo) + bias

    eid = lax.broadcasted_iota(I32, (E, tm), 0).astype(F32)
    work = logits
    vals, idxs = [], []
    for _ in range(TOP_K):
        m = jnp.max(work, axis=0, keepdims=True)
        sel = jnp.min(jnp.where(work == m, eid, float(E)), axis=0, keepdims=True)
        vals.append(m)
        idxs.append(sel)
        work = jnp.where(eid == sel, -jnp.inf, work)
    exps = [jnp.exp(v - vals[0]) for v in vals]
    denom = exps[0] + exps[1] + exps[2] + exps[3]

    chosen = jnp.zeros((E, tm), F32)
    for sel in idxs:
        chosen = chosen + jnp.where(eid == sel, 1.0, 0.0)
    earlier = jnp.where(lax.broadcasted_iota(I32, (tm, tm), 0) < lax.broadcasted_iota(I32, (tm, tm), 1), 1.0, 0.0)
    before = jnp.dot(chosen.astype(BF16), earlier.astype(BF16), preferred_element_type=F32)
    for k in range(TOP_K):
        gate_ref[k:k + 1, cols] = exps[k] / denom
        idx_ref[k:k + 1, cols] = idxs[k].astype(I32)
        rank_ref[k:k + 1, cols] = jnp.sum(jnp.where(eid == idxs[k], before, 0.0), axis=0, keepdims=True).astype(I32)
    cnt_ref[...] = jnp.broadcast_to(jnp.sum(chosen, axis=1, keepdims=True), cnt_ref.shape)


def _copy_caps():
    local_rows = TOP_K * ROW_TILE + N_EXPERTS * SEG_ALIGN
    return (local_rows // COPY_ROWS[0],) + (N_EXPERTS,) * (len(COPY_ROWS) - 1)


def _segment_copies(tab_ref, make_copy, slot, wait):
    base = len(COPY_ROWS)
    for ci, (rows, cap) in enumerate(zip(COPY_ROWS, _copy_caps())):
        def body(p, carry, base=base, rows=rows, cap=cap):
            copy = make_copy(slot, tab_ref[0, base + p], tab_ref[0, base + cap + p], rows)
            if wait:
                copy.wait()
            else:
                copy.start()
            return carry
        lax.fori_loop(0, tab_ref[0, ci], body, 0)
        base += 2 * cap


def _dispatch_body(tail_ref, seg_ref, seg_prev_ref, x_ref, idx_ref, rank_ref, loff_ref, xs_hbm, lrow_ref,
                   buf_ref, zero_ref, sem, zsem, *, n_tok_blocks):
    b = pl.program_id(0)
    slot = b % 2
    tm = x_ref.shape[0]
    R = buf_ref.shape[1]

    def seg_copy(s, local_row, dst_row, rows):
        return pltpu.make_async_copy(buf_ref.at[s, pl.ds(pl.multiple_of(local_row, SEG_ALIGN), rows)],
                                     xs_hbm.at[pl.ds(pl.multiple_of(dst_row, SEG_ALIGN), rows)], sem.at[s])

    @pl.when(b == 0)
    def _():
        zero_ref[...] = jnp.zeros_like(zero_ref)
        for e in range(N_EXPERTS):
            fill = pltpu.make_async_copy(
                zero_ref, xs_hbm.at[pl.ds(pl.multiple_of(tail_ref[e], SEG_ALIGN), EXPERT_ROWS)], zsem)
            fill.start()
            fill.wait()

    eid = lax.broadcasted_iota(I32, (N_EXPERTS, tm), 0)
    loff = loff_ref[...].astype(F32)
    rid = lax.broadcasted_iota(I32, (R, tm), 0).astype(jnp.int16)
    sel_t = jnp.zeros((R, tm), BF16)
    for k in range(TOP_K):
        base = jnp.sum(jnp.where(eid == idx_ref[k:k + 1, :], loff, 0.0), axis=0, keepdims=True).astype(I32)
        row = base + rank_ref[k:k + 1, :]
        lrow_ref[k:k + 1, :] = row
        sel_t = jnp.where(rid == row.astype(jnp.int16), jnp.ones((), BF16), sel_t)
    xb = x_ref[...].astype(BF16)
    lo = jnp.dot(sel_t, xb[:, :HALF], preferred_element_type=F32)
    hi = jnp.dot(sel_t, xb[:, HALF:], preferred_element_type=F32)
    packed = (lax.bitcast_convert_type(hi, U32) & jnp.uint32(0xFFFF0000)) | (lax.bitcast_convert_type(lo, U32) >> 16)
    buf_ref[slot] = packed

    @pl.when(b >= 1)
    def _():
        _segment_copies(seg_prev_ref, seg_copy, 1 - slot, wait=True)
    _segment_copies(seg_ref, seg_copy, slot, wait=False)

    @pl.when(b == n_tok_blocks - 1)
    def _():
        _segment_copies(seg_ref, seg_copy, slot, wait=True)


def _dispatch(x, idx, rank, loff, seg_table, tail_start, n_rows):
    T = x.shape[0]
    tm = ROW_TILE
    nb = T // tm
    R = TOP_K * tm + N_EXPERTS * SEG_ALIGN
    tok = pl.BlockSpec((TOP_K, tm), lambda b, tl: (0, b))
    seg_w = seg_table.shape[-1]
    return pl.pallas_call(
        functools.partial(_dispatch_body, n_tok_blocks=nb),
        grid_spec=pltpu.PrefetchScalarGridSpec(
            num_scalar_prefetch=1,
            grid=(nb,),
            in_specs=[pl.BlockSpec((None, 1, seg_w), lambda b, tl: (b, 0, 0), memory_space=pltpu.SMEM),
                      pl.BlockSpec((None, 1, seg_w), lambda b, tl: (jnp.maximum(b - 1, 0), 0, 0),
                                   memory_space=pltpu.SMEM),
                      pl.BlockSpec((tm, D_MODEL), lambda b, tl: (b, 0)),
                      tok, tok,
                      pl.BlockSpec((None, N_EXPERTS, 1), lambda b, tl: (b, 0, 0))],
            out_specs=[pl.BlockSpec(memory_space=pl.ANY), tok],
            scratch_shapes=[pltpu.VMEM((2, R, HALF), U32), pltpu.VMEM((EXPERT_ROWS, HALF), U32),
                            pltpu.SemaphoreType.DMA((2,)), pltpu.SemaphoreType.DMA(())]),
        out_shape=[jax.ShapeDtypeStruct((n_rows, HALF), U32), jax.ShapeDtypeStruct((TOP_K, T), I32)],
        compiler_params=_params("arbitrary"),
        name="moe_dispatch",
    )(tail_start, seg_table, seg_table, x, idx, rank, loff)


def _ffn_body(be_ref, nu_ref, xs_ref, w1_ref, b1_ref, w2_ref, b2_ref, ys_ref, w1b_ref, w2b_ref):
    i = pl.program_id(0)
    used = i < nu_ref[0]
    fresh = jnp.logical_or(i == 0, be_ref[i] != be_ref[jnp.maximum(i - 1, 0)])

    @pl.when(jnp.logical_and(used, fresh))
    def _():
        w1b_ref[...] = w1_ref[...].astype(BF16)
        w2b_ref[...] = w2_ref[...].astype(BF16)

    @pl.when(used)
    def _():
        for r0 in range(0, EXPERT_ROWS, FFN_ROW_CHUNK):
            rows = slice(r0, r0 + FFN_ROW_CHUNK)
            lo, hi = _unpack_halves(xs_ref[rows, :])
            h = (jnp.dot(lo.astype(BF16), w1b_ref[0:HALF, :], preferred_element_type=F32)
                 + jnp.dot(hi.astype(BF16), w1b_ref[HALF:D_MODEL, :], preferred_element_type=F32) + b1_ref[...])
            hb = h.astype(BF16)
            hg = jnp.minimum(hb[:, :D_FF], SWIGLU_LIMIT)
            hu = jnp.clip(hb[:, D_FF:], -SWIGLU_LIMIT, SWIGLU_LIMIT)
            act = (hu + 1.0) * (hg * jax.nn.sigmoid(hg * SWIGLU_ALPHA))
            y = jnp.dot(act.astype(BF16), w2b_ref[...], preferred_element_type=F32) + b2_ref[...]
            ys_ref[rows, :] = _pack_halves(y)


def _expert_ffn(xs, blk_expert, n_used, w1, b1, w2, b2, layer, n_blocks):
    rows = pl.BlockSpec((EXPERT_ROWS, HALF), lambda i, be, nu: (jnp.minimum(i, nu[0] - 1), 0))
    return pl.pallas_call(
        _ffn_body,
        grid_spec=pltpu.PrefetchScalarGridSpec(
            num_scalar_prefetch=2,
            grid=(n_blocks,),
            in_specs=[rows,
                      pl.BlockSpec((None, None, D_MODEL, 2 * D_FF), lambda i, be, nu: (layer, be[i], 0, 0)),
                      pl.BlockSpec((None, None, 1, 2 * D_FF), lambda i, be, nu: (layer, be[i], 0, 0)),
                      pl.BlockSpec((None, None, D_FF, D_MODEL), lambda i, be, nu: (layer, be[i], 0, 0)),
                      pl.BlockSpec((None, None, 1, D_MODEL), lambda i, be, nu: (layer, be[i], 0, 0))],
            out_specs=rows,
            scratch_shapes=[pltpu.VMEM((D_MODEL, 2 * D_FF), BF16), pltpu.VMEM((D_FF, D_MODEL), BF16)]),
        out_shape=jax.ShapeDtypeStruct((xs.shape[0], HALF), U32),
        compiler_params=_params("arbitrary"),
        name="moe_expert_ffn",
    )(blk_expert, n_used, xs, w1, b1, w2, b2)


def _combine_body(seg_ref, seg_next_ref, lrow_ref, gate_ref, x_ref, g_ref, beta_ref, ys_hbm, xo_ref,
                  buf_ref, sem, *, n_tok_blocks):
    b = pl.program_id(0)
    slot = b % 2
    tm = x_ref.shape[0]
    R = buf_ref.shape[1]

    def seg_copy(s, local_row, src_row, rows):
        return pltpu.make_async_copy(ys_hbm.at[pl.ds(pl.multiple_of(src_row, SEG_ALIGN), rows)],
                                     buf_ref.at[s, pl.ds(pl.multiple_of(local_row, SEG_ALIGN), rows)], sem.at[s])

    @pl.when(b == 0)
    def _():
        buf_ref[...] = jnp.zeros_like(buf_ref)
        _segment_copies(seg_ref, seg_copy, 0, wait=False)

    @pl.when(b + 1 < n_tok_blocks)
    def _():
        _segment_copies(seg_next_ref, seg_copy, 1 - slot, wait=False)

    _segment_copies(seg_ref, seg_copy, slot, wait=True)

    lo, hi = _unpack_halves(buf_ref[slot])
    lo = lo.astype(BF16)
    hi = hi.astype(BF16)
    tc = tm // 2
    cid = lax.broadcasted_iota(I32, (tc, R), 1).astype(jnp.int16)
    lrow_t = lrow_ref[...].astype(F32).T
    gate_t = gate_ref[...].T
    for t0 in range(0, tm, tc):
        lrow = lrow_t[t0:t0 + tc, :].astype(jnp.int16)
        gates = gate_t[t0:t0 + tc, :].astype(BF16)
        sel = jnp.zeros((tc, R), BF16)
        for k in range(TOP_K):
            sel = jnp.where(cid == lrow[:, k:k + 1], gates[:, k:k + 1], sel)
        f = jnp.concatenate([jnp.dot(sel, lo, preferred_element_type=F32),
                             jnp.dot(sel, hi, preferred_element_type=F32)], axis=1)
        xo_ref[t0:t0 + tc, :] = _layer_norm_rows(DEEPNORM_ALPHA * x_ref[t0:t0 + tc, :] + f, g_ref[...], beta_ref[...])


def _combine_ln(lrow, gate, x, g, beta, ys, seg_table):
    T = x.shape[0]
    tm = ROW_TILE
    nb = T // tm
    R = TOP_K * tm + N_EXPERTS * SEG_ALIGN
    seg_w = seg_table.shape[-1]
    return pl.pallas_call(
        functools.partial(_combine_body, n_tok_blocks=nb),
        grid=(nb,),
        in_specs=[pl.BlockSpec((None, 1, seg_w), lambda b: (b, 0, 0), memory_space=pltpu.SMEM),
                  pl.BlockSpec((None, 1, seg_w), lambda b: (jnp.minimum(b + 1, nb - 1), 0, 0),
                               memory_space=pltpu.SMEM),
                  pl.BlockSpec((TOP_K, tm), lambda b: (0, b)),
                  pl.BlockSpec((TOP_K, tm), lambda b: (0, b)),
                  pl.BlockSpec((tm, D_MODEL), lambda b: (b, 0)),
                  pl.BlockSpec((1, D_MODEL), lambda b: (0, 0)),
                  pl.BlockSpec((1, D_MODEL), lambda b: (0, 0)),
                  pl.BlockSpec(memory_space=pl.ANY)],
        out_specs=pl.BlockSpec((tm, D_MODEL), lambda b: (b, 0)),
        out_shape=jax.ShapeDtypeStruct((T, D_MODEL), F32),
        scratch_shapes=[pltpu.VMEM((2, R, HALF), U32), pltpu.SemaphoreType.DMA((2,))],
        compiler_params=_params("arbitrary"),
        name="moe_combine_ln",
    )(seg_table, seg_table, lrow, gate, x, g, beta, ys)


def _dft_tables(P):
    n2 = 4 * P
    k = jnp.arange(P, dtype=I32)
    m = ((2 * k[:, None] + 1) * k[None, :]) % n2
    ang = m.astype(F32) * F32(2.0 * math.pi / n2)
    fc32, fs32 = jnp.cos(ang), -jnp.sin(ang)
    scale = F32(1.0 / P)
    gc, gs = (fc32.T * scale).astype(BF16), (fs32.T * scale).astype(BF16)
    return fc32, fs32, (fc32.astype(BF16), fs32.astype(BF16), gc, gs)


def _hyena_positional(S):
    pos = jnp.arange(S, dtype=F32)
    t = jnp.linspace(0.0, 1.0, S, dtype=F32)[:, None]
    bands = (HYENA_EMB_DIM - 1) // 2
    f = jnp.linspace(1e-4, bands - 1, bands, dtype=F32)
    ang = (2.0 * math.pi / S) * pos[:, None] * f[None, :]
    feats = jnp.concatenate([t, jnp.cos(ang), -jnp.sin(ang)], axis=-1)
    feats = jnp.pad(feats, ((0, 0), (0, LANES - HYENA_EMB_DIM)))
    max_decay = math.log(HYENA_DECAY_TARGET) / HYENA_SHORT_DECAY_PCT
    min_decay = math.log(HYENA_DECAY_TARGET) / HYENA_LONG_DECAY_PCT
    deltas = jnp.abs(jnp.linspace(min_decay, max_decay, HYENA_CH, dtype=F32))[None, :]
    return feats, t, deltas


def _alibi_slopes():
    return jnp.asarray(np.array([2.0 ** (-8.0 * (i + 1) / N_HEADS) for i in range(N_HEADS)], dtype=np.float32))


def _even_mixer(x, xshape, tabs, w_in, b_in, short_w, short_b, f1_w, f1_b, f1_freq, f2_w, f2_b, f2_freq, f3_w,
                skip, dw_w, dw_b, cln_g, cln_b, w_out, b_out, ln_g, ln_b, w_r, b_r):
    B, S = xshape
    fc32, fs32, tabs16 = tabs
    proj = _project(x, w_in.astype(BF16), b_in[None, :])
    hy = _short_conv(proj, short_w, short_b[None, :], B, S)
    u = _conformer(proj, tabs, dw_w, dw_b[None, :], cln_g[None, :], cln_b[None, :], B, S)
    feats, tcol, deltas = _hyena_positional(S)
    f1_wp = jnp.pad(f1_w, ((0, LANES - HYENA_EMB_DIM), (0, 0)))
    flip = lambda a: jnp.concatenate([a[:1], a[:0:-1]], axis=0)
    h2 = _hyena_filters(jnp.stack([flip(feats), feats]), f1_wp, f1_b[None, :], f1_freq[None, :], f2_w, f2_b[None, :],
                        f2_freq[None, :], f3_w, jnp.stack([flip(tcol), tcol]), deltas)
    hre, him = _lag_spectra(fc32, fs32, h2, S // fc32.shape[0] - 1)
    z = _long_conv(hy, 2, hy, 0, tabs16, hre, him, 0, skip, B, S)
    z = _long_conv(z, 0, hy, 1, tabs16, hre, him, 1, skip, B, S)
    return _outproj_ln(z, 0, u, 0, w_out.astype(BF16), b_out[None, :], x, ln_g[None, :], ln_b[None, :],
                       w_r.T, b_r[:, None])


def _odd_mixer(x, xshape, layer_idx, w_qkv, lq1, lk1, lq2, lk2, subln_g, w_out, ln_g, ln_b, w_r, b_r):
    B, S = xshape
    lam_init = 0.8 - 0.6 * math.exp(-0.3 * layer_idx)
    lam = (jnp.exp(jnp.sum(lq1 * lk1)) - jnp.exp(jnp.sum(lq2 * lk2)) + lam_init).reshape(1)
    q_scale = jnp.concatenate([jnp.full((ATTN_W,), HEAD_DIM ** -0.5, F32), jnp.ones((2 * ATTN_W,), F32)])
    w = (w_qkv * q_scale).astype(BF16)
    qkv = _project(x, w, jnp.zeros((1, 3 * ATTN_W), F32))
    o = _diff_attention(qkv, _alibi_slopes(), lam, subln_g[None, :], lam_init, B, S)
    return _outproj_ln(o, 0, o, 1, w_out.astype(BF16), jnp.zeros((1, D_MODEL), F32), x, ln_g[None, :], ln_b[None, :],
                       w_r.T, b_r[:, None])


def _round_up(a, m):
    return (a + m - 1) // m * m


def _copy_lists(loff, goff, units):
    E = N_EXPERTS
    caps = _copy_caps()
    big = COPY_ROWS[0]
    n_big = units // (big // SEG_ALIGN)
    cum = jnp.cumsum(n_big, axis=1)
    first = (cum - n_big)[:, None, :]
    p = jnp.arange(caps[0], dtype=I32)[None, :, None]
    mine = (first <= p) & (p < cum[:, None, :])
    within = (p - first) * big
    counts = [cum[:, -1]]
    cols = [jnp.sum(jnp.where(mine, loff[:, None, :] + within, 0), axis=2),
            jnp.sum(jnp.where(mine, goff[:, None, :] + within, 0), axis=2)]
    off = n_big * big
    p = jnp.arange(E, dtype=I32)[None, :, None]
    for rows in COPY_ROWS[1:]:
        has = (units & (rows // SEG_ALIGN)) != 0
        pos = jnp.cumsum(has.astype(I32), axis=1) - has.astype(I32)
        mine = has[:, None, :] & (pos[:, None, :] == p)
        counts.append(jnp.sum(has.astype(I32), axis=1))
        cols += [jnp.sum(jnp.where(mine, (loff + off)[:, None, :], 0), axis=2),
                 jnp.sum(jnp.where(mine, (goff + off)[:, None, :], 0), axis=2)]
        off = off + jnp.where(has, rows, 0)
    return jnp.concatenate([jnp.stack(counts, axis=1)] + cols, axis=1).astype(I32)


def _routing_tables(cnt_blocks, n_ffn_blocks):
    E = N_EXPERTS
    cnt8 = _round_up(cnt_blocks[:, :, 0].astype(I32), SEG_ALIGN)
    seg_end = jnp.cumsum(cnt8, axis=1)
    loff = seg_end - cnt8
    tot8 = jnp.sum(cnt8, axis=0)
    group = _round_up(tot8, EXPERT_ROWS)
    group_end = jnp.cumsum(group)
    group_start = group_end - group
    goff = group_start[None, :] + jnp.cumsum(cnt8, axis=0) - cnt8
    seg_table = _copy_lists(loff, goff, cnt8 // SEG_ALIGN)
    starts = jnp.arange(n_ffn_blocks, dtype=I32) * EXPERT_ROWS
    blk_expert = jnp.minimum(jnp.sum((group_end[None, :] <= starts[:, None]).astype(I32), axis=1), E - 1)
    n_used = group_end[-1:] // EXPERT_ROWS
    tail_start = group_start + tot8
    return loff[:, :, None], seg_table[:, None, :], blk_expert, n_used, tail_start


def _moe_layer(x, routing, layer, w1, b1, w2, b2, ln_g, ln_b):
    T = x.shape[0]
    nb = T // ROW_TILE
    n_rows = _round_up(T * TOP_K + nb * N_EXPERTS * (SEG_ALIGN - 1), EXPERT_ROWS) + N_EXPERTS * EXPERT_ROWS
    n_ffn_blocks = n_rows // EXPERT_ROWS
    idx, gate, rank, cnt = routing
    loff, seg_table, blk_expert, n_used, tail_start = _routing_tables(cnt, n_ffn_blocks)
    xs, lrow = _dispatch(x, idx, rank, loff, seg_table, tail_start, n_rows + EXPERT_ROWS)
    ys = _expert_ffn(xs, blk_expert, n_used, w1, b1[:, :, None, :], w2, b2[:, :, None, :], layer, n_ffn_blocks)
    return _combine_ln(lrow, gate, x, ln_g[None, :], ln_b[None, :], ys, seg_table)


def kernel(x, hy_cf_w_in, hy_cf_b_in, hy_short_w, hy_short_b, hy_f1_w, hy_f1_b, hy_f1_freq, hy_f2_w, hy_f2_b, hy_f2_freq, hy_f3_w, hy_skip, cf_dw_w, cf_dw_b, cf_ln_g, cf_ln_b, even_w_out, even_b_out, attn_w_qkv, attn_lq1, attn_lk1, attn_lq2, attn_lk2, attn_subln_g, attn_w_out, ln1_g, ln1_b, ln2_g, ln2_b, moe_w_r, moe_b_r, moe_w1, moe_b1, moe_w2, moe_b2):
    B, S, D = x.shape
    assert D == D_MODEL and S % LANES == 0
    assert (B * S) % ROW_TILE == 0 and (B * S) % min(B * S, DENSE_ROW_TILE) == 0
    depth = ln1_g.shape[0]
    xf = x.reshape(B * S, D)
    tabs = _dft_tables(S // CONV_BLOCKS)
    for i in range(depth):
        j = i // 2
        if i % 2 == 0:
            xf, routing = _even_mixer(xf, (B, S), tabs, hy_cf_w_in[j], hy_cf_b_in[j], hy_short_w[j], hy_short_b[j],
                                      hy_f1_w[j], hy_f1_b[j], hy_f1_freq[j], hy_f2_w[j], hy_f2_b[j], hy_f2_freq[j],
                                      hy_f3_w[j], hy_skip[j], cf_dw_w[j], cf_dw_b[j], cf_ln_g[j], cf_ln_b[j],
                                      even_w_out[j], even_b_out[j], ln1_g[i], ln1_b[i], moe_w_r[i], moe_b_r[i])
        else:
            xf, routing = _odd_mixer(xf, (B, S), i, attn_w_qkv[j], attn_lq1[j], attn_lk1[j], attn_lq2[j],
                                     attn_lk2[j], attn_subln_g[j], attn_w_out[j], ln1_g[i], ln1_b[i],
                                     moe_w_r[i], moe_b_r[i])
        xf = _moe_layer(xf, routing, i, moe_w1, moe_b1, moe_w2, moe_b2, ln2_g[i], ln2_b[i])
    return xf.reshape(B, S, D)
```

```python
import functools
import math

import jax
import jax.numpy as jnp
import numpy as np
from jax import lax
from jax.experimental import pallas as pl
from jax.experimental.pallas import tpu as pltpu

F32 = jnp.float32
BF16 = jnp.bfloat16
U32 = jnp.uint32
I32 = jnp.int32

D_MODEL = 1024
HALF = D_MODEL // 2
DEPTH = 4
HYENA_CH = D_MODEL // 2
CONF_CH = D_MODEL // 2
HYENA_ORDER = 2
HYENA_EMB_DIM = 33
HYENA_FILTER_DIM = 64
HYENA_SHORT_DECAY_PCT = 0.3
HYENA_LONG_DECAY_PCT = 1.5
HYENA_DECAY_TARGET = 1e-2
CONF_WIDTH = 31
EVEN_IN = 3 * HYENA_CH + 2 * CONF_CH
N_HEADS = 8
HEAD_DIM = 64
ATTN_W = N_HEADS * 2 * HEAD_DIM
N_EXPERTS = 32
TOP_K = 4
D_FF = D_MODEL
SWIGLU_LIMIT = 7.0
SWIGLU_ALPHA = 1.702
DEEPNORM_ALPHA = (2 * DEPTH) ** 0.25
LN_EPS = 1e-5

LANES = 128
VMEM_LIMIT_BYTES = 56 * 1024 * 1024
ROW_TILE = 512
DENSE_ROW_TILE = 1024
EXPERT_ROWS = 512
FFN_ROW_CHUNK = 512
SEG_ALIGN = 8
COPY_ROWS = (64, 32, 16, 8)
CONV_BLOCKS = 4
CONV_CH_TILE = 256
ATTN_Q_TILE = 2048
ATTN_HEADS_PER_STEP = 1
ATTN_ROW_CHUNK = 256

_NT = (((1,), (1,)), ((), ()))


def _params(*sem):
    return pltpu.CompilerParams(dimension_semantics=sem, vmem_limit_bytes=VMEM_LIMIT_BYTES)


def _split_bf16(a):
    hi = a.astype(BF16)
    lo = (a - hi.astype(F32)).astype(BF16)
    return hi, lo


def _dot3(a, b):
    a_hi, a_lo = _split_bf16(a)
    b_hi, b_lo = _split_bf16(b)
    d = functools.partial(jnp.dot, preferred_element_type=F32)
    return d(a_hi, b_hi) + d(a_hi, b_lo) + d(a_lo, b_hi)


def _layer_norm_rows(y, g, b):
    mu = jnp.mean(y, axis=-1, keepdims=True)
    yc = y - mu
    var = jnp.mean(yc * yc, axis=-1, keepdims=True)
    return yc * lax.rsqrt(var + LN_EPS) * g + b


def _pack_halves(y):
    lo = lax.bitcast_convert_type(y[:, :HALF].astype(BF16).astype(F32), U32)
    hi = lax.bitcast_convert_type(y[:, HALF:].astype(BF16).astype(F32), U32)
    return hi | (lo >> 16)


def _unpack_halves(p):
    lo = lax.bitcast_convert_type(p << 16, F32)
    hi = lax.bitcast_convert_type(p & jnp.uint32(0xFFFF0000), F32)
    return lo, hi


def _proj_body(x_ref, w_ref, b_ref, o_ref, *, col_chunk):
    x = x_ref[...].astype(BF16)
    for j in range(0, o_ref.shape[1], col_chunk):
        acc = jnp.dot(x, w_ref[:, j:j + col_chunk], preferred_element_type=F32)
        o_ref[:, j:j + col_chunk] = (acc + b_ref[:, j:j + col_chunk]).astype(o_ref.dtype)


def _project(x, w, b):
    T, K = x.shape
    N = w.shape[1]
    tm = min(T, DENSE_ROW_TILE)
    return pl.pallas_call(
        functools.partial(_proj_body, col_chunk=512),
        grid=(T // tm,),
        in_specs=[pl.BlockSpec((tm, K), lambda i: (i, 0)),
                  pl.BlockSpec((K, N), lambda i: (0, 0)),
                  pl.BlockSpec((1, N), lambda i: (0, 0))],
        out_specs=pl.BlockSpec((tm, N), lambda i: (i, 0)),
        out_shape=jax.ShapeDtypeStruct((T, N), BF16),
        compiler_params=_params("parallel"),
        name="project",
    )(x, w, b)


def _outproj_ln_body(a1_ref, a2_ref, w1_ref, w2_ref, b_ref, x_ref, g_ref, beta_ref, wr_ref, br_ref,
                     xo_ref, idx_ref, gate_ref, rank_ref, cnt_ref):
    m = (jnp.dot(a1_ref[...], w1_ref[...], preferred_element_type=F32)
         + jnp.dot(a2_ref[...], w2_ref[...], preferred_element_type=F32) + b_ref[...])
    y = _layer_norm_rows(DEEPNORM_ALPHA * x_ref[...] + m, g_ref[...], beta_ref[...])
    xo_ref[...] = y
    w_hi, w_lo = _split_bf16(wr_ref[...])
    for s in range(y.shape[0] // ROW_TILE):
        cols = slice(s * ROW_TILE, (s + 1) * ROW_TILE)
        _route_block(y[cols, :], w_hi, w_lo, br_ref[...], idx_ref, gate_ref, rank_ref, cnt_ref.at[s], cols)


def _outproj_ln(a1, a1_col, a2, a2_col, w, b, x, g, beta, w_rt, b_r):
    T = x.shape[0]
    E = N_EXPERTS
    tm = min(T, DENSE_ROW_TILE)
    nsub = tm // ROW_TILE
    tok = pl.BlockSpec((TOP_K, tm), lambda i: (0, i))
    out = pl.pallas_call(
        _outproj_ln_body,
        grid=(T // tm,),
        in_specs=[pl.BlockSpec((tm, HALF), lambda i: (i, a1_col)),
                  pl.BlockSpec((tm, HALF), lambda i: (i, a2_col)),
                  pl.BlockSpec((HALF, D_MODEL), lambda i: (0, 0)),
                  pl.BlockSpec((HALF, D_MODEL), lambda i: (1, 0)),
                  pl.BlockSpec((1, D_MODEL), lambda i: (0, 0)),
                  pl.BlockSpec((tm, D_MODEL), lambda i: (i, 0)),
                  pl.BlockSpec((1, D_MODEL), lambda i: (0, 0)),
                  pl.BlockSpec((1, D_MODEL), lambda i: (0, 0)),
                  pl.BlockSpec((E, D_MODEL), lambda i: (0, 0)),
                  pl.BlockSpec((E, 1), lambda i: (0, 0))],
        out_specs=[pl.BlockSpec((tm, D_MODEL), lambda i: (i, 0)), tok, tok, tok,
                   pl.BlockSpec((nsub, E, LANES), lambda i: (i, 0, 0))],
        out_shape=[jax.ShapeDtypeStruct((T, D_MODEL), F32),
                   jax.ShapeDtypeStruct((TOP_K, T), I32), jax.ShapeDtypeStruct((TOP_K, T), F32),
                   jax.ShapeDtypeStruct((TOP_K, T), I32), jax.ShapeDtypeStruct((T // ROW_TILE, E, LANES), F32)],
        compiler_params=_params("parallel"),
        name="outproj_ln_route",
    )(a1, a2, w, w, b, x, g, beta, w_rt, b_r)
    return out[0], tuple(out[1:])


def _short_conv_body(x_ref, w_ref, b_ref, o_ref):
    x = x_ref[...].astype(F32)
    S = x.shape[0]
    row = lax.broadcasted_iota(I32, x.shape, 0)
    prev = jnp.where(row == 0, 0.0, pltpu.roll(x, 1, 0))
    nxt = jnp.where(row == S - 1, 0.0, pltpu.roll(x, S - 1, 0))
    y = w_ref[0:1, :] * prev + w_ref[1:2, :] * x + w_ref[2:3, :] * nxt + b_ref[...]
    o_ref[...] = y.astype(o_ref.dtype)


def _short_conv(proj, w, b, B, S):
    T = B * S
    C = HYENA_CH
    return pl.pallas_call(
        _short_conv_body,
        grid=(B, 3),
        in_specs=[pl.BlockSpec((S, C), lambda bi, j: (bi, j)),
                  pl.BlockSpec((3, C), lambda bi, j: (0, j)),
                  pl.BlockSpec((1, C), lambda bi, j: (0, j))],
        out_specs=pl.BlockSpec((S, C), lambda bi, j: (bi, j)),
        out_shape=jax.ShapeDtypeStruct((T, 3 * C), BF16),
        compiler_params=_params("parallel", "parallel"),
        name="hyena_short_conv",
    )(proj, w, b)


def _conformer_body(a_ref, g_ref, fc_ref, fs_ref, hre_ref, him_ref, gc_ref, gs_ref, b_ref, lg_ref, lb_ref, o_ref,
                    *, n_blk):
    P = fc_ref.shape[0]
    fc, fs, gc, gs = fc_ref[...], fs_ref[...], gc_ref[...], gs_ref[...]
    vre, vim = [], []
    for j in range(n_blk):
        rows = slice(j * P, (j + 1) * P)
        u = (a_ref[rows, :].astype(F32) * jax.nn.sigmoid(g_ref[rows, :].astype(F32))).astype(BF16)
        vre.append(jnp.dot(fc, u, preferred_element_type=F32).astype(BF16))
        vim.append(jnp.dot(fs, u, preferred_element_type=F32).astype(BF16))
    for i in range(n_blk):
        yre, yim = _mix_block_lags(vre, vim, hre_ref, him_ref, i, 1)
        y = jnp.dot(gc, yre, preferred_element_type=F32) + jnp.dot(gs, yim, preferred_element_type=F32) + b_ref[...]
        y = _layer_norm_rows(y, lg_ref[...], lb_ref[...])
        o_ref[i * P:(i + 1) * P, :] = (y * jax.nn.sigmoid(y)).astype(o_ref.dtype)


def _conformer(proj, tabs, w, b, lg, lb, B, S):
    fc32, fs32, (fc, fs, gc, gs) = tabs
    P = fc.shape[0]
    T = B * S
    C = CONF_CH
    half = CONF_WIDTH // 2
    assert half < P
    span = ((S - half, S - half - 1), (0, 0))
    hre, him = _lag_spectra(fc32, fs32, jnp.stack([jnp.pad(w[::-1], span), jnp.pad(w, span)]), 1)
    tab = pl.BlockSpec((P, P), lambda bi: (0, 0))
    spec = pl.BlockSpec((3 * P, C), lambda bi: (0, 0))
    vec = pl.BlockSpec((1, C), lambda bi: (0, 0))
    return pl.pallas_call(
        functools.partial(_conformer_body, n_blk=S // P),
        grid=(B,),
        in_specs=[pl.BlockSpec((S, C), lambda bi: (bi, 3)),
                  pl.BlockSpec((S, C), lambda bi: (bi, 4)),
                  tab, tab, spec, spec, tab, tab, vec, vec, vec],
        out_specs=pl.BlockSpec((S, C), lambda bi: (bi, 0)),
        out_shape=jax.ShapeDtypeStruct((T, C), BF16),
        compiler_params=_params("parallel"),
        name="conformer_conv",
    )(proj, proj, fc, fs, hre, him, gc, gs, b, lg, lb)


def _filter_body(feat_ref, w1_ref, b1_ref, q1_ref, w2_ref, b2_ref, q2_ref, w3_ref, t_ref, delta_ref, o_ref):
    h = jnp.sin(q1_ref[...] * (_dot3(feat_ref[...], w1_ref[...]) + b1_ref[...]))
    h = jnp.sin(q2_ref[...] * (_dot3(h, w2_ref[...]) + b2_ref[...]))
    h = _dot3(h, w3_ref[...])
    o_ref[...] = h * jnp.exp(-t_ref[...] * delta_ref[...])


def _hyena_filters(feats2, w1, b1, q1, w2, b2, q2, w3, tcol2, deltas):
    S = feats2.shape[1]
    C = HYENA_CH
    fd = HYENA_FILTER_DIM
    fixed = lambda p, d, o: (0, 0)
    return pl.pallas_call(
        _filter_body,
        grid=(2, 2, HYENA_ORDER),
        in_specs=[pl.BlockSpec((None, S, LANES), lambda p, d, o: (p, 0, 0)),
                  pl.BlockSpec((LANES, fd), fixed),
                  pl.BlockSpec((1, fd), fixed),
                  pl.BlockSpec((1, fd), fixed),
                  pl.BlockSpec((fd, fd), fixed),
                  pl.BlockSpec((1, fd), fixed),
                  pl.BlockSpec((1, fd), fixed),
                  pl.BlockSpec((fd, C), lambda p, d, o: (0, d * HYENA_ORDER + o)),
                  pl.BlockSpec((None, S, 1), lambda p, d, o: (p, 0, 0)),
                  pl.BlockSpec((1, C), fixed)],
        out_specs=pl.BlockSpec((None, S, C), lambda p, d, o: ((d + 1 - p) % 2, p, o)),
        out_shape=jax.ShapeDtypeStruct((2, 2 * S, HYENA_ORDER * C), F32),
        compiler_params=_params("parallel", "parallel", "parallel"),
        name="hyena_filter_mlp",
    )(feats2, w1, b1, q1, w2, b2, q2, w3, tcol2, deltas)


def _spectrum_body(fc_ref, fs_ref, a_ref, b_ref, hre_ref, him_ref):
    a = a_ref[...]
    row = lax.broadcasted_iota(I32, a.shape, 0)
    b = jnp.where(row == 0, 0.0, b_ref[...])
    hre_ref[...] = _dot3(fc_ref[...], a + b)
    him_ref[...] = _dot3(fs_ref[...], a - b)


def _lag_spectra(fc32, fs32, h2, d_max):
    P = fc32.shape[0]
    n = h2.shape[1] // (2 * P)
    n_ch = h2.shape[2]
    C = HYENA_CH
    nd = 2 * d_max + 1
    spec = pl.BlockSpec((P, C), lambda di, o: (di, o))
    return pl.pallas_call(
        _spectrum_body,
        grid=(nd, n_ch // C),
        in_specs=[pl.BlockSpec((P, P), lambda di, o: (0, 0)),
                  pl.BlockSpec((P, P), lambda di, o: (0, 0)),
                  pl.BlockSpec((None, P, C), lambda di, o: (0, n - d_max + di, o)),
                  pl.BlockSpec((None, P, C), lambda di, o: (1, n + d_max - di, o))],
        out_specs=[spec, spec],
        out_shape=[jax.ShapeDtypeStruct((nd * P, n_ch), F32)] * 2,
        compiler_params=_params("parallel", "parallel"),
        name="block_lag_spectra",
    )(fc32, fs32, h2, h2)


def _mix_block_lags(vre, vim, hre_ref, him_ref, i, d_max):
    P = vre[0].shape[0]
    yre = yim = None
    for j in range(max(0, i - d_max), min(len(vre), i + d_max + 1)):
        r0 = (i - j + d_max) * P
        hre = hre_ref[r0:r0 + P, :].astype(BF16)
        him = him_ref[r0:r0 + P, :].astype(BF16)
        tre = vre[j] * hre - vim[j] * him
        tim = vre[j] * him + vim[j] * hre
        yre = tre if yre is None else yre + tre
        yim = tim if yim is None else yim + tim
    return yre, yim


def _long_conv_body(v_ref, gate_ref, fc_ref, fs_ref, hre_ref, him_ref, gc_ref, gs_ref, skip_ref, o_ref, *, n_blk):
    P = fc_ref.shape[0]
    fc, fs, gc, gs = fc_ref[...], fs_ref[...], gc_ref[...], gs_ref[...]
    vre, vim = [], []
    for j in range(n_blk):
        vj = v_ref[j * P:(j + 1) * P, :]
        vre.append(jnp.dot(fc, vj, preferred_element_type=F32).astype(BF16))
        vim.append(jnp.dot(fs, vj, preferred_element_type=F32).astype(BF16))
    for i in range(n_blk):
        yre, yim = _mix_block_lags(vre, vim, hre_ref, him_ref, i, n_blk - 1)
        y = jnp.dot(gc, yre, preferred_element_type=F32) + jnp.dot(gs, yim, preferred_element_type=F32)
        rows = slice(i * P, (i + 1) * P)
        y = y + v_ref[rows, :].astype(F32) * skip_ref[...]
        o_ref[rows, :] = (gate_ref[rows, :].astype(F32) * y).astype(o_ref.dtype)


def _long_conv(v_arr, v_col, gate_arr, gate_col, tabs, hre, him, order, skip, B, S):
    fc, fs, gc, gs = tabs
    P = fc.shape[0]
    T = B * S
    C = HYENA_CH
    nc = C // CONV_CH_TILE
    cc = CONV_CH_TILE
    n_h = hre.shape[0]
    tab = pl.BlockSpec((P, P), lambda bi, c: (0, 0))
    return pl.pallas_call(
        functools.partial(_long_conv_body, n_blk=S // P),
        grid=(B, nc),
        in_specs=[pl.BlockSpec((S, cc), lambda bi, c: (bi, v_col * nc + c)),
                  pl.BlockSpec((S, cc), lambda bi, c: (bi, gate_col * nc + c)),
                  tab, tab,
                  pl.BlockSpec((n_h, cc), lambda bi, c: (0, order * nc + c)),
                  pl.BlockSpec((n_h, cc), lambda bi, c: (0, order * nc + c)),
                  tab, tab,
                  pl.BlockSpec((1, cc), lambda bi, c: (0, c))],
        out_specs=pl.BlockSpec((S, cc), lambda bi, c: (bi, c)),
        out_shape=jax.ShapeDtypeStruct((T, C), BF16),
        compiler_params=_params("parallel", "parallel"),
        name="hyena_long_conv",
    )(v_arr, gate_arr, fc, fs, hre, him, gc, gs, skip[order][None, :])


def _attn_body(slope_ref, lam_ref, q_ref, k_ref, v_ref, g_ref, o_ref, vaug_ref, *, lam_init, row_chunk, heads):
    hg = pl.program_id(1)
    qi = pl.program_id(2)
    tq = q_ref.shape[0]
    S = k_ref.shape[0]
    hw = 2 * HEAD_DIM

    @pl.when(qi == 0)
    def _():
        ones_col = jnp.where(lax.broadcasted_iota(I32, (S, hw), 1) == 0, 1.0, 0.0).astype(BF16)
        for hh in range(heads):
            vaug_ref[hh, :, :hw] = v_ref[:, hh * hw:(hh + 1) * hw]
            vaug_ref[hh, :, hw:] = ones_col

    for hh in range(heads):
        cols = slice(hh * hw, (hh + 1) * hw)
        k = k_ref[:, cols]
        v_aug = vaug_ref[hh]
        slope = slope_ref[hg * heads + hh]
        kpos = lax.broadcasted_iota(I32, (1, S), 1).astype(F32) * slope
        for r0 in range(0, tq, row_chunk):
            q = q_ref[r0:r0 + row_chunk, cols]
            lane = lax.broadcasted_iota(I32, q.shape, 1)
            zero = jnp.zeros_like(q)
            qpos = (qi * tq + r0 + lax.broadcasted_iota(I32, (row_chunk, 1), 0)).astype(F32) * slope
            bias = lax.bitcast_convert_type(lax.bitcast_convert_type(qpos - kpos, U32) | jnp.uint32(0x80000000), F32)

            def weighted_values(qh):
                s = lax.dot_general(qh, k, _NT, preferred_element_type=F32) + bias
                e = jnp.exp((s - jnp.max(s, axis=-1, keepdims=True)).astype(BF16))
                return jnp.dot(e, v_aug, preferred_element_type=F32)

            o1 = weighted_values(jnp.where(lane < HEAD_DIM, q, zero))
            o2 = weighted_values(jnp.where(lane >= HEAD_DIM, q, zero))
            o = o1[:, :hw] * (1.0 / o1[:, hw:hw + 1]) - o2[:, :hw] * (lam_ref[0] / o2[:, hw:hw + 1])
            o = o * lax.rsqrt(jnp.mean(o * o, axis=-1, keepdims=True) + LN_EPS) * g_ref[...]
            o_ref[r0:r0 + row_chunk, cols] = (o * (1.0 - lam_init)).astype(o_ref.dtype)


def _diff_attention(qkv, slopes, lam, subln_g, lam_init, B, S):
    T = B * S
    hw = 2 * HEAD_DIM
    hp = ATTN_HEADS_PER_STEP
    ng = N_HEADS // hp
    tq = min(S, ATTN_Q_TILE)
    nq = S // tq
    smem = pl.BlockSpec(memory_space=pltpu.SMEM)
    return pl.pallas_call(
        functools.partial(_attn_body, lam_init=lam_init, row_chunk=min(tq, ATTN_ROW_CHUNK), heads=hp),
        grid=(B, ng, nq),
        in_specs=[smem, smem,
                  pl.BlockSpec((tq, hp * hw), lambda bi, h, qi: (bi * nq + qi, h)),
                  pl.BlockSpec((S, hp * hw), lambda bi, h, qi: (bi, ng + h)),
                  pl.BlockSpec((S, hp * hw), lambda bi, h, qi: (bi, 2 * ng + h)),
                  pl.BlockSpec((1, hw), lambda bi, h, qi: (0, 0))],
        out_specs=pl.BlockSpec((tq, hp * hw), lambda bi, h, qi: (bi * nq + qi, h)),
        out_shape=jax.ShapeDtypeStruct((T, ATTN_W), BF16),
        scratch_shapes=[pltpu.VMEM((hp, S, 2 * hw), BF16)],
        compiler_params=_params("parallel", "parallel", "arbitrary"),
        name="diff_attention",
    )(slopes, lam, qkv, qkv, qkv, subln_g)


def _route_block(x, w_hi, w_lo, bias, idx_ref, gate_ref, rank_ref, cnt_ref, cols):
    E = N_EXPERTS
    tm = x.shape[0]
    x_hi, x_lo = _split_bf16(x)
    nt = functools.partial(lax.dot_general, dimension_numbers=_NT, preferred_element_type=F32)
    logits = nt(w_hi, x_hi) + nt(w_lo, x_hi) + nt(w_hi, x_lo) + bias

    eid = lax.broadcasted_iota(I32, (E, tm), 0).astype(F32)
    work = logits
    vals, idxs = [], []
    for _ in range(TOP_K):
        m = jnp.max(work, axis=0, keepdims=True)
        sel = jnp.min(jnp.where(work == m, eid, float(E)), axis=0, keepdims=True)
        vals.append(m)
        idxs.append(sel)
        work = jnp.where(eid == sel, -jnp.inf, work)
    exps = [jnp.exp(v - vals[0]) for v in vals]
    denom = exps[0] + exps[1] + exps[2] + exps[3]

    chosen = jnp.zeros((E, tm), F32)
    for sel in idxs:
        chosen = chosen + jnp.where(eid == sel, 1.0, 0.0)
    earlier = jnp.where(lax.broadcasted_iota(I32, (tm, tm), 0) < lax.broadcasted_iota(I32, (tm, tm), 1), 1.0, 0.0)
    before = jnp.dot(chosen.astype(BF16), earlier.astype(BF16), preferred_element_type=F32)
    for k in range(TOP_K):
        gate_ref[k:k + 1, cols] = exps[k] / denom
        idx_ref[k:k + 1, cols] = idxs[k].astype(I32)
        rank_ref[k:k + 1, cols] = jnp.sum(jnp.where(eid == idxs[k], before, 0.0), axis=0, keepdims=True).astype(I32)
    cnt_ref[...] = jnp.broadcast_to(jnp.sum(chosen, axis=1, keepdims=True), cnt_ref.shape)


def _copy_caps():
    local_rows = TOP_K * ROW_TILE + N_EXPERTS * SEG_ALIGN
    return (local_rows // COPY_ROWS[0],) + (N_EXPERTS,) * (len(COPY_ROWS) - 1)


def _segment_copies(tab_ref, make_copy, slot, wait):
    base = len(COPY_ROWS)
    for ci, (rows, cap) in enumerate(zip(COPY_ROWS, _copy_caps())):
        def body(p, carry, base=base, rows=rows, cap=cap):
            copy = make_copy(slot, tab_ref[0, base + p], tab_ref[0, base + cap + p], rows)
            if wait:
                copy.wait()
            else:
                copy.start()
            return carry
        lax.fori_loop(0, tab_ref[0, ci], body, 0)
        base += 2 * cap


def _dispatch_body(tail_ref, seg_ref, seg_prev_ref, x_ref, idx_ref, rank_ref, loff_ref, xs_hbm, lrow_ref,
                   buf_ref, zero_ref, sem, zsem, *, n_tok_blocks):
    b = pl.program_id(0)
    slot = b % 2
    tm = x_ref.shape[0]
    R = buf_ref.shape[1]

    def seg_copy(s, local_row, dst_row, rows):
        return pltpu.make_async_copy(buf_ref.at[s, pl.ds(pl.multiple_of(local_row, SEG_ALIGN), rows)],
                                     xs_hbm.at[pl.ds(pl.multiple_of(dst_row, SEG_ALIGN), rows)], sem.at[s])

    @pl.when(b == 0)
    def _():
        zero_ref[...] = jnp.zeros_like(zero_ref)
        for e in range(N_EXPERTS):
            fill = pltpu.make_async_copy(
                zero_ref, xs_hbm.at[pl.ds(pl.multiple_of(tail_ref[e], SEG_ALIGN), EXPERT_ROWS)], zsem)
            fill.start()
            fill.wait()

    eid = lax.broadcasted_iota(I32, (N_EXPERTS, tm), 0)
    loff = loff_ref[...].astype(F32)
    rid = lax.broadcasted_iota(I32, (R, tm), 0).astype(jnp.int16)
    sel_t = jnp.zeros((R, tm), BF16)
    for k in range(TOP_K):
        base = jnp.sum(jnp.where(eid == idx_ref[k:k + 1, :], loff, 0.0), axis=0, keepdims=True).astype(I32)
        row = base + rank_ref[k:k + 1, :]
        lrow_ref[k:k + 1, :] = row
        sel_t = jnp.where(rid == row.astype(jnp.int16), jnp.ones((), BF16), sel_t)
    xb = x_ref[...].astype(BF16)
    lo = jnp.dot(sel_t, xb[:, :HALF], preferred_element_type=F32)
    hi = jnp.dot(sel_t, xb[:, HALF:], preferred_element_type=F32)
    packed = (lax.bitcast_convert_type(hi, U32) & jnp.uint32(0xFFFF0000)) | (lax.bitcast_convert_type(lo, U32) >> 16)
    buf_ref[slot] = packed

    @pl.when(b >= 1)
    def _():
        _segment_copies(seg_prev_ref, seg_copy, 1 - slot, wait=True)
    _segment_copies(seg_ref, seg_copy, slot, wait=False)

    @pl.when(b == n_tok_blocks - 1)
    def _():
        _segment_copies(seg_ref, seg_copy, slot, wait=True)


def _dispatch(x, idx, rank, loff, seg_table, tail_start, n_rows):
    T = x.shape[0]
    tm = ROW_TILE
    nb = T // tm
    R = TOP_K * tm + N_EXPERTS * SEG_ALIGN
    tok = pl.BlockSpec((TOP_K, tm), lambda b, tl: (0, b))
    seg_w = seg_table.shape[-1]
    return pl.pallas_call(
        functools.partial(_dispatch_body, n_tok_blocks=nb),
        grid_spec=pltpu.PrefetchScalarGridSpec(
            num_scalar_prefetch=1,
            grid=(nb,),
            in_specs=[pl.BlockSpec((None, 1, seg_w), lambda b, tl: (b, 0, 0), memory_space=pltpu.SMEM),
                      pl.BlockSpec((None, 1, seg_w), lambda b, tl: (jnp.maximum(b - 1, 0), 0, 0),
                                   memory_space=pltpu.SMEM),
                      pl.BlockSpec((tm, D_MODEL), lambda b, tl: (b, 0)),
                      tok, tok,
                      pl.BlockSpec((None, N_EXPERTS, 1), lambda b, tl: (b, 0, 0))],
            out_specs=[pl.BlockSpec(memory_space=pl.ANY), tok],
            scratch_shapes=[pltpu.VMEM((2, R, HALF), U32), pltpu.VMEM((EXPERT_ROWS, HALF), U32),
                            pltpu.SemaphoreType.DMA((2,)), pltpu.SemaphoreType.DMA(())]),
        out_shape=[jax.ShapeDtypeStruct((n_rows, HALF), U32), jax.ShapeDtypeStruct((TOP_K, T), I32)],
        compiler_params=_params("arbitrary"),
        name="moe_dispatch",
    )(tail_start, seg_table, seg_table, x, idx, rank, loff)


def _ffn_body(be_ref, nu_ref, xs_ref, w1_ref, b1_ref, w2_ref, b2_ref, ys_ref, w1b_ref, w2b_ref):
    i = pl.program_id(0)
    used = i < nu_ref[0]
    fresh = jnp.logical_or(i == 0, be_ref[i] != be_ref[jnp.maximum(i - 1, 0)])

    @pl.when(jnp.logical_and(used, fresh))
    def _():
        w1b_ref[...] = w1_ref[...].astype(BF16)
        w2b_ref[...] = w2_ref[...].astype(BF16)

    @pl.when(used)
    def _():
        for r0 in range(0, EXPERT_ROWS, FFN_ROW_CHUNK):
            rows = slice(r0, r0 + FFN_ROW_CHUNK)
            lo, hi = _unpack_halves(xs_ref[rows, :])
            h = (jnp.dot(lo.astype(BF16), w1b_ref[0:HALF, :], preferred_element_type=F32)
                 + jnp.dot(hi.astype(BF16), w1b_ref[HALF:D_MODEL, :], preferred_element_type=F32) + b1_ref[...])
            hb = h.astype(BF16)
            hg = jnp.minimum(hb[:, :D_FF], SWIGLU_LIMIT)
            hu = jnp.clip(hb[:, D_FF:], -SWIGLU_LIMIT, SWIGLU_LIMIT)
            act = (hu + 1.0) * (hg * jax.nn.sigmoid(hg * SWIGLU_ALPHA))
            y = jnp.dot(act.astype(BF16), w2b_ref[...], preferred_element_type=F32) + b2_ref[...]
            ys_ref[rows, :] = _pack_halves(y)


def _expert_ffn(xs, blk_expert, n_used, w1, b1, w2, b2, layer, n_blocks):
    rows = pl.BlockSpec((EXPERT_ROWS, HALF), lambda i, be, nu: (jnp.minimum(i, nu[0] - 1), 0))
    return pl.pallas_call(
        _ffn_body,
        grid_spec=pltpu.PrefetchScalarGridSpec(
            num_scalar_prefetch=2,
            grid=(n_blocks,),
            in_specs=[rows,
                      pl.BlockSpec((None, None, D_MODEL, 2 * D_FF), lambda i, be, nu: (layer, be[i], 0, 0)),
                      pl.BlockSpec((None, None, 1, 2 * D_FF), lambda i, be, nu: (layer, be[i], 0, 0)),
                      pl.BlockSpec((None, None, D_FF, D_MODEL), lambda i, be, nu: (layer, be[i], 0, 0)),
                      pl.BlockSpec((None, None, 1, D_MODEL), lambda i, be, nu: (layer, be[i], 0, 0))],
            out_specs=rows,
            scratch_shapes=[pltpu.VMEM((D_MODEL, 2 * D_FF), BF16), pltpu.VMEM((D_FF, D_MODEL), BF16)]),
        out_shape=jax.ShapeDtypeStruct((xs.shape[0], HALF), U32),
        compiler_params=_params("arbitrary"),
        name="moe_expert_ffn",
    )(blk_expert, n_used, xs, w1, b1, w2, b2)


def _combine_body(seg_ref, seg_next_ref, lrow_ref, gate_ref, x_ref, g_ref, beta_ref, ys_hbm, xo_ref,
                  buf_ref, sem, *, n_tok_blocks):
    b = pl.program_id(0)
    slot = b % 2
    tm = x_ref.shape[0]
    R = buf_ref.shape[1]

    def seg_copy(s, local_row, src_row, rows):
        return pltpu.make_async_copy(ys_hbm.at[pl.ds(pl.multiple_of(src_row, SEG_ALIGN), rows)],
                                     buf_ref.at[s, pl.ds(pl.multiple_of(local_row, SEG_ALIGN), rows)], sem.at[s])

    @pl.when(b == 0)
    def _():
        buf_ref[...] = jnp.zeros_like(buf_ref)
        _segment_copies(seg_ref, seg_copy, 0, wait=False)

    @pl.when(b + 1 < n_tok_blocks)
    def _():
        _segment_copies(seg_next_ref, seg_copy, 1 - slot, wait=False)

    _segment_copies(seg_ref, seg_copy, slot, wait=True)

    lo, hi = _unpack_halves(buf_ref[slot])
    lo = lo.astype(BF16)
    hi = hi.astype(BF16)
    tc = tm // 2
    cid = lax.broadcasted_iota(I32, (tc, R), 1).astype(jnp.int16)
    lrow_t = lrow_ref[...].astype(F32).T
    gate_t = gate_ref[...].T
    for t0 in range(0, tm, tc):
        lrow = lrow_t[t0:t0 + tc, :].astype(jnp.int16)
        gates = gate_t[t0:t0 + tc, :].astype(BF16)
        sel = jnp.zeros((tc, R), BF16)
        for k in range(TOP_K):
            sel = jnp.where(cid == lrow[:, k:k + 1], gates[:, k:k + 1], sel)
        f = jnp.concatenate([jnp.dot(sel, lo, preferred_element_type=F32),
                             jnp.dot(sel, hi, preferred_element_type=F32)], axis=1)
        xo_ref[t0:t0 + tc, :] = _layer_norm_rows(DEEPNORM_ALPHA * x_ref[t0:t0 + tc, :] + f, g_ref[...], beta_ref[...])


def _combine_ln(lrow, gate, x, g, beta, ys, seg_table):
    T = x.shape[0]
    tm = ROW_TILE
    nb = T // tm
    R = TOP_K * tm + N_EXPERTS * SEG_ALIGN
    seg_w = seg_table.shape[-1]
    return pl.pallas_call(
        functools.partial(_combine_body, n_tok_blocks=nb),
        grid=(nb,),
        in_specs=[pl.BlockSpec((None, 1, seg_w), lambda b: (b, 0, 0), memory_space=pltpu.SMEM),
                  pl.BlockSpec((None, 1, seg_w), lambda b: (jnp.minimum(b + 1, nb - 1), 0, 0),
                               memory_space=pltpu.SMEM),
                  pl.BlockSpec((TOP_K, tm), lambda b: (0, b)),
                  pl.BlockSpec((TOP_K, tm), lambda b: (0, b)),
                  pl.BlockSpec((tm, D_MODEL), lambda b: (b, 0)),
                  pl.BlockSpec((1, D_MODEL), lambda b: (0, 0)),
                  pl.BlockSpec((1, D_MODEL), lambda b: (0, 0)),
                  pl.BlockSpec(memory_space=pl.ANY)],
        out_specs=pl.BlockSpec((tm, D_MODEL), lambda b: (b, 0)),
        out_shape=jax.ShapeDtypeStruct((T, D_MODEL), F32),
        scratch_shapes=[pltpu.VMEM((2, R, HALF), U32), pltpu.SemaphoreType.DMA((2,))],
        compiler_params=_params("arbitrary"),
        name="moe_combine_ln",
    )(seg_table, seg_table, lrow, gate, x, g, beta, ys)


def _dft_tables(P):
    n2 = 4 * P
    k = jnp.arange(P, dtype=I32)
    m = ((2 * k[:, None] + 1) * k[None, :]) % n2
    ang = m.astype(F32) * F32(2.0 * math.pi / n2)
    fc32, fs32 = jnp.cos(ang), -jnp.sin(ang)
    scale = F32(1.0 / P)
    gc, gs = (fc32.T * scale).astype(BF16), (fs32.T * scale).astype(BF16)
    return fc32, fs32, (fc32.astype(BF16), fs32.astype(BF16), gc, gs)


def _hyena_positional(S):
    pos = jnp.arange(S, dtype=F32)
    t = jnp.linspace(0.0, 1.0, S, dtype=F32)[:, None]
    bands = (HYENA_EMB_DIM - 1) // 2
    f = jnp.linspace(1e-4, bands - 1, bands, dtype=F32)
    ang = (2.0 * math.pi / S) * pos[:, None] * f[None, :]
    feats = jnp.concatenate([t, jnp.cos(ang), -jnp.sin(ang)], axis=-1)
    feats = jnp.pad(feats, ((0, 0), (0, LANES - HYENA_EMB_DIM)))
    max_decay = math.log(HYENA_DECAY_TARGET) / HYENA_SHORT_DECAY_PCT
    min_decay = math.log(HYENA_DECAY_TARGET) / HYENA_LONG_DECAY_PCT
    deltas = jnp.abs(jnp.linspace(min_decay, max_decay, HYENA_CH, dtype=F32))[None, :]
    return feats, t, deltas


def _alibi_slopes():
    return jnp.asarray(np.array([2.0 ** (-8.0 * (i + 1) / N_HEADS) for i in range(N_HEADS)], dtype=np.float32))


def _even_mixer(x, xshape, tabs, w_in, b_in, short_w, short_b, f1_w, f1_b, f1_freq, f2_w, f2_b, f2_freq, f3_w,
                skip, dw_w, dw_b, cln_g, cln_b, w_out, b_out, ln_g, ln_b, w_r, b_r):
    B, S = xshape
    fc32, fs32, tabs16 = tabs
    proj = _project(x, w_in.astype(BF16), b_in[None, :])
    hy = _short_conv(proj, short_w, short_b[None, :], B, S)
    u = _conformer(proj, tabs, dw_w, dw_b[None, :], cln_g[None, :], cln_b[None, :], B, S)
    feats, tcol, deltas = _hyena_positional(S)
    f1_wp = jnp.pad(f1_w, ((0, LANES - HYENA_EMB_DIM), (0, 0)))
    flip = lambda a: jnp.concatenate([a[:1], a[:0:-1]], axis=0)
    h2 = _hyena_filters(jnp.stack([flip(feats), feats]), f1_wp, f1_b[None, :], f1_freq[None, :], f2_w, f2_b[None, :],
                        f2_freq[None, :], f3_w, jnp.stack([flip(tcol), tcol]), deltas)
    hre, him = _lag_spectra(fc32, fs32, h2, S // fc32.shape[0] - 1)
    z = _long_conv(hy, 2, hy, 0, tabs16, hre, him, 0, skip, B, S)
    z = _long_conv(z, 0, hy, 1, tabs16, hre, him, 1, skip, B, S)
    return _outproj_ln(z, 0, u, 0, w_out.astype(BF16), b_out[None, :], x, ln_g[None, :], ln_b[None, :],
                       w_r.T, b_r[:, None])


def _odd_mixer(x, xshape, layer_idx, w_qkv, lq1, lk1, lq2, lk2, subln_g, w_out, ln_g, ln_b, w_r, b_r):
    B, S = xshape
    lam_init = 0.8 - 0.6 * math.exp(-0.3 * layer_idx)
    lam = (jnp.exp(jnp.sum(lq1 * lk1)) - jnp.exp(jnp.sum(lq2 * lk2)) + lam_init).reshape(1)
    q_scale = jnp.concatenate([jnp.full((ATTN_W,), HEAD_DIM ** -0.5, F32), jnp.ones((2 * ATTN_W,), F32)])
    w = (w_qkv * q_scale).astype(BF16)
    qkv = _project(x, w, jnp.zeros((1, 3 * ATTN_W), F32))
    o = _diff_attention(qkv, _alibi_slopes(), lam, subln_g[None, :], lam_init, B, S)
    return _outproj_ln(o, 0, o, 1, w_out.astype(BF16), jnp.zeros((1, D_MODEL), F32), x, ln_g[None, :], ln_b[None, :],
                       w_r.T, b_r[:, None])


def _round_up(a, m):
    return (a + m - 1) // m * m


def _copy_lists(loff, goff, units):
    E = N_EXPERTS
    caps = _copy_caps()
    big = COPY_ROWS[0]
    n_big = units // (big // SEG_ALIGN)
    cum = jnp.cumsum(n_big, axis=1)
    first = (cum - n_big)[:, None, :]
    p = jnp.arange(caps[0], dtype=I32)[None, :, None]
    mine = (first <= p) & (p < cum[:, None, :])
    within = (p - first) * big
    counts = [cum[:, -1]]
    cols = [jnp.sum(jnp.where(mine, loff[:, None, :] + within, 0), axis=2),
            jnp.sum(jnp.where(mine, goff[:, None, :] + within, 0), axis=2)]
    off = n_big * big
    p = jnp.arange(E, dtype=I32)[None, :, None]
    for rows in COPY_ROWS[1:]:
        has = (units & (rows // SEG_ALIGN)) != 0
        pos = jnp.cumsum(has.astype(I32), axis=1) - has.astype(I32)
        mine = has[:, None, :] & (pos[:, None, :] == p)
        counts.append(jnp.sum(has.astype(I32), axis=1))
        cols += [jnp.sum(jnp.where(mine, (loff + off)[:, None, :], 0), axis=2),
                 jnp.sum(jnp.where(mine, (goff + off)[:, None, :], 0), axis=2)]
        off = off + jnp.where(has, rows, 0)
    return jnp.concatenate([jnp.stack(counts, axis=1)] + cols, axis=1).astype(I32)


def _routing_tables(cnt_blocks, n_ffn_blocks):
    E = N_EXPERTS
    cnt8 = _round_up(cnt_blocks[:, :, 0].astype(I32), SEG_ALIGN)
    seg_end = jnp.cumsum(cnt8, axis=1)
    loff = seg_end - cnt8
    tot8 = jnp.sum(cnt8, axis=0)
    group = _round_up(tot8, EXPERT_ROWS)
    group_end = jnp.cumsum(group)
    group_start = group_end - group
    goff = group_start[None, :] + jnp.cumsum(cnt8, axis=0) - cnt8
    seg_table = _copy_lists(loff, goff, cnt8 // SEG_ALIGN)
    starts = jnp.arange(n_ffn_blocks, dtype=I32) * EXPERT_ROWS
    blk_expert = jnp.minimum(jnp.sum((group_end[None, :] <= starts[:, None]).astype(I32), axis=1), E - 1)
    n_used = group_end[-1:] // EXPERT_ROWS
    tail_start = group_start + tot8
    return loff[:, :, None], seg_table[:, None, :], blk_expert, n_used, tail_start


def _moe_layer(x, routing, layer, w1, b1, w2, b2, ln_g, ln_b):
    T = x.shape[0]
    nb = T // ROW_TILE
    n_rows = _round_up(T * TOP_K + nb * N_EXPERTS * (SEG_ALIGN - 1), EXPERT_ROWS) + N_EXPERTS * EXPERT_ROWS
    n_ffn_blocks = n_rows // EXPERT_ROWS
    idx, gate, rank, cnt = routing
    loff, seg_table, blk_expert, n_used, tail_start = _routing_tables(cnt, n_ffn_blocks)
    xs, lrow = _dispatch(x, idx, rank, loff, seg_table, tail_start, n_rows + EXPERT_ROWS)
    ys = _expert_ffn(xs, blk_expert, n_used, w1, b1[:, :, None, :], w2, b2[:, :, None, :], layer, n_ffn_blocks)
    return _combine_ln(lrow, gate, x, ln_g[None, :], ln_b[None, :], ys, seg_table)


def kernel(x, hy_cf_w_in, hy_cf_b_in, hy_short_w, hy_short_b, hy_f1_w, hy_f1_b, hy_f1_freq, hy_f2_w, hy_f2_b, hy_f2_freq, hy_f3_w, hy_skip, cf_dw_w, cf_dw_b, cf_ln_g, cf_ln_b, even_w_out, even_b_out, attn_w_qkv, attn_lq1, attn_lk1, attn_lq2, attn_lk2, attn_subln_g, attn_w_out, ln1_g, ln1_b, ln2_g, ln2_b, moe_w_r, moe_b_r, moe_w1, moe_b1, moe_w2, moe_b2):
    B, S, D = x.shape
    assert D == D_MODEL and S % LANES == 0
    assert (B * S) % ROW_TILE == 0 and (B * S) % min(B * S, DENSE_ROW_TILE) == 0
    depth = ln1_g.shape[0]
    xf = x.reshape(B * S, D)
    tabs = _dft_tables(S // CONV_BLOCKS)
    for i in range(depth):
        j = i // 2
        if i % 2 == 0:
            xf, routing = _even_mixer(xf, (B, S), tabs, hy_cf_w_in[j], hy_cf_b_in[j], hy_short_w[j], hy_short_b[j],
                                      hy_f1_w[j], hy_f1_b[j], hy_f1_freq[j], hy_f2_w[j], hy_f2_b[j], hy_f2_freq[j],
                                      hy_f3_w[j], hy_skip[j], cf_dw_w[j], cf_dw_b[j], cf_ln_g[j], cf_ln_b[j],
                                      even_w_out[j], even_b_out[j], ln1_g[i], ln1_b[i], moe_w_r[i], moe_b_r[i])
        else:
            xf, routing = _odd_mixer(xf, (B, S), i, attn_w_qkv[j], attn_lq1[j], attn_lk1[j], attn_lq2[j],
                                     attn_lk2[j], attn_subln_g[j], attn_w_out[j], ln1_g[i], ln1_b[i],
                                     moe_w_r[i], moe_b_r[i])
        xf = _moe_layer(xf, routing, i, moe_w1, moe_b1, moe_w2, moe_b2, ln2_g[i], ln2_b[i])
    return xf.reshape(B, S, D)
```

```python
import functools
import math

import jax
import jax.numpy as jnp
import numpy as np
from jax import lax
from jax.experimental import pallas as pl
from jax.experimental.pallas import tpu as pltpu

F32 = jnp.float32
BF16 = jnp.bfloat16
U32 = jnp.uint32
I32 = jnp.int32

D_MODEL = 1024
HALF = D_MODEL // 2
DEPTH = 4
HYENA_CH = D_MODEL // 2
CONF_CH = D_MODEL // 2
HYENA_ORDER = 2
HYENA_EMB_DIM = 33
HYENA_FILTER_DIM = 64
HYENA_SHORT_DECAY_PCT = 0.3
HYENA_LONG_DECAY_PCT = 1.5
HYENA_DECAY_TARGET = 1e-2
CONF_WIDTH = 31
EVEN_IN = 3 * HYENA_CH + 2 * CONF_CH
N_HEADS = 8
HEAD_DIM = 64
ATTN_W = N_HEADS * 2 * HEAD_DIM
N_EXPERTS = 32
TOP_K = 4
D_FF = D_MODEL
SWIGLU_LIMIT = 7.0
SWIGLU_ALPHA = 1.702
DEEPNORM_ALPHA = (2 * DEPTH) ** 0.25
LN_EPS = 1e-5

LANES = 128
VMEM_LIMIT_BYTES = 56 * 1024 * 1024
ROW_TILE = 512
DENSE_ROW_TILE = 1024
EXPERT_ROWS = 512
FFN_ROW_CHUNK = 512
SEG_ALIGN = 8
COPY_ROWS = (64, 32, 16, 8)
CONV_BLOCKS = 4
CONV_CH_TILE = 256
ATTN_Q_TILE = 2048
ATTN_HEADS_PER_STEP = 1
ATTN_ROW_CHUNK = 256

_NT = (((1,), (1,)), ((), ()))


def _params(*sem):
    return pltpu.CompilerParams(dimension_semantics=sem, vmem_limit_bytes=VMEM_LIMIT_BYTES)


def _split_bf16(a):
    hi = a.astype(BF16)
    lo = (a - hi.astype(F32)).astype(BF16)
    return hi, lo


def _dot3(a, b):
    a_hi, a_lo = _split_bf16(a)
    b_hi, b_lo = _split_bf16(b)
    d = functools.partial(jnp.dot, preferred_element_type=F32)
    return d(a_hi, b_hi) + d(a_hi, b_lo) + d(a_lo, b_hi)


def _layer_norm_rows(y, g, b):
    mu = jnp.mean(y, axis=-1, keepdims=True)
    yc = y - mu
    var = jnp.mean(yc * yc, axis=-1, keepdims=True)
    return yc * lax.rsqrt(var + LN_EPS) * g + b


def _pack_halves(y):
    lo = lax.bitcast_convert_type(y[:, :HALF].astype(BF16).astype(F32), U32)
    hi = lax.bitcast_convert_type(y[:, HALF:].astype(BF16).astype(F32), U32)
    return hi | (lo >> 16)


def _unpack_halves(p):
    lo = lax.bitcast_convert_type(p << 16, F32)
    hi = lax.bitcast_convert_type(p & jnp.uint32(0xFFFF0000), F32)
    return lo, hi


def _proj_body(x_ref, w_ref, b_ref, o_ref, *, col_chunk):
    x = x_ref[...].astype(BF16)
    for j in range(0, o_ref.shape[1], col_chunk):
        acc = jnp.dot(x, w_ref[:, j:j + col_chunk], preferred_element_type=F32)
        o_ref[:, j:j + col_chunk] = (acc + b_ref[:, j:j + col_chunk]).astype(o_ref.dtype)


def _project(x, w, b):
    T, K = x.shape
    N = w.shape[1]
    tm = min(T, DENSE_ROW_TILE)
    return pl.pallas_call(
        functools.partial(_proj_body, col_chunk=512),
        grid=(T // tm,),
        in_specs=[pl.BlockSpec((tm, K), lambda i: (i, 0)),
                  pl.BlockSpec((K, N), lambda i: (0, 0)),
                  pl.BlockSpec((1, N), lambda i: (0, 0))],
        out_specs=pl.BlockSpec((tm, N), lambda i: (i, 0)),
        out_shape=jax.ShapeDtypeStruct((T, N), BF16),
        compiler_params=_params("parallel"),
        name="project",
    )(x, w, b)


def _outproj_ln_body(a1_ref, a2_ref, w1_ref, w2_ref, b_ref, x_ref, g_ref, beta_ref, wr_ref, br_ref,
                     xo_ref, idx_ref, gate_ref, rank_ref, cnt_ref):
    m = (jnp.dot(a1_ref[...], w1_ref[...], preferred_element_type=F32)
         + jnp.dot(a2_ref[...], w2_ref[...], preferred_element_type=F32) + b_ref[...])
    y = _layer_norm_rows(DEEPNORM_ALPHA * x_ref[...] + m, g_ref[...], beta_ref[...])
    xo_ref[...] = y
    w_hi, w_lo = _split_bf16(wr_ref[...])
    for s in range(y.shape[0] // ROW_TILE):
        cols = slice(s * ROW_TILE, (s + 1) * ROW_TILE)
        _route_block(y[cols, :], w_hi, w_lo, br_ref[...], idx_ref, gate_ref, rank_ref, cnt_ref.at[s], cols)


def _outproj_ln(a1, a1_col, a2, a2_col, w, b, x, g, beta, w_rt, b_r):
    T = x.shape[0]
    E = N_EXPERTS
    tm = min(T, DENSE_ROW_TILE)
    nsub = tm // ROW_TILE
    tok = pl.BlockSpec((TOP_K, tm), lambda i: (0, i))
    out = pl.pallas_call(
        _outproj_ln_body,
        grid=(T // tm,),
        in_specs=[pl.BlockSpec((tm, HALF), lambda i: (i, a1_col)),
                  pl.BlockSpec((tm, HALF), lambda i: (i, a2_col)),
                  pl.BlockSpec((HALF, D_MODEL), lambda i: (0, 0)),
                  pl.BlockSpec((HALF, D_MODEL), lambda i: (1, 0)),
                  pl.BlockSpec((1, D_MODEL), lambda i: (0, 0)),
                  pl.BlockSpec((tm, D_MODEL), lambda i: (i, 0)),
                  pl.BlockSpec((1, D_MODEL), lambda i: (0, 0)),
                  pl.BlockSpec((1, D_MODEL), lambda i: (0, 0)),
                  pl.BlockSpec((E, D_MODEL), lambda i: (0, 0)),
                  pl.BlockSpec((E, 1), lambda i: (0, 0))],
        out_specs=[pl.BlockSpec((tm, D_MODEL), lambda i: (i, 0)), tok, tok, tok,
                   pl.BlockSpec((nsub, E, LANES), lambda i: (i, 0, 0))],
        out_shape=[jax.ShapeDtypeStruct((T, D_MODEL), F32),
                   jax.ShapeDtypeStruct((TOP_K, T), I32), jax.ShapeDtypeStruct((TOP_K, T), F32),
                   jax.ShapeDtypeStruct((TOP_K, T), I32), jax.ShapeDtypeStruct((T // ROW_TILE, E, LANES), F32)],
        compiler_params=_params("parallel"),
        name="outproj_ln_route",
    )(a1, a2, w, w, b, x, g, beta, w_rt, b_r)
    return out[0], tuple(out[1:])


def _short_conv_body(x_ref, w_ref, b_ref, o_ref):
    x = x_ref[...].astype(F32)
    S = x.shape[0]
    row = lax.broadcasted_iota(I32, x.shape, 0)
    prev = jnp.where(row == 0, 0.0, pltpu.roll(x, 1, 0))
    nxt = jnp.where(row == S - 1, 0.0, pltpu.roll(x, S - 1, 0))
    y = w_ref[0:1, :] * prev + w_ref[1:2, :] * x + w_ref[2:3, :] * nxt + b_ref[...]
    o_ref[...] = y.astype(o_ref.dtype)


def _short_conv(proj, w, b, B, S):
    T = B * S
    C = HYENA_CH
    return pl.pallas_call(
        _short_conv_body,
        grid=(B, 3),
        in_specs=[pl.BlockSpec((S, C), lambda bi, j: (bi, j)),
                  pl.BlockSpec((3, C), lambda bi, j: (0, j)),
                  pl.BlockSpec((1, C), lambda bi, j: (0, j))],
        out_specs=pl.BlockSpec((S, C), lambda bi, j: (bi, j)),
        out_shape=jax.ShapeDtypeStruct((T, 3 * C), BF16),
        compiler_params=_params("parallel", "parallel"),
        name="hyena_short_conv",
    )(proj, w, b)


def _conformer_body(a_ref, g_ref, fc_ref, fs_ref, hre_ref, him_ref, gc_ref, gs_ref, b_ref, lg_ref, lb_ref, o_ref,
                    *, n_blk):
    P = fc_ref.shape[0]
    fc, fs, gc, gs = fc_ref[...], fs_ref[...], gc_ref[...], gs_ref[...]
    vre, vim = [], []
    for j in range(n_blk):
        rows = slice(j * P, (j + 1) * P)
        u = (a_ref[rows, :].astype(F32) * jax.nn.sigmoid(g_ref[rows, :].astype(F32))).astype(BF16)
        vre.append(jnp.dot(fc, u, preferred_element_type=F32).astype(BF16))
        vim.append(jnp.dot(fs, u, preferred_element_type=F32).astype(BF16))
    for i in range(n_blk):
        yre, yim = _mix_block_lags(vre, vim, hre_ref, him_ref, i, 1)
        y = jnp.dot(gc, yre, preferred_element_type=F32) + jnp.dot(gs, yim, preferred_element_type=F32) + b_ref[...]
        y = _layer_norm_rows(y, lg_ref[...], lb_ref[...])
        o_ref[i * P:(i + 1) * P, :] = (y * jax.nn.sigmoid(y)).astype(o_ref.dtype)


def _conformer(proj, tabs, w, b, lg, lb, B, S):
    fc32, fs32, (fc, fs, gc, gs) = tabs
    P = fc.shape[0]
    T = B * S
    C = CONF_CH
    half = CONF_WIDTH // 2
    assert half < P
    span = ((S - half, S - half - 1), (0, 0))
    hre, him = _lag_spectra(fc32, fs32, jnp.stack([jnp.pad(w[::-1], span), jnp.pad(w, span)]), 1)
    tab = pl.BlockSpec((P, P), lambda bi: (0, 0))
    spec = pl.BlockSpec((3 * P, C), lambda bi: (0, 0))
    vec = pl.BlockSpec((1, C), lambda bi: (0, 0))
    return pl.pallas_call(
        functools.partial(_conformer_body, n_blk=S // P),
        grid=(B,),
        in_specs=[pl.BlockSpec((S, C), lambda bi: (bi, 3)),
                  pl.BlockSpec((S, C), lambda bi: (bi, 4)),
                  tab, tab, spec, spec, tab, tab, vec, vec, vec],
        out_specs=pl.BlockSpec((S, C), lambda bi: (bi, 0)),
        out_shape=jax.ShapeDtypeStruct((T, C), BF16),
        compiler_params=_params("parallel"),
        name="conformer_conv",
    )(proj, proj, fc, fs, hre, him, gc, gs, b, lg, lb)


def _filter_body(feat_ref, w1_ref, b1_ref, q1_ref, w2_ref, b2_ref, q2_ref, w3_ref, t_ref, delta_ref, o_ref):
    h = jnp.sin(q1_ref[...] * (_dot3(feat_ref[...], w1_ref[...]) + b1_ref[...]))
    h = jnp.sin(q2_ref[...] * (_dot3(h, w2_ref[...]) + b2_ref[...]))
    h = _dot3(h, w3_ref[...])
    o_ref[...] = h * jnp.exp(-t_ref[...] * delta_ref[...])


def _hyena_filters(feats2, w1, b1, q1, w2, b2, q2, w3, tcol2, deltas):
    S = feats2.shape[1]
    C = HYENA_CH
    fd = HYENA_FILTER_DIM
    fixed = lambda p, d, o: (0, 0)
    return pl.pallas_call(
        _filter_body,
        grid=(2, 2, HYENA_ORDER),
        in_specs=[pl.BlockSpec((None, S, LANES), lambda p, d, o: (p, 0, 0)),
                  pl.BlockSpec((LANES, fd), fixed),
                  pl.BlockSpec((1, fd), fixed),
                  pl.BlockSpec((1, fd), fixed),
                  pl.BlockSpec((fd, fd), fixed),
                  pl.BlockSpec((1, fd), fixed),
                  pl.BlockSpec((1, fd), fixed),
                  pl.BlockSpec((fd, C), lambda p, d, o: (0, d * HYENA_ORDER + o)),
                  pl.BlockSpec((None, S, 1), lambda p, d, o: (p, 0, 0)),
                  pl.BlockSpec((1, C), fixed)],
        out_specs=pl.BlockSpec((None, S, C), lambda p, d, o: ((d + 1 - p) % 2, p, o)),
        out_shape=jax.ShapeDtypeStruct((2, 2 * S, HYENA_ORDER * C), F32),
        compiler_params=_params("parallel", "parallel", "parallel"),
        name="hyena_filter_mlp",
    )(feats2, w1, b1, q1, w2, b2, q2, w3, tcol2, deltas)


def _spectrum_body(fc_ref, fs_ref, a_ref, b_ref, hre_ref, him_ref):
    a = a_ref[...]
    row = lax.broadcasted_iota(I32, a.shape, 0)
    b = jnp.where(row == 0, 0.0, b_ref[...])
    hre_ref[...] = _dot3(fc_ref[...], a + b)
    him_ref[...] = _dot3(fs_ref[...], a - b)


def _lag_spectra(fc32, fs32, h2, d_max):
    P = fc32.shape[0]
    n = h2.shape[1] // (2 * P)
    n_ch = h2.shape[2]
    C = HYENA_CH
    nd = 2 * d_max + 1
    spec = pl.BlockSpec((P, C), lambda di, o: (di, o))
    return pl.pallas_call(
        _spectrum_body,
        grid=(nd, n_ch // C),
        in_specs=[pl.BlockSpec((P, P), lambda di, o: (0, 0)),
                  pl.BlockSpec((P, P), lambda di, o: (0, 0)),
                  pl.BlockSpec((None, P, C), lambda di, o: (0, n - d_max + di, o)),
                  pl.BlockSpec((None, P, C), lambda di, o: (1, n + d_max - di, o))],
        out_specs=[spec, spec],
        out_shape=[jax.ShapeDtypeStruct((nd * P, n_ch), F32)] * 2,
        compiler_params=_params("parallel", "parallel"),
        name="block_lag_spectra",
    )(fc32, fs32, h2, h2)


def _mix_block_lags(vre, vim, hre_ref, him_ref, i, d_max):
    P = vre[0].shape[0]
    yre = yim = None
    for j in range(max(0, i - d_max), min(len(vre), i + d_max + 1)):
        r0 = (i - j + d_max) * P
        hre = hre_ref[r0:r0 + P, :].astype(BF16)
        him = him_ref[r0:r0 + P, :].astype(BF16)
        tre = vre[j] * hre - vim[j] * him
        tim = vre[j] * him + vim[j] * hre
        yre = tre if yre is None else yre + tre
        yim = tim if yim is None else yim + tim
    return yre, yim


def _long_conv_body(v_ref, gate_ref, fc_ref, fs_ref, hre_ref, him_ref, gc_ref, gs_ref, skip_ref, o_ref, *, n_blk):
    P = fc_ref.shape[0]
    fc, fs, gc, gs = fc_ref[...], fs_ref[...], gc_ref[...], gs_ref[...]
    vre, vim = [], []
    for j in range(n_blk):
        vj = v_ref[j * P:(j + 1) * P, :]
        vre.append(jnp.dot(fc, vj, preferred_element_type=F32).astype(BF16))
        vim.append(jnp.dot(fs, vj, preferred_element_type=F32).astype(BF16))
    for i in range(n_blk):
        yre, yim = _mix_block_lags(vre, vim, hre_ref, him_ref, i, n_blk - 1)
        y = jnp.dot(gc, yre, preferred_element_type=F32) + jnp.dot(gs, yim, preferred_element_type=F32)
        rows = slice(i * P, (i + 1) * P)
        y = y + v_ref[rows, :].astype(F32) * skip_ref[...]
        o_ref[rows, :] = (gate_ref[rows, :].astype(F32) * y).astype(o_ref.dtype)


def _long_conv(v_arr, v_col, gate_arr, gate_col, tabs, hre, him, order, skip, B, S):
    fc, fs, gc, gs = tabs
    P = fc.shape[0]
    T = B * S
    C = HYENA_CH
    nc = C // CONV_CH_TILE
    cc = CONV_CH_TILE
    n_h = hre.shape[0]
    tab = pl.BlockSpec((P, P), lambda bi, c: (0, 0))
    return pl.pallas_call(
        functools.partial(_long_conv_body, n_blk=S // P),
        grid=(B, nc),
        in_specs=[pl.BlockSpec((S, cc), lambda bi, c: (bi, v_col * nc + c)),
                  pl.BlockSpec((S, cc), lambda bi, c: (bi, gate_col * nc + c)),
                  tab, tab,
                  pl.BlockSpec((n_h, cc), lambda bi, c: (0, order * nc + c)),
                  pl.BlockSpec((n_h, cc), lambda bi, c: (0, order * nc + c)),
                  tab, tab,
                  pl.BlockSpec((1, cc), lambda bi, c: (0, c))],
        out_specs=pl.BlockSpec((S, cc), lambda bi, c: (bi, c)),
        out_shape=jax.ShapeDtypeStruct((T, C), BF16),
        compiler_params=_params("parallel", "parallel"),
        name="hyena_long_conv",
    )(v_arr, gate_arr, fc, fs, hre, him, gc, gs, skip[order][None, :])


def _attn_body(slope_ref, lam_ref, q_ref, k_ref, v_ref, g_ref, o_ref, vaug_ref, *, lam_init, row_chunk, heads):
    hg = pl.program_id(1)
    qi = pl.program_id(2)
    tq = q_ref.shape[0]
    S = k_ref.shape[0]
    hw = 2 * HEAD_DIM

    @pl.when(qi == 0)
    def _():
        ones_col = jnp.where(lax.broadcasted_iota(I32, (S, hw), 1) == 0, 1.0, 0.0).astype(BF16)
        for hh in range(heads):
            vaug_ref[hh, :, :hw] = v_ref[:, hh * hw:(hh + 1) * hw]
            vaug_ref[hh, :, hw:] = ones_col

    for hh in range(heads):
        cols = slice(hh * hw, (hh + 1) * hw)
        k = k_ref[:, cols]
        v_aug = vaug_ref[hh]
        slope = slope_ref[hg * heads + hh]
        kpos = lax.broadcasted_iota(I32, (1, S), 1).astype(F32) * slope
        for r0 in range(0, tq, row_chunk):
            q = q_ref[r0:r0 + row_chunk, cols]
            lane = lax.broadcasted_iota(I32, q.shape, 1)
            zero = jnp.zeros_like(q)
            qpos = (qi * tq + r0 + lax.broadcasted_iota(I32, (row_chunk, 1), 0)).astype(F32) * slope
            bias = lax.bitcast_convert_type(lax.bitcast_convert_type(qpos - kpos, U32) | jnp.uint32(0x80000000), F32)

            def weighted_values(qh):
                s = lax.dot_general(qh, k, _NT, preferred_element_type=F32) + bias
                e = jnp.exp((s - jnp.max(s, axis=-1, keepdims=True)).astype(BF16))
                return jnp.dot(e, v_aug, preferred_element_type=F32)

            o1 = weighted_values(jnp.where(lane < HEAD_DIM, q, zero))
            o2 = weighted_values(jnp.where(lane >= HEAD_DIM, q, zero))
            o = o1[:, :hw] * (1.0 / o1[:, hw:hw + 1]) - o2[:, :hw] * (lam_ref[0] / o2[:, hw:hw + 1])
            o = o * lax.rsqrt(jnp.mean(o * o, axis=-1, keepdims=True) + LN_EPS) * g_ref[...]
            o_ref[r0:r0 + row_chunk, cols] = (o * (1.0 - lam_init)).astype(o_ref.dtype)


def _diff_attention(qkv, slopes, lam, subln_g, lam_init, B, S):
    T = B * S
    hw = 2 * HEAD_DIM
    hp = ATTN_HEADS_PER_STEP
    ng = N_HEADS // hp
    tq = min(S, ATTN_Q_TILE)
    nq = S // tq
    smem = pl.BlockSpec(memory_space=pltpu.SMEM)
    return pl.pallas_call(
        functools.partial(_attn_body, lam_init=lam_init, row_chunk=min(tq, ATTN_ROW_CHUNK), heads=hp),
        grid=(B, ng, nq),
        in_specs=[smem, smem,
                  pl.BlockSpec((tq, hp * hw), lambda bi, h, qi: (bi * nq + qi, h)),
                  pl.BlockSpec((S, hp * hw), lambda bi, h, qi: (bi, ng + h)),
                  pl.BlockSpec((S, hp * hw), lambda bi, h, qi: (bi, 2 * ng + h)),
                  pl.BlockSpec((1, hw), lambda bi, h, qi: (0, 0))],
        out_specs=pl.BlockSpec((tq, hp * hw), lambda bi, h, qi: (bi * nq + qi, h)),
        out_shape=jax.ShapeDtypeStruct((T, ATTN_W), BF16),
        scratch_shapes=[pltpu.VMEM((hp, S, 2 * hw), BF16)],
        compiler_params=_params("parallel", "parallel", "arbitrary"),
        name="diff_attention",
    )(slopes, lam, qkv, qkv, qkv, subln_g)


def _route_block(x, w_hi, w_lo, bias, idx_ref, gate_ref, rank_ref, cnt_ref, cols):
    E = N_EXPERTS
    tm = x.shape[0]
    x_hi, x_lo = _split_bf16(x)
    nt = functools.partial(lax.dot_general, dimension_numbers=_NT, preferred_element_type=F32)
    logits = nt(w_hi, x_hi) + nt(w_lo, x_hi) + nt(w_hi, x_lo) + bias

    eid = lax.broadcasted_iota(I32, (E, tm), 0).astype(F32)
    work = logits
    vals, idxs = [], []
    for _ in range(TOP_K):
        m = jnp.max(work, axis=0, keepdims=True)
        sel = jnp.min(jnp.where(work == m, eid, float(E)), axis=0, keepdims=True)
        vals.append(m)
        idxs.append(sel)
        work = jnp.where(eid == sel, -jnp.inf, work)
    exps = [jnp.exp(v - vals[0]) for v in vals]
    denom = exps[0] + exps[1] + exps[2] + exps[3]

    chosen = jnp.zeros((E, tm), F32)
    for sel in idxs:
        chosen = chosen + jnp.where(eid == sel, 1.0, 0.0)
    earlier = jnp.where(lax.broadcasted_iota(I32, (tm, tm), 0) < lax.broadcasted_iota(I32, (tm, tm), 1), 1.0, 0.0)
    before = jnp.dot(chosen.astype(BF16), earlier.astype(BF16), preferred_element_type=F32)
    for k in range(TOP_K):
        gate_ref[k:k + 1, cols] = exps[k] / denom
        idx_ref[k:k + 1, cols] = idxs[k].astype(I32)
        rank_ref[k:k + 1, cols] = jnp.sum(jnp.where(eid == idxs[k], before, 0.0), axis=0, keepdims=True).astype(I32)
    cnt_ref[...] = jnp.broadcast_to(jnp.sum(chosen, axis=1, keepdims=True), cnt_ref.shape)


def _copy_caps():
    local_rows = TOP_K * ROW_TILE + N_EXPERTS * SEG_ALIGN
    return (local_rows // COPY_ROWS[0],) + (N_EXPERTS,) * (len(COPY_ROWS) - 1)


def _segment_copies(tab_ref, make_copy, slot, wait):
    base = len(COPY_ROWS)
    for ci, (rows, cap) in enumerate(zip(COPY_ROWS, _copy_caps())):
        def body(p, carry, base=base, rows=rows, cap=cap, priority=ci % 2):
            copy = make_copy(slot, tab_ref[0, base + p], tab_ref[0, base + cap + p], rows)
            if wait:
                copy.wait()
            else:
                copy.start(priority=priority)
            return carry
        lax.fori_loop(0, tab_ref[0, ci], body, 0)
        base += 2 * cap


def _dispatch_body(tail_ref, seg_ref, seg_prev_ref, x_ref, idx_ref, rank_ref, loff_ref, xs_hbm, lrow_ref,
                   buf_ref, zero_ref, sem, zsem, *, n_tok_blocks):
    b = pl.program_id(0)
    slot = b % 2
    tm = x_ref.shape[0]
    R = buf_ref.shape[1]

    def seg_copy(s, local_row, dst_row, rows):
        return pltpu.make_async_copy(buf_ref.at[s, pl.ds(pl.multiple_of(local_row, SEG_ALIGN), rows)],
                                     xs_hbm.at[pl.ds(pl.multiple_of(dst_row, SEG_ALIGN), rows)], sem.at[s])

    @pl.when(b == 0)
    def _():
        zero_ref[...] = jnp.zeros_like(zero_ref)
        for e in range(N_EXPERTS):
            fill = pltpu.make_async_copy(
                zero_ref, xs_hbm.at[pl.ds(pl.multiple_of(tail_ref[e], SEG_ALIGN), EXPERT_ROWS)], zsem)
            fill.start()
            fill.wait()

    eid = lax.broadcasted_iota(I32, (N_EXPERTS, tm), 0)
    loff = loff_ref[...].astype(F32)
    rid = lax.broadcasted_iota(I32, (R, tm), 0).astype(jnp.int16)
    sel_t = jnp.zeros((R, tm), BF16)
    for k in range(TOP_K):
        base = jnp.sum(jnp.where(eid == idx_ref[k:k + 1, :], loff, 0.0), axis=0, keepdims=True).astype(I32)
        row = base + rank_ref[k:k + 1, :]
        lrow_ref[k:k + 1, :] = row
        sel_t = jnp.where(rid == row.astype(jnp.int16), jnp.ones((), BF16), sel_t)
    xb = x_ref[...].astype(BF16)
    lo = jnp.dot(sel_t, xb[:, :HALF], preferred_element_type=F32)
    hi = jnp.dot(sel_t, xb[:, HALF:], preferred_element_type=F32)
    packed = (lax.bitcast_convert_type(hi, U32) & jnp.uint32(0xFFFF0000)) | (lax.bitcast_convert_type(lo, U32) >> 16)
    buf_ref[slot] = packed

    @pl.when(b >= 1)
    def _():
        _segment_copies(seg_prev_ref, seg_copy, 1 - slot, wait=True)
    _segment_copies(seg_ref, seg_copy, slot, wait=False)

    @pl.when(b == n_tok_blocks - 1)
    def _():
        _segment_copies(seg_ref, seg_copy, slot, wait=True)


def _dispatch(x, idx, rank, loff, seg_table, tail_start, n_rows):
    T = x.shape[0]
    tm = ROW_TILE
    nb = T // tm
    R = TOP_K * tm + N_EXPERTS * SEG_ALIGN
    tok = pl.BlockSpec((TOP_K, tm), lambda b, tl: (0, b))
    seg_w = seg_table.shape[-1]
    return pl.pallas_call(
        functools.partial(_dispatch_body, n_tok_blocks=nb),
        grid_spec=pltpu.PrefetchScalarGridSpec(
            num_scalar_prefetch=1,
            grid=(nb,),
            in_specs=[pl.BlockSpec((None, 1, seg_w), lambda b, tl: (b, 0, 0), memory_space=pltpu.SMEM),
                      pl.BlockSpec((None, 1, seg_w), lambda b, tl: (jnp.maximum(b - 1, 0), 0, 0),
                                   memory_space=pltpu.SMEM),
                      pl.BlockSpec((tm, D_MODEL), lambda b, tl: (b, 0)),
                      tok, tok,
                      pl.BlockSpec((None, N_EXPERTS, 1), lambda b, tl: (b, 0, 0))],
            out_specs=[pl.BlockSpec(memory_space=pl.ANY), tok],
            scratch_shapes=[pltpu.VMEM((2, R, HALF), U32), pltpu.VMEM((EXPERT_ROWS, HALF), U32),
                            pltpu.SemaphoreType.DMA((2,)), pltpu.SemaphoreType.DMA(())]),
        out_shape=[jax.ShapeDtypeStruct((n_rows, HALF), U32), jax.ShapeDtypeStruct((TOP_K, T), I32)],
        compiler_params=_params("arbitrary"),
        name="moe_dispatch",
    )(tail_start, seg_table, seg_table, x, idx, rank, loff)


def _ffn_body(be_ref, nu_ref, xs_ref, w1_ref, b1_ref, w2_ref, b2_ref, ys_ref, w1b_ref, w2b_ref):
    i = pl.program_id(0)
    used = i < nu_ref[0]
    fresh = jnp.logical_or(i == 0, be_ref[i] != be_ref[jnp.maximum(i - 1, 0)])

    @pl.when(jnp.logical_and(used, fresh))
    def _():
        w1b_ref[...] = w1_ref[...].astype(BF16)
        w2b_ref[...] = w2_ref[...].astype(BF16)

    @pl.when(used)
    def _():
        for r0 in range(0, EXPERT_ROWS, FFN_ROW_CHUNK):
            rows = slice(r0, r0 + FFN_ROW_CHUNK)
            lo, hi = _unpack_halves(xs_ref[rows, :])
            h = (jnp.dot(lo.astype(BF16), w1b_ref[0:HALF, :], preferred_element_type=F32)
                 + jnp.dot(hi.astype(BF16), w1b_ref[HALF:D_MODEL, :], preferred_element_type=F32) + b1_ref[...])
            hb = h.astype(BF16)
            hg = jnp.minimum(hb[:, :D_FF], SWIGLU_LIMIT)
            hu = jnp.clip(hb[:, D_FF:], -SWIGLU_LIMIT, SWIGLU_LIMIT)
            act = (hu + 1.0) * (hg * jax.nn.sigmoid(hg * SWIGLU_ALPHA))
            y = jnp.dot(act.astype(BF16), w2b_ref[...], preferred_element_type=F32) + b2_ref[...]
            ys_ref[rows, :] = _pack_halves(y)


def _expert_ffn(xs, blk_expert, n_used, w1, b1, w2, b2, layer, n_blocks):
    rows = pl.BlockSpec((EXPERT_ROWS, HALF), lambda i, be, nu: (jnp.minimum(i, nu[0] - 1), 0))
    return pl.pallas_call(
        _ffn_body,
        grid_spec=pltpu.PrefetchScalarGridSpec(
            num_scalar_prefetch=2,
            grid=(n_blocks,),
            in_specs=[rows,
                      pl.BlockSpec((None, None, D_MODEL, 2 * D_FF), lambda i, be, nu: (layer, be[i], 0, 0)),
                      pl.BlockSpec((None, None, 1, 2 * D_FF), lambda i, be, nu: (layer, be[i], 0, 0)),
                      pl.BlockSpec((None, None, D_FF, D_MODEL), lambda i, be, nu: (layer, be[i], 0, 0)),
                      pl.BlockSpec((None, None, 1, D_MODEL), lambda i, be, nu: (layer, be[i], 0, 0))],
            out_specs=rows,
            scratch_shapes=[pltpu.VMEM((D_MODEL, 2 * D_FF), BF16), pltpu.VMEM((D_FF, D_MODEL), BF16)]),
        out_shape=jax.ShapeDtypeStruct((xs.shape[0], HALF), U32),
        compiler_params=_params("arbitrary"),
        name="moe_expert_ffn",
    )(blk_expert, n_used, xs, w1, b1, w2, b2)


def _combine_body(seg_ref, seg_next_ref, lrow_ref, gate_ref, x_ref, g_ref, beta_ref, ys_hbm, xo_ref,
                  buf_ref, sem, *, n_tok_blocks):
    b = pl.program_id(0)
    slot = b % 2
    tm = x_ref.shape[0]
    R = buf_ref.shape[1]

    def seg_copy(s, local_row, src_row, rows):
        return pltpu.make_async_copy(ys_hbm.at[pl.ds(pl.multiple_of(src_row, SEG_ALIGN), rows)],
                                     buf_ref.at[s, pl.ds(pl.multiple_of(local_row, SEG_ALIGN), rows)], sem.at[s])

    @pl.when(b == 0)
    def _():
        buf_ref[...] = jnp.zeros_like(buf_ref)
        _segment_copies(seg_ref, seg_copy, 0, wait=False)

    @pl.when(b + 1 < n_tok_blocks)
    def _():
        _segment_copies(seg_next_ref, seg_copy, 1 - slot, wait=False)

    _segment_copies(seg_ref, seg_copy, slot, wait=True)

    lo, hi = _unpack_halves(buf_ref[slot])
    lo = lo.astype(BF16)
    hi = hi.astype(BF16)
    tc = tm // 2
    cid = lax.broadcasted_iota(I32, (tc, R), 1).astype(jnp.int16)
    lrow_t = lrow_ref[...].astype(F32).T
    gate_t = gate_ref[...].T
    for t0 in range(0, tm, tc):
        lrow = lrow_t[t0:t0 + tc, :].astype(jnp.int16)
        gates = gate_t[t0:t0 + tc, :].astype(BF16)
        sel = jnp.zeros((tc, R), BF16)
        for k in range(TOP_K):
            sel = jnp.where(cid == lrow[:, k:k + 1], gates[:, k:k + 1], sel)
        f = jnp.concatenate([jnp.dot(sel, lo, preferred_element_type=F32),
                             jnp.dot(sel, hi, preferred_element_type=F32)], axis=1)
        xo_ref[t0:t0 + tc, :] = _layer_norm_rows(DEEPNORM_ALPHA * x_ref[t0:t0 + tc, :] + f, g_ref[...], beta_ref[...])


def _combine_ln(lrow, gate, x, g, beta, ys, seg_table):
    T = x.shape[0]
    tm = ROW_TILE
    nb = T // tm
    R = TOP_K * tm + N_EXPERTS * SEG_ALIGN
    seg_w = seg_table.shape[-1]
    return pl.pallas_call(
        functools.partial(_combine_body, n_tok_blocks=nb),
        grid=(nb,),
        in_specs=[pl.BlockSpec((None, 1, seg_w), lambda b: (b, 0, 0), memory_space=pltpu.SMEM),
                  pl.BlockSpec((None, 1, seg_w), lambda b: (jnp.minimum(b + 1, nb - 1), 0, 0),
                               memory_space=pltpu.SMEM),
                  pl.BlockSpec((TOP_K, tm), lambda b: (0, b)),
                  pl.BlockSpec((TOP_K, tm), lambda b: (0, b)),
                  pl.BlockSpec((tm, D_MODEL), lambda b: (b, 0)),
                  pl.BlockSpec((1, D_MODEL), lambda b: (0, 0)),
                  pl.BlockSpec((1, D_MODEL), lambda b: (0, 0)),
                  pl.BlockSpec(memory_space=pl.ANY)],
        out_specs=pl.BlockSpec((tm, D_MODEL), lambda b: (b, 0)),
        out_shape=jax.ShapeDtypeStruct((T, D_MODEL), F32),
        scratch_shapes=[pltpu.VMEM((2, R, HALF), U32), pltpu.SemaphoreType.DMA((2,))],
        compiler_params=_params("arbitrary"),
        name="moe_combine_ln",
    )(seg_table, seg_table, lrow, gate, x, g, beta, ys)


def _dft_tables(P):
    n2 = 4 * P
    k = jnp.arange(P, dtype=I32)
    m = ((2 * k[:, None] + 1) * k[None, :]) % n2
    ang = m.astype(F32) * F32(2.0 * math.pi / n2)
    fc32, fs32 = jnp.cos(ang), -jnp.sin(ang)
    scale = F32(1.0 / P)
    gc, gs = (fc32.T * scale).astype(BF16), (fs32.T * scale).astype(BF16)
    return fc32, fs32, (fc32.astype(BF16), fs32.astype(BF16), gc, gs)


def _hyena_positional(S):
    pos = jnp.arange(S, dtype=F32)
    t = jnp.linspace(0.0, 1.0, S, dtype=F32)[:, None]
    bands = (HYENA_EMB_DIM - 1) // 2
    f = jnp.linspace(1e-4, bands - 1, bands, dtype=F32)
    ang = (2.0 * math.pi / S) * pos[:, None] * f[None, :]
    feats = jnp.concatenate([t, jnp.cos(ang), -jnp.sin(ang)], axis=-1)
    feats = jnp.pad(feats, ((0, 0), (0, LANES - HYENA_EMB_DIM)))
    max_decay = math.log(HYENA_DECAY_TARGET) / HYENA_SHORT_DECAY_PCT
    min_decay = math.log(HYENA_DECAY_TARGET) / HYENA_LONG_DECAY_PCT
    deltas = jnp.abs(jnp.linspace(min_decay, max_decay, HYENA_CH, dtype=F32))[None, :]
    return feats, t, deltas


def _alibi_slopes():
    return jnp.asarray(np.array([2.0 ** (-8.0 * (i + 1) / N_HEADS) for i in range(N_HEADS)], dtype=np.float32))


def _even_mixer(x, xshape, tabs, w_in, b_in, short_w, short_b, f1_w, f1_b, f1_freq, f2_w, f2_b, f2_freq, f3_w,
                skip, dw_w, dw_b, cln_g, cln_b, w_out, b_out, ln_g, ln_b, w_r, b_r):
    B, S = xshape
    fc32, fs32, tabs16 = tabs
    proj = _project(x, w_in.astype(BF16), b_in[None, :])
    hy = _short_conv(proj, short_w, short_b[None, :], B, S)
    u = _conformer(proj, tabs, dw_w, dw_b[None, :], cln_g[None, :], cln_b[None, :], B, S)
    feats, tcol, deltas = _hyena_positional(S)
    f1_wp = jnp.pad(f1_w, ((0, LANES - HYENA_EMB_DIM), (0, 0)))
    flip = lambda a: jnp.concatenate([a[:1], a[:0:-1]], axis=0)
    h2 = _hyena_filters(jnp.stack([flip(feats), feats]), f1_wp, f1_b[None, :], f1_freq[None, :], f2_w, f2_b[None, :],
                        f2_freq[None, :], f3_w, jnp.stack([flip(tcol), tcol]), deltas)
    hre, him = _lag_spectra(fc32, fs32, h2, S // fc32.shape[0] - 1)
    z = _long_conv(hy, 2, hy, 0, tabs16, hre, him, 0, skip, B, S)
    z = _long_conv(z, 0, hy, 1, tabs16, hre, him, 1, skip, B, S)
    return _outproj_ln(z, 0, u, 0, w_out.astype(BF16), b_out[None, :], x, ln_g[None, :], ln_b[None, :],
                       w_r.T, b_r[:, None])


def _odd_mixer(x, xshape, layer_idx, w_qkv, lq1, lk1, lq2, lk2, subln_g, w_out, ln_g, ln_b, w_r, b_r):
    B, S = xshape
    lam_init = 0.8 - 0.6 * math.exp(-0.3 * layer_idx)
    lam = (jnp.exp(jnp.sum(lq1 * lk1)) - jnp.exp(jnp.sum(lq2 * lk2)) + lam_init).reshape(1)
    q_scale = jnp.concatenate([jnp.full((ATTN_W,), HEAD_DIM ** -0.5, F32), jnp.ones((2 * ATTN_W,), F32)])
    w = (w_qkv * q_scale).astype(BF16)
    qkv = _project(x, w, jnp.zeros((1, 3 * ATTN_W), F32))
    o = _diff_attention(qkv, _alibi_slopes(), lam, subln_g[None, :], lam_init, B, S)
    return _outproj_ln(o, 0, o, 1, w_out.astype(BF16), jnp.zeros((1, D_MODEL), F32), x, ln_g[None, :], ln_b[None, :],
                       w_r.T, b_r[:, None])


def _round_up(a, m):
    return (a + m - 1) // m * m


def _copy_lists(loff, goff, units):
    E = N_EXPERTS
    caps = _copy_caps()
    big = COPY_ROWS[0]
    n_big = units // (big // SEG_ALIGN)
    cum = jnp.cumsum(n_big, axis=1)
    first = (cum - n_big)[:, None, :]
    p = jnp.arange(caps[0], dtype=I32)[None, :, None]
    mine = (first <= p) & (p < cum[:, None, :])
    within = (p - first) * big
    counts = [cum[:, -1]]
    cols = [jnp.sum(jnp.where(mine, loff[:, None, :] + within, 0), axis=2),
            jnp.sum(jnp.where(mine, goff[:, None, :] + within, 0), axis=2)]
    off = n_big * big
    p = jnp.arange(E, dtype=I32)[None, :, None]
    for rows in COPY_ROWS[1:]:
        has = (units & (rows // SEG_ALIGN)) != 0
        pos = jnp.cumsum(has.astype(I32), axis=1) - has.astype(I32)
        mine = has[:, None, :] & (pos[:, None, :] == p)
        counts.append(jnp.sum(has.astype(I32), axis=1))
        cols += [jnp.sum(jnp.where(mine, (loff + off)[:, None, :], 0), axis=2),
                 jnp.sum(jnp.where(mine, (goff + off)[:, None, :], 0), axis=2)]
        off = off + jnp.where(has, rows, 0)
    return jnp.concatenate([jnp.stack(counts, axis=1)] + cols, axis=1).astype(I32)


def _routing_tables(cnt_blocks, n_ffn_blocks):
    E = N_EXPERTS
    cnt8 = _round_up(cnt_blocks[:, :, 0].astype(I32), SEG_ALIGN)
    seg_end = jnp.cumsum(cnt8, axis=1)
    loff = seg_end - cnt8
    tot8 = jnp.sum(cnt8, axis=0)
    group = _round_up(tot8, EXPERT_ROWS)
    group_end = jnp.cumsum(group)
    group_start = group_end - group
    goff = group_start[None, :] + jnp.cumsum(cnt8, axis=0) - cnt8
    seg_table = _copy_lists(loff, goff, cnt8 // SEG_ALIGN)
    starts = jnp.arange(n_ffn_blocks, dtype=I32) * EXPERT_ROWS
    blk_expert = jnp.minimum(jnp.sum((group_end[None, :] <= starts[:, None]).astype(I32), axis=1), E - 1)
    n_used = group_end[-1:] // EXPERT_ROWS
    tail_start = group_start + tot8
    return loff[:, :, None], seg_table[:, None, :], blk_expert, n_used, tail_start


def _moe_layer(x, routing, layer, w1, b1, w2, b2, ln_g, ln_b):
    T = x.shape[0]
    nb = T // ROW_TILE
    n_rows = _round_up(T * TOP_K + nb * N_EXPERTS * (SEG_ALIGN - 1), EXPERT_ROWS) + N_EXPERTS * EXPERT_ROWS
    n_ffn_blocks = n_rows // EXPERT_ROWS
    idx, gate, rank, cnt = routing
    loff, seg_table, blk_expert, n_used, tail_start = _routing_tables(cnt, n_ffn_blocks)
    xs, lrow = _dispatch(x, idx, rank, loff, seg_table, tail_start, n_rows + EXPERT_ROWS)
    ys = _expert_ffn(xs, blk_expert, n_used, w1, b1[:, :, None, :], w2, b2[:, :, None, :], layer, n_ffn_blocks)
    return _combine_ln(lrow, gate, x, ln_g[None, :], ln_b[None, :], ys, seg_table)


def kernel(x, hy_cf_w_in, hy_cf_b_in, hy_short_w, hy_short_b, hy_f1_w, hy_f1_b, hy_f1_freq, hy_f2_w, hy_f2_b, hy_f2_freq, hy_f3_w, hy_skip, cf_dw_w, cf_dw_b, cf_ln_g, cf_ln_b, even_w_out, even_b_out, attn_w_qkv, attn_lq1, attn_lk1, attn_lq2, attn_lk2, attn_subln_g, attn_w_out, ln1_g, ln1_b, ln2_g, ln2_b, moe_w_r, moe_b_r, moe_w1, moe_b1, moe_w2, moe_b2):
    B, S, D = x.shape
    assert D == D_MODEL and S % LANES == 0
    assert (B * S) % ROW_TILE == 0 and (B * S) % min(B * S, DENSE_ROW_TILE) == 0
    depth = ln1_g.shape[0]
    xf = x.reshape(B * S, D)
    tabs = _dft_tables(S // CONV_BLOCKS)
    for i in range(depth):
        j = i // 2
        if i % 2 == 0:
            xf, routing = _even_mixer(xf, (B, S), tabs, hy_cf_w_in[j], hy_cf_b_in[j], hy_short_w[j], hy_short_b[j],
                                      hy_f1_w[j], hy_f1_b[j], hy_f1_freq[j], hy_f2_w[j], hy_f2_b[j], hy_f2_freq[j],
                                      hy_f3_w[j], hy_skip[j], cf_dw_w[j], cf_dw_b[j], cf_ln_g[j], cf_ln_b[j],
                                      even_w_out[j], even_b_out[j], ln1_g[i], ln1_b[i], moe_w_r[i], moe_b_r[i])
        else:
            xf, routing = _odd_mixer(xf, (B, S), i, attn_w_qkv[j], attn_lq1[j], attn_lk1[j], attn_lq2[j],
                                     attn_lk2[j], attn_subln_g[j], attn_w_out[j], ln1_g[i], ln1_b[i],
                                     moe_w_r[i], moe_b_r[i])
        xf = _moe_layer(xf, routing, i, moe_w1, moe_b1, moe_w2, moe_b2, ln2_g[i], ln2_b[i])
    return xf.reshape(B, S, D)
```
